```python
import jax, jax.numpy as jnp
from jax import lax
import numpy as np

D_MODEL = 1024
BATCH = 8
SEQ = 2048
DEPTH = 1
DEC_BATCH = 128
DEC_SEQ = 4
PAST_LEN = 16384
PAGE_SIZE = 128

D_MIX = D_MODEL
D_RWKV = D_MIX // 2
D_POOL = D_MIX - D_RWKV
HEAD_SIZE = 64
N_HEADS = D_RWKV // HEAD_SIZE
D_DECAY_LORA = 64
D_AAA_LORA = 64
POOL_WINDOWS = (2, 4, 8, 16)
N_POOL_GROUPS = len(POOL_WINDOWS)
POOL_GROUP = D_POOL // N_POOL_GROUPS
POOL_BUF = max(POOL_WINDOWS) - 1
D_SHIFT = 3 * D_RWKV + D_DECAY_LORA + D_AAA_LORA
D_IN = D_SHIFT + D_RWKV + 2 * D_POOL
NORM_EPS = 1e-6
GN_EPS = 64e-5
L2_EPS = 1e-12

kernel_name = "hymba_rwkv7_pool_decode_step"


def _rmsnorm(x, w):
    xf = x.astype(jnp.float32)
    return xf * lax.rsqrt(jnp.mean(xf * xf, axis=-1, keepdims=True) + NORM_EPS) * w.astype(jnp.float32)


def _wkv_step(S, inp):
    r, w, k, v, kk, a = inp
    sa = jnp.einsum('bhvk,bhk->bhv', S, -kk)
    S = S * w[:, :, None, :] + sa[..., None] * (kk * a)[:, :, None, :] + v[..., None] * k[:, :, None, :]
    y = jnp.einsum('bhvk,bhk->bhv', S, r)
    return S, y


def _layer(x, shift_prev, wkv_prev, pool_prev, t0, norm_w, w_in, mu_shift, w_decay_b, w0,
           w_aaa_b, a0, k_k, k_a, r_k, gn_w, gn_b, pool_w, pool_scale, w_out):
    f32 = jnp.float32
    B, T, _ = x.shape
    h = _rmsnorm(x, norm_w)
    proj = jnp.einsum('btd,de->bte', h, w_in.astype(f32))
    p_rwkv, g_rwkv, u_pool, g_pool = jnp.split(
        proj, [D_SHIFT, D_SHIFT + D_RWKV, D_SHIFT + D_RWKV + D_POOL], axis=-1)

    prev = jnp.concatenate([shift_prev.astype(f32)[:, None, :], p_rwkv[:, :-1]], axis=1)
    ps = p_rwkv + mu_shift.astype(f32) * (prev - p_rwkv)
    r, k, v, xw, xa = jnp.split(
        ps, [D_RWKV, 2 * D_RWKV, 3 * D_RWKV, 3 * D_RWKV + D_DECAY_LORA], axis=-1)
    w_raw = -jax.nn.softplus(-(w0.astype(f32) + jnp.tanh(xw) @ w_decay_b.astype(f32))) - 0.5
    decay = jnp.exp(-jnp.exp(w_raw))
    a = jax.nn.sigmoid(a0.astype(f32) + xa @ w_aaa_b.astype(f32))
    heads = lambda z: z.reshape(B, T, N_HEADS, HEAD_SIZE)
    kk = heads(k * k_k.astype(f32))
    kk = kk * lax.rsqrt(jnp.sum(kk * kk, axis=-1, keepdims=True) + L2_EPS)
    k = k * (1.0 + (a - 1.0) * k_a.astype(f32))
    r, k, v, decay, a = heads(r), heads(k), heads(v), heads(decay), heads(a)
    xs = tuple(jnp.moveaxis(z, 1, 0) for z in (r, decay, k, v, kk, a))
    S_fin, ys = lax.scan(_wkv_step, wkv_prev.astype(f32), xs)
    y = jnp.moveaxis(ys, 0, 1)
    mu = jnp.mean(y, axis=-1, keepdims=True)
    var = jnp.mean(jnp.square(y - mu), axis=-1, keepdims=True)
    y = ((y - mu) * lax.rsqrt(var + GN_EPS)).reshape(B, T, D_RWKV) * gn_w.astype(f32) + gn_b.astype(f32)
    bonus = jnp.sum(r * k * r_k.astype(f32), axis=-1, keepdims=True) * v
    o_rwkv = (y + bonus.reshape(B, T, D_RWKV)) * jax.nn.silu(g_rwkv)

    u_ext = jnp.concatenate([pool_prev.astype(f32), u_pool], axis=1)
    cs = jnp.concatenate([jnp.zeros((B, 1, D_POOL), f32), jnp.cumsum(u_ext, axis=1)], axis=1)
    pos = t0 + jnp.arange(T, dtype=jnp.int32)
    diffs = []
    for gi, win in enumerate(POOL_WINDOWS):
        sl = slice(gi * POOL_GROUP, (gi + 1) * POOL_GROUP)
        total = cs[:, POOL_BUF + 1:, sl] - cs[:, POOL_BUF + 1 - win:POOL_BUF + 1 - win + T, sl]
        cnt = jnp.minimum(pos + 1, win).astype(f32)[None, :, None]
        diffs.append(total / cnt - u_pool[..., sl])
    d = jnp.stack(diffs, axis=2)
    o_pool = jnp.einsum('btgc,gce->btge', d, pool_w.astype(f32)).reshape(B, T, D_POOL)
    o_pool = o_pool * pool_scale.astype(f32) * jax.nn.silu(g_pool)

    out = jnp.concatenate([o_rwkv, o_pool], axis=-1) @ w_out.astype(f32)
    return x.astype(f32) + out, p_rwkv[:, -1], S_fin, u_ext[:, -POOL_BUF:]


def setup_inputs(seed: int = 0) -> dict:
    key = jax.random.key(seed)
    ks = jax.random.split(key, 24)
    n = jax.random.normal
    L = DEPTH
    return {
        "x_prompt": n(ks[0], (BATCH, SEQ, D_MODEL), jnp.float32),
        "x_sample": n(ks[1], (DEC_BATCH, DEC_SEQ, D_MODEL), jnp.float32),
        "state_shift": n(ks[2], (L, DEC_BATCH, D_SHIFT), jnp.float32),
        "state_wkv": 0.3 * n(ks[3], (L, DEC_BATCH, N_HEADS, HEAD_SIZE, HEAD_SIZE), jnp.float32),
        "state_pool": n(ks[4], (L, DEC_BATCH, POOL_BUF, D_POOL), jnp.float32),
        "norm_w": 1.0 + 0.05 * n(ks[5], (L, D_MODEL), jnp.float32),
        "w_in": n(ks[6], (L, D_MODEL, D_IN), jnp.float32) * D_MODEL ** -0.5,
        "mu_shift": jax.random.uniform(ks[7], (L, D_SHIFT), jnp.float32),
        "w_decay_b": 0.1 * n(ks[8], (L, D_DECAY_LORA, D_RWKV), jnp.float32) * D_DECAY_LORA ** -0.5,
        "w0": n(ks[9], (L, D_RWKV), jnp.float32),
        "w_aaa_b": 0.5 * n(ks[10], (L, D_AAA_LORA, D_RWKV), jnp.float32) * D_AAA_LORA ** -0.5,
        "a0": 0.1 * n(ks[11], (L, D_RWKV), jnp.float32),
        "k_k": 0.85 + 0.1 * n(ks[12], (L, D_RWKV), jnp.float32),
        "k_a": 1.0 + 0.1 * n(ks[13], (L, D_RWKV), jnp.float32),
        "r_k": 0.1 * n(ks[14], (L, N_HEADS, HEAD_SIZE), jnp.float32),
        "gn_w": 1.0 + 0.05 * n(ks[15], (L, D_RWKV), jnp.float32),
        "gn_b": 0.02 * n(ks[16], (L, D_RWKV), jnp.float32),
        "pool_w": n(ks[17], (L, N_POOL_GROUPS, POOL_GROUP, POOL_GROUP), jnp.float32) * POOL_GROUP ** -0.5,
        "pool_scale": 0.5 + 0.1 * n(ks[18], (L, D_POOL), jnp.float32),
        "w_out": n(ks[19], (L, D_MIX, D_MODEL), jnp.float32) * D_MIX ** -0.5,
        "norm_f": 1.0 + 0.05 * n(ks[20], (D_MODEL,), jnp.float32),
    }


def reference(x_prompt, x_sample, state_shift, state_wkv, state_pool, norm_w, w_in, mu_shift,
              w_decay_b, w0, w_aaa_b, a0, k_k, k_a, r_k, gn_w, gn_b, pool_w, pool_scale,
              w_out, norm_f):
    f32 = jnp.float32
    Bp = x_prompt.shape[0]
    hp = x_prompt.astype(f32)
    hs = x_sample.astype(f32)
    sh_p, wkv_p, pool_p, sh_s, wkv_s, pool_s = [], [], [], [], [], []
    for l in range(DEPTH):
        params = (norm_w[l], w_in[l], mu_shift[l], w_decay_b[l], w0[l], w_aaa_b[l], a0[l],
                  k_k[l], k_a[l], r_k[l], gn_w[l], gn_b[l], pool_w[l], pool_scale[l], w_out[l])
        hp, s1, s2, s3 = _layer(hp, jnp.zeros((Bp, D_SHIFT), f32),
                                jnp.zeros((Bp, N_HEADS, HEAD_SIZE, HEAD_SIZE), f32),
                                jnp.zeros((Bp, POOL_BUF, D_POOL), f32), 0, *params)
        sh_p.append(s1); wkv_p.append(s2); pool_p.append(s3)
        hs, s1, s2, s3 = _layer(hs, state_shift[l], state_wkv[l], state_pool[l], PAST_LEN, *params)
        sh_s.append(s1); wkv_s.append(s2); pool_s.append(s3)
    y_prompt = _rmsnorm(hp, norm_f).astype(x_prompt.dtype)
    y_sample = _rmsnorm(hs, norm_f).astype(x_sample.dtype)
    new_shift_prompt = jnp.stack(sh_p, axis=0)
    new_wkv_prompt = jnp.stack(wkv_p, axis=0)
    new_pool_prompt = jnp.stack(pool_p, axis=0)
    new_shift_sample = jnp.stack(sh_s, axis=0)
    new_wkv_sample = jnp.stack(wkv_s, axis=0)
    new_pool_sample = jnp.stack(pool_s, axis=0)
    return (y_prompt, y_sample, new_shift_prompt, new_wkv_prompt, new_pool_prompt,
            new_shift_sample, new_wkv_sample, new_pool_sample)
```

```python
import functools

import jax
import jax.numpy as jnp
from jax import lax
from jax.experimental import pallas as pl
from jax.experimental.pallas import tpu as pltpu

F32 = jnp.float32
BF16 = jnp.bfloat16

D_MODEL = 1024
D_RWKV = 512
D_POOL = 512
HEAD = 64
N_HEADS = D_RWKV // HEAD
LANES = 128
SUBLANES = 8
N_PAIRS = D_RWKV // LANES
D_LORA = 64
POOL_WINDOWS = (2, 4, 8, 16)
POOL_GROUP = D_POOL // len(POOL_WINDOWS)
POOL_HIST = max(POOL_WINDOWS)
D_SHIFT = 3 * D_RWKV + 2 * D_LORA
D_IN = D_SHIFT + D_RWKV + 2 * D_POOL
NORM_EPS = 1e-6
GN_EPS = 64e-5
L2_EPS = 1e-12
CHUNK = 64
PROMPT_TILE = 256
VMEM_LIMIT = 56 * 1024 * 1024

NN = (((1,), (0,)), ((), ()))
NT = (((1,), (1,)), ((), ()))
TN = (((0,), (0,)), ((), ()))


def _pieces(x, n):
    if x.dtype == BF16:
        return [x]
    out = []
    rem = x
    for i in range(n):
        p = rem.astype(BF16)
        out.append(p)
        if i + 1 < n:
            rem = rem - p.astype(F32)
    return out


def _mm(a, b, dims=NN, na=1, nb=1):
    ap = _pieces(a, na)
    bp = _pieces(b, nb)
    order = max(len(ap), len(bp))
    acc = None
    for i, x in enumerate(ap):
        for j, y in enumerate(bp):
            if i + j >= order:
                continue
            t = lax.dot_general(x, y, dims, preferred_element_type=F32)
            acc = t if acc is None else acc + t
    return acc


def _seg_sum(x, g_ones):
    cols = []
    for s in range(x.shape[1] // LANES):
        cols.append(_mm(x[:, s * LANES:(s + 1) * LANES], g_ones, NN, 2, 1))
    return cols[0] if len(cols) == 1 else jnp.concatenate(cols, axis=1)


def _sigmoid(x):
    return 1.0 / (1.0 + jnp.exp(-x))


def _prep_kernel(x_ref, sh_ref, ph_ref, nw_ref, win_ref, mu_ref, wl_ref, w0_ref, a0_ref, kk_ref,
                 ka_ref, rk_ref, pw_ref, ps_ref, g_ref,
                 r_o, k_o, v_o, lw_o, a_o, b_o, bon_o, sg_o, op_o, nsh_o, npl_o,
                 p_ext, u_ext, *, stride, t0, hs):
    j = pl.program_id(1)
    rows = x_ref.shape[0]
    ph = POOL_HIST * stride

    @pl.when(j == 0)
    def _():
        p_ext[0:hs, :] = sh_ref[0]
        u_ext[0:ph, :] = ph_ref[0]

    x = x_ref[...]
    h = x * lax.rsqrt(jnp.mean(x * x, axis=-1, keepdims=True) + NORM_EPS) * nw_ref[...]
    proj = jnp.dot(h.astype(BF16), win_ref[...], preferred_element_type=F32)

    p = proj[:, :D_SHIFT]
    p_ext[hs:hs + rows, :] = p
    prev = p_ext[pl.ds(hs - stride, rows), :]
    ps = p + mu_ref[...] * (prev - p)
    r = ps[:, 0:D_RWKV]
    k = ps[:, D_RWKV:2 * D_RWKV]
    v = ps[:, 2 * D_RWKV:3 * D_RWKV]
    xwa = ps[:, 3 * D_RWKV:D_SHIFT]
    lane = lax.broadcasted_iota(jnp.int32, xwa.shape, 1)
    lora_in = jnp.where(lane < D_LORA, jnp.tanh(xwa), xwa)
    lora = jnp.dot(lora_in.astype(BF16), wl_ref[...], preferred_element_type=F32)
    lw = (-0.6065306597126334) * _sigmoid(w0_ref[...] + lora[:, :D_RWKV])
    alpha = _sigmoid(a0_ref[...] + lora[:, D_RWKV:])
    g_ones = g_ref[...]
    kk = k * kk_ref[...]
    kk = kk * lax.rsqrt(_seg_sum(kk * kk, g_ones) + L2_EPS)
    k2 = k * (1.0 + (alpha - 1.0) * ka_ref[...])
    r_o[...] = r
    k_o[...] = k2
    v_o[...] = v
    lw_o[...] = lw
    a_o[...] = -kk
    b_o[...] = kk * alpha
    bon_o[...] = _seg_sum(r * k2 * rk_ref[...], g_ones) * v
    g_rwkv = proj[:, D_SHIFT:D_SHIFT + D_RWKV]
    sg_o[...] = g_rwkv * _sigmoid(g_rwkv)

    u = proj[:, D_SHIFT + D_RWKV:D_SHIFT + D_RWKV + D_POOL]
    g_pool = proj[:, D_SHIFT + D_RWKV + D_POOL:]
    u_ext[ph:ph + rows, :] = u
    row = lax.broadcasted_iota(jnp.int32, (rows, 1), 0)
    pos = t0 + (j * rows + row) // stride
    for gi, win in enumerate(POOL_WINDOWS):
        sl = slice(gi * POOL_GROUP, (gi + 1) * POOL_GROUP)
        tot = u[:, sl]
        for back in range(1, win):
            tot = tot + u_ext[pl.ds(ph - back * stride, rows), sl]
        inv_cnt = 1.0 / jnp.minimum(pos + 1, win).astype(F32)
        d = tot * inv_cnt - u[:, sl]
        o = jnp.dot(d.astype(BF16), pw_ref[gi], preferred_element_type=F32)
        gp = g_pool[:, sl]
        op_o[:, sl] = o * ps_ref[:, sl] * (gp * _sigmoid(gp))

    last_p = p_ext[rows:rows + hs, :]
    last_u = u_ext[rows:rows + ph, :]
    p_ext[0:hs, :] = last_p
    u_ext[0:ph, :] = last_u
    nsh_o[0] = last_p
    npl_o[0] = last_u


def _prep_call(x, sh_hist, pool_hist, params, *, n_seq, rows, stride, t0):
    total = x.shape[0]
    tiles = total // (n_seq * rows)
    hs = sh_hist.shape[1]
    ph = POOL_HIST * stride
    row_blk = lambda b, j: (b * tiles + j, 0)
    const2 = lambda b, j: (0, 0)
    seq3 = lambda b, j: (b, 0, 0)
    act = lambda w: pl.BlockSpec((rows, w), row_blk)
    in_specs = [
        act(D_MODEL),
        pl.BlockSpec((1, hs, D_SHIFT), seq3),
        pl.BlockSpec((1, ph, D_POOL), seq3),
        pl.BlockSpec((1, D_MODEL), const2),
        pl.BlockSpec((D_MODEL, D_IN), const2),
        pl.BlockSpec((1, D_SHIFT), const2),
        pl.BlockSpec((2 * D_LORA, 2 * D_RWKV), const2),
        pl.BlockSpec((1, D_RWKV), const2),
        pl.BlockSpec((1, D_RWKV), const2),
        pl.BlockSpec((1, D_RWKV), const2),
        pl.BlockSpec((1, D_RWKV), const2),
        pl.BlockSpec((1, D_RWKV), const2),
        pl.BlockSpec((len(POOL_WINDOWS), POOL_GROUP, POOL_GROUP), lambda b, j: (0, 0, 0)),
        pl.BlockSpec((1, D_POOL), const2),
        pl.BlockSpec((LANES, LANES), const2),
    ]
    out_shape = [jax.ShapeDtypeStruct((total, D_RWKV), F32)] * 9 + [
        jax.ShapeDtypeStruct((n_seq, hs, D_SHIFT), F32),
        jax.ShapeDtypeStruct((n_seq, ph, D_POOL), F32),
    ]
    out_specs = [act(D_RWKV)] * 9 + [
        pl.BlockSpec((1, hs, D_SHIFT), seq3),
        pl.BlockSpec((1, ph, D_POOL), seq3),
    ]
    return pl.pallas_call(
        functools.partial(_prep_kernel, stride=stride, t0=t0, hs=hs),
        grid=(n_seq, tiles),
        in_specs=in_specs,
        out_specs=out_specs,
        out_shape=out_shape,
        scratch_shapes=[pltpu.VMEM((hs + rows, D_SHIFT), F32), pltpu.VMEM((ph + rows, D_POOL), F32)],
        compiler_params=pltpu.CompilerParams(
            dimension_semantics=("arbitrary", "arbitrary"), vmem_limit_bytes=VMEM_LIMIT),
        name="prep",
    )(x, sh_hist, pool_hist, *params)


def _stack(x, m0):
    return jnp.concatenate([jnp.where(m0, x, 0.0), jnp.where(m0, 0.0, x)], axis=0)


def _wkv_kernel(r_ref, k_ref, v_ref, lw_ref, a_ref, b_ref, s0_ref, tri_ref, msk_ref,
                y_ref, so_ref, s2_ref):
    c = pl.program_id(1)
    n_c = pl.num_programs(1)
    zeros = jnp.zeros((HEAD, HEAD), F32)

    @pl.when(c == 0)
    def _():
        for p in range(N_PAIRS):
            top = jnp.concatenate([s0_ref[0, 2 * p], zeros], axis=1)
            bot = jnp.concatenate([zeros, s0_ref[0, 2 * p + 1]], axis=1)
            s2_ref[p] = jnp.concatenate([top, bot], axis=0)

    lw = lw_ref[...]
    cs = _mm(tri_ref[...], lw, NN, 1, 3)
    cs_end = cs[CHUNK - 1:CHUNK, :]
    g_t = jnp.exp(cs)
    g_inv = jnp.exp(-cs)
    g_prev = jnp.exp(cs - lw)
    g_tail = jnp.exp(cs_end - cs)
    g_end = jnp.exp(cs_end)
    a_t = a_ref[...] * g_prev
    r_t = r_ref[...] * g_t
    b_t = b_ref[...] * g_inv
    k_t = k_ref[...] * g_inv
    b_h = b_ref[...] * g_tail
    k_h = k_ref[...] * g_tail
    v_all = v_ref[...]

    lane = lax.broadcasted_iota(jnp.int32, (CHUNK, LANES), 1)
    m0 = lane < HEAD
    strict = msk_ref[0] > 0.5
    incl = msk_ref[1] > 0.5
    eye = msk_ref[2]
    n_levels = msk_ref.shape[0] - 3

    for p in range(N_PAIRS):
        sl = slice(p * LANES, (p + 1) * LANES)
        at, rt, bt, kt = (_stack(z[:, sl], m0) for z in (a_t, r_t, b_t, k_t))
        bh, kh, v2 = (_stack(z[:, sl], m0) for z in (b_h, k_h, v_all))
        a_ab = jnp.where(strict, _mm(at, bt, NT, 2, 2), 0.0)
        a_ak = jnp.where(strict, _mm(at, kt, NT, 2, 2), 0.0)
        a_rb = jnp.where(incl, _mm(rt, bt, NT, 2, 2), 0.0)
        a_rk = jnp.where(incl, _mm(rt, kt, NT, 2, 2), 0.0)
        t_inv = eye + a_ab * msk_ref[3]
        for lvl in range(1, n_levels):
            am = a_ab * msk_ref[3 + lvl]
            t_inv = t_inv + _mm(t_inv, _mm(am, t_inv, NN, 2, 2), NN, 2, 2)
        pm = _mm(t_inv, at, NN, 2, 2)
        q = _mm(t_inv, _mm(a_ak, v2, NN, 2, 2), NN, 2, 2)
        m_mat = eye * g_end[:, sl] + _mm(pm, bh, TN, 2, 2)
        n_mat = _mm(q, bh, TN, 2, 2) + _mm(v2, kh, TN, 2, 2)
        y1 = rt + _mm(a_rb, pm, NN, 2, 2)
        y2 = _mm(a_rb, q, NN, 2, 2) + _mm(a_rk, v2, NN, 2, 2)
        s2 = s2_ref[p]
        o2 = _mm(y1, s2, NT, 2, 2) + y2
        s2_ref[p] = _mm(s2, m_mat, NN, 2, 2) + n_mat
        y_ref[:, sl] = o2[:CHUNK] + o2[CHUNK:]

    @pl.when(c == n_c - 1)
    def _():
        for p in range(N_PAIRS):
            s2 = s2_ref[p]
            so_ref[0, 2 * p] = s2[:HEAD, :HEAD]
            so_ref[0, 2 * p + 1] = s2[HEAD:, HEAD:]


def _wkv_masks():
    n = 2 * CHUNK
    ri = lax.broadcasted_iota(jnp.int32, (n, n), 0)
    ci = lax.broadcasted_iota(jnp.int32, (n, n), 1)
    same = (ri // CHUNK) == (ci // CHUNK)
    masks = [same & (ci < ri), same & (ci <= ri), ri == ci]
    m = 1
    while m < CHUNK:
        masks.append(((ri // (2 * m)) == (ci // (2 * m))) & ((ri // m) % 2 == 1) & ((ci // m) % 2 == 0))
        m *= 2
    tri = (lax.broadcasted_iota(jnp.int32, (CHUNK, CHUNK), 1)
           <= lax.broadcasted_iota(jnp.int32, (CHUNK, CHUNK), 0))
    return tri.astype(BF16), jnp.stack(masks).astype(F32)


def _wkv_call(seqs, s0, *, n_seq):
    total = seqs[0].shape[0]
    n_c = total // (n_seq * CHUNK)
    tri, masks = _wkv_masks()
    act = pl.BlockSpec((CHUNK, D_RWKV), lambda b, c: (b * n_c + c, 0))
    st = pl.BlockSpec((1, N_HEADS, HEAD, HEAD), lambda b, c: (b, 0, 0, 0))
    return pl.pallas_call(
        _wkv_kernel,
        grid=(n_seq, n_c),
        in_specs=[act] * 6 + [st,
                              pl.BlockSpec(tri.shape, lambda b, c: (0, 0)),
                              pl.BlockSpec(masks.shape, lambda b, c: (0, 0, 0))],
        out_specs=[act, st],
        out_shape=[jax.ShapeDtypeStruct((total, D_RWKV), F32),
                   jax.ShapeDtypeStruct((n_seq, N_HEADS, HEAD, HEAD), F32)],
        scratch_shapes=[pltpu.VMEM((N_PAIRS, LANES, LANES), F32)],
        compiler_params=pltpu.CompilerParams(
            dimension_semantics=("arbitrary", "arbitrary"), vmem_limit_bytes=VMEM_LIMIT),
        name="wkv",
    )(*seqs, s0, tri, masks)


def _out_kernel(y_ref, bon_ref, sg_ref, op_ref, x_ref, gw_ref, gb_ref, wo_ref, nf_ref, g_ref, o_ref):
    g_ones = g_ref[...]
    y = y_ref[...]
    mu = _seg_sum(y, g_ones) * (1.0 / HEAD)
    d = y - mu
    var = _seg_sum(d * d, g_ones) * (1.0 / HEAD)
    yn = d * lax.rsqrt(var + GN_EPS) * gw_ref[...] + gb_ref[...]
    o_rwkv = (yn + bon_ref[...]) * sg_ref[...]
    out = (jnp.dot(o_rwkv.astype(BF16), wo_ref[0:D_RWKV, :], preferred_element_type=F32)
           + jnp.dot(op_ref[...].astype(BF16), wo_ref[D_RWKV:, :], preferred_element_type=F32))
    res = x_ref[...] + out
    o_ref[...] = res * lax.rsqrt(jnp.mean(res * res, axis=-1, keepdims=True) + NORM_EPS) * nf_ref[...]


def _out_call(y, bon, sg, op, x, gn_w, gn_b, w_out, norm_f, g_ones, *, rows):
    total = x.shape[0]
    row_blk = lambda i: (i, 0)
    const2 = lambda i: (0, 0)
    act = lambda w: pl.BlockSpec((rows, w), row_blk)
    return pl.pallas_call(
        _out_kernel,
        grid=(total // rows,),
        in_specs=[act(D_RWKV)] * 4 + [act(D_MODEL),
                                      pl.BlockSpec((1, D_RWKV), const2),
                                      pl.BlockSpec((1, D_RWKV), const2),
                                      pl.BlockSpec((D_RWKV + D_POOL, D_MODEL), const2),
                                      pl.BlockSpec((1, D_MODEL), const2),
                                      pl.BlockSpec((LANES, LANES), const2)],
        out_specs=act(D_MODEL),
        out_shape=jax.ShapeDtypeStruct((total, D_MODEL), F32),
        compiler_params=pltpu.CompilerParams(
            dimension_semantics=("arbitrary",), vmem_limit_bytes=VMEM_LIMIT),
        name="out",
    )(y, bon, sg, op, x, gn_w, gn_b, w_out, norm_f, g_ones)


def _head_ones():
    i = lax.broadcasted_iota(jnp.int32, (LANES, LANES), 0) // HEAD
    j = lax.broadcasted_iota(jnp.int32, (LANES, LANES), 1) // HEAD
    return (i == j).astype(BF16)


def kernel(x_prompt, x_sample, state_shift, state_wkv, state_pool, norm_w, w_in, mu_shift, w_decay_b,
           w0, w_aaa_b, a0, k_k, k_a, r_k, gn_w, gn_b, pool_w, pool_scale, w_out, norm_f):
    depth = norm_w.shape[0]
    n_p, t_p, _ = x_prompt.shape
    n_s, t_s, _ = x_sample.shape
    assert t_p % PROMPT_TILE == 0 and PROMPT_TILE % CHUNK == 0 and t_s <= CHUNK
    past_len = 16384
    g_ones = _head_ones()
    row = lambda z: z.reshape(1, -1).astype(F32)

    hp = x_prompt.astype(F32).reshape(n_p * t_p, D_MODEL)
    hs = jnp.transpose(x_sample.astype(F32), (1, 0, 2)).reshape(t_s * n_s, D_MODEL)
    outs = [[] for _ in range(6)]
    for l in range(depth):
        zl = jnp.zeros((D_LORA, D_RWKV), F32)
        w_lora = jnp.concatenate([jnp.concatenate([w_decay_b[l], zl], axis=1),
                                  jnp.concatenate([zl, w_aaa_b[l]], axis=1)], axis=0).astype(BF16)
        prep_params = (row(norm_w[l]), w_in[l].astype(BF16), row(mu_shift[l]), w_lora, row(w0[l]),
                       row(a0[l]), row(k_k[l]), row(k_a[l]), row(r_k[l]), pool_w[l].astype(BF16),
                       row(pool_scale[l]), g_ones)
        out_params = (row(gn_w[l]), row(gn_b[l]), w_out[l].astype(BF16))
        last = l == depth - 1
        norm_last = row(norm_f) if last else None

        res = _prep_call(hp, jnp.zeros((n_p, SUBLANES, D_SHIFT), F32),
                         jnp.zeros((n_p, POOL_HIST, D_POOL), F32), prep_params,
                         n_seq=n_p, rows=PROMPT_TILE, stride=1, t0=0)
        seqs, (bon, sg, op), nsh, npl = res[:6], res[6:9], res[9], res[10]
        y, s_fin = _wkv_call(seqs, jnp.zeros((n_p, N_HEADS, HEAD, HEAD), F32), n_seq=n_p)
        assert last, "stacked layers need a residual-only output kernel"
        hp = _out_call(y, bon, sg, op, hp, *out_params, norm_last, g_ones, rows=PROMPT_TILE)
        outs[0].append(nsh[:, SUBLANES - 1])
        outs[1].append(s_fin)
        outs[2].append(npl[:, 1:])

        sh_hist = state_shift[l].astype(F32)[None]
        pool_hist = jnp.transpose(state_pool[l].astype(F32), (1, 0, 2))
        pool_hist = jnp.pad(pool_hist, ((1, 0), (0, 0), (0, 0))).reshape(1, POOL_HIST * n_s, D_POOL)
        res = _prep_call(hs, sh_hist, pool_hist, prep_params,
                         n_seq=1, rows=t_s * n_s, stride=n_s, t0=past_len)
        seqs, (bon, sg, op), nsh, npl = res[:6], res[6:9], res[9], res[10]

        def to_seq_major(z):
            z = jnp.transpose(z.reshape(t_s, n_s, D_RWKV), (1, 0, 2))
            return jnp.pad(z, ((0, 0), (0, CHUNK - t_s), (0, 0))).reshape(n_s * CHUNK, D_RWKV)

        y, s_fin = _wkv_call([to_seq_major(z) for z in seqs], state_wkv[l].astype(F32), n_seq=n_s)
        y = jnp.transpose(y.reshape(n_s, CHUNK, D_RWKV)[:, :t_s], (1, 0, 2)).reshape(t_s * n_s, D_RWKV)
        hs = _out_call(y, bon, sg, op, hs, *out_params, norm_last, g_ones, rows=t_s * n_s)
        outs[3].append(nsh[0])
        outs[4].append(s_fin)
        outs[5].append(jnp.transpose(npl[0, n_s:].reshape(POOL_HIST - 1, n_s, D_POOL), (1, 0, 2)))

    y_prompt = hp.reshape(n_p, t_p, D_MODEL).astype(x_prompt.dtype)
    y_sample = jnp.transpose(hs.reshape(t_s, n_s, D_MODEL), (1, 0, 2)).astype(x_sample.dtype)
    return (y_prompt, y_sample) + tuple(jnp.stack(o, axis=0) for o in outs)
```

```python
import functools

import jax
import jax.numpy as jnp
from jax import lax
from jax.experimental import pallas as pl
from jax.experimental.pallas import tpu as pltpu

F32 = jnp.float32
BF16 = jnp.bfloat16

D_MODEL = 1024
D_RWKV = 512
D_POOL = 512
HEAD = 64
N_HEADS = D_RWKV // HEAD
LANES = 128
SUBLANES = 8
N_PAIRS = D_RWKV // LANES
D_LORA = 64
POOL_WINDOWS = (2, 4, 8, 16)
POOL_GROUP = D_POOL // len(POOL_WINDOWS)
POOL_HIST = max(POOL_WINDOWS)
D_SHIFT = 3 * D_RWKV + 2 * D_LORA
D_IN = D_SHIFT + D_RWKV + 2 * D_POOL
PAST_LEN = 16384
NORM_EPS = 1e-6
GN_EPS = 64e-5
L2_EPS = 1e-12
CHUNK = 64
PROMPT_TILE = 256
VMEM_LIMIT = 56 * 1024 * 1024
P_A = (1, 1)
P_T = (1, 1)
P_U = (1, 1)
P_S = (1, 1)
P_Y = (1, 1)
WKV_CHUNKS_PER_STEP = 8

NN = (((1,), (0,)), ((), ()))
NT = (((1,), (1,)), ((), ()))
TN = (((0,), (0,)), ((), ()))


def _pieces(x, n):
    if x.dtype == BF16:
        return [x]
    out = []
    rem = x
    for i in range(n):
        p = rem.astype(BF16)
        out.append(p)
        if i + 1 < n:
            rem = rem - p.astype(F32)
    return out


def _mm(a, b, dims=NN, na=1, nb=1):
    ap = _pieces(a, na)
    bp = _pieces(b, nb)
    order = max(len(ap), len(bp))
    acc = None
    for i, x in enumerate(ap):
        for j, y in enumerate(bp):
            if i + j >= order:
                continue
            t = lax.dot_general(x, y, dims, preferred_element_type=F32)
            acc = t if acc is None else acc + t
    return acc


def _seg_sum(x, g_ones):
    cols = []
    for s in range(x.shape[1] // LANES):
        cols.append(_mm(x[:, s * LANES:(s + 1) * LANES], g_ones, NN, 2, 1))
    return cols[0] if len(cols) == 1 else jnp.concatenate(cols, axis=1)


def _sigmoid(x):
    return 1.0 / (1.0 + jnp.exp(-x))


def _prep_kernel(x_ref, sh_ref, ph_ref, nw_ref, win_ref, mu_ref, wl_ref, w0_ref, a0_ref, kk_ref,
                 ka_ref, rk_ref, pw_ref, ps_ref, g_ref,
                 r_o, k_o, v_o, lw_o, a_o, b_o, bon_o, sg_o, op_o, nsh_o, npl_o,
                 p_ext, u_ext, *, stride, t0, hs):
    j = pl.program_id(1)
    rows = x_ref.shape[0]
    ph = POOL_HIST * stride

    @pl.when(j == 0)
    def _():
        p_ext[0:hs, :] = sh_ref[0]
        u_ext[0:ph, :] = ph_ref[0]

    x = x_ref[...]
    h = x * lax.rsqrt(jnp.mean(x * x, axis=-1, keepdims=True) + NORM_EPS) * nw_ref[...]
    proj = jnp.dot(h.astype(BF16), win_ref[...], preferred_element_type=F32)

    p = proj[:, :D_SHIFT]
    p_ext[hs:hs + rows, :] = p
    prev = p_ext[pl.ds(hs - stride, rows), :]
    ps = p + mu_ref[...] * (prev - p)
    r = ps[:, 0:D_RWKV]
    k = ps[:, D_RWKV:2 * D_RWKV]
    v = ps[:, 2 * D_RWKV:3 * D_RWKV]
    xwa = ps[:, 3 * D_RWKV:D_SHIFT]
    lane = lax.broadcasted_iota(jnp.int32, xwa.shape, 1)
    lora_in = jnp.where(lane < D_LORA, jnp.tanh(xwa), xwa)
    lora = jnp.dot(lora_in.astype(BF16), wl_ref[...], preferred_element_type=F32)
    lw = (-0.6065306597126334) * _sigmoid(w0_ref[...] + lora[:, :D_RWKV])
    alpha = _sigmoid(a0_ref[...] + lora[:, D_RWKV:])
    g_ones = g_ref[...]
    kk = k * kk_ref[...]
    kk = kk * lax.rsqrt(_seg_sum(kk * kk, g_ones) + L2_EPS)
    k2 = k * (1.0 + (alpha - 1.0) * ka_ref[...])
    r_o[...] = r
    k_o[...] = k2
    v_o[...] = v
    lw_o[...] = lw
    a_o[...] = -kk
    b_o[...] = kk * alpha
    bon_o[...] = _seg_sum(r * k2 * rk_ref[...], g_ones) * v
    g_rwkv = proj[:, D_SHIFT:D_SHIFT + D_RWKV]
    sg_o[...] = g_rwkv * _sigmoid(g_rwkv)

    u = proj[:, D_SHIFT + D_RWKV:D_SHIFT + D_RWKV + D_POOL]
    g_pool = proj[:, D_SHIFT + D_RWKV + D_POOL:]
    u_ext[ph:ph + rows, :] = u
    row = lax.broadcasted_iota(jnp.int32, (rows, 1), 0)
    pos = t0 + (j * rows + row) // stride
    for gi, win in enumerate(POOL_WINDOWS):
        sl = slice(gi * POOL_GROUP, (gi + 1) * POOL_GROUP)
        tot = u[:, sl]
        for back in range(1, win):
            tot = tot + u_ext[pl.ds(ph - back * stride, rows), sl]
        inv_cnt = 1.0 / jnp.minimum(pos + 1, win).astype(F32)
        d = tot * inv_cnt - u[:, sl]
        o = jnp.dot(d.astype(BF16), pw_ref[gi], preferred_element_type=F32)
        gp = g_pool[:, sl]
        op_o[:, sl] = o * ps_ref[:, sl] * (gp * _sigmoid(gp))

    last_p = p_ext[rows:rows + hs, :]
    last_u = u_ext[rows:rows + ph, :]
    p_ext[0:hs, :] = last_p
    u_ext[0:ph, :] = last_u
    nsh_o[0] = last_p
    npl_o[0] = last_u


def _prep_call(x, sh_hist, pool_hist, params, *, n_seq, rows, stride, t0):
    total = x.shape[0]
    tiles = total // (n_seq * rows)
    hs = sh_hist.shape[1]
    ph = POOL_HIST * stride
    row_blk = lambda b, j: (b * tiles + j, 0)
    const2 = lambda b, j: (0, 0)
    seq3 = lambda b, j: (b, 0, 0)
    act = lambda w: pl.BlockSpec((rows, w), row_blk)
    in_specs = [
        act(D_MODEL),
        pl.BlockSpec((1, hs, D_SHIFT), seq3),
        pl.BlockSpec((1, ph, D_POOL), seq3),
        pl.BlockSpec((1, D_MODEL), const2),
        pl.BlockSpec((D_MODEL, D_IN), const2),
        pl.BlockSpec((1, D_SHIFT), const2),
        pl.BlockSpec((2 * D_LORA, 2 * D_RWKV), const2),
        pl.BlockSpec((1, D_RWKV), const2),
        pl.BlockSpec((1, D_RWKV), const2),
        pl.BlockSpec((1, D_RWKV), const2),
        pl.BlockSpec((1, D_RWKV), const2),
        pl.BlockSpec((1, D_RWKV), const2),
        pl.BlockSpec((len(POOL_WINDOWS), POOL_GROUP, POOL_GROUP), lambda b, j: (0, 0, 0)),
        pl.BlockSpec((1, D_POOL), const2),
        pl.BlockSpec((LANES, LANES), const2),
    ]
    out_shape = [jax.ShapeDtypeStruct((total, D_RWKV), F32)] * 9 + [
        jax.ShapeDtypeStruct((n_seq, hs, D_SHIFT), F32),
        jax.ShapeDtypeStruct((n_seq, ph, D_POOL), F32),
    ]
    out_specs = [act(D_RWKV)] * 9 + [
        pl.BlockSpec((1, hs, D_SHIFT), seq3),
        pl.BlockSpec((1, ph, D_POOL), seq3),
    ]
    return pl.pallas_call(
        functools.partial(_prep_kernel, stride=stride, t0=t0, hs=hs),
        grid=(n_seq, tiles),
        in_specs=in_specs,
        out_specs=out_specs,
        out_shape=out_shape,
        scratch_shapes=[pltpu.VMEM((hs + rows, D_SHIFT), F32), pltpu.VMEM((ph + rows, D_POOL), F32)],
        compiler_params=pltpu.CompilerParams(
            dimension_semantics=("arbitrary", "arbitrary"), vmem_limit_bytes=VMEM_LIMIT),
        name="prep",
    )(x, sh_hist, pool_hist, *params)


def _stack(x, m0):
    return jnp.concatenate([jnp.where(m0, x, 0.0), jnp.where(m0, 0.0, x)], axis=0)


def _wkv_kernel(r_ref, k_ref, v_ref, lw_ref, a_ref, b_ref, s0_ref, tri_ref, mss_ref, msq_ref,
                y_ref, so_ref, z_ref, *, n_chunks):
    c = pl.program_id(1)
    n_c = pl.num_programs(1)
    zeros = jnp.zeros((HEAD, HEAD), F32)

    @pl.when(c == 0)
    def _():
        for p in range(N_PAIRS):
            top = jnp.concatenate([s0_ref[0, 2 * p], zeros], axis=1)
            bot = jnp.concatenate([zeros, s0_ref[0, 2 * p + 1]], axis=1)
            z_ref[p] = jnp.concatenate([top, bot], axis=0).T

    lane = lax.broadcasted_iota(jnp.int32, (CHUNK, LANES), 1)
    m0 = lane < HEAD
    strict = mss_ref[0] > 0.5
    incl2 = jnp.concatenate([mss_ref[1], mss_ref[1]], axis=1) > 0.5
    eye_ss = mss_ref[2]
    n_levels = mss_ref.shape[0] - 3
    eye = msq_ref[0]
    same_head = msq_ref[1] > 0.5
    zblk = jnp.zeros((CHUNK, LANES), F32)
    zblk2 = jnp.zeros((2 * CHUNK, LANES), F32)

    scaled = []
    for ci in range(n_chunks):
        rows = slice(ci * CHUNK, (ci + 1) * CHUNK)
        lw = lw_ref[rows, :]
        cs = _mm(tri_ref[...], lw, NN, 1, 3)
        cs_end = cs[CHUNK - 1:CHUNK, :]
        g_inv = jnp.exp(-cs)
        g_tail = jnp.exp(cs_end - cs)
        scaled.append(dict(
            g_end=jnp.exp(cs_end),
            a_t=a_ref[rows, :] * jnp.exp(cs - lw), r_t=r_ref[rows, :] * jnp.exp(cs),
            b_t=b_ref[rows, :] * g_inv, k_t=k_ref[rows, :] * g_inv,
            b_h=b_ref[rows, :] * g_tail, k_h=k_ref[rows, :] * g_tail, v=v_ref[rows, :]))
    chains = [(ci, p) for ci in range(n_chunks) for p in range(N_PAIRS)]
    lanes = lambda p: slice(p * LANES, (p + 1) * LANES)
    pair = lambda name: [scaled[ci][name][:, lanes(p)] for ci, p in chains]
    each = lambda fn, *lists: [fn(*args) for args in zip(*lists)]
    cat0 = lambda *xs: jnp.concatenate(xs, axis=0)
    cat1 = lambda *xs: jnp.concatenate(xs, axis=1)
    at, rt, bt, kt, bh, kh, v = (pair(n) for n in ("a_t", "r_t", "b_t", "k_t", "b_h", "k_h", "v"))
    v2 = each(lambda x: _stack(x, m0), v)
    scores = each(lambda a, r, b, k: _mm(cat0(a, r), cat0(_stack(b, m0), _stack(k, m0)), NT, *P_A),
                  at, rt, bt, kt)
    a_ab = each(lambda s: jnp.where(strict, s[:CHUNK, :LANES], 0.0), scores)
    a_ak = each(lambda s: jnp.where(strict, s[:CHUNK, LANES:], 0.0), scores)
    a_r = each(lambda s: jnp.where(incl2, s[CHUNK:, :], 0.0), scores)
    t_inv = each(lambda x: eye_ss + x * mss_ref[3], a_ab)
    for lvl in range(1, n_levels):
        inner = each(lambda x, t: _mm(x * mss_ref[3 + lvl], _stack(t, m0), NN, *P_T), a_ab, t_inv)
        t_inv = each(lambda t, w: t + _mm(t, _stack(w, m0), NN, *P_T), t_inv, inner)
    akv = each(lambda x, y: _mm(x, y, NN, *P_U), a_ak, v2)
    pq = each(lambda t, x, y: _mm(t, cat1(_stack(x, m0), _stack(y, m0)), NN, *P_U), t_inv, at, akv)
    bpq = each(lambda b, k, x, y: _mm(cat0(b, k), cat0(x, cat1(zblk, y)), TN, *P_S), bh, kh, pq, v)
    g_end = [scaled[ci]["g_end"][:, lanes(p)] for ci, p in chains]
    m_t = each(lambda g, x: eye * g + jnp.where(same_head, x[:, :LANES], 0.0), g_end, bpq)
    n_t = each(lambda x: jnp.where(same_head, x[:, LANES:], 0.0), bpq)
    yy = each(lambda x, y, z: _mm(x, cat0(cat1(_stack(y[:, :LANES], m0), _stack(y[:, LANES:], m0)),
                                         cat1(zblk2, z)), NN, *P_Y), a_r, pq, v2)
    y1 = each(lambda r, x: r + x[:, :LANES], rt, yy)
    y2 = each(lambda x: x[:, LANES:], yy)

    state = [z_ref[p] for p in range(N_PAIRS)]
    for ci in range(n_chunks):
        for p in range(N_PAIRS):
            i = ci * N_PAIRS + p
            both = _mm(cat0(y1[i], m_t[i]), state[p], NN, *P_S)
            y_ref[ci * CHUNK:(ci + 1) * CHUNK, lanes(p)] = both[:CHUNK] + y2[i]
            state[p] = both[CHUNK:] + n_t[i]
    for p in range(N_PAIRS):
        z_ref[p] = state[p]

    @pl.when(c == n_c - 1)
    def _():
        for p in range(N_PAIRS):
            s2 = z_ref[p].T
            so_ref[0, 2 * p] = s2[:HEAD, :HEAD]
            so_ref[0, 2 * p + 1] = s2[HEAD:, HEAD:]


def _wkv_masks():
    ri = lax.broadcasted_iota(jnp.int32, (CHUNK, 2 * CHUNK), 0)
    ci = lax.broadcasted_iota(jnp.int32, (CHUNK, 2 * CHUNK), 1) % CHUNK
    side = [ci < ri, ci <= ri, ri == ci]
    m = 1
    while m < CHUNK:
        side.append(((ri // (2 * m)) == (ci // (2 * m))) & ((ri // m) % 2 == 1) & ((ci // m) % 2 == 0))
        m *= 2
    rq = lax.broadcasted_iota(jnp.int32, (LANES, LANES), 0)
    cq = lax.broadcasted_iota(jnp.int32, (LANES, LANES), 1)
    square = [rq == cq, (rq // HEAD) == (cq // HEAD)]
    tri = (lax.broadcasted_iota(jnp.int32, (CHUNK, CHUNK), 1)
           <= lax.broadcasted_iota(jnp.int32, (CHUNK, CHUNK), 0))
    return tri.astype(BF16), jnp.stack(side).astype(F32), jnp.stack(square).astype(F32)


def _wkv_call(seqs, s0, *, n_seq, n_chunks):
    total = seqs[0].shape[0]
    n_c = total // (n_seq * n_chunks * CHUNK)
    assert CHUNK == HEAD and 2 * CHUNK == LANES
    tri, m_side, m_square = _wkv_masks()
    act = pl.BlockSpec((n_chunks * CHUNK, D_RWKV), lambda b, c: (b * n_c + c, 0))
    st = pl.BlockSpec((1, N_HEADS, HEAD, HEAD), lambda b, c: (b, 0, 0, 0))
    return pl.pallas_call(
        functools.partial(_wkv_kernel, n_chunks=n_chunks),
        grid=(n_seq, n_c),
        in_specs=[act] * 6 + [st,
                              pl.BlockSpec(tri.shape, lambda b, c: (0, 0)),
                              pl.BlockSpec(m_side.shape, lambda b, c: (0, 0, 0)),
                              pl.BlockSpec(m_square.shape, lambda b, c: (0, 0, 0))],
        out_specs=[act, st],
        out_shape=[jax.ShapeDtypeStruct((total, D_RWKV), F32),
                   jax.ShapeDtypeStruct((n_seq, N_HEADS, HEAD, HEAD), F32)],
        scratch_shapes=[pltpu.VMEM((N_PAIRS, LANES, LANES), F32)],
        compiler_params=pltpu.CompilerParams(
            dimension_semantics=("arbitrary", "arbitrary"), vmem_limit_bytes=VMEM_LIMIT),
        name="wkv",
    )(*seqs, s0, tri, m_side, m_square)


def _out_kernel(y_ref, bon_ref, sg_ref, op_ref, x_ref, gw_ref, gb_ref, wo_ref, nf_ref, g_ref, o_ref):
    g_ones = g_ref[...]
    y = y_ref[...]
    mu = _seg_sum(y, g_ones) * (1.0 / HEAD)
    d = y - mu
    var = _seg_sum(d * d, g_ones) * (1.0 / HEAD)
    yn = d * lax.rsqrt(var + GN_EPS) * gw_ref[...] + gb_ref[...]
    o_rwkv = (yn + bon_ref[...]) * sg_ref[...]
    out = (jnp.dot(o_rwkv.astype(BF16), wo_ref[0:D_RWKV, :], preferred_element_type=F32)
           + jnp.dot(op_ref[...].astype(BF16), wo_ref[D_RWKV:, :], preferred_element_type=F32))
    res = x_ref[...] + out
    o_ref[...] = res * lax.rsqrt(jnp.mean(res * res, axis=-1, keepdims=True) + NORM_EPS) * nf_ref[...]


def _out_call(y, bon, sg, op, x, gn_w, gn_b, w_out, norm_f, g_ones, *, rows):
    total = x.shape[0]
    row_blk = lambda i: (i, 0)
    const2 = lambda i: (0, 0)
    act = lambda w: pl.BlockSpec((rows, w), row_blk)
    return pl.pallas_call(
        _out_kernel,
        grid=(total // rows,),
        in_specs=[act(D_RWKV)] * 4 + [act(D_MODEL),
                                      pl.BlockSpec((1, D_RWKV), const2),
                                      pl.BlockSpec((1, D_RWKV), const2),
                                      pl.BlockSpec((D_RWKV + D_POOL, D_MODEL), const2),
                                      pl.BlockSpec((1, D_MODEL), const2),
                                      pl.BlockSpec((LANES, LANES), const2)],
        out_specs=act(D_MODEL),
        out_shape=jax.ShapeDtypeStruct((total, D_MODEL), F32),
        compiler_params=pltpu.CompilerParams(
            dimension_semantics=("arbitrary",), vmem_limit_bytes=VMEM_LIMIT),
        name="out",
    )(y, bon, sg, op, x, gn_w, gn_b, w_out, norm_f, g_ones)


def _head_ones():
    i = lax.broadcasted_iota(jnp.int32, (LANES, LANES), 0) // HEAD
    j = lax.broadcasted_iota(jnp.int32, (LANES, LANES), 1) // HEAD
    return (i == j).astype(BF16)


def kernel(x_prompt, x_sample, state_shift, state_wkv, state_pool, norm_w, w_in, mu_shift, w_decay_b,
           w0, w_aaa_b, a0, k_k, k_a, r_k, gn_w, gn_b, pool_w, pool_scale, w_out, norm_f):
    depth = norm_w.shape[0]
    n_p, t_p, _ = x_prompt.shape
    n_s, t_s, _ = x_sample.shape
    assert depth == 1, "a stacked trunk needs a residual-only output kernel between layers"
    assert t_p % PROMPT_TILE == 0 and t_p % (WKV_CHUNKS_PER_STEP * CHUNK) == 0 and t_s <= CHUNK
    g_ones = _head_ones()
    row = lambda z: z.reshape(1, -1).astype(F32)

    hp = x_prompt.astype(F32).reshape(n_p * t_p, D_MODEL)
    hs = jnp.transpose(x_sample.astype(F32), (1, 0, 2)).reshape(t_s * n_s, D_MODEL)
    outs = [[] for _ in range(6)]
    for l in range(depth):
        zl = jnp.zeros((D_LORA, D_RWKV), F32)
        w_lora = jnp.concatenate([jnp.concatenate([w_decay_b[l], zl], axis=1),
                                  jnp.concatenate([zl, w_aaa_b[l]], axis=1)], axis=0).astype(BF16)
        prep_params = (row(norm_w[l]), w_in[l].astype(BF16), row(mu_shift[l]), w_lora, row(w0[l]),
                       row(a0[l]), row(k_k[l]), row(k_a[l]), row(r_k[l]), pool_w[l].astype(BF16),
                       row(pool_scale[l]), g_ones)
        out_params = (row(gn_w[l]), row(gn_b[l]), w_out[l].astype(BF16), row(norm_f), g_ones)

        res = _prep_call(hp, jnp.zeros((n_p, SUBLANES, D_SHIFT), F32),
                         jnp.zeros((n_p, POOL_HIST, D_POOL), F32), prep_params,
                         n_seq=n_p, rows=PROMPT_TILE, stride=1, t0=0)
        seqs, (bon, sg, op), nsh, npl = res[:6], res[6:9], res[9], res[10]
        y, s_fin = _wkv_call(seqs, jnp.zeros((n_p, N_HEADS, HEAD, HEAD), F32), n_seq=n_p,
                             n_chunks=WKV_CHUNKS_PER_STEP)
        hp = _out_call(y, bon, sg, op, hp, *out_params, rows=PROMPT_TILE)
        outs[0].append(nsh[:, SUBLANES - 1])
        outs[1].append(s_fin)
        outs[2].append(npl[:, 1:])

        sh_hist = state_shift[l].astype(F32)[None]
        pool_hist = jnp.transpose(state_pool[l].astype(F32), (1, 0, 2))
        pool_hist = jnp.pad(pool_hist, ((1, 0), (0, 0), (0, 0))).reshape(1, POOL_HIST * n_s, D_POOL)
        res = _prep_call(hs, sh_hist, pool_hist, prep_params,
                         n_seq=1, rows=t_s * n_s, stride=n_s, t0=PAST_LEN)
        seqs, (bon, sg, op), nsh, npl = res[:6], res[6:9], res[9], res[10]

        def to_seq_major(z):
            z = jnp.transpose(z.reshape(t_s, n_s, D_RWKV), (1, 0, 2))
            return jnp.pad(z, ((0, 0), (0, CHUNK - t_s), (0, 0))).reshape(n_s * CHUNK, D_RWKV)

        y, s_fin = _wkv_call([to_seq_major(z) for z in seqs], state_wkv[l].astype(F32), n_seq=n_s,
                             n_chunks=1)
        y = jnp.transpose(y.reshape(n_s, CHUNK, D_RWKV)[:, :t_s], (1, 0, 2)).reshape(t_s * n_s, D_RWKV)
        hs = _out_call(y, bon, sg, op, hs, *out_params, rows=t_s * n_s)
        outs[3].append(nsh[0])
        outs[4].append(s_fin)
        outs[5].append(jnp.transpose(npl[0, n_s:].reshape(POOL_HIST - 1, n_s, D_POOL), (1, 0, 2)))

    y_prompt = hp.reshape(n_p, t_p, D_MODEL).astype(x_prompt.dtype)
    y_sample = jnp.transpose(hs.reshape(t_s, n_s, D_MODEL), (1, 0, 2)).astype(x_sample.dtype)
    return (y_prompt, y_sample) + tuple(jnp.stack(o, axis=0) for o in outs)
```

```python
import functools

import jax
import jax.numpy as jnp
from jax import lax
from jax.experimental import pallas as pl
from jax.experimental.pallas import tpu as pltpu

F32 = jnp.float32
BF16 = jnp.bfloat16

D_MODEL = 1024
D_RWKV = 512
D_POOL = 512
HEAD = 64
N_HEADS = D_RWKV // HEAD
LANES = 128
SUBLANES = 8
N_PAIRS = D_RWKV // LANES
D_LORA = 64
POOL_WINDOWS = (2, 4, 8, 16)
POOL_GROUP = D_POOL // len(POOL_WINDOWS)
POOL_HIST = max(POOL_WINDOWS)
D_SHIFT = 3 * D_RWKV + 2 * D_LORA
D_IN = D_SHIFT + D_RWKV + 2 * D_POOL
PAST_LEN = 16384
NORM_EPS = 1e-6
GN_EPS = 64e-5
L2_EPS = 1e-12
CHUNK = 64
PROMPT_TILE = 256
VMEM_LIMIT = 56 * 1024 * 1024
P_A = (1, 1)
P_T = (1, 1)
P_U = (1, 1)
P_S = (1, 1)
P_Y = (1, 1)
WKV_CHUNKS_PER_STEP = 8
SHORT_SEQS_PER_STEP = 8

NN = (((1,), (0,)), ((), ()))
NT = (((1,), (1,)), ((), ()))
TN = (((0,), (0,)), ((), ()))


def _pieces(x, n):
    if x.dtype == BF16:
        return [x]
    out = []
    rem = x
    for i in range(n):
        p = rem.astype(BF16)
        out.append(p)
        if i + 1 < n:
            rem = rem - p.astype(F32)
    return out


def _mm(a, b, dims=NN, na=1, nb=1):
    ap = _pieces(a, na)
    bp = _pieces(b, nb)
    order = max(len(ap), len(bp))
    acc = None
    for i, x in enumerate(ap):
        for j, y in enumerate(bp):
            if i + j >= order:
                continue
            t = lax.dot_general(x, y, dims, preferred_element_type=F32)
            acc = t if acc is None else acc + t
    return acc


def _seg_sum(x, g_ones):
    cols = []
    for s in range(x.shape[1] // LANES):
        cols.append(_mm(x[:, s * LANES:(s + 1) * LANES], g_ones, NN, 2, 1))
    return cols[0] if len(cols) == 1 else jnp.concatenate(cols, axis=1)


def _sigmoid(x):
    return 1.0 / (1.0 + jnp.exp(-x))


def _prep_kernel(x_ref, sh_ref, ph_ref, nw_ref, win_ref, mu_ref, wl_ref, w0_ref, a0_ref, kk_ref,
                 ka_ref, rk_ref, pw_ref, ps_ref, g_ref,
                 r_o, k_o, v_o, lw_o, a_o, b_o, bon_o, sg_o, op_o, nsh_o, npl_o,
                 p_ext, u_ext, *, stride, t0, hs):
    j = pl.program_id(1)
    rows = x_ref.shape[0]
    ph = POOL_HIST * stride

    @pl.when(j == 0)
    def _():
        p_ext[0:hs, :] = sh_ref[0]
        u_ext[0:ph, :] = ph_ref[0]

    x = x_ref[...]
    h = x * lax.rsqrt(jnp.mean(x * x, axis=-1, keepdims=True) + NORM_EPS) * nw_ref[...]
    proj = jnp.dot(h.astype(BF16), win_ref[...], preferred_element_type=F32)

    p = proj[:, :D_SHIFT]
    p_ext[hs:hs + rows, :] = p
    prev = p_ext[pl.ds(hs - stride, rows), :]
    ps = p + mu_ref[...] * (prev - p)
    r = ps[:, 0:D_RWKV]
    k = ps[:, D_RWKV:2 * D_RWKV]
    v = ps[:, 2 * D_RWKV:3 * D_RWKV]
    xwa = ps[:, 3 * D_RWKV:D_SHIFT]
    lane = lax.broadcasted_iota(jnp.int32, xwa.shape, 1)
    lora_in = jnp.where(lane < D_LORA, jnp.tanh(xwa), xwa)
    lora = jnp.dot(lora_in.astype(BF16), wl_ref[...], preferred_element_type=F32)
    lw = (-0.6065306597126334) * _sigmoid(w0_ref[...] + lora[:, :D_RWKV])
    alpha = _sigmoid(a0_ref[...] + lora[:, D_RWKV:])
    g_ones = g_ref[...]
    kk = k * kk_ref[...]
    kk = kk * lax.rsqrt(_seg_sum(kk * kk, g_ones) + L2_EPS)
    k2 = k * (1.0 + (alpha - 1.0) * ka_ref[...])
    r_o[...] = r
    k_o[...] = k2
    v_o[...] = v
    lw_o[...] = lw
    a_o[...] = -kk
    b_o[...] = kk * alpha
    bon_o[...] = _seg_sum(r * k2 * rk_ref[...], g_ones) * v
    g_rwkv = proj[:, D_SHIFT:D_SHIFT + D_RWKV]
    sg_o[...] = g_rwkv * _sigmoid(g_rwkv)

    u = proj[:, D_SHIFT + D_RWKV:D_SHIFT + D_RWKV + D_POOL]
    g_pool = proj[:, D_SHIFT + D_RWKV + D_POOL:]
    u_ext[ph:ph + rows, :] = u
    row = lax.broadcasted_iota(jnp.int32, (rows, 1), 0)
    pos = t0 + (j * rows + row) // stride
    for gi, win in enumerate(POOL_WINDOWS):
        sl = slice(gi * POOL_GROUP, (gi + 1) * POOL_GROUP)
        tot = u[:, sl]
        for back in range(1, win):
            tot = tot + u_ext[pl.ds(ph - back * stride, rows), sl]
        inv_cnt = 1.0 / jnp.minimum(pos + 1, win).astype(F32)
        d = tot * inv_cnt - u[:, sl]
        o = jnp.dot(d.astype(BF16), pw_ref[gi], preferred_element_type=F32)
        gp = g_pool[:, sl]
        op_o[:, sl] = o * ps_ref[:, sl] * (gp * _sigmoid(gp))

    last_p = p_ext[rows:rows + hs, :]
    last_u = u_ext[rows:rows + ph, :]
    p_ext[0:hs, :] = last_p
    u_ext[0:ph, :] = last_u
    nsh_o[0] = last_p
    npl_o[0] = last_u


def _prep_call(x, sh_hist, pool_hist, params, *, n_seq, rows, stride, t0):
    total = x.shape[0]
    tiles = total // (n_seq * rows)
    hs = sh_hist.shape[1]
    ph = POOL_HIST * stride
    row_blk = lambda b, j: (b * tiles + j, 0)
    const2 = lambda b, j: (0, 0)
    seq3 = lambda b, j: (b, 0, 0)
    act = lambda w: pl.BlockSpec((rows, w), row_blk)
    in_specs = [
        act(D_MODEL),
        pl.BlockSpec((1, hs, D_SHIFT), seq3),
        pl.BlockSpec((1, ph, D_POOL), seq3),
        pl.BlockSpec((1, D_MODEL), const2),
        pl.BlockSpec((D_MODEL, D_IN), const2),
        pl.BlockSpec((1, D_SHIFT), const2),
        pl.BlockSpec((2 * D_LORA, 2 * D_RWKV), const2),
        pl.BlockSpec((1, D_RWKV), const2),
        pl.BlockSpec((1, D_RWKV), const2),
        pl.BlockSpec((1, D_RWKV), const2),
        pl.BlockSpec((1, D_RWKV), const2),
        pl.BlockSpec((1, D_RWKV), const2),
        pl.BlockSpec((len(POOL_WINDOWS), POOL_GROUP, POOL_GROUP), lambda b, j: (0, 0, 0)),
        pl.BlockSpec((1, D_POOL), const2),
        pl.BlockSpec((LANES, LANES), const2),
    ]
    out_shape = [jax.ShapeDtypeStruct((total, D_RWKV), F32)] * 9 + [
        jax.ShapeDtypeStruct((n_seq, hs, D_SHIFT), F32),
        jax.ShapeDtypeStruct((n_seq, ph, D_POOL), F32),
    ]
    out_specs = [act(D_RWKV)] * 9 + [
        pl.BlockSpec((1, hs, D_SHIFT), seq3),
        pl.BlockSpec((1, ph, D_POOL), seq3),
    ]
    return pl.pallas_call(
        functools.partial(_prep_kernel, stride=stride, t0=t0, hs=hs),
        grid=(n_seq, tiles),
        in_specs=in_specs,
        out_specs=out_specs,
        out_shape=out_shape,
        scratch_shapes=[pltpu.VMEM((hs + rows, D_SHIFT), F32), pltpu.VMEM((ph + rows, D_POOL), F32)],
        compiler_params=pltpu.CompilerParams(
            dimension_semantics=("arbitrary", "arbitrary"), vmem_limit_bytes=VMEM_LIMIT),
        name="prep",
    )(x, sh_hist, pool_hist, *params)


def _stack(x, m0):
    return jnp.concatenate([jnp.where(m0, x, 0.0), jnp.where(m0, 0.0, x)], axis=0)


def _wkv_kernel(r_ref, k_ref, v_ref, lw_ref, a_ref, b_ref, s0_ref, tri_ref, mss_ref, msq_ref,
                y_ref, so_ref, z_ref, *, n_chunks):
    c = pl.program_id(1)
    n_c = pl.num_programs(1)
    zeros = jnp.zeros((HEAD, HEAD), F32)

    @pl.when(c == 0)
    def _():
        for p in range(N_PAIRS):
            top = jnp.concatenate([s0_ref[0, 2 * p], zeros], axis=1)
            bot = jnp.concatenate([zeros, s0_ref[0, 2 * p + 1]], axis=1)
            z_ref[p] = jnp.concatenate([top, bot], axis=0).T

    lane = lax.broadcasted_iota(jnp.int32, (CHUNK, LANES), 1)
    m0 = lane < HEAD
    strict = mss_ref[0] > 0.5
    incl2 = jnp.concatenate([mss_ref[1], mss_ref[1]], axis=1) > 0.5
    eye_ss = mss_ref[2]
    n_levels = mss_ref.shape[0] - 3
    eye = msq_ref[0]
    same_head = msq_ref[1] > 0.5
    zblk = jnp.zeros((CHUNK, LANES), F32)
    zblk2 = jnp.zeros((2 * CHUNK, LANES), F32)

    scaled = []
    for ci in range(n_chunks):
        rows = slice(ci * CHUNK, (ci + 1) * CHUNK)
        lw = lw_ref[rows, :]
        cs = _mm(tri_ref[...], lw, NN, 1, 3)
        cs_end = cs[CHUNK - 1:CHUNK, :]
        g_inv = jnp.exp(-cs)
        g_tail = jnp.exp(cs_end - cs)
        scaled.append(dict(
            g_end=jnp.exp(cs_end),
            a_t=a_ref[rows, :] * jnp.exp(cs - lw), r_t=r_ref[rows, :] * jnp.exp(cs),
            b_t=b_ref[rows, :] * g_inv, k_t=k_ref[rows, :] * g_inv,
            b_h=b_ref[rows, :] * g_tail, k_h=k_ref[rows, :] * g_tail, v=v_ref[rows, :]))
    chains = [(ci, p) for ci in range(n_chunks) for p in range(N_PAIRS)]
    lanes = lambda p: slice(p * LANES, (p + 1) * LANES)
    pair = lambda name: [scaled[ci][name][:, lanes(p)] for ci, p in chains]
    each = lambda fn, *lists: [fn(*args) for args in zip(*lists)]
    cat0 = lambda *xs: jnp.concatenate(xs, axis=0)
    cat1 = lambda *xs: jnp.concatenate(xs, axis=1)
    at, rt, bt, kt, bh, kh, v = (pair(n) for n in ("a_t", "r_t", "b_t", "k_t", "b_h", "k_h", "v"))
    v2 = each(lambda x: _stack(x, m0), v)
    scores = each(lambda a, r, b, k: _mm(cat0(a, r), cat0(_stack(b, m0), _stack(k, m0)), NT, *P_A),
                  at, rt, bt, kt)
    a_ab = each(lambda s: jnp.where(strict, s[:CHUNK, :LANES], 0.0), scores)
    a_ak = each(lambda s: jnp.where(strict, s[:CHUNK, LANES:], 0.0), scores)
    a_r = each(lambda s: jnp.where(incl2, s[CHUNK:, :], 0.0), scores)
    t_inv = each(lambda x: eye_ss + x * mss_ref[3], a_ab)
    for lvl in range(1, n_levels):
        inner = each(lambda x, t: _mm(x * mss_ref[3 + lvl], _stack(t, m0), NN, *P_T), a_ab, t_inv)
        t_inv = each(lambda t, w: t + _mm(t, _stack(w, m0), NN, *P_T), t_inv, inner)
    akv = each(lambda x, y: _mm(x, y, NN, *P_U), a_ak, v2)
    pq = each(lambda t, x, y: _mm(t, cat1(_stack(x, m0), _stack(y, m0)), NN, *P_U), t_inv, at, akv)
    bpq = each(lambda b, k, x, y: _mm(cat0(b, k), cat0(x, cat1(zblk, y)), TN, *P_S), bh, kh, pq, v)
    g_end = [scaled[ci]["g_end"][:, lanes(p)] for ci, p in chains]
    m_t = each(lambda g, x: eye * g + jnp.where(same_head, x[:, :LANES], 0.0), g_end, bpq)
    n_t = each(lambda x: jnp.where(same_head, x[:, LANES:], 0.0), bpq)
    yy = each(lambda x, y, z: _mm(x, cat0(cat1(_stack(y[:, :LANES], m0), _stack(y[:, LANES:], m0)),
                                         cat1(zblk2, z)), NN, *P_Y), a_r, pq, v2)
    y1 = each(lambda r, x: r + x[:, :LANES], rt, yy)
    y2 = each(lambda x: x[:, LANES:], yy)

    state = [z_ref[p] for p in range(N_PAIRS)]
    for ci in range(n_chunks):
        for p in range(N_PAIRS):
            i = ci * N_PAIRS + p
            both = _mm(cat0(y1[i], m_t[i]), state[p], NN, *P_S)
            y_ref[ci * CHUNK:(ci + 1) * CHUNK, lanes(p)] = both[:CHUNK] + y2[i]
            state[p] = both[CHUNK:] + n_t[i]
    for p in range(N_PAIRS):
        z_ref[p] = state[p]

    @pl.when(c == n_c - 1)
    def _():
        for p in range(N_PAIRS):
            s2 = z_ref[p].T
            so_ref[0, 2 * p] = s2[:HEAD, :HEAD]
            so_ref[0, 2 * p + 1] = s2[HEAD:, HEAD:]


def _wkv_masks():
    ri = lax.broadcasted_iota(jnp.int32, (CHUNK, 2 * CHUNK), 0)
    ci = lax.broadcasted_iota(jnp.int32, (CHUNK, 2 * CHUNK), 1) % CHUNK
    side = [ci < ri, ci <= ri, ri == ci]
    m = 1
    while m < CHUNK:
        side.append(((ri // (2 * m)) == (ci // (2 * m))) & ((ri // m) % 2 == 1) & ((ci // m) % 2 == 0))
        m *= 2
    rq = lax.broadcasted_iota(jnp.int32, (LANES, LANES), 0)
    cq = lax.broadcasted_iota(jnp.int32, (LANES, LANES), 1)
    square = [rq == cq, (rq // HEAD) == (cq // HEAD)]
    tri = (lax.broadcasted_iota(jnp.int32, (CHUNK, CHUNK), 1)
           <= lax.broadcasted_iota(jnp.int32, (CHUNK, CHUNK), 0))
    return tri.astype(BF16), jnp.stack(side).astype(F32), jnp.stack(square).astype(F32)


def _wkv_call(seqs, s0, *, n_seq, n_chunks):
    total = seqs[0].shape[0]
    n_c = total // (n_seq * n_chunks * CHUNK)
    assert CHUNK == HEAD and 2 * CHUNK == LANES
    tri, m_side, m_square = _wkv_masks()
    act = pl.BlockSpec((n_chunks * CHUNK, D_RWKV), lambda b, c: (b * n_c + c, 0))
    st = pl.BlockSpec((1, N_HEADS, HEAD, HEAD), lambda b, c: (b, 0, 0, 0))
    return pl.pallas_call(
        functools.partial(_wkv_kernel, n_chunks=n_chunks),
        grid=(n_seq, n_c),
        in_specs=[act] * 6 + [st,
                              pl.BlockSpec(tri.shape, lambda b, c: (0, 0)),
                              pl.BlockSpec(m_side.shape, lambda b, c: (0, 0, 0)),
                              pl.BlockSpec(m_square.shape, lambda b, c: (0, 0, 0))],
        out_specs=[act, st],
        out_shape=[jax.ShapeDtypeStruct((total, D_RWKV), F32),
                   jax.ShapeDtypeStruct((n_seq, N_HEADS, HEAD, HEAD), F32)],
        scratch_shapes=[pltpu.VMEM((N_PAIRS, LANES, LANES), F32)],
        compiler_params=pltpu.CompilerParams(
            dimension_semantics=("arbitrary", "arbitrary"), vmem_limit_bytes=VMEM_LIMIT),
        name="wkv",
    )(*seqs, s0, tri, m_side, m_square)


def _short_pairs(t_s):
    idx = {}
    for kind, inclusive in (("ab", False), ("ak", False), ("rb", True), ("rk", True)):
        for t in range(t_s):
            for s in range(t + 1 if inclusive else t):
                idx[(kind, t, s)] = len(idx)
    return idx


def _short_scale_kernel(r_ref, k_ref, v_ref, lw_ref, a_ref, b_ref, g_ref, x_o, bk_o, as_o, g4_o, *, t_s, n_s):
    g_ones = g_ref[...]
    slab = lambda ref, t: ref[t * n_s:(t + 1) * n_s, :]
    cs = []
    for t in range(t_s):
        cs.append(slab(lw_ref, t) if t == 0 else cs[-1] + slab(lw_ref, t))
    a_t = [slab(a_ref, t) * (jnp.exp(cs[t - 1]) if t else 1.0) for t in range(t_s)]
    r_t = [slab(r_ref, t) * jnp.exp(cs[t]) for t in range(t_s)]
    g_inv = [jnp.exp(-cs[t]) for t in range(t_s)]
    b_t = [slab(b_ref, t) * g_inv[t] for t in range(t_s)]
    k_t = [slab(k_ref, t) * g_inv[t] for t in range(t_s)]
    for t in range(t_s):
        g_tail = jnp.exp(cs[t_s - 1] - cs[t])
        x_o[t] = a_t[t]
        x_o[t_s + t] = r_t[t]
        bk_o[t] = slab(b_ref, t) * g_tail
        bk_o[t_s + t] = slab(k_ref, t) * g_tail
    g4_o[...] = jnp.exp(cs[t_s - 1])
    left = {"ab": a_t, "ak": a_t, "rb": r_t, "rk": r_t}
    right = {"ab": b_t, "ak": k_t, "rb": b_t, "rk": k_t}
    for (kind, t, s), row in _short_pairs(t_s).items():
        as_o[row] = _seg_sum(left[kind][t] * right[kind][s], g_ones)


def _head_rows(x):
    return jnp.concatenate([x[:, h * HEAD:(h + 1) * HEAD] for h in range(N_HEADS)], axis=0)


def _short_state_kernel(s_ref, x_ref, bk_ref, as_ref, v_ref, g4_ref, so_ref, y_ref, *, t_s, n_blk):
    idx = _short_pairs(t_s)
    rows8 = 2 * t_s
    lane_head = lax.broadcasted_iota(jnp.int32, (rows8, D_RWKV), 1) // HEAD
    row8 = lax.broadcasted_iota(jnp.int32, (rows8, D_RWKV), 0)
    for i in range(n_blk):
        s_b = s_ref[i]
        x = x_ref[:, i, :]
        zt = _mm(_head_rows(x), s_b, NT, *P_S)
        zz = jnp.zeros((rows8, D_RWKV), F32)
        for h in range(N_HEADS):
            zz = jnp.where(lane_head == h, zt[h * rows8:(h + 1) * rows8, :], zz)
        sc = as_ref[:, i, :]
        scal = lambda kind, t, s: sc[idx[(kind, t, s)]:idx[(kind, t, s)] + 1, :]
        v = [v_ref[t, i:i + 1, :] for t in range(t_s)]
        u = []
        for t in range(t_s):
            acc = zz[t:t + 1, :]
            for s in range(t):
                acc = acc + scal("ab", t, s) * u[s] + scal("ak", t, s) * v[s]
            u.append(acc)
        for t in range(t_s):
            acc = zz[t_s + t:t_s + t + 1, :]
            for s in range(t + 1):
                acc = acc + scal("rb", t, s) * u[s] + scal("rk", t, s) * v[s]
            y_ref[t, i:i + 1, :] = acc
        uv = jnp.zeros((rows8, D_RWKV), F32)
        for j, vec in enumerate(u + v):
            uv = jnp.where(row8 == j, jnp.broadcast_to(vec, (rows8, D_RWKV)), uv)
        uv_exp = jnp.concatenate([jnp.where(lane_head == h, uv, 0.0) for h in range(N_HEADS)], axis=0)
        upd = _mm(uv_exp, _head_rows(bk_ref[:, i, :]), TN, *P_S)
        g4 = g4_ref[i:i + 1, :]
        g_rows = jnp.concatenate(
            [jnp.broadcast_to(g4[:, h * HEAD:(h + 1) * HEAD], (HEAD, HEAD)) for h in range(N_HEADS)], axis=0)
        so_ref[i] = s_b * g_rows + upd


def _short_wkv_call(seqs, s0, g_ones, *, t_s, n_s, n_blk):
    assert 2 * t_s == SUBLANES and n_s % n_blk == 0 and n_blk % SUBLANES == 0
    n_scal = len(_short_pairs(t_s))
    full = lambda shape: pl.BlockSpec(shape, lambda: tuple(0 for _ in shape))
    x8, bk8, scal, g4 = pl.pallas_call(
        functools.partial(_short_scale_kernel, t_s=t_s, n_s=n_s),
        in_specs=[full((t_s * n_s, D_RWKV))] * 6 + [full((LANES, LANES))],
        out_specs=[full((2 * t_s, n_s, D_RWKV)), full((2 * t_s, n_s, D_RWKV)),
                   full((n_scal, n_s, D_RWKV)), full((n_s, D_RWKV))],
        out_shape=[jax.ShapeDtypeStruct((2 * t_s, n_s, D_RWKV), F32),
                   jax.ShapeDtypeStruct((2 * t_s, n_s, D_RWKV), F32),
                   jax.ShapeDtypeStruct((n_scal, n_s, D_RWKV), F32),
                   jax.ShapeDtypeStruct((n_s, D_RWKV), F32)],
        compiler_params=pltpu.CompilerParams(vmem_limit_bytes=VMEM_LIMIT),
        name="wkv_short_scale",
    )(*seqs, g_ones)
    v3 = seqs[2].reshape(t_s, n_s, D_RWKV)
    st = pl.BlockSpec((n_blk, N_HEADS * HEAD, HEAD), lambda i: (i, 0, 0))
    tm = lambda n: pl.BlockSpec((n, n_blk, D_RWKV), lambda i: (0, i, 0))
    s_new, y = pl.pallas_call(
        functools.partial(_short_state_kernel, t_s=t_s, n_blk=n_blk),
        grid=(n_s // n_blk,),
        in_specs=[st, tm(2 * t_s), tm(2 * t_s), tm(n_scal), tm(t_s),
                  pl.BlockSpec((n_blk, D_RWKV), lambda i: (i, 0))],
        out_specs=[st, tm(t_s)],
        out_shape=[jax.ShapeDtypeStruct((n_s, N_HEADS * HEAD, HEAD), F32),
                   jax.ShapeDtypeStruct((t_s, n_s, D_RWKV), F32)],
        compiler_params=pltpu.CompilerParams(
            dimension_semantics=("arbitrary",), vmem_limit_bytes=VMEM_LIMIT),
        name="wkv_short_state",
    )(s0.reshape(n_s, N_HEADS * HEAD, HEAD), x8, bk8, scal, v3, g4)
    return y.reshape(t_s * n_s, D_RWKV), s_new.reshape(n_s, N_HEADS, HEAD, HEAD)


def _out_kernel(y_ref, bon_ref, sg_ref, op_ref, x_ref, gw_ref, gb_ref, wo_ref, nf_ref, g_ref, o_ref):
    g_ones = g_ref[...]
    y = y_ref[...]
    mu = _seg_sum(y, g_ones) * (1.0 / HEAD)
    d = y - mu
    var = _seg_sum(d * d, g_ones) * (1.0 / HEAD)
    yn = d * lax.rsqrt(var + GN_EPS) * gw_ref[...] + gb_ref[...]
    o_rwkv = (yn + bon_ref[...]) * sg_ref[...]
    out = (jnp.dot(o_rwkv.astype(BF16), wo_ref[0:D_RWKV, :], preferred_element_type=F32)
           + jnp.dot(op_ref[...].astype(BF16), wo_ref[D_RWKV:, :], preferred_element_type=F32))
    res = x_ref[...] + out
    o_ref[...] = res * lax.rsqrt(jnp.mean(res * res, axis=-1, keepdims=True) + NORM_EPS) * nf_ref[...]


def _out_call(y, bon, sg, op, x, gn_w, gn_b, w_out, norm_f, g_ones, *, rows):
    total = x.shape[0]
    row_blk = lambda i: (i, 0)
    const2 = lambda i: (0, 0)
    act = lambda w: pl.BlockSpec((rows, w), row_blk)
    return pl.pallas_call(
        _out_kernel,
        grid=(total // rows,),
        in_specs=[act(D_RWKV)] * 4 + [act(D_MODEL),
                                      pl.BlockSpec((1, D_RWKV), const2),
                                      pl.BlockSpec((1, D_RWKV), const2),
                                      pl.BlockSpec((D_RWKV + D_POOL, D_MODEL), const2),
                                      pl.BlockSpec((1, D_MODEL), const2),
                                      pl.BlockSpec((LANES, LANES), const2)],
        out_specs=act(D_MODEL),
        out_shape=jax.ShapeDtypeStruct((total, D_MODEL), F32),
        compiler_params=pltpu.CompilerParams(
            dimension_semantics=("arbitrary",), vmem_limit_bytes=VMEM_LIMIT),
        name="out",
    )(y, bon, sg, op, x, gn_w, gn_b, w_out, norm_f, g_ones)


def _head_ones():
    i = lax.broadcasted_iota(jnp.int32, (LANES, LANES), 0) // HEAD
    j = lax.broadcasted_iota(jnp.int32, (LANES, LANES), 1) // HEAD
    return (i == j).astype(BF16)


def kernel(x_prompt, x_sample, state_shift, state_wkv, state_pool, norm_w, w_in, mu_shift, w_decay_b,
           w0, w_aaa_b, a0, k_k, k_a, r_k, gn_w, gn_b, pool_w, pool_scale, w_out, norm_f):
    depth = norm_w.shape[0]
    n_p, t_p, _ = x_prompt.shape
    n_s, t_s, _ = x_sample.shape
    assert depth == 1, "a stacked trunk needs a residual-only output kernel between layers"
    assert t_p % PROMPT_TILE == 0 and t_p % (WKV_CHUNKS_PER_STEP * CHUNK) == 0 and t_s <= CHUNK
    g_ones = _head_ones()
    row = lambda z: z.reshape(1, -1).astype(F32)

    hp = x_prompt.astype(F32).reshape(n_p * t_p, D_MODEL)
    hs = jnp.transpose(x_sample.astype(F32), (1, 0, 2)).reshape(t_s * n_s, D_MODEL)
    outs = [[] for _ in range(6)]
    for l in range(depth):
        zl = jnp.zeros((D_LORA, D_RWKV), F32)
        w_lora = jnp.concatenate([jnp.concatenate([w_decay_b[l], zl], axis=1),
                                  jnp.concatenate([zl, w_aaa_b[l]], axis=1)], axis=0).astype(BF16)
        prep_params = (row(norm_w[l]), w_in[l].astype(BF16), row(mu_shift[l]), w_lora, row(w0[l]),
                       row(a0[l]), row(k_k[l]), row(k_a[l]), row(r_k[l]), pool_w[l].astype(BF16),
                       row(pool_scale[l]), g_ones)
        out_params = (row(gn_w[l]), row(gn_b[l]), w_out[l].astype(BF16), row(norm_f), g_ones)

        res = _prep_call(hp, jnp.zeros((n_p, SUBLANES, D_SHIFT), F32),
                         jnp.zeros((n_p, POOL_HIST, D_POOL), F32), prep_params,
                         n_seq=n_p, rows=PROMPT_TILE, stride=1, t0=0)
        seqs, (bon, sg, op), nsh, npl = res[:6], res[6:9], res[9], res[10]
        y, s_fin = _wkv_call(seqs, jnp.zeros((n_p, N_HEADS, HEAD, HEAD), F32), n_seq=n_p,
                             n_chunks=WKV_CHUNKS_PER_STEP)
        hp = _out_call(y, bon, sg, op, hp, *out_params, rows=PROMPT_TILE)
        outs[0].append(nsh[:, SUBLANES - 1])
        outs[1].append(s_fin)
        outs[2].append(npl[:, 1:])

        sh_hist = state_shift[l].astype(F32)[None]
        pool_hist = jnp.transpose(state_pool[l].astype(F32), (1, 0, 2))
        pool_hist = jnp.pad(pool_hist, ((1, 0), (0, 0), (0, 0))).reshape(1, POOL_HIST * n_s, D_POOL)
        res = _prep_call(hs, sh_hist, pool_hist, prep_params,
                         n_seq=1, rows=t_s * n_s, stride=n_s, t0=PAST_LEN)
        seqs, (bon, sg, op), nsh, npl = res[:6], res[6:9], res[9], res[10]

        y, s_fin = _short_wkv_call(seqs, state_wkv[l].astype(F32), g_ones, t_s=t_s, n_s=n_s,
                                   n_blk=SHORT_SEQS_PER_STEP)
        hs = _out_call(y, bon, sg, op, hs, *out_params, rows=t_s * n_s)
        outs[3].append(nsh[0])
        outs[4].append(s_fin)
        outs[5].append(jnp.transpose(npl[0, n_s:].reshape(POOL_HIST - 1, n_s, D_POOL), (1, 0, 2)))

    y_prompt = hp.reshape(n_p, t_p, D_MODEL).astype(x_prompt.dtype)
    y_sample = jnp.transpose(hs.reshape(t_s, n_s, D_MODEL), (1, 0, 2)).astype(x_sample.dtype)
    return (y_prompt, y_sample) + tuple(jnp.stack(o, axis=0) for o in outs)
```

```python
import functools

import jax
import jax.numpy as jnp
from jax import lax
from jax.experimental import pallas as pl
from jax.experimental.pallas import tpu as pltpu

F32 = jnp.float32
BF16 = jnp.bfloat16

D_MODEL = 1024
D_RWKV = 512
D_POOL = 512
HEAD = 64
N_HEADS = D_RWKV // HEAD
LANES = 128
SUBLANES = 8
N_PAIRS = D_RWKV // LANES
D_LORA = 64
POOL_WINDOWS = (2, 4, 8, 16)
POOL_GROUP = D_POOL // len(POOL_WINDOWS)
POOL_HIST = max(POOL_WINDOWS)
D_SHIFT = 3 * D_RWKV + 2 * D_LORA
D_IN = D_SHIFT + D_RWKV + 2 * D_POOL
PAST_LEN = 16384
NORM_EPS = 1e-6
GN_EPS = 64e-5
L2_EPS = 1e-12
CHUNK = 64
VMEM_LIMIT = 56 * 1024 * 1024
P_A = (1, 1)
P_T = (1, 1)
P_U = (1, 1)
P_S = (1, 1)
P_Y = (1, 1)
WKV_CHUNKS_PER_STEP = 8
SHORT_SEQS_PER_STEP = 8

NN = (((1,), (0,)), ((), ()))
NT = (((1,), (1,)), ((), ()))
TN = (((0,), (0,)), ((), ()))


def _pieces(x, n):
    if x.dtype == BF16:
        return [x]
    out = []
    rem = x
    for i in range(n):
        p = rem.astype(BF16)
        out.append(p)
        if i + 1 < n:
            rem = rem - p.astype(F32)
    return out


def _mm(a, b, dims=NN, na=1, nb=1):
    ap = _pieces(a, na)
    bp = _pieces(b, nb)
    order = max(len(ap), len(bp))
    acc = None
    for i, x in enumerate(ap):
        for j, y in enumerate(bp):
            if i + j >= order:
                continue
            t = lax.dot_general(x, y, dims, preferred_element_type=F32)
            acc = t if acc is None else acc + t
    return acc


def _seg_sum(x, g_ones):
    cols = []
    for s in range(x.shape[1] // LANES):
        cols.append(_mm(x[:, s * LANES:(s + 1) * LANES], g_ones, NN, 2, 1))
    return cols[0] if len(cols) == 1 else jnp.concatenate(cols, axis=1)


def _sigmoid(x):
    return 1.0 / (1.0 + jnp.exp(-x))


def _prep_kernel(x_ref, sh_ref, ph_ref, nw_ref, win_ref, mu_ref, wl_ref, w0_ref, a0_ref, kk_ref,
                 ka_ref, rk_ref, pw_ref, ps_ref, g_ref,
                 r_o, k_o, v_o, lw_o, a_o, b_o, bon_o, sg_o, op_o, nsh_o, npl_o,
                 p_ext, u_ext, *, stride, t0, hs):
    j = pl.program_id(1)
    rows = x_ref.shape[0]
    ph = POOL_HIST * stride

    @pl.when(j == 0)
    def _():
        p_ext[0:hs, :] = sh_ref[0]
        u_ext[0:ph, :] = ph_ref[0]

    x = x_ref[...]
    h = x * lax.rsqrt(jnp.mean(x * x, axis=-1, keepdims=True) + NORM_EPS) * nw_ref[...]
    proj = jnp.dot(h.astype(BF16), win_ref[...], preferred_element_type=F32)

    p = proj[:, :D_SHIFT]
    p_ext[hs:hs + rows, :] = p
    prev = p_ext[pl.ds(hs - stride, rows), :]
    ps = p + mu_ref[...] * (prev - p)
    r = ps[:, 0:D_RWKV]
    k = ps[:, D_RWKV:2 * D_RWKV]
    v = ps[:, 2 * D_RWKV:3 * D_RWKV]
    xwa = ps[:, 3 * D_RWKV:D_SHIFT]
    lane = lax.broadcasted_iota(jnp.int32, xwa.shape, 1)
    lora_in = jnp.where(lane < D_LORA, jnp.tanh(xwa), xwa)
    lora = jnp.dot(lora_in.astype(BF16), wl_ref[...], preferred_element_type=F32)
    lw = (-0.6065306597126334) * _sigmoid(w0_ref[...] + lora[:, :D_RWKV])
    alpha = _sigmoid(a0_ref[...] + lora[:, D_RWKV:])
    g_ones = g_ref[...]
    kk = k * kk_ref[...]
    kk = kk * lax.rsqrt(_seg_sum(kk * kk, g_ones) + L2_EPS)
    k2 = k * (1.0 + (alpha - 1.0) * ka_ref[...])
    r_o[...] = r
    k_o[...] = k2
    v_o[...] = v
    lw_o[...] = lw
    a_o[...] = -kk
    b_o[...] = kk * alpha
    bon_o[...] = _seg_sum(r * k2 * rk_ref[...], g_ones) * v
    g_rwkv = proj[:, D_SHIFT:D_SHIFT + D_RWKV]
    sg_o[...] = g_rwkv * _sigmoid(g_rwkv)

    u = proj[:, D_SHIFT + D_RWKV:D_SHIFT + D_RWKV + D_POOL]
    g_pool = proj[:, D_SHIFT + D_RWKV + D_POOL:]
    u_ext[ph:ph + rows, :] = u
    row = lax.broadcasted_iota(jnp.int32, (rows, 1), 0)
    pos = t0 + (j * rows + row) // stride
    for gi, win in enumerate(POOL_WINDOWS):
        sl = slice(gi * POOL_GROUP, (gi + 1) * POOL_GROUP)
        tot = u[:, sl]
        for back in range(1, win):
            tot = tot + u_ext[pl.ds(ph - back * stride, rows), sl]
        inv_cnt = 1.0 / jnp.minimum(pos + 1, win).astype(F32)
        d = tot * inv_cnt - u[:, sl]
        o = jnp.dot(d.astype(BF16), pw_ref[gi], preferred_element_type=F32)
        gp = g_pool[:, sl]
        op_o[:, sl] = o * ps_ref[:, sl] * (gp * _sigmoid(gp))

    last_p = p_ext[rows:rows + hs, :]
    last_u = u_ext[rows:rows + ph, :]
    p_ext[0:hs, :] = last_p
    u_ext[0:ph, :] = last_u
    nsh_o[0] = last_p
    npl_o[0] = last_u


def _prep_call(x, sh_hist, pool_hist, params, *, n_seq, rows, stride, t0):
    total = x.shape[0]
    tiles = total // (n_seq * rows)
    hs = sh_hist.shape[1]
    ph = POOL_HIST * stride
    row_blk = lambda b, j: (b * tiles + j, 0)
    const2 = lambda b, j: (0, 0)
    seq3 = lambda b, j: (b, 0, 0)
    act = lambda w: pl.BlockSpec((rows, w), row_blk)
    in_specs = [
        act(D_MODEL),
        pl.BlockSpec((1, hs, D_SHIFT), seq3),
        pl.BlockSpec((1, ph, D_POOL), seq3),
        pl.BlockSpec((1, D_MODEL), const2),
        pl.BlockSpec((D_MODEL, D_IN), const2),
        pl.BlockSpec((1, D_SHIFT), const2),
        pl.BlockSpec((2 * D_LORA, 2 * D_RWKV), const2),
        pl.BlockSpec((1, D_RWKV), const2),
        pl.BlockSpec((1, D_RWKV), const2),
        pl.BlockSpec((1, D_RWKV), const2),
        pl.BlockSpec((1, D_RWKV), const2),
        pl.BlockSpec((1, D_RWKV), const2),
        pl.BlockSpec((len(POOL_WINDOWS), POOL_GROUP, POOL_GROUP), lambda b, j: (0, 0, 0)),
        pl.BlockSpec((1, D_POOL), const2),
        pl.BlockSpec((LANES, LANES), const2),
    ]
    out_shape = [jax.ShapeDtypeStruct((total, D_RWKV), F32)] * 9 + [
        jax.ShapeDtypeStruct((n_seq, hs, D_SHIFT), F32),
        jax.ShapeDtypeStruct((n_seq, ph, D_POOL), F32),
    ]
    out_specs = [act(D_RWKV)] * 9 + [
        pl.BlockSpec((1, hs, D_SHIFT), seq3),
        pl.BlockSpec((1, ph, D_POOL), seq3),
    ]
    return pl.pallas_call(
        functools.partial(_prep_kernel, stride=stride, t0=t0, hs=hs),
        grid=(n_seq, tiles),
        in_specs=in_specs,
        out_specs=out_specs,
        out_shape=out_shape,
        scratch_shapes=[pltpu.VMEM((hs + rows, D_SHIFT), F32), pltpu.VMEM((ph + rows, D_POOL), F32)],
        compiler_params=pltpu.CompilerParams(
            dimension_semantics=("arbitrary", "arbitrary"), vmem_limit_bytes=VMEM_LIMIT),
        name="prep",
    )(x, sh_hist, pool_hist, *params)


def _stack(x, m0):
    return jnp.concatenate([jnp.where(m0, x, 0.0), jnp.where(m0, 0.0, x)], axis=0)


def _wkv_kernel(r_ref, k_ref, v_ref, lw_ref, a_ref, b_ref, s0_ref, tri_ref, mss_ref, msq_ref,
                y_ref, so_ref, z_ref, *, n_chunks):
    c = pl.program_id(1)
    n_c = pl.num_programs(1)
    zeros = jnp.zeros((HEAD, HEAD), F32)

    @pl.when(c == 0)
    def _():
        for p in range(N_PAIRS):
            top = jnp.concatenate([s0_ref[0, 2 * p], zeros], axis=1)
            bot = jnp.concatenate([zeros, s0_ref[0, 2 * p + 1]], axis=1)
            z_ref[p] = jnp.concatenate([top, bot], axis=0).T

    lane = lax.broadcasted_iota(jnp.int32, (CHUNK, LANES), 1)
    m0 = lane < HEAD
    strict = mss_ref[0] > 0.5
    incl2 = jnp.concatenate([mss_ref[1], mss_ref[1]], axis=1) > 0.5
    eye_ss = mss_ref[2]
    n_levels = mss_ref.shape[0] - 3
    eye = msq_ref[0]
    same_head = msq_ref[1] > 0.5
    zblk = jnp.zeros((CHUNK, LANES), F32)
    zblk2 = jnp.zeros((2 * CHUNK, LANES), F32)

    scaled = []
    for ci in range(n_chunks):
        rows = slice(ci * CHUNK, (ci + 1) * CHUNK)
        lw = lw_ref[rows, :]
        cs = _mm(tri_ref[...], lw, NN, 1, 3)
        cs_end = cs[CHUNK - 1:CHUNK, :]
        g_inv = jnp.exp(-cs)
        g_tail = jnp.exp(cs_end - cs)
        scaled.append(dict(
            g_end=jnp.exp(cs_end),
            a_t=a_ref[rows, :] * jnp.exp(cs - lw), r_t=r_ref[rows, :] * jnp.exp(cs),
            b_t=b_ref[rows, :] * g_inv, k_t=k_ref[rows, :] * g_inv,
            b_h=b_ref[rows, :] * g_tail, k_h=k_ref[rows, :] * g_tail, v=v_ref[rows, :]))
    chains = [(ci, p) for ci in range(n_chunks) for p in range(N_PAIRS)]
    lanes = lambda p: slice(p * LANES, (p + 1) * LANES)
    pair = lambda name: [scaled[ci][name][:, lanes(p)] for ci, p in chains]
    each = lambda fn, *lists: [fn(*args) for args in zip(*lists)]
    cat0 = lambda *xs: jnp.concatenate(xs, axis=0)
    cat1 = lambda *xs: jnp.concatenate(xs, axis=1)
    at, rt, bt, kt, bh, kh, v = (pair(n) for n in ("a_t", "r_t", "b_t", "k_t", "b_h", "k_h", "v"))
    v2 = each(lambda x: _stack(x, m0), v)
    scores = each(lambda a, r, b, k: _mm(cat0(a, r), cat0(_stack(b, m0), _stack(k, m0)), NT, *P_A),
                  at, rt, bt, kt)
    a_ab = each(lambda s: jnp.where(strict, s[:CHUNK, :LANES], 0.0), scores)
    a_ak = each(lambda s: jnp.where(strict, s[:CHUNK, LANES:], 0.0), scores)
    a_r = each(lambda s: jnp.where(incl2, s[CHUNK:, :], 0.0), scores)
    t_inv = each(lambda x: eye_ss + x * mss_ref[3], a_ab)
    for lvl in range(1, n_levels):
        inner = each(lambda x, t: _mm(x * mss_ref[3 + lvl], _stack(t, m0), NN, *P_T), a_ab, t_inv)
        t_inv = each(lambda t, w: t + _mm(t, _stack(w, m0), NN, *P_T), t_inv, inner)
    akv = each(lambda x, y: _mm(x, y, NN, *P_U), a_ak, v2)
    pq = each(lambda t, x, y: _mm(t, cat1(_stack(x, m0), _stack(y, m0)), NN, *P_U), t_inv, at, akv)
    bpq = each(lambda b, k, x, y: _mm(cat0(b, k), cat0(x, cat1(zblk, y)), TN, *P_S), bh, kh, pq, v)
    g_end = [scaled[ci]["g_end"][:, lanes(p)] for ci, p in chains]
    m_t = each(lambda g, x: eye * g + jnp.where(same_head, x[:, :LANES], 0.0), g_end, bpq)
    n_t = each(lambda x: jnp.where(same_head, x[:, LANES:], 0.0), bpq)
    yy = each(lambda x, y, z: _mm(x, cat0(cat1(_stack(y[:, :LANES], m0), _stack(y[:, LANES:], m0)),
                                         cat1(zblk2, z)), NN, *P_Y), a_r, pq, v2)
    y1 = each(lambda r, x: r + x[:, :LANES], rt, yy)
    y2 = each(lambda x: x[:, LANES:], yy)

    state = [z_ref[p] for p in range(N_PAIRS)]
    for ci in range(n_chunks):
        for p in range(N_PAIRS):
            i = ci * N_PAIRS + p
            both = _mm(cat0(y1[i], m_t[i]), state[p], NN, *P_S)
            y_ref[ci * CHUNK:(ci + 1) * CHUNK, lanes(p)] = both[:CHUNK] + y2[i]
            state[p] = both[CHUNK:] + n_t[i]
    for p in range(N_PAIRS):
        z_ref[p] = state[p]

    @pl.when(c == n_c - 1)
    def _():
        for p in range(N_PAIRS):
            s2 = z_ref[p].T
            so_ref[0, 2 * p] = s2[:HEAD, :HEAD]
            so_ref[0, 2 * p + 1] = s2[HEAD:, HEAD:]


def _wkv_masks():
    ri = lax.broadcasted_iota(jnp.int32, (CHUNK, 2 * CHUNK), 0)
    ci = lax.broadcasted_iota(jnp.int32, (CHUNK, 2 * CHUNK), 1) % CHUNK
    side = [ci < ri, ci <= ri, ri == ci]
    m = 1
    while m < CHUNK:
        side.append(((ri // (2 * m)) == (ci // (2 * m))) & ((ri // m) % 2 == 1) & ((ci // m) % 2 == 0))
        m *= 2
    rq = lax.broadcasted_iota(jnp.int32, (LANES, LANES), 0)
    cq = lax.broadcasted_iota(jnp.int32, (LANES, LANES), 1)
    square = [rq == cq, (rq // HEAD) == (cq // HEAD)]
    tri = (lax.broadcasted_iota(jnp.int32, (CHUNK, CHUNK), 1)
           <= lax.broadcasted_iota(jnp.int32, (CHUNK, CHUNK), 0))
    return tri.astype(BF16), jnp.stack(side).astype(F32), jnp.stack(square).astype(F32)


def _short_pairs(t_s):
    idx = {}
    for kind, inclusive in (("ab", False), ("ak", False), ("rb", True), ("rk", True)):
        for t in range(t_s):
            for s in range(t + 1 if inclusive else t):
                idx[(kind, t, s)] = len(idx)
    return idx


def _short_scale_kernel(r_ref, k_ref, v_ref, lw_ref, a_ref, b_ref, g_ref, x_o, bk_o, as_o, g4_o, *, t_s, n_s):
    g_ones = g_ref[...]
    slab = lambda ref, t: ref[t * n_s:(t + 1) * n_s, :]
    cs = []
    for t in range(t_s):
        cs.append(slab(lw_ref, t) if t == 0 else cs[-1] + slab(lw_ref, t))
    a_t = [slab(a_ref, t) * (jnp.exp(cs[t - 1]) if t else 1.0) for t in range(t_s)]
    r_t = [slab(r_ref, t) * jnp.exp(cs[t]) for t in range(t_s)]
    g_inv = [jnp.exp(-cs[t]) for t in range(t_s)]
    b_t = [slab(b_ref, t) * g_inv[t] for t in range(t_s)]
    k_t = [slab(k_ref, t) * g_inv[t] for t in range(t_s)]
    for t in range(t_s):
        g_tail = jnp.exp(cs[t_s - 1] - cs[t])
        x_o[t] = a_t[t]
        x_o[t_s + t] = r_t[t]
        bk_o[t] = slab(b_ref, t) * g_tail
        bk_o[t_s + t] = slab(k_ref, t) * g_tail
    g4_o[...] = jnp.exp(cs[t_s - 1])
    left = {"ab": a_t, "ak": a_t, "rb": r_t, "rk": r_t}
    right = {"ab": b_t, "ak": k_t, "rb": b_t, "rk": k_t}
    for (kind, t, s), row in _short_pairs(t_s).items():
        as_o[row] = _seg_sum(left[kind][t] * right[kind][s], g_ones)


def _head_rows(x):
    return jnp.concatenate([x[:, h * HEAD:(h + 1) * HEAD] for h in range(N_HEADS)], axis=0)


def _short_state_kernel(s_ref, x_ref, bk_ref, as_ref, v_ref, g4_ref, so_ref, y_ref, *, t_s, n_blk):
    idx = _short_pairs(t_s)
    rows8 = 2 * t_s
    lane_head = lax.broadcasted_iota(jnp.int32, (rows8, D_RWKV), 1) // HEAD
    row8 = lax.broadcasted_iota(jnp.int32, (rows8, D_RWKV), 0)
    for i in range(n_blk):
        s_b = s_ref[i]
        x = x_ref[:, i, :]
        zt = _mm(_head_rows(x), s_b, NT, *P_S)
        zz = jnp.zeros((rows8, D_RWKV), F32)
        for h in range(N_HEADS):
            zz = jnp.where(lane_head == h, zt[h * rows8:(h + 1) * rows8, :], zz)
        sc = as_ref[:, i, :]
        scal = lambda kind, t, s: sc[idx[(kind, t, s)]:idx[(kind, t, s)] + 1, :]
        v = [v_ref[t, i:i + 1, :] for t in range(t_s)]
        u = []
        for t in range(t_s):
            acc = zz[t:t + 1, :]
            for s in range(t):
                acc = acc + scal("ab", t, s) * u[s] + scal("ak", t, s) * v[s]
            u.append(acc)
        for t in range(t_s):
            acc = zz[t_s + t:t_s + t + 1, :]
            for s in range(t + 1):
                acc = acc + scal("rb", t, s) * u[s] + scal("rk", t, s) * v[s]
            y_ref[t, i:i + 1, :] = acc
        uv = jnp.zeros((rows8, D_RWKV), F32)
        for j, vec in enumerate(u + v):
            uv = jnp.where(row8 == j, jnp.broadcast_to(vec, (rows8, D_RWKV)), uv)
        uv_exp = jnp.concatenate([jnp.where(lane_head == h, uv, 0.0) for h in range(N_HEADS)], axis=0)
        upd = _mm(uv_exp, _head_rows(bk_ref[:, i, :]), TN, *P_S)
        g4 = g4_ref[i:i + 1, :]
        g_rows = jnp.concatenate(
            [jnp.broadcast_to(g4[:, h * HEAD:(h + 1) * HEAD], (HEAD, HEAD)) for h in range(N_HEADS)], axis=0)
        so_ref[i] = s_b * g_rows + upd


def _short_wkv_call(seqs, s0, g_ones, *, t_s, n_s, n_blk):
    assert 2 * t_s == SUBLANES and n_s % n_blk == 0 and n_blk % SUBLANES == 0
    n_scal = len(_short_pairs(t_s))
    full = lambda shape: pl.BlockSpec(shape, lambda: tuple(0 for _ in shape))
    x8, bk8, scal, g4 = pl.pallas_call(
        functools.partial(_short_scale_kernel, t_s=t_s, n_s=n_s),
        in_specs=[full((t_s * n_s, D_RWKV))] * 6 + [full((LANES, LANES))],
        out_specs=[full((2 * t_s, n_s, D_RWKV)), full((2 * t_s, n_s, D_RWKV)),
                   full((n_scal, n_s, D_RWKV)), full((n_s, D_RWKV))],
        out_shape=[jax.ShapeDtypeStruct((2 * t_s, n_s, D_RWKV), F32),
                   jax.ShapeDtypeStruct((2 * t_s, n_s, D_RWKV), F32),
                   jax.ShapeDtypeStruct((n_scal, n_s, D_RWKV), F32),
                   jax.ShapeDtypeStruct((n_s, D_RWKV), F32)],
        compiler_params=pltpu.CompilerParams(vmem_limit_bytes=VMEM_LIMIT),
        name="wkv_short_scale",
    )(*seqs, g_ones)
    v3 = seqs[2].reshape(t_s, n_s, D_RWKV)
    st = pl.BlockSpec((n_blk, N_HEADS * HEAD, HEAD), lambda i: (i, 0, 0))
    tm = lambda n: pl.BlockSpec((n, n_blk, D_RWKV), lambda i: (0, i, 0))
    s_new, y = pl.pallas_call(
        functools.partial(_short_state_kernel, t_s=t_s, n_blk=n_blk),
        grid=(n_s // n_blk,),
        in_specs=[st, tm(2 * t_s), tm(2 * t_s), tm(n_scal), tm(t_s),
                  pl.BlockSpec((n_blk, D_RWKV), lambda i: (i, 0))],
        out_specs=[st, tm(t_s)],
        out_shape=[jax.ShapeDtypeStruct((n_s, N_HEADS * HEAD, HEAD), F32),
                   jax.ShapeDtypeStruct((t_s, n_s, D_RWKV), F32)],
        compiler_params=pltpu.CompilerParams(
            dimension_semantics=("arbitrary",), vmem_limit_bytes=VMEM_LIMIT),
        name="wkv_short_state",
    )(s0.reshape(n_s, N_HEADS * HEAD, HEAD), x8, bk8, scal, v3, g4)
    return y.reshape(t_s * n_s, D_RWKV), s_new.reshape(n_s, N_HEADS, HEAD, HEAD)


def _out_kernel(y_ref, bon_ref, sg_ref, op_ref, x_ref, gw_ref, gb_ref, wo_ref, nf_ref, g_ref, o_ref):
    g_ones = g_ref[...]
    y = y_ref[...]
    mu = _seg_sum(y, g_ones) * (1.0 / HEAD)
    d = y - mu
    var = _seg_sum(d * d, g_ones) * (1.0 / HEAD)
    yn = d * lax.rsqrt(var + GN_EPS) * gw_ref[...] + gb_ref[...]
    o_rwkv = (yn + bon_ref[...]) * sg_ref[...]
    out = (jnp.dot(o_rwkv.astype(BF16), wo_ref[0:D_RWKV, :], preferred_element_type=F32)
           + jnp.dot(op_ref[...].astype(BF16), wo_ref[D_RWKV:, :], preferred_element_type=F32))
    res = x_ref[...] + out
    o_ref[...] = res * lax.rsqrt(jnp.mean(res * res, axis=-1, keepdims=True) + NORM_EPS) * nf_ref[...]


def _out_call(y, bon, sg, op, x, gn_w, gn_b, w_out, norm_f, g_ones, *, rows):
    total = x.shape[0]
    row_blk = lambda i: (i, 0)
    const2 = lambda i: (0, 0)
    act = lambda w: pl.BlockSpec((rows, w), row_blk)
    return pl.pallas_call(
        _out_kernel,
        grid=(total // rows,),
        in_specs=[act(D_RWKV)] * 4 + [act(D_MODEL),
                                      pl.BlockSpec((1, D_RWKV), const2),
                                      pl.BlockSpec((1, D_RWKV), const2),
                                      pl.BlockSpec((D_RWKV + D_POOL, D_MODEL), const2),
                                      pl.BlockSpec((1, D_MODEL), const2),
                                      pl.BlockSpec((LANES, LANES), const2)],
        out_specs=act(D_MODEL),
        out_shape=jax.ShapeDtypeStruct((total, D_MODEL), F32),
        compiler_params=pltpu.CompilerParams(
            dimension_semantics=("arbitrary",), vmem_limit_bytes=VMEM_LIMIT),
        name="out",
    )(y, bon, sg, op, x, gn_w, gn_b, w_out, norm_f, g_ones)


N_PREP_PARAMS = 12
N_WKV_CONSTS = 3
N_OUT_PARAMS = 5


def _fused_kernel(*refs, n_chunks, t0):
    x_ref, sh_ref, ph_ref, s0_ref = refs[:4]
    pos = 4
    prep_params = refs[pos:pos + N_PREP_PARAMS]
    pos += N_PREP_PARAMS
    wkv_consts = refs[pos:pos + N_WKV_CONSTS]
    pos += N_WKV_CONSTS
    gw_ref, gb_ref, wo_ref, nf_ref, g_ref = refs[pos:pos + N_OUT_PARAMS]
    pos += N_OUT_PARAMS
    o_ref, nsh_o, npl_o, so_ref = refs[pos:pos + 4]
    pos += 4
    seq_s = refs[pos:pos + 6]
    bon_s, sg_s, op_s, y_s, p_ext, u_ext, z_ref = refs[pos + 6:]
    _prep_kernel(x_ref, sh_ref, ph_ref, *prep_params, *seq_s, bon_s, sg_s, op_s, nsh_o, npl_o,
                 p_ext, u_ext, stride=1, t0=t0, hs=sh_ref.shape[1])
    _wkv_kernel(*seq_s, s0_ref, *wkv_consts, y_s, so_ref, z_ref, n_chunks=n_chunks)
    _out_kernel(y_s, bon_s, sg_s, op_s, x_ref, gw_ref, gb_ref, wo_ref, nf_ref, g_ref, o_ref)


def _fused_call(x, sh_hist, pool_hist, s0, prep_params, out_params, *, n_seq, n_chunks, t0):
    assert len(prep_params) == N_PREP_PARAMS and len(out_params) == N_OUT_PARAMS
    total = x.shape[0]
    rows = n_chunks * CHUNK
    tiles = total // (n_seq * rows)
    hs = sh_hist.shape[1]
    assert CHUNK == HEAD and 2 * CHUNK == LANES
    wkv_consts = _wkv_masks()
    act = pl.BlockSpec((rows, D_MODEL), lambda b, j: (b * tiles + j, 0))
    per_seq = lambda a: pl.BlockSpec((1,) + a.shape[1:], lambda b, j: (b,) + (0,) * (a.ndim - 1))
    const = lambda a: pl.BlockSpec(a.shape, lambda b, j: (0,) * a.ndim, pipeline_mode=pl.Buffered(1))
    consts = tuple(prep_params) + tuple(wkv_consts) + tuple(out_params)
    seq_scratch = pltpu.VMEM((rows, D_RWKV), F32)
    return pl.pallas_call(
        functools.partial(_fused_kernel, n_chunks=n_chunks, t0=t0),
        grid=(n_seq, tiles),
        in_specs=[act, per_seq(sh_hist), per_seq(pool_hist), per_seq(s0)] + [const(a) for a in consts],
        out_specs=[act, per_seq(sh_hist), per_seq(pool_hist), per_seq(s0)],
        out_shape=[jax.ShapeDtypeStruct((total, D_MODEL), F32),
                   jax.ShapeDtypeStruct(sh_hist.shape, F32),
                   jax.ShapeDtypeStruct(pool_hist.shape, F32),
                   jax.ShapeDtypeStruct(s0.shape, F32)],
        scratch_shapes=[seq_scratch] * 10 + [pltpu.VMEM((hs + rows, D_SHIFT), F32),
                                             pltpu.VMEM((POOL_HIST + rows, D_POOL), F32),
                                             pltpu.VMEM((N_PAIRS, LANES, LANES), F32)],
        compiler_params=pltpu.CompilerParams(
            dimension_semantics=("arbitrary", "arbitrary"), vmem_limit_bytes=VMEM_LIMIT),
        name="layer_long",
    )(x, sh_hist, pool_hist, s0, *consts)


def _head_ones():
    i = lax.broadcasted_iota(jnp.int32, (LANES, LANES), 0) // HEAD
    j = lax.broadcasted_iota(jnp.int32, (LANES, LANES), 1) // HEAD
    return (i == j).astype(BF16)


def kernel(x_prompt, x_sample, state_shift, state_wkv, state_pool, norm_w, w_in, mu_shift, w_decay_b,
           w0, w_aaa_b, a0, k_k, k_a, r_k, gn_w, gn_b, pool_w, pool_scale, w_out, norm_f):
    depth = norm_w.shape[0]
    n_p, t_p, _ = x_prompt.shape
    n_s, t_s, _ = x_sample.shape
    assert depth == 1, "a stacked trunk needs a residual-only output kernel between layers"
    assert t_p % (WKV_CHUNKS_PER_STEP * CHUNK) == 0
    g_ones = _head_ones()
    row = lambda z: z.reshape(1, -1).astype(F32)

    hp = x_prompt.astype(F32).reshape(n_p * t_p, D_MODEL)
    hs = jnp.transpose(x_sample.astype(F32), (1, 0, 2)).reshape(t_s * n_s, D_MODEL)
    outs = [[] for _ in range(6)]
    for l in range(depth):
        zl = jnp.zeros((D_LORA, D_RWKV), F32)
        w_lora = jnp.concatenate([jnp.concatenate([w_decay_b[l], zl], axis=1),
                                  jnp.concatenate([zl, w_aaa_b[l]], axis=1)], axis=0).astype(BF16)
        prep_params = (row(norm_w[l]), w_in[l].astype(BF16), row(mu_shift[l]), w_lora, row(w0[l]),
                       row(a0[l]), row(k_k[l]), row(k_a[l]), row(r_k[l]), pool_w[l].astype(BF16),
                       row(pool_scale[l]), g_ones)
        out_params = (row(gn_w[l]), row(gn_b[l]), w_out[l].astype(BF16), row(norm_f), g_ones)

        hp, nsh, npl, s_fin = _fused_call(
            hp, jnp.zeros((n_p, SUBLANES, D_SHIFT), F32), jnp.zeros((n_p, POOL_HIST, D_POOL), F32),
            jnp.zeros((n_p, N_HEADS, HEAD, HEAD), F32), prep_params, out_params,
            n_seq=n_p, n_chunks=WKV_CHUNKS_PER_STEP, t0=0)
        outs[0].append(nsh[:, SUBLANES - 1])
        outs[1].append(s_fin)
        outs[2].append(npl[:, 1:])

        sh_hist = state_shift[l].astype(F32)[None]
        pool_hist = jnp.transpose(state_pool[l].astype(F32), (1, 0, 2))
        pool_hist = jnp.pad(pool_hist, ((1, 0), (0, 0), (0, 0))).reshape(1, POOL_HIST * n_s, D_POOL)
        res = _prep_call(hs, sh_hist, pool_hist, prep_params,
                         n_seq=1, rows=t_s * n_s, stride=n_s, t0=PAST_LEN)
        seqs, (bon, sg, op), nsh, npl = res[:6], res[6:9], res[9], res[10]

        y, s_fin = _short_wkv_call(seqs, state_wkv[l].astype(F32), g_ones, t_s=t_s, n_s=n_s,
                                   n_blk=SHORT_SEQS_PER_STEP)
        hs = _out_call(y, bon, sg, op, hs, *out_params, rows=t_s * n_s)
        outs[3].append(nsh[0])
        outs[4].append(s_fin)
        outs[5].append(jnp.transpose(npl[0, n_s:].reshape(POOL_HIST - 1, n_s, D_POOL), (1, 0, 2)))

    y_prompt = hp.reshape(n_p, t_p, D_MODEL).astype(x_prompt.dtype)
    y_sample = jnp.transpose(hs.reshape(t_s, n_s, D_MODEL), (1, 0, 2)).astype(x_sample.dtype)
    return (y_prompt, y_sample) + tuple(jnp.stack(o, axis=0) for o in outs)
```

```python
import functools

import jax
import jax.numpy as jnp
from jax import lax
from jax.experimental import pallas as pl
from jax.experimental.pallas import tpu as pltpu

F32 = jnp.float32
BF16 = jnp.bfloat16

D_MODEL = 1024
D_RWKV = 512
D_POOL = 512
HEAD = 64
N_HEADS = D_RWKV // HEAD
LANES = 128
SUBLANES = 8
MXU_DIM = 256
N_PAIRS = D_RWKV // LANES
D_LORA = 64
POOL_WINDOWS = (2, 4, 8, 16)
POOL_GROUP = D_POOL // len(POOL_WINDOWS)
POOL_HIST = max(POOL_WINDOWS)
D_SHIFT = 3 * D_RWKV + 2 * D_LORA
D_IN = D_SHIFT + D_RWKV + 2 * D_POOL
PAST_LEN = 16384
NORM_EPS = 1e-6
GN_EPS = 64e-5
L2_EPS = 1e-12
CHUNK = 64
VMEM_LIMIT = 56 * 1024 * 1024
P_A = (1, 1)
P_T = (1, 1)
P_U = (1, 1)
P_S = (1, 1)
P_Y = (1, 1)
WKV_CHUNKS_PER_STEP = 8
SHORT_SEQS_PER_STEP = 8

NN = (((1,), (0,)), ((), ()))
NT = (((1,), (1,)), ((), ()))
TN = (((0,), (0,)), ((), ()))


def _pieces(x, n):
    if x.dtype == BF16:
        return [x]
    out = []
    rem = x
    for i in range(n):
        p = rem.astype(BF16)
        out.append(p)
        if i + 1 < n:
            rem = rem - p.astype(F32)
    return out


def _mm(a, b, dims=NN, na=1, nb=1):
    ap = _pieces(a, na)
    bp = _pieces(b, nb)
    order = max(len(ap), len(bp))
    acc = None
    for i, x in enumerate(ap):
        for j, y in enumerate(bp):
            if i + j >= order:
                continue
            t = lax.dot_general(x, y, dims, preferred_element_type=F32)
            acc = t if acc is None else acc + t
    return acc


def _seg_sum(x, g_ones, pieces=1):
    cols = []
    for s in range(x.shape[1] // MXU_DIM):
        cols.append(_mm(x[:, s * MXU_DIM:(s + 1) * MXU_DIM], g_ones, NN, pieces, 1))
    return cols[0] if len(cols) == 1 else jnp.concatenate(cols, axis=1)


def _sigmoid(x):
    return 1.0 / (1.0 + jnp.exp(-x))


def _prep_kernel(x_ref, sh_ref, ph_ref, nw_ref, win_ref, mu_ref, wl_ref, w0_ref, a0_ref, kk_ref,
                 ka_ref, rk_ref, pw_ref, ps_ref, g_ref,
                 r_o, k_o, v_o, lw_o, a_o, b_o, bon_o, sg_o, op_o, nsh_o, npl_o,
                 p_ext, u_ext, *, stride, t0, hs):
    j = pl.program_id(1)
    rows = x_ref.shape[0]
    ph = POOL_HIST * stride

    @pl.when(j == 0)
    def _():
        p_ext[0:hs, :] = sh_ref[0]
        u_ext[0:ph, :] = ph_ref[0]

    x = x_ref[...]
    h = x * lax.rsqrt(jnp.mean(x * x, axis=-1, keepdims=True) + NORM_EPS) * nw_ref[...]
    proj = jnp.dot(h.astype(BF16), win_ref[...], preferred_element_type=F32)

    p = proj[:, :D_SHIFT]
    p_ext[hs:hs + rows, :] = p
    prev = p_ext[pl.ds(hs - stride, rows), :]
    ps = p + mu_ref[...] * (prev - p)
    r = ps[:, 0:D_RWKV]
    k = ps[:, D_RWKV:2 * D_RWKV]
    v = ps[:, 2 * D_RWKV:3 * D_RWKV]
    xwa = ps[:, 3 * D_RWKV:D_SHIFT]
    lane = lax.broadcasted_iota(jnp.int32, xwa.shape, 1)
    lora_in = jnp.where(lane < D_LORA, jnp.tanh(xwa), xwa)
    lora = jnp.dot(lora_in.astype(BF16), wl_ref[...], preferred_element_type=F32)
    lw = (-0.6065306597126334) * _sigmoid(w0_ref[...] + lora[:, :D_RWKV])
    alpha = _sigmoid(a0_ref[...] + lora[:, D_RWKV:])
    g_ones = g_ref[...]
    kk = k * kk_ref[...]
    kk = kk * lax.rsqrt(_seg_sum(kk * kk, g_ones) + L2_EPS)
    k2 = k * (1.0 + (alpha - 1.0) * ka_ref[...])
    r_o[...] = r
    k_o[...] = k2
    v_o[...] = v
    lw_o[...] = lw
    a_o[...] = -kk
    b_o[...] = kk * alpha
    bon_o[...] = _seg_sum(r * k2 * rk_ref[...], g_ones) * v
    g_rwkv = proj[:, D_SHIFT:D_SHIFT + D_RWKV]
    sg_o[...] = g_rwkv * _sigmoid(g_rwkv)

    u = proj[:, D_SHIFT + D_RWKV:D_SHIFT + D_RWKV + D_POOL]
    g_pool = proj[:, D_SHIFT + D_RWKV + D_POOL:]
    u_ext[ph:ph + rows, :] = u
    row = lax.broadcasted_iota(jnp.int32, (rows, 1), 0)
    pos = t0 + (j * rows + row) // stride
    for gi, win in enumerate(POOL_WINDOWS):
        sl = slice(gi * POOL_GROUP, (gi + 1) * POOL_GROUP)
        tot = u[:, sl]
        for back in range(1, win):
            tot = tot + u_ext[pl.ds(ph - back * stride, rows), sl]
        inv_cnt = 1.0 / jnp.minimum(pos + 1, win).astype(F32)
        d = tot * inv_cnt - u[:, sl]
        o = jnp.dot(d.astype(BF16), pw_ref[gi], preferred_element_type=F32)
        gp = g_pool[:, sl]
        op_o[:, sl] = o * ps_ref[:, sl] * (gp * _sigmoid(gp))

    last_p = p_ext[rows:rows + hs, :]
    last_u = u_ext[rows:rows + ph, :]
    p_ext[0:hs, :] = last_p
    u_ext[0:ph, :] = last_u
    nsh_o[0] = last_p
    npl_o[0] = last_u


def _prep_call(x, sh_hist, pool_hist, params, *, n_seq, rows, stride, t0):
    total = x.shape[0]
    tiles = total // (n_seq * rows)
    hs = sh_hist.shape[1]
    ph = POOL_HIST * stride
    row_blk = lambda b, j: (b * tiles + j, 0)
    const2 = lambda b, j: (0, 0)
    seq3 = lambda b, j: (b, 0, 0)
    act = lambda w: pl.BlockSpec((rows, w), row_blk)
    in_specs = [
        act(D_MODEL),
        pl.BlockSpec((1, hs, D_SHIFT), seq3),
        pl.BlockSpec((1, ph, D_POOL), seq3),
        pl.BlockSpec((1, D_MODEL), const2),
        pl.BlockSpec((D_MODEL, D_IN), const2),
        pl.BlockSpec((1, D_SHIFT), const2),
        pl.BlockSpec((2 * D_LORA, 2 * D_RWKV), const2),
        pl.BlockSpec((1, D_RWKV), const2),
        pl.BlockSpec((1, D_RWKV), const2),
        pl.BlockSpec((1, D_RWKV), const2),
        pl.BlockSpec((1, D_RWKV), const2),
        pl.BlockSpec((1, D_RWKV), const2),
        pl.BlockSpec((len(POOL_WINDOWS), POOL_GROUP, POOL_GROUP), lambda b, j: (0, 0, 0)),
        pl.BlockSpec((1, D_POOL), const2),
        pl.BlockSpec((MXU_DIM, MXU_DIM), const2),
    ]
    out_shape = [jax.ShapeDtypeStruct((total, D_RWKV), F32)] * 9 + [
        jax.ShapeDtypeStruct((n_seq, hs, D_SHIFT), F32),
        jax.ShapeDtypeStruct((n_seq, ph, D_POOL), F32),
    ]
    out_specs = [act(D_RWKV)] * 9 + [
        pl.BlockSpec((1, hs, D_SHIFT), seq3),
        pl.BlockSpec((1, ph, D_POOL), seq3),
    ]
    return pl.pallas_call(
        functools.partial(_prep_kernel, stride=stride, t0=t0, hs=hs),
        grid=(n_seq, tiles),
        in_specs=in_specs,
        out_specs=out_specs,
        out_shape=out_shape,
        scratch_shapes=[pltpu.VMEM((hs + rows, D_SHIFT), F32), pltpu.VMEM((ph + rows, D_POOL), F32)],
        compiler_params=pltpu.CompilerParams(
            dimension_semantics=("arbitrary", "arbitrary"), vmem_limit_bytes=VMEM_LIMIT),
        name="prep",
    )(x, sh_hist, pool_hist, *params)


def _stack(x, m0):
    return jnp.concatenate([jnp.where(m0, x, 0.0), jnp.where(m0, 0.0, x)], axis=0)


def _wkv_kernel(r_ref, k_ref, v_ref, lw_ref, a_ref, b_ref, s0_ref, tri_ref, mss_ref, msq_ref,
                y_ref, so_ref, z_ref, *, n_chunks):
    c = pl.program_id(1)
    n_c = pl.num_programs(1)
    zeros = jnp.zeros((HEAD, HEAD), F32)

    @pl.when(c == 0)
    def _():
        for p in range(N_PAIRS):
            top = jnp.concatenate([s0_ref[0, 2 * p], zeros], axis=1)
            bot = jnp.concatenate([zeros, s0_ref[0, 2 * p + 1]], axis=1)
            z_ref[p] = jnp.concatenate([top, bot], axis=0).T

    lane = lax.broadcasted_iota(jnp.int32, (CHUNK, LANES), 1)
    m0 = lane < HEAD
    strict = mss_ref[0][:, :LANES] > 0.5
    incl2 = mss_ref[1] > 0.5
    n_levels = mss_ref.shape[0] - 3
    lane_head4 = lax.broadcasted_iota(jnp.int32, (CHUNK, 2 * LANES), 1) // HEAD
    eye = msq_ref[0]
    same_head = msq_ref[1] > 0.5
    zblk = jnp.zeros((CHUNK, LANES), F32)
    zblk2 = jnp.zeros((2 * CHUNK, LANES), F32)

    scaled = []
    for ci in range(n_chunks):
        rows = slice(ci * CHUNK, (ci + 1) * CHUNK)
        lw = lw_ref[rows, :]
        cs = _mm(tri_ref[...], lw, NN, 1, 3)
        cs_end = cs[CHUNK - 1:CHUNK, :]
        g_inv = jnp.exp(-cs)
        g_tail = jnp.exp(cs_end - cs)
        scaled.append(dict(
            g_end=jnp.exp(cs_end),
            a_t=a_ref[rows, :] * jnp.exp(cs - lw), r_t=r_ref[rows, :] * jnp.exp(cs),
            b_t=b_ref[rows, :] * g_inv, k_t=k_ref[rows, :] * g_inv,
            b_h=b_ref[rows, :] * g_tail, k_h=k_ref[rows, :] * g_tail, v=v_ref[rows, :]))
    chains = [(ci, p) for ci in range(n_chunks) for p in range(N_PAIRS)]
    lanes = lambda p: slice(p * LANES, (p + 1) * LANES)
    pair = lambda name: [scaled[ci][name][:, lanes(p)] for ci, p in chains]
    each = lambda fn, *lists: [fn(*args) for args in zip(*lists)]
    cat0 = lambda *xs: jnp.concatenate(xs, axis=0)
    cat1 = lambda *xs: jnp.concatenate(xs, axis=1)
    at, rt, bt, kt, bh, kh, v = (pair(n) for n in ("a_t", "r_t", "b_t", "k_t", "b_h", "k_h", "v"))
    v2 = each(lambda x: _stack(x, m0), v)
    scores = each(lambda a, r, b, k: _mm(cat0(a, r), cat0(_stack(b, m0), _stack(k, m0)), NT, *P_A),
                  at, rt, bt, kt)
    a_ab = each(lambda s: jnp.where(strict, s[:CHUNK, :LANES], 0.0), scores)
    a_ak = each(lambda s: jnp.where(strict, s[:CHUNK, LANES:], 0.0), scores)
    a_r = each(lambda s: jnp.where(incl2, s[CHUNK:, :], 0.0), scores)
    stack4 = lambda x: cat0(*[jnp.where(lane_head4 == h, x, 0.0) for h in range(4)])
    join = lambda xs: [cat1(xs[i], xs[i + 1]) for i in range(0, len(xs), 2)]
    split = lambda xs: [x[:, s] for x in xs for s in (slice(0, LANES), slice(LANES, 2 * LANES))]
    a_ab4 = join(a_ab)
    t_inv4 = each(lambda x: mss_ref[2] + x * mss_ref[3], a_ab4)
    for lvl in range(1, n_levels):
        inner = each(lambda x, t: _mm(x * mss_ref[3 + lvl], stack4(t), NN, *P_T), a_ab4, t_inv4)
        t_inv4 = each(lambda t, w: t + _mm(t, stack4(w), NN, *P_T), t_inv4, inner)
    t_inv = split(t_inv4)
    akv = split(each(lambda x, y: _mm(x, stack4(y), NN, *P_U), join(a_ak), join(v)))
    pq = each(lambda t, x, y: _mm(t, cat1(_stack(x, m0), _stack(y, m0)), NN, *P_U), t_inv, at, akv)
    bpq = each(lambda b, k, x, y: _mm(cat0(b, k), cat0(x, cat1(zblk, y)), TN, *P_S), bh, kh, pq, v)
    g_end = [scaled[ci]["g_end"][:, lanes(p)] for ci, p in chains]
    m_t = each(lambda g, x: eye * g + jnp.where(same_head, x[:, :LANES], 0.0), g_end, bpq)
    n_t = each(lambda x: jnp.where(same_head, x[:, LANES:], 0.0), bpq)
    yy = each(lambda x, y, z: _mm(x, cat0(cat1(_stack(y[:, :LANES], m0), _stack(y[:, LANES:], m0)),
                                         cat1(zblk2, z)), NN, *P_Y), a_r, pq, v2)
    y1 = each(lambda r, x: r + x[:, :LANES], rt, yy)
    y2 = each(lambda x: x[:, LANES:], yy)

    state = [z_ref[p] for p in range(N_PAIRS)]
    for ci in range(n_chunks):
        for p in range(N_PAIRS):
            i = ci * N_PAIRS + p
            both = _mm(cat0(y1[i], m_t[i]), state[p], NN, *P_S)
            y_ref[ci * CHUNK:(ci + 1) * CHUNK, lanes(p)] = both[:CHUNK] + y2[i]
            state[p] = both[CHUNK:] + n_t[i]
    for p in range(N_PAIRS):
        z_ref[p] = state[p]

    @pl.when(c == n_c - 1)
    def _():
        for p in range(N_PAIRS):
            s2 = z_ref[p].T
            so_ref[0, 2 * p] = s2[:HEAD, :HEAD]
            so_ref[0, 2 * p + 1] = s2[HEAD:, HEAD:]


def _wkv_masks():
    ri = lax.broadcasted_iota(jnp.int32, (CHUNK, 4 * CHUNK), 0)
    ci = lax.broadcasted_iota(jnp.int32, (CHUNK, 4 * CHUNK), 1) % CHUNK
    side = [ci < ri, ci <= ri, ri == ci]
    m = 1
    while m < CHUNK:
        side.append(((ri // (2 * m)) == (ci // (2 * m))) & ((ri // m) % 2 == 1) & ((ci // m) % 2 == 0))
        m *= 2
    rq = lax.broadcasted_iota(jnp.int32, (LANES, LANES), 0)
    cq = lax.broadcasted_iota(jnp.int32, (LANES, LANES), 1)
    square = [rq == cq, (rq // HEAD) == (cq // HEAD)]
    tri = (lax.broadcasted_iota(jnp.int32, (CHUNK, CHUNK), 1)
           <= lax.broadcasted_iota(jnp.int32, (CHUNK, CHUNK), 0))
    return tri.astype(BF16), jnp.stack(side).astype(F32), jnp.stack(square).astype(F32)


def _short_pairs(t_s):
    idx = {}
    for kind, inclusive in (("ab", False), ("ak", False), ("rb", True), ("rk", True)):
        for t in range(t_s):
            for s in range(t + 1 if inclusive else t):
                idx[(kind, t, s)] = len(idx)
    return idx


def _short_scale_kernel(r_ref, k_ref, v_ref, lw_ref, a_ref, b_ref, g_ref, x_o, bk_o, as_o, g4_o, *, t_s, n_s):
    g_ones = g_ref[...]
    slab = lambda ref, t: ref[t * n_s:(t + 1) * n_s, :]
    cs = []
    for t in range(t_s):
        cs.append(slab(lw_ref, t) if t == 0 else cs[-1] + slab(lw_ref, t))
    a_t = [slab(a_ref, t) * (jnp.exp(cs[t - 1]) if t else 1.0) for t in range(t_s)]
    r_t = [slab(r_ref, t) * jnp.exp(cs[t]) for t in range(t_s)]
    g_inv = [jnp.exp(-cs[t]) for t in range(t_s)]
    b_t = [slab(b_ref, t) * g_inv[t] for t in range(t_s)]
    k_t = [slab(k_ref, t) * g_inv[t] for t in range(t_s)]
    for t in range(t_s):
        g_tail = jnp.exp(cs[t_s - 1] - cs[t])
        x_o[t] = a_t[t]
        x_o[t_s + t] = r_t[t]
        bk_o[t] = slab(b_ref, t) * g_tail
        bk_o[t_s + t] = slab(k_ref, t) * g_tail
    g4_o[...] = jnp.exp(cs[t_s - 1])
    left = {"ab": a_t, "ak": a_t, "rb": r_t, "rk": r_t}
    right = {"ab": b_t, "ak": k_t, "rb": b_t, "rk": k_t}
    for (kind, t, s), row in _short_pairs(t_s).items():
        as_o[row] = _seg_sum(left[kind][t] * right[kind][s], g_ones)


def _head_rows(x):
    return jnp.concatenate([x[:, h * HEAD:(h + 1) * HEAD] for h in range(N_HEADS)], axis=0)


def _short_state_kernel(s_ref, x_ref, bk_ref, as_ref, v_ref, g4_ref, so_ref, y_ref, *, t_s, n_blk):
    idx = _short_pairs(t_s)
    rows8 = 2 * t_s
    lane_head = lax.broadcasted_iota(jnp.int32, (rows8, D_RWKV), 1) // HEAD
    row8 = lax.broadcasted_iota(jnp.int32, (rows8, D_RWKV), 0)
    for i in range(n_blk):
        s_b = s_ref[i]
        x = x_ref[:, i, :]
        zt = _mm(_head_rows(x), s_b, NT, *P_S)
        zz = jnp.zeros((rows8, D_RWKV), F32)
        for h in range(N_HEADS):
            zz = jnp.where(lane_head == h, zt[h * rows8:(h + 1) * rows8, :], zz)
        sc = as_ref[:, i, :]
        scal = lambda kind, t, s: sc[idx[(kind, t, s)]:idx[(kind, t, s)] + 1, :]
        v = [v_ref[t, i:i + 1, :] for t in range(t_s)]
        u = []
        for t in range(t_s):
            acc = zz[t:t + 1, :]
            for s in range(t):
                acc = acc + scal("ab", t, s) * u[s] + scal("ak", t, s) * v[s]
            u.append(acc)
        for t in range(t_s):
            acc = zz[t_s + t:t_s + t + 1, :]
            for s in range(t + 1):
                acc = acc + scal("rb", t, s) * u[s] + scal("rk", t, s) * v[s]
            y_ref[t, i:i + 1, :] = acc
        uv = jnp.zeros((rows8, D_RWKV), F32)
        for j, vec in enumerate(u + v):
            uv = jnp.where(row8 == j, jnp.broadcast_to(vec, (rows8, D_RWKV)), uv)
        uv_exp = jnp.concatenate([jnp.where(lane_head == h, uv, 0.0) for h in range(N_HEADS)], axis=0)
        upd = _mm(uv_exp, _head_rows(bk_ref[:, i, :]), TN, *P_S)
        g4 = g4_ref[i:i + 1, :]
        g_rows = jnp.concatenate(
            [jnp.broadcast_to(g4[:, h * HEAD:(h + 1) * HEAD], (HEAD, HEAD)) for h in range(N_HEADS)], axis=0)
        so_ref[i] = s_b * g_rows + upd


def _short_wkv_call(seqs, s0, g_ones, *, t_s, n_s, n_blk):
    assert 2 * t_s == SUBLANES and n_s % n_blk == 0 and n_blk % SUBLANES == 0
    n_scal = len(_short_pairs(t_s))
    full = lambda shape: pl.BlockSpec(shape, lambda: tuple(0 for _ in shape))
    x8, bk8, scal, g4 = pl.pallas_call(
        functools.partial(_short_scale_kernel, t_s=t_s, n_s=n_s),
        in_specs=[full((t_s * n_s, D_RWKV))] * 6 + [full((MXU_DIM, MXU_DIM))],
        out_specs=[full((2 * t_s, n_s, D_RWKV)), full((2 * t_s, n_s, D_RWKV)),
                   full((n_scal, n_s, D_RWKV)), full((n_s, D_RWKV))],
        out_shape=[jax.ShapeDtypeStruct((2 * t_s, n_s, D_RWKV), F32),
                   jax.ShapeDtypeStruct((2 * t_s, n_s, D_RWKV), F32),
                   jax.ShapeDtypeStruct((n_scal, n_s, D_RWKV), F32),
                   jax.ShapeDtypeStruct((n_s, D_RWKV), F32)],
        compiler_params=pltpu.CompilerParams(vmem_limit_bytes=VMEM_LIMIT),
        name="wkv_short_scale",
    )(*seqs, g_ones)
    v3 = seqs[2].reshape(t_s, n_s, D_RWKV)
    st = pl.BlockSpec((n_blk, N_HEADS * HEAD, HEAD), lambda i: (i, 0, 0))
    tm = lambda n: pl.BlockSpec((n, n_blk, D_RWKV), lambda i: (0, i, 0))
    s_new, y = pl.pallas_call(
        functools.partial(_short_state_kernel, t_s=t_s, n_blk=n_blk),
        grid=(n_s // n_blk,),
        in_specs=[st, tm(2 * t_s), tm(2 * t_s), tm(n_scal), tm(t_s),
                  pl.BlockSpec((n_blk, D_RWKV), lambda i: (i, 0))],
        out_specs=[st, tm(t_s)],
        out_shape=[jax.ShapeDtypeStruct((n_s, N_HEADS * HEAD, HEAD), F32),
                   jax.ShapeDtypeStruct((t_s, n_s, D_RWKV), F32)],
        compiler_params=pltpu.CompilerParams(
            dimension_semantics=("arbitrary",), vmem_limit_bytes=VMEM_LIMIT),
        name="wkv_short_state",
    )(s0.reshape(n_s, N_HEADS * HEAD, HEAD), x8, bk8, scal, v3, g4)
    return y.reshape(t_s * n_s, D_RWKV), s_new.reshape(n_s, N_HEADS, HEAD, HEAD)


def _out_kernel(y_ref, bon_ref, sg_ref, op_ref, x_ref, gw_ref, gb_ref, wo_ref, nf_ref, g_ref, o_ref):
    g_ones = g_ref[...]
    y = y_ref[...]
    mu = _seg_sum(y, g_ones) * (1.0 / HEAD)
    d = y - mu
    var = _seg_sum(d * d, g_ones) * (1.0 / HEAD)
    yn = d * lax.rsqrt(var + GN_EPS) * gw_ref[...] + gb_ref[...]
    o_rwkv = (yn + bon_ref[...]) * sg_ref[...]
    out = (jnp.dot(o_rwkv.astype(BF16), wo_ref[0:D_RWKV, :], preferred_element_type=F32)
           + jnp.dot(op_ref[...].astype(BF16), wo_ref[D_RWKV:, :], preferred_element_type=F32))
    res = x_ref[...] + out
    o_ref[...] = res * lax.rsqrt(jnp.mean(res * res, axis=-1, keepdims=True) + NORM_EPS) * nf_ref[...]


def _out_call(y, bon, sg, op, x, gn_w, gn_b, w_out, norm_f, g_ones, *, rows):
    total = x.shape[0]
    row_blk = lambda i: (i, 0)
    const2 = lambda i: (0, 0)
    act = lambda w: pl.BlockSpec((rows, w), row_blk)
    return pl.pallas_call(
        _out_kernel,
        grid=(total // rows,),
        in_specs=[act(D_RWKV)] * 4 + [act(D_MODEL),
                                      pl.BlockSpec((1, D_RWKV), const2),
                                      pl.BlockSpec((1, D_RWKV), const2),
                                      pl.BlockSpec((D_RWKV + D_POOL, D_MODEL), const2),
                                      pl.BlockSpec((1, D_MODEL), const2),
                                      pl.BlockSpec((MXU_DIM, MXU_DIM), const2)],
        out_specs=act(D_MODEL),
        out_shape=jax.ShapeDtypeStruct((total, D_MODEL), F32),
        compiler_params=pltpu.CompilerParams(
            dimension_semantics=("arbitrary",), vmem_limit_bytes=VMEM_LIMIT),
        name="out",
    )(y, bon, sg, op, x, gn_w, gn_b, w_out, norm_f, g_ones)


N_PREP_PARAMS = 12
N_WKV_CONSTS = 3
N_OUT_PARAMS = 5


def _fused_kernel(*refs, n_chunks, t0, has_carry):
    x_ref, sh_ref, ph_ref, s0_ref = refs[:4]
    pos = 4 + int(has_carry)
    prep_params = refs[pos:pos + N_PREP_PARAMS]
    pos += N_PREP_PARAMS
    wkv_consts = refs[pos:pos + N_WKV_CONSTS]
    pos += N_WKV_CONSTS
    gw_ref, gb_ref, wo_ref, nf_ref, g_ref = refs[pos:pos + N_OUT_PARAMS]
    pos += N_OUT_PARAMS
    o_ref, nsh_o, npl_o, so_ref = refs[pos:pos + 4]
    pos += 4
    seq_s = refs[pos:pos + 6]
    bon_s, sg_s, op_s, y_s, p_ext, u_ext, z_ref = refs[pos + 6:]
    _prep_kernel(x_ref, sh_ref, ph_ref, *prep_params, *seq_s, bon_s, sg_s, op_s, nsh_o, npl_o,
                 p_ext, u_ext, stride=1, t0=t0, hs=sh_ref.shape[1])
    _wkv_kernel(*seq_s, s0_ref, *wkv_consts, y_s, so_ref, z_ref, n_chunks=n_chunks)
    _out_kernel(y_s, bon_s, sg_s, op_s, x_ref, gw_ref, gb_ref, wo_ref, nf_ref, g_ref, o_ref)


def _fused_call(x, sh_hist, pool_hist, s0, prep_params, out_params, *, n_seq, seq_lo, n_chunks, t0,
                carry=None):
    assert len(prep_params) == N_PREP_PARAMS and len(out_params) == N_OUT_PARAMS
    total = x.shape[0]
    rows = n_chunks * CHUNK
    tiles = total // (n_seq * rows)
    hs = sh_hist.shape[1]
    assert CHUNK == HEAD and 2 * CHUNK == LANES
    wkv_consts = _wkv_masks()
    act = pl.BlockSpec((rows, D_MODEL), lambda b, j: ((b + seq_lo) * tiles + j, 0))
    per_seq = lambda a: pl.BlockSpec((1,) + a.shape[1:], lambda b, j: (b,) + (0,) * (a.ndim - 1))
    const = lambda a: pl.BlockSpec(a.shape, lambda b, j: (0,) * a.ndim, pipeline_mode=pl.Buffered(1))
    consts = tuple(prep_params) + tuple(wkv_consts) + tuple(out_params)
    seq_scratch = pltpu.VMEM((rows, D_RWKV), F32)
    carry_args = () if carry is None else (carry,)
    carry_specs = [] if carry is None else [pl.BlockSpec(memory_space=pl.ANY)]
    return pl.pallas_call(
        functools.partial(_fused_kernel, n_chunks=n_chunks, t0=t0, has_carry=carry is not None),
        grid=(s0.shape[0], tiles),
        in_specs=([act, per_seq(sh_hist), per_seq(pool_hist), per_seq(s0)] + carry_specs
                  + [const(a) for a in consts]),
        input_output_aliases={} if carry is None else {4: 0},
        out_specs=[act, per_seq(sh_hist), per_seq(pool_hist), per_seq(s0)],
        out_shape=[jax.ShapeDtypeStruct((total, D_MODEL), F32),
                   jax.ShapeDtypeStruct(sh_hist.shape, F32),
                   jax.ShapeDtypeStruct(pool_hist.shape, F32),
                   jax.ShapeDtypeStruct(s0.shape, F32)],
        scratch_shapes=[seq_scratch] * 10 + [pltpu.VMEM((hs + rows, D_SHIFT), F32),
                                             pltpu.VMEM((POOL_HIST + rows, D_POOL), F32),
                                             pltpu.VMEM((N_PAIRS, LANES, LANES), F32)],
        compiler_params=pltpu.CompilerParams(
            dimension_semantics=("arbitrary", "arbitrary"), vmem_limit_bytes=VMEM_LIMIT),
        name="layer_long",
    )(x, sh_hist, pool_hist, s0, *carry_args, *consts)


def _head_ones():
    i = lax.broadcasted_iota(jnp.int32, (MXU_DIM, MXU_DIM), 0) // HEAD
    j = lax.broadcasted_iota(jnp.int32, (MXU_DIM, MXU_DIM), 1) // HEAD
    return (i == j).astype(BF16)


def kernel(x_prompt, x_sample, state_shift, state_wkv, state_pool, norm_w, w_in, mu_shift, w_decay_b,
           w0, w_aaa_b, a0, k_k, k_a, r_k, gn_w, gn_b, pool_w, pool_scale, w_out, norm_f):
    depth = norm_w.shape[0]
    n_p, t_p, _ = x_prompt.shape
    n_s, t_s, _ = x_sample.shape
    assert depth == 1, "a stacked trunk needs a residual-only output kernel between layers"
    assert t_p % (WKV_CHUNKS_PER_STEP * CHUNK) == 0
    g_ones = _head_ones()
    row = lambda z: z.reshape(1, -1).astype(F32)

    hp = x_prompt.astype(F32).reshape(n_p * t_p, D_MODEL)
    hs = jnp.transpose(x_sample.astype(F32), (1, 0, 2)).reshape(t_s * n_s, D_MODEL)
    outs = [[] for _ in range(6)]
    for l in range(depth):
        zl = jnp.zeros((D_LORA, D_RWKV), F32)
        w_lora = jnp.concatenate([jnp.concatenate([w_decay_b[l], zl], axis=1),
                                  jnp.concatenate([zl, w_aaa_b[l]], axis=1)], axis=0).astype(BF16)
        prep_params = (row(norm_w[l]), w_in[l].astype(BF16), row(mu_shift[l]), w_lora, row(w0[l]),
                       row(a0[l]), row(k_k[l]), row(k_a[l]), row(r_k[l]), pool_w[l].astype(BF16),
                       row(pool_scale[l]), g_ones)
        out_params = (row(gn_w[l]), row(gn_b[l]), w_out[l].astype(BF16), row(norm_f), g_ones)

        half = n_p // 2
        long_call = lambda lo, n, carry: _fused_call(
            hp, jnp.zeros((n, SUBLANES, D_SHIFT), F32), jnp.zeros((n, POOL_HIST, D_POOL), F32),
            jnp.zeros((n, N_HEADS, HEAD, HEAD), F32), prep_params, out_params,
            n_seq=n_p, seq_lo=lo, n_chunks=WKV_CHUNKS_PER_STEP, t0=0, carry=carry)
        first = long_call(0, half, None)
        first, hs = lax.optimization_barrier((first, hs))

        sh_hist = state_shift[l].astype(F32)[None]
        pool_hist = jnp.transpose(state_pool[l].astype(F32), (1, 0, 2))
        pool_hist = jnp.pad(pool_hist, ((1, 0), (0, 0), (0, 0))).reshape(1, POOL_HIST * n_s, D_POOL)
        res = _prep_call(hs, sh_hist, pool_hist, prep_params,
                         n_seq=1, rows=t_s * n_s, stride=n_s, t0=PAST_LEN)
        seqs, (bon, sg, op), nsh, npl = res[:6], res[6:9], res[9], res[10]

        y, s_fin = _short_wkv_call(seqs, state_wkv[l].astype(F32), g_ones, t_s=t_s, n_s=n_s,
                                   n_blk=SHORT_SEQS_PER_STEP)
        hs = _out_call(y, bon, sg, op, hs, *out_params, rows=t_s * n_s)
        first, hs = lax.optimization_barrier((first, hs))
        second = long_call(half, n_p - half, first[0])
        hp = second[0]
        nsh_p, npl_p, s_fin_p = (jnp.concatenate([u, w], axis=0) for u, w in zip(first[1:], second[1:]))
        outs[0].append(nsh_p[:, SUBLANES - 1])
        outs[1].append(s_fin_p)
        outs[2].append(npl_p[:, 1:])
        outs[3].append(nsh[0])
        outs[4].append(s_fin)
        outs[5].append(jnp.transpose(npl[0, n_s:].reshape(POOL_HIST - 1, n_s, D_POOL), (1, 0, 2)))

    y_prompt = hp.reshape(n_p, t_p, D_MODEL).astype(x_prompt.dtype)
    y_sample = jnp.transpose(hs.reshape(t_s, n_s, D_MODEL), (1, 0, 2)).astype(x_sample.dtype)
    return (y_prompt, y_sample) + tuple(jnp.stack(o, axis=0) for o in outs)
```

```python
import functools

import jax
import jax.numpy as jnp
from jax import lax
from jax.experimental import pallas as pl
from jax.experimental.pallas import tpu as pltpu

F32 = jnp.float32
BF16 = jnp.bfloat16

D_MODEL = 1024
D_RWKV = 512
D_POOL = 512
HEAD = 64
N_HEADS = D_RWKV // HEAD
LANES = 128
SUBLANES = 8
MXU_DIM = 256
N_PAIRS = D_RWKV // LANES
D_LORA = 64
POOL_WINDOWS = (2, 4, 8, 16)
POOL_GROUP = D_POOL // len(POOL_WINDOWS)
POOL_HIST = max(POOL_WINDOWS)
D_SHIFT = 3 * D_RWKV + 2 * D_LORA
D_IN = D_SHIFT + D_RWKV + 2 * D_POOL
PAST_LEN = 16384
NORM_EPS = 1e-6
GN_EPS = 64e-5
L2_EPS = 1e-12
CHUNK = 64
VMEM_LIMIT = 56 * 1024 * 1024
P_A = (1, 1)
P_T = (1, 1)
P_U = (1, 1)
P_S = (1, 1)
P_Y = (1, 1)
WKV_CHUNKS_PER_STEP = 8
SHORT_SEQS_PER_STEP = 8

NN = (((1,), (0,)), ((), ()))
NT = (((1,), (1,)), ((), ()))
TN = (((0,), (0,)), ((), ()))


def _pieces(x, n):
    if x.dtype == BF16:
        return [x]
    out = []
    rem = x
    for i in range(n):
        p = rem.astype(BF16)
        out.append(p)
        if i + 1 < n:
            rem = rem - p.astype(F32)
    return out


def _mm(a, b, dims=NN, na=1, nb=1):
    ap = _pieces(a, na)
    bp = _pieces(b, nb)
    order = max(len(ap), len(bp))
    acc = None
    for i, x in enumerate(ap):
        for j, y in enumerate(bp):
            if i + j >= order:
                continue
            t = lax.dot_general(x, y, dims, preferred_element_type=F32)
            acc = t if acc is None else acc + t
    return acc


def _seg_sum(x, g_ones, pieces=1):
    cols = []
    for s in range(x.shape[1] // MXU_DIM):
        cols.append(_mm(x[:, s * MXU_DIM:(s + 1) * MXU_DIM], g_ones, NN, pieces, 1))
    return cols[0] if len(cols) == 1 else jnp.concatenate(cols, axis=1)


def _sigmoid(x):
    return 1.0 / (1.0 + jnp.exp(-x))


def _prep_stages(x_ref, sh_ref, ph_ref, nw_ref, win_ref, mu_ref, wl_ref, w0_ref, a0_ref, kk_ref,
                 ka_ref, rk_ref, pw_ref, ps_ref, g_ref,
                 r_o, k_o, v_o, lw_o, a_o, b_o, bon_o, sg_o, op_o, nsh_o, npl_o,
                 p_ext, u_ext, *, stride, t0, hs):
    j = pl.program_id(1)
    rows = x_ref.shape[0]
    ph = POOL_HIST * stride

    @pl.when(j == 0)
    def _():
        p_ext[0:hs, :] = sh_ref[0]
        u_ext[0:ph, :] = ph_ref[0]

    x = x_ref[...]
    h = x * lax.rsqrt(jnp.mean(x * x, axis=-1, keepdims=True) + NORM_EPS) * nw_ref[...]
    proj = jnp.dot(h.astype(BF16), win_ref[...], preferred_element_type=F32)

    p = proj[:, :D_SHIFT]
    p_ext[hs:hs + rows, :] = p
    prev = p_ext[pl.ds(hs - stride, rows), :]
    ps = p + mu_ref[...] * (prev - p)
    r = ps[:, 0:D_RWKV]
    k = ps[:, D_RWKV:2 * D_RWKV]
    v = ps[:, 2 * D_RWKV:3 * D_RWKV]
    xwa = ps[:, 3 * D_RWKV:D_SHIFT]
    lane = lax.broadcasted_iota(jnp.int32, xwa.shape, 1)
    lora_in = jnp.where(lane < D_LORA, jnp.tanh(xwa), xwa)
    lora = jnp.dot(lora_in.astype(BF16), wl_ref[...], preferred_element_type=F32)
    lw = (-0.6065306597126334) * _sigmoid(w0_ref[...] + lora[:, :D_RWKV])
    alpha = _sigmoid(a0_ref[...] + lora[:, D_RWKV:])
    kk = k * kk_ref[...]
    kk = kk * lax.rsqrt(_seg_sum(kk * kk, g_ref[...]) + L2_EPS)
    r_o[...] = r
    k_o[...] = k * (1.0 + (alpha - 1.0) * ka_ref[...])
    v_o[...] = v
    lw_o[...] = lw
    a_o[...] = -kk
    b_o[...] = kk * alpha
    last_p = p_ext[rows:rows + hs, :]
    p_ext[0:hs, :] = last_p
    nsh_o[0] = last_p
    sg_o[...] = proj[:, D_SHIFT:D_SHIFT + D_RWKV]
    op_o[...] = proj[:, D_SHIFT + D_RWKV + D_POOL:]
    u_ext[ph:ph + rows, :] = proj[:, D_SHIFT + D_RWKV:D_SHIFT + D_RWKV + D_POOL]

    def gate():
        g_rwkv = sg_o[...]
        sg_o[...] = g_rwkv * _sigmoid(g_rwkv)

    def bonus():
        bon_o[...] = _seg_sum(r_o[...] * k_o[...] * rk_ref[...], g_ref[...]) * v_o[...]

    def pool(gi, win):
        sl = slice(gi * POOL_GROUP, (gi + 1) * POOL_GROUP)
        row = lax.broadcasted_iota(jnp.int32, (rows, 1), 0)
        pos = t0 + (j * rows + row) // stride
        u = u_ext[ph:ph + rows, sl]
        tot = u
        for back in range(1, win):
            tot = tot + u_ext[pl.ds(ph - back * stride, rows), sl]
        inv_cnt = 1.0 / jnp.minimum(pos + 1, win).astype(F32)
        d = tot * inv_cnt - u
        o = jnp.dot(d.astype(BF16), pw_ref[gi], preferred_element_type=F32)
        gp = op_o[:, sl]
        op_o[:, sl] = o * ps_ref[:, sl] * (gp * _sigmoid(gp))

    def carry_pool():
        last_u = u_ext[rows:rows + ph, :]
        u_ext[0:ph, :] = last_u
        npl_o[0] = last_u

    pools = [functools.partial(pool, gi, win) for gi, win in enumerate(POOL_WINDOWS)]
    return [gate, bonus] + pools + [carry_pool]


def _prep_kernel(*refs, **static):
    for finish in _prep_stages(*refs, **static):
        finish()


def _prep_call(x, sh_hist, pool_hist, params, *, n_seq, rows, stride, t0):
    total = x.shape[0]
    tiles = total // (n_seq * rows)
    hs = sh_hist.shape[1]
    ph = POOL_HIST * stride
    row_blk = lambda b, j: (b * tiles + j, 0)
    const2 = lambda b, j: (0, 0)
    seq3 = lambda b, j: (b, 0, 0)
    act = lambda w: pl.BlockSpec((rows, w), row_blk)
    in_specs = [
        act(D_MODEL),
        pl.BlockSpec((1, hs, D_SHIFT), seq3),
        pl.BlockSpec((1, ph, D_POOL), seq3),
        pl.BlockSpec((1, D_MODEL), const2),
        pl.BlockSpec((D_MODEL, D_IN), const2),
        pl.BlockSpec((1, D_SHIFT), const2),
        pl.BlockSpec((2 * D_LORA, 2 * D_RWKV), const2),
        pl.BlockSpec((1, D_RWKV), const2),
        pl.BlockSpec((1, D_RWKV), const2),
        pl.BlockSpec((1, D_RWKV), const2),
        pl.BlockSpec((1, D_RWKV), const2),
        pl.BlockSpec((1, D_RWKV), const2),
        pl.BlockSpec((len(POOL_WINDOWS), POOL_GROUP, POOL_GROUP), lambda b, j: (0, 0, 0)),
        pl.BlockSpec((1, D_POOL), const2),
        pl.BlockSpec((MXU_DIM, MXU_DIM), const2),
    ]
    out_shape = [jax.ShapeDtypeStruct((total, D_RWKV), F32)] * 9 + [
        jax.ShapeDtypeStruct((n_seq, hs, D_SHIFT), F32),
        jax.ShapeDtypeStruct((n_seq, ph, D_POOL), F32),
    ]
    out_specs = [act(D_RWKV)] * 9 + [
        pl.BlockSpec((1, hs, D_SHIFT), seq3),
        pl.BlockSpec((1, ph, D_POOL), seq3),
    ]
    return pl.pallas_call(
        functools.partial(_prep_kernel, stride=stride, t0=t0, hs=hs),
        grid=(n_seq, tiles),
        in_specs=in_specs,
        out_specs=out_specs,
        out_shape=out_shape,
        scratch_shapes=[pltpu.VMEM((hs + rows, D_SHIFT), F32), pltpu.VMEM((ph + rows, D_POOL), F32)],
        compiler_params=pltpu.CompilerParams(
            dimension_semantics=("arbitrary", "arbitrary"), vmem_limit_bytes=VMEM_LIMIT),
        name="prep",
    )(x, sh_hist, pool_hist, *params)


def _stack(x, m0):
    return jnp.concatenate([jnp.where(m0, x, 0.0), jnp.where(m0, 0.0, x)], axis=0)


def _wkv_load_state(s0_ref, z_ref):
    zeros = jnp.zeros((HEAD, HEAD), F32)

    @pl.when(pl.program_id(1) == 0)
    def _():
        for p in range(N_PAIRS):
            top = jnp.concatenate([s0_ref[0, 2 * p], zeros], axis=1)
            bot = jnp.concatenate([zeros, s0_ref[0, 2 * p + 1]], axis=1)
            z_ref[p] = jnp.concatenate([top, bot], axis=0).T


def _wkv_store_state(z_ref, so_ref):
    @pl.when(pl.program_id(1) == pl.num_programs(1) - 1)
    def _():
        for p in range(N_PAIRS):
            s2 = z_ref[p].T
            so_ref[0, 2 * p] = s2[:HEAD, :HEAD]
            so_ref[0, 2 * p + 1] = s2[HEAD:, HEAD:]


def _wkv_tile(r_ref, k_ref, v_ref, lw_ref, a_ref, b_ref, tri_ref, mss_ref, msq_ref, y_ref, z_ref, *,
              n_chunks, fillers=()):
    fillers = list(fillers)
    fill = lambda: fillers.pop(0)() if fillers else None
    lane = lax.broadcasted_iota(jnp.int32, (CHUNK, LANES), 1)
    m0 = lane < HEAD
    strict = mss_ref[0][:, :LANES] > 0.5
    incl2 = mss_ref[1] > 0.5
    n_levels = mss_ref.shape[0] - 3
    lane_head4 = lax.broadcasted_iota(jnp.int32, (CHUNK, 2 * LANES), 1) // HEAD
    eye = msq_ref[0]
    same_head = msq_ref[1] > 0.5
    zblk = jnp.zeros((CHUNK, LANES), F32)
    zblk2 = jnp.zeros((2 * CHUNK, LANES), F32)

    scaled = []
    for ci in range(n_chunks):
        rows = slice(ci * CHUNK, (ci + 1) * CHUNK)
        lw = lw_ref[rows, :]
        cs = _mm(tri_ref[...], lw, NN, 1, 3)
        cs_end = cs[CHUNK - 1:CHUNK, :]
        g_inv = jnp.exp(-cs)
        g_tail = jnp.exp(cs_end - cs)
        scaled.append(dict(
            g_end=jnp.exp(cs_end),
            a_t=a_ref[rows, :] * jnp.exp(cs - lw), r_t=r_ref[rows, :] * jnp.exp(cs),
            b_t=b_ref[rows, :] * g_inv, k_t=k_ref[rows, :] * g_inv,
            b_h=b_ref[rows, :] * g_tail, k_h=k_ref[rows, :] * g_tail, v=v_ref[rows, :]))
    chains = [(ci, p) for ci in range(n_chunks) for p in range(N_PAIRS)]
    lanes = lambda p: slice(p * LANES, (p + 1) * LANES)
    pair = lambda name: [scaled[ci][name][:, lanes(p)] for ci, p in chains]
    each = lambda fn, *lists: [fn(*args) for args in zip(*lists)]
    cat0 = lambda *xs: jnp.concatenate(xs, axis=0)
    cat1 = lambda *xs: jnp.concatenate(xs, axis=1)
    at, rt, bt, kt, bh, kh, v = (pair(n) for n in ("a_t", "r_t", "b_t", "k_t", "b_h", "k_h", "v"))
    v2 = each(lambda x: _stack(x, m0), v)
    scores = each(lambda a, r, b, k: _mm(cat0(a, r), cat0(_stack(b, m0), _stack(k, m0)), NT, *P_A),
                  at, rt, bt, kt)
    a_ab = each(lambda s: jnp.where(strict, s[:CHUNK, :LANES], 0.0), scores)
    a_ak = each(lambda s: jnp.where(strict, s[:CHUNK, LANES:], 0.0), scores)
    a_r = each(lambda s: jnp.where(incl2, s[CHUNK:, :], 0.0), scores)
    fill()
    stack4 = lambda x: cat0(*[jnp.where(lane_head4 == h, x, 0.0) for h in range(4)])
    join = lambda xs: [cat1(xs[i], xs[i + 1]) for i in range(0, len(xs), 2)]
    split = lambda xs: [x[:, s] for x in xs for s in (slice(0, LANES), slice(LANES, 2 * LANES))]
    a_ab4 = join(a_ab)
    t_inv4 = each(lambda x: mss_ref[2] + x * mss_ref[3], a_ab4)
    for lvl in range(1, n_levels):
        inner = each(lambda x, t: _mm(x * mss_ref[3 + lvl], stack4(t), NN, *P_T), a_ab4, t_inv4)
        t_inv4 = each(lambda t, w: t + _mm(t, stack4(w), NN, *P_T), t_inv4, inner)
        fill()
    t_inv = split(t_inv4)
    akv = split(each(lambda x, y: _mm(x, stack4(y), NN, *P_U), join(a_ak), join(v)))
    pq = each(lambda t, x, y: _mm(t, cat1(_stack(x, m0), _stack(y, m0)), NN, *P_U), t_inv, at, akv)
    bpq = each(lambda b, k, x, y: _mm(cat0(b, k), cat0(x, cat1(zblk, y)), TN, *P_S), bh, kh, pq, v)
    g_end = [scaled[ci]["g_end"][:, lanes(p)] for ci, p in chains]
    m_t = each(lambda g, x: eye * g + jnp.where(same_head, x[:, :LANES], 0.0), g_end, bpq)
    n_t = each(lambda x: jnp.where(same_head, x[:, LANES:], 0.0), bpq)
    yy = each(lambda x, y, z: _mm(x, cat0(cat1(_stack(y[:, :LANES], m0), _stack(y[:, LANES:], m0)),
                                         cat1(zblk2, z)), NN, *P_Y), a_r, pq, v2)
    y1 = each(lambda r, x: r + x[:, :LANES], rt, yy)
    y2 = each(lambda x: x[:, LANES:], yy)

    state = [z_ref[p] for p in range(N_PAIRS)]
    for ci in range(n_chunks):
        for p in range(N_PAIRS):
            i = ci * N_PAIRS + p
            both = _mm(cat0(y1[i], m_t[i]), state[p], NN, *P_S)
            y_ref[ci * CHUNK:(ci + 1) * CHUNK, lanes(p)] = both[:CHUNK] + y2[i]
            state[p] = both[CHUNK:] + n_t[i]
    for p in range(N_PAIRS):
        z_ref[p] = state[p]
    while fillers:
        fill()


def _wkv_masks():
    ri = lax.broadcasted_iota(jnp.int32, (CHUNK, 4 * CHUNK), 0)
    ci = lax.broadcasted_iota(jnp.int32, (CHUNK, 4 * CHUNK), 1) % CHUNK
    side = [ci < ri, ci <= ri, ri == ci]
    m = 1
    while m < CHUNK:
        side.append(((ri // (2 * m)) == (ci // (2 * m))) & ((ri // m) % 2 == 1) & ((ci // m) % 2 == 0))
        m *= 2
    rq = lax.broadcasted_iota(jnp.int32, (LANES, LANES), 0)
    cq = lax.broadcasted_iota(jnp.int32, (LANES, LANES), 1)
    square = [rq == cq, (rq // HEAD) == (cq // HEAD)]
    tri = (lax.broadcasted_iota(jnp.int32, (CHUNK, CHUNK), 1)
           <= lax.broadcasted_iota(jnp.int32, (CHUNK, CHUNK), 0))
    return tri.astype(BF16), jnp.stack(side).astype(F32), jnp.stack(square).astype(F32)


def _short_pairs(t_s):
    idx = {}
    for kind, inclusive in (("ab", False), ("ak", False), ("rb", True), ("rk", True)):
        for t in range(t_s):
            for s in range(t + 1 if inclusive else t):
                idx[(kind, t, s)] = len(idx)
    return idx


def _short_scale_kernel(r_ref, k_ref, v_ref, lw_ref, a_ref, b_ref, g_ref, x_o, bk_o, as_o, g4_o, *, t_s, n_s):
    g_ones = g_ref[...]
    slab = lambda ref, t: ref[t * n_s:(t + 1) * n_s, :]
    cs = []
    for t in range(t_s):
        cs.append(slab(lw_ref, t) if t == 0 else cs[-1] + slab(lw_ref, t))
    a_t = [slab(a_ref, t) * (jnp.exp(cs[t - 1]) if t else 1.0) for t in range(t_s)]
    r_t = [slab(r_ref, t) * jnp.exp(cs[t]) for t in range(t_s)]
    g_inv = [jnp.exp(-cs[t]) for t in range(t_s)]
    b_t = [slab(b_ref, t) * g_inv[t] for t in range(t_s)]
    k_t = [slab(k_ref, t) * g_inv[t] for t in range(t_s)]
    for t in range(t_s):
        g_tail = jnp.exp(cs[t_s - 1] - cs[t])
        x_o[t] = a_t[t]
        x_o[t_s + t] = r_t[t]
        bk_o[t] = slab(b_ref, t) * g_tail
        bk_o[t_s + t] = slab(k_ref, t) * g_tail
    g4_o[...] = jnp.exp(cs[t_s - 1])
    left = {"ab": a_t, "ak": a_t, "rb": r_t, "rk": r_t}
    right = {"ab": b_t, "ak": k_t, "rb": b_t, "rk": k_t}
    for (kind, t, s), row in _short_pairs(t_s).items():
        as_o[row] = _seg_sum(left[kind][t] * right[kind][s], g_ones)


def _head_rows(x):
    return jnp.concatenate([x[:, h * HEAD:(h + 1) * HEAD] for h in range(N_HEADS)], axis=0)


def _short_state_kernel(s_ref, x_ref, bk_ref, as_ref, v_ref, g4_ref, so_ref, y_ref, *, t_s, n_blk):
    idx = _short_pairs(t_s)
    rows8 = 2 * t_s
    lane_head = lax.broadcasted_iota(jnp.int32, (rows8, D_RWKV), 1) // HEAD
    row8 = lax.broadcasted_iota(jnp.int32, (rows8, D_RWKV), 0)
    for i in range(n_blk):
        s_b = s_ref[i]
        x = x_ref[:, i, :]
        zt = _mm(_head_rows(x), s_b, NT, *P_S)
        zz = jnp.zeros((rows8, D_RWKV), F32)
        for h in range(N_HEADS):
            zz = jnp.where(lane_head == h, zt[h * rows8:(h + 1) * rows8, :], zz)
        sc = as_ref[:, i, :]
        scal = lambda kind, t, s: sc[idx[(kind, t, s)]:idx[(kind, t, s)] + 1, :]
        v = [v_ref[t, i:i + 1, :] for t in range(t_s)]
        u = []
        for t in range(t_s):
            acc = zz[t:t + 1, :]
            for s in range(t):
                acc = acc + scal("ab", t, s) * u[s] + scal("ak", t, s) * v[s]
            u.append(acc)
        for t in range(t_s):
            acc = zz[t_s + t:t_s + t + 1, :]
            for s in range(t + 1):
                acc = acc + scal("rb", t, s) * u[s] + scal("rk", t, s) * v[s]
            y_ref[t, i:i + 1, :] = acc
        uv = jnp.zeros((rows8, D_RWKV), F32)
        for j, vec in enumerate(u + v):
            uv = jnp.where(row8 == j, jnp.broadcast_to(vec, (rows8, D_RWKV)), uv)
        uv_exp = jnp.concatenate([jnp.where(lane_head == h, uv, 0.0) for h in range(N_HEADS)], axis=0)
        upd = _mm(uv_exp, _head_rows(bk_ref[:, i, :]), TN, *P_S)
        g4 = g4_ref[i:i + 1, :]
        g_rows = jnp.concatenate(
            [jnp.broadcast_to(g4[:, h * HEAD:(h + 1) * HEAD], (HEAD, HEAD)) for h in range(N_HEADS)], axis=0)
        so_ref[i] = s_b * g_rows + upd


def _short_wkv_call(seqs, s0, g_ones, *, t_s, n_s, n_blk):
    assert 2 * t_s == SUBLANES and n_s % n_blk == 0 and n_blk % SUBLANES == 0
    n_scal = len(_short_pairs(t_s))
    full = lambda shape: pl.BlockSpec(shape, lambda: tuple(0 for _ in shape))
    x8, bk8, scal, g4 = pl.pallas_call(
        functools.partial(_short_scale_kernel, t_s=t_s, n_s=n_s),
        in_specs=[full((t_s * n_s, D_RWKV))] * 6 + [full((MXU_DIM, MXU_DIM))],
        out_specs=[full((2 * t_s, n_s, D_RWKV)), full((2 * t_s, n_s, D_RWKV)),
                   full((n_scal, n_s, D_RWKV)), full((n_s, D_RWKV))],
        out_shape=[jax.ShapeDtypeStruct((2 * t_s, n_s, D_RWKV), F32),
                   jax.ShapeDtypeStruct((2 * t_s, n_s, D_RWKV), F32),
                   jax.ShapeDtypeStruct((n_scal, n_s, D_RWKV), F32),
                   jax.ShapeDtypeStruct((n_s, D_RWKV), F32)],
        compiler_params=pltpu.CompilerParams(vmem_limit_bytes=VMEM_LIMIT),
        name="wkv_short_scale",
    )(*seqs, g_ones)
    v3 = seqs[2].reshape(t_s, n_s, D_RWKV)
    st = pl.BlockSpec((n_blk, N_HEADS * HEAD, HEAD), lambda i: (i, 0, 0))
    tm = lambda n: pl.BlockSpec((n, n_blk, D_RWKV), lambda i: (0, i, 0))
    s_new, y = pl.pallas_call(
        functools.partial(_short_state_kernel, t_s=t_s, n_blk=n_blk),
        grid=(n_s // n_blk,),
        in_specs=[st, tm(2 * t_s), tm(2 * t_s), tm(n_scal), tm(t_s),
                  pl.BlockSpec((n_blk, D_RWKV), lambda i: (i, 0))],
        out_specs=[st, tm(t_s)],
        out_shape=[jax.ShapeDtypeStruct((n_s, N_HEADS * HEAD, HEAD), F32),
                   jax.ShapeDtypeStruct((t_s, n_s, D_RWKV), F32)],
        compiler_params=pltpu.CompilerParams(
            dimension_semantics=("arbitrary",), vmem_limit_bytes=VMEM_LIMIT),
        name="wkv_short_state",
    )(s0.reshape(n_s, N_HEADS * HEAD, HEAD), x8, bk8, scal, v3, g4)
    return y.reshape(t_s * n_s, D_RWKV), s_new.reshape(n_s, N_HEADS, HEAD, HEAD)


def _out_kernel(y_ref, bon_ref, sg_ref, op_ref, x_ref, gw_ref, gb_ref, wo_ref, nf_ref, g_ref, o_ref):
    g_ones = g_ref[...]
    y = y_ref[...]
    mu = _seg_sum(y, g_ones) * (1.0 / HEAD)
    d = y - mu
    var = _seg_sum(d * d, g_ones) * (1.0 / HEAD)
    yn = d * lax.rsqrt(var + GN_EPS) * gw_ref[...] + gb_ref[...]
    o_rwkv = (yn + bon_ref[...]) * sg_ref[...]
    out = (jnp.dot(o_rwkv.astype(BF16), wo_ref[0:D_RWKV, :], preferred_element_type=F32)
           + jnp.dot(op_ref[...].astype(BF16), wo_ref[D_RWKV:, :], preferred_element_type=F32))
    res = x_ref[...] + out
    o_ref[...] = res * lax.rsqrt(jnp.mean(res * res, axis=-1, keepdims=True) + NORM_EPS) * nf_ref[...]


def _out_call(y, bon, sg, op, x, gn_w, gn_b, w_out, norm_f, g_ones, *, rows):
    total = x.shape[0]
    row_blk = lambda i: (i, 0)
    const2 = lambda i: (0, 0)
    act = lambda w: pl.BlockSpec((rows, w), row_blk)
    return pl.pallas_call(
        _out_kernel,
        grid=(total // rows,),
        in_specs=[act(D_RWKV)] * 4 + [act(D_MODEL),
                                      pl.BlockSpec((1, D_RWKV), const2),
                                      pl.BlockSpec((1, D_RWKV), const2),
                                      pl.BlockSpec((D_RWKV + D_POOL, D_MODEL), const2),
                                      pl.BlockSpec((1, D_MODEL), const2),
                                      pl.BlockSpec((MXU_DIM, MXU_DIM), const2)],
        out_specs=act(D_MODEL),
        out_shape=jax.ShapeDtypeStruct((total, D_MODEL), F32),
        compiler_params=pltpu.CompilerParams(
            dimension_semantics=("arbitrary",), vmem_limit_bytes=VMEM_LIMIT),
        name="out",
    )(y, bon, sg, op, x, gn_w, gn_b, w_out, norm_f, g_ones)


N_PREP_PARAMS = 12
N_WKV_CONSTS = 3
N_OUT_PARAMS = 5


def _fused_kernel(*refs, n_chunks, t0):
    x_ref, sh_ref, ph_ref, s0_ref = refs[:4]
    pos = 4
    prep_params = refs[pos:pos + N_PREP_PARAMS]
    pos += N_PREP_PARAMS
    wkv_consts = refs[pos:pos + N_WKV_CONSTS]
    pos += N_WKV_CONSTS
    gw_ref, gb_ref, wo_ref, nf_ref, g_ref = refs[pos:pos + N_OUT_PARAMS]
    pos += N_OUT_PARAMS
    o_ref, nsh_o, npl_o, so_ref = refs[pos:pos + 4]
    pos += 4
    seq_s = refs[pos:pos + 6]
    bon_s, sg_s, op_s, y_s, p_ext, u_ext, z_ref = refs[pos + 6:]
    _wkv_load_state(s0_ref, z_ref)
    finish_prep = _prep_stages(x_ref, sh_ref, ph_ref, *prep_params, *seq_s, bon_s, sg_s, op_s, nsh_o,
                               npl_o, p_ext, u_ext, stride=1, t0=t0, hs=sh_ref.shape[1])
    _wkv_tile(*seq_s, *wkv_consts, y_s, z_ref, n_chunks=n_chunks, fillers=finish_prep)
    _out_kernel(y_s, bon_s, sg_s, op_s, x_ref, gw_ref, gb_ref, wo_ref, nf_ref, g_ref, o_ref)
    _wkv_store_state(z_ref, so_ref)


def _fused_call(x, sh_hist, pool_hist, s0, prep_params, out_params, *, n_seq, n_chunks, t0):
    assert len(prep_params) == N_PREP_PARAMS and len(out_params) == N_OUT_PARAMS
    total = x.shape[0]
    rows = n_chunks * CHUNK
    tiles = total // (n_seq * rows)
    hs = sh_hist.shape[1]
    assert CHUNK == HEAD and 2 * CHUNK == LANES
    wkv_consts = _wkv_masks()
    act = pl.BlockSpec((rows, D_MODEL), lambda b, j: (b * tiles + j, 0))
    per_seq = lambda a: pl.BlockSpec((1,) + a.shape[1:], lambda b, j: (b,) + (0,) * (a.ndim - 1))
    const = lambda a: pl.BlockSpec(a.shape, lambda b, j: (0,) * a.ndim, pipeline_mode=pl.Buffered(1))
    consts = tuple(prep_params) + tuple(wkv_consts) + tuple(out_params)
    seq_scratch = pltpu.VMEM((rows, D_RWKV), F32)
    return pl.pallas_call(
        functools.partial(_fused_kernel, n_chunks=n_chunks, t0=t0),
        grid=(n_seq, tiles),
        in_specs=[act, per_seq(sh_hist), per_seq(pool_hist), per_seq(s0)] + [const(a) for a in consts],
        out_specs=[act, per_seq(sh_hist), per_seq(pool_hist), per_seq(s0)],
        out_shape=[jax.ShapeDtypeStruct((total, D_MODEL), F32),
                   jax.ShapeDtypeStruct(sh_hist.shape, F32),
                   jax.ShapeDtypeStruct(pool_hist.shape, F32),
                   jax.ShapeDtypeStruct(s0.shape, F32)],
        scratch_shapes=[seq_scratch] * 10 + [pltpu.VMEM((hs + rows, D_SHIFT), F32),
                                             pltpu.VMEM((POOL_HIST + rows, D_POOL), F32),
                                             pltpu.VMEM((N_PAIRS, LANES, LANES), F32)],
        compiler_params=pltpu.CompilerParams(
            dimension_semantics=("arbitrary", "arbitrary"), vmem_limit_bytes=VMEM_LIMIT),
        name="layer_long",
    )(x, sh_hist, pool_hist, s0, *consts)


def _head_ones():
    i = lax.broadcasted_iota(jnp.int32, (MXU_DIM, MXU_DIM), 0) // HEAD
    j = lax.broadcasted_iota(jnp.int32, (MXU_DIM, MXU_DIM), 1) // HEAD
    return (i == j).astype(BF16)


def kernel(x_prompt, x_sample, state_shift, state_wkv, state_pool, norm_w, w_in, mu_shift, w_decay_b,
           w0, w_aaa_b, a0, k_k, k_a, r_k, gn_w, gn_b, pool_w, pool_scale, w_out, norm_f):
    depth = norm_w.shape[0]
    n_p, t_p, _ = x_prompt.shape
    n_s, t_s, _ = x_sample.shape
    assert depth == 1, "a stacked trunk needs a residual-only output kernel between layers"
    assert t_p % (WKV_CHUNKS_PER_STEP * CHUNK) == 0
    g_ones = _head_ones()
    row = lambda z: z.reshape(1, -1).astype(F32)

    hp = x_prompt.astype(F32).reshape(n_p * t_p, D_MODEL)
    hs = jnp.transpose(x_sample.astype(F32), (1, 0, 2)).reshape(t_s * n_s, D_MODEL)
    outs = [[] for _ in range(6)]
    for l in range(depth):
        zl = jnp.zeros((D_LORA, D_RWKV), F32)
        w_lora = jnp.concatenate([jnp.concatenate([w_decay_b[l], zl], axis=1),
                                  jnp.concatenate([zl, w_aaa_b[l]], axis=1)], axis=0).astype(BF16)
        prep_params = (row(norm_w[l]), w_in[l].astype(BF16), row(mu_shift[l]), w_lora, row(w0[l]),
                       row(a0[l]), row(k_k[l]), row(k_a[l]), row(r_k[l]), pool_w[l].astype(BF16),
                       row(pool_scale[l]), g_ones)
        out_params = (row(gn_w[l]), row(gn_b[l]), w_out[l].astype(BF16), row(norm_f), g_ones)

        hp, nsh, npl, s_fin = _fused_call(
            hp, jnp.zeros((n_p, SUBLANES, D_SHIFT), F32), jnp.zeros((n_p, POOL_HIST, D_POOL), F32),
            jnp.zeros((n_p, N_HEADS, HEAD, HEAD), F32), prep_params, out_params,
            n_seq=n_p, n_chunks=WKV_CHUNKS_PER_STEP, t0=0)
        outs[0].append(nsh[:, SUBLANES - 1])
        outs[1].append(s_fin)
        outs[2].append(npl[:, 1:])

        sh_hist = state_shift[l].astype(F32)[None]
        pool_hist = jnp.transpose(state_pool[l].astype(F32), (1, 0, 2))
        pool_hist = jnp.pad(pool_hist, ((1, 0), (0, 0), (0, 0))).reshape(1, POOL_HIST * n_s, D_POOL)
        res = _prep_call(hs, sh_hist, pool_hist, prep_params,
                         n_seq=1, rows=t_s * n_s, stride=n_s, t0=PAST_LEN)
        seqs, (bon, sg, op), nsh, npl = res[:6], res[6:9], res[9], res[10]

        y, s_fin = _short_wkv_call(seqs, state_wkv[l].astype(F32), g_ones, t_s=t_s, n_s=n_s,
                                   n_blk=SHORT_SEQS_PER_STEP)
        hs = _out_call(y, bon, sg, op, hs, *out_params, rows=t_s * n_s)
        outs[3].append(nsh[0])
        outs[4].append(s_fin)
        outs[5].append(jnp.transpose(npl[0, n_s:].reshape(POOL_HIST - 1, n_s, D_POOL), (1, 0, 2)))

    y_prompt = hp.reshape(n_p, t_p, D_MODEL).astype(x_prompt.dtype)
    y_sample = jnp.transpose(hs.reshape(t_s, n_s, D_MODEL), (1, 0, 2)).astype(x_sample.dtype)
    return (y_prompt, y_sample) + tuple(jnp.stack(o, axis=0) for o in outs)
```

```python
import functools

import jax
import jax.numpy as jnp
from jax import lax
from jax.experimental import pallas as pl
from jax.experimental.pallas import tpu as pltpu

F32 = jnp.float32
BF16 = jnp.bfloat16

D_MODEL = 1024
D_RWKV = 512
D_POOL = 512
HEAD = 64
N_HEADS = D_RWKV // HEAD
LANES = 128
SUBLANES = 8
MXU_DIM = 256
N_PAIRS = D_RWKV // LANES
D_LORA = 64
POOL_WINDOWS = (2, 4, 8, 16)
POOL_GROUP = D_POOL // len(POOL_WINDOWS)
POOL_HIST = max(POOL_WINDOWS)
D_SHIFT = 3 * D_RWKV + 2 * D_LORA
D_IN = D_SHIFT + D_RWKV + 2 * D_POOL
PAST_LEN = 16384
NORM_EPS = 1e-6
GN_EPS = 64e-5
L2_EPS = 1e-12
CHUNK = 64
VMEM_LIMIT = 56 * 1024 * 1024
P_A = (1, 1)
P_T = (1, 1)
P_U = (1, 1)
P_S = (1, 1)
P_Y = (1, 1)
WKV_CHUNKS_PER_STEP = 8

NN = (((1,), (0,)), ((), ()))
NT = (((1,), (1,)), ((), ()))
TN = (((0,), (0,)), ((), ()))


def _pieces(x, n):
    if x.dtype == BF16:
        return [x]
    out = []
    rem = x
    for i in range(n):
        p = rem.astype(BF16)
        out.append(p)
        if i + 1 < n:
            rem = rem - p.astype(F32)
    return out


def _mm(a, b, dims=NN, na=1, nb=1):
    ap = _pieces(a, na)
    bp = _pieces(b, nb)
    order = max(len(ap), len(bp))
    acc = None
    for i, x in enumerate(ap):
        for j, y in enumerate(bp):
            if i + j >= order:
                continue
            t = lax.dot_general(x, y, dims, preferred_element_type=F32)
            acc = t if acc is None else acc + t
    return acc


def _seg_sum(x, g_ones, pieces=1):
    cols = []
    for s in range(x.shape[1] // MXU_DIM):
        cols.append(_mm(x[:, s * MXU_DIM:(s + 1) * MXU_DIM], g_ones, NN, pieces, 1))
    return cols[0] if len(cols) == 1 else jnp.concatenate(cols, axis=1)


def _sigmoid(x):
    return 1.0 / (1.0 + jnp.exp(-x))


def _norm_operands(x_ref, nw_ref):
    x = x_ref[...]
    rstd = lax.rsqrt(jnp.mean(x * x, axis=-1, keepdims=True) + NORM_EPS)
    return (x * nw_ref[...]).astype(BF16), rstd


def _prep_stages(h, rstd, win_ref, j, sh_ref, ph_ref, mu_ref, wl_ref, w0_ref, a0_ref, kk_ref,
                 ka_ref, rk_ref, pw_ref, ps_ref, g_ref,
                 r_o, k_o, v_o, lw_o, a_o, b_o, bon_o, sg_o, op_o, nsh_o, npl_o,
                 p_ext, u_ext, *, stride, t0, hs):
    rows = h.shape[0]
    ph = POOL_HIST * stride
    width = MXU_DIM

    @pl.when(j == 0)
    def _():
        p_ext[0:hs, :] = sh_ref[0]
        u_ext[0:ph, :] = ph_ref[0]

    def proj_cols(c0, c1):
        return jnp.dot(h, win_ref[:, c0:c1], preferred_element_type=F32) * rstd

    def shifted(c0, c1):
        p = proj_cols(c0, c1)
        p_ext[hs:hs + rows, c0:c1] = p
        prev = p_ext[pl.ds(hs - stride, rows), c0:c1]
        return p + mu_ref[:, c0:c1] * (prev - p)

    xwa = shifted(3 * D_RWKV, D_SHIFT)
    lane = lax.broadcasted_iota(jnp.int32, xwa.shape, 1)
    lora_in = jnp.where(lane < D_LORA, jnp.tanh(xwa), xwa)
    lora = jnp.dot(lora_in.astype(BF16), wl_ref[...], preferred_element_type=F32)
    lw_o[...] = (-0.6065306597126334) * _sigmoid(w0_ref[...] + lora[:, :D_RWKV])
    alpha = _sigmoid(a0_ref[...] + lora[:, D_RWKV:])
    for c in range(0, D_RWKV, width):
        sl = slice(c, c + width)
        r_o[:, sl] = shifted(c, c + width)
        k = shifted(D_RWKV + c, D_RWKV + c + width)
        k_o[:, sl] = k * (1.0 + (alpha[:, sl] - 1.0) * ka_ref[:, sl])
        kk = k * kk_ref[:, sl]
        kk = kk * lax.rsqrt(_seg_sum(kk * kk, g_ref[...]) + L2_EPS)
        a_o[:, sl] = -kk
        b_o[:, sl] = kk * alpha[:, sl]
        v_o[:, sl] = shifted(2 * D_RWKV + c, 2 * D_RWKV + c + width)
    last_p = p_ext[rows:rows + hs, :]
    p_ext[0:hs, :] = last_p
    nsh_o[0] = last_p
    for c in range(0, D_RWKV, width):
        g_rwkv = proj_cols(D_SHIFT + c, D_SHIFT + c + width)
        sg_o[:, c:c + width] = g_rwkv * _sigmoid(g_rwkv)
    for c in range(0, D_POOL, width):
        u_ext[ph:ph + rows, c:c + width] = proj_cols(D_SHIFT + D_RWKV + c, D_SHIFT + D_RWKV + c + width)
        op_o[:, c:c + width] = proj_cols(D_SHIFT + D_RWKV + D_POOL + c,
                                         D_SHIFT + D_RWKV + D_POOL + c + width)

    def bonus():
        bon_o[...] = _seg_sum(r_o[...] * k_o[...] * rk_ref[...], g_ref[...]) * v_o[...]

    def pool(gi, win):
        sl = slice(gi * POOL_GROUP, (gi + 1) * POOL_GROUP)
        row = lax.broadcasted_iota(jnp.int32, (rows, 1), 0)
        pos = t0 + (j * rows + row) // stride
        u = u_ext[ph:ph + rows, sl]
        tot = u
        for back in range(1, win):
            tot = tot + u_ext[pl.ds(ph - back * stride, rows), sl]
        inv_cnt = 1.0 / jnp.minimum(pos + 1, win).astype(F32)
        d = tot * inv_cnt - u
        o = jnp.dot(d.astype(BF16), pw_ref[gi], preferred_element_type=F32)
        gp = op_o[:, sl]
        op_o[:, sl] = o * ps_ref[:, sl] * (gp * _sigmoid(gp))

    def carry_pool():
        last_u = u_ext[rows:rows + ph, :]
        u_ext[0:ph, :] = last_u
        npl_o[0] = last_u

    pools = [functools.partial(pool, gi, win) for gi, win in enumerate(POOL_WINDOWS)]
    return [bonus] + pools + [carry_pool]


def _prep_kernel(x_ref, sh_ref, ph_ref, nw_ref, win_ref, *rest, **static):
    h, rstd = _norm_operands(x_ref, nw_ref)
    for finish in _prep_stages(h, rstd, win_ref, pl.program_id(1), sh_ref, ph_ref, *rest, **static):
        finish()


def _prep_call(x, sh_hist, pool_hist, params, *, n_seq, rows, stride, t0):
    total = x.shape[0]
    tiles = total // (n_seq * rows)
    hs = sh_hist.shape[1]
    ph = POOL_HIST * stride
    row_blk = lambda b, j: (b * tiles + j, 0)
    const2 = lambda b, j: (0, 0)
    seq3 = lambda b, j: (b, 0, 0)
    act = lambda w: pl.BlockSpec((rows, w), row_blk)
    in_specs = [
        act(D_MODEL),
        pl.BlockSpec((1, hs, D_SHIFT), seq3),
        pl.BlockSpec((1, ph, D_POOL), seq3),
        pl.BlockSpec((1, D_MODEL), const2),
        pl.BlockSpec((D_MODEL, D_IN), const2),
        pl.BlockSpec((1, D_SHIFT), const2),
        pl.BlockSpec((2 * D_LORA, 2 * D_RWKV), const2),
        pl.BlockSpec((1, D_RWKV), const2),
        pl.BlockSpec((1, D_RWKV), const2),
        pl.BlockSpec((1, D_RWKV), const2),
        pl.BlockSpec((1, D_RWKV), const2),
        pl.BlockSpec((1, D_RWKV), const2),
        pl.BlockSpec((len(POOL_WINDOWS), POOL_GROUP, POOL_GROUP), lambda b, j: (0, 0, 0)),
        pl.BlockSpec((1, D_POOL), const2),
        pl.BlockSpec((MXU_DIM, MXU_DIM), const2),
    ]
    out_shape = [jax.ShapeDtypeStruct((total, D_RWKV), F32)] * 9 + [
        jax.ShapeDtypeStruct((n_seq, hs, D_SHIFT), F32),
        jax.ShapeDtypeStruct((n_seq, ph, D_POOL), F32),
    ]
    out_specs = [act(D_RWKV)] * 9 + [
        pl.BlockSpec((1, hs, D_SHIFT), seq3),
        pl.BlockSpec((1, ph, D_POOL), seq3),
    ]
    return pl.pallas_call(
        functools.partial(_prep_kernel, stride=stride, t0=t0, hs=hs),
        grid=(n_seq, tiles),
        in_specs=in_specs,
        out_specs=out_specs,
        out_shape=out_shape,
        scratch_shapes=[pltpu.VMEM((hs + rows, D_SHIFT), F32), pltpu.VMEM((ph + rows, D_POOL), F32)],
        compiler_params=pltpu.CompilerParams(
            dimension_semantics=("arbitrary", "arbitrary"), vmem_limit_bytes=VMEM_LIMIT),
        name="prep",
    )(x, sh_hist, pool_hist, *params)


def _stack(x, m0):
    return jnp.concatenate([jnp.where(m0, x, 0.0), jnp.where(m0, 0.0, x)], axis=0)


def _wkv_load_state(s0_ref, z_ref, is_first):
    zeros = jnp.zeros((HEAD, HEAD), F32)

    @pl.when(is_first)
    def _():
        for p in range(N_PAIRS):
            top = jnp.concatenate([s0_ref[0, 2 * p], zeros], axis=1)
            bot = jnp.concatenate([zeros, s0_ref[0, 2 * p + 1]], axis=1)
            z_ref[p] = jnp.concatenate([top, bot], axis=0).T


def _wkv_store_state(z_ref, so_ref, is_last):
    @pl.when(is_last)
    def _():
        for p in range(N_PAIRS):
            s2 = z_ref[p].T
            so_ref[0, 2 * p] = s2[:HEAD, :HEAD]
            so_ref[0, 2 * p + 1] = s2[HEAD:, HEAD:]


def _wkv_tile(r_ref, k_ref, v_ref, lw_ref, a_ref, b_ref, tri_ref, mss_ref, msq_ref, y_ref, z_ref, *,
              n_chunks, fillers=()):
    fillers = list(fillers)
    fill = lambda: fillers.pop(0)() if fillers else None
    lane = lax.broadcasted_iota(jnp.int32, (CHUNK, LANES), 1)
    m0 = lane < HEAD
    strict = mss_ref[0][:, :LANES] > 0.5
    incl2 = mss_ref[1] > 0.5
    n_levels = mss_ref.shape[0] - 3
    lane_head4 = lax.broadcasted_iota(jnp.int32, (CHUNK, 2 * LANES), 1) // HEAD
    eye = msq_ref[0]
    same_head = msq_ref[1] > 0.5
    zblk = jnp.zeros((CHUNK, LANES), F32)
    zblk2 = jnp.zeros((2 * CHUNK, LANES), F32)

    scaled = []
    for ci in range(n_chunks):
        rows = slice(ci * CHUNK, (ci + 1) * CHUNK)
        lw = lw_ref[rows, :]
        cs = _mm(tri_ref[...], lw, NN, 1, 3)
        cs_end = cs[CHUNK - 1:CHUNK, :]
        g_inv = jnp.exp(-cs)
        g_tail = jnp.exp(cs_end - cs)
        scaled.append(dict(
            g_end=jnp.exp(cs_end),
            a_t=a_ref[rows, :] * jnp.exp(cs - lw), r_t=r_ref[rows, :] * jnp.exp(cs),
            b_t=b_ref[rows, :] * g_inv, k_t=k_ref[rows, :] * g_inv,
            b_h=b_ref[rows, :] * g_tail, k_h=k_ref[rows, :] * g_tail, v=v_ref[rows, :]))
    chains = [(ci, p) for ci in range(n_chunks) for p in range(N_PAIRS)]
    lanes = lambda p: slice(p * LANES, (p + 1) * LANES)
    pair = lambda name: [scaled[ci][name][:, lanes(p)] for ci, p in chains]
    each = lambda fn, *lists: [fn(*args) for args in zip(*lists)]
    cat0 = lambda *xs: jnp.concatenate(xs, axis=0)
    cat1 = lambda *xs: jnp.concatenate(xs, axis=1)
    at, rt, bt, kt, bh, kh, v = (pair(n) for n in ("a_t", "r_t", "b_t", "k_t", "b_h", "k_h", "v"))
    v2 = each(lambda x: _stack(x, m0), v)
    scores = each(lambda a, r, b, k: _mm(cat0(a, r), cat0(_stack(b, m0), _stack(k, m0)), NT, *P_A),
                  at, rt, bt, kt)
    a_ab = each(lambda s: jnp.where(strict, s[:CHUNK, :LANES], 0.0), scores)
    a_ak = each(lambda s: jnp.where(strict, s[:CHUNK, LANES:], 0.0), scores)
    a_r = each(lambda s: jnp.where(incl2, s[CHUNK:, :], 0.0), scores)
    fill()
    stack4 = lambda x: cat0(*[jnp.where(lane_head4 == h, x, 0.0) for h in range(4)])
    join = lambda xs: [cat1(xs[i], xs[i + 1]) for i in range(0, len(xs), 2)]
    split = lambda xs: [x[:, s] for x in xs for s in (slice(0, LANES), slice(LANES, 2 * LANES))]
    a_ab4 = join(a_ab)
    t_inv4 = each(lambda x: mss_ref[2] + x * mss_ref[3], a_ab4)
    for lvl in range(1, n_levels):
        inner = each(lambda x, t: _mm(x * mss_ref[3 + lvl], stack4(t), NN, *P_T), a_ab4, t_inv4)
        t_inv4 = each(lambda t, w: t + _mm(t, stack4(w), NN, *P_T), t_inv4, inner)
        fill()
    t_inv = split(t_inv4)
    akv = split(each(lambda x, y: _mm(x, stack4(y), NN, *P_U), join(a_ak), join(v)))
    pq = each(lambda t, x, y: _mm(t, cat1(_stack(x, m0), _stack(y, m0)), NN, *P_U), t_inv, at, akv)
    bpq = each(lambda b, k, x, y: _mm(cat0(b, k), cat0(x, cat1(zblk, y)), TN, *P_S), bh, kh, pq, v)
    g_end = [scaled[ci]["g_end"][:, lanes(p)] for ci, p in chains]
    m_t = each(lambda g, x: eye * g + jnp.where(same_head, x[:, :LANES], 0.0), g_end, bpq)
    n_t = each(lambda x: jnp.where(same_head, x[:, LANES:], 0.0), bpq)
    yy = each(lambda x, y, z: _mm(x, cat0(cat1(_stack(y[:, :LANES], m0), _stack(y[:, LANES:], m0)),
                                         cat1(zblk2, z)), NN, *P_Y), a_r, pq, v2)
    y1 = each(lambda r, x: r + x[:, :LANES], rt, yy)
    y2 = each(lambda x: x[:, LANES:], yy)

    state = [z_ref[p] for p in range(N_PAIRS)]
    for ci in range(n_chunks):
        for p in range(N_PAIRS):
            i = ci * N_PAIRS + p
            both = _mm(cat0(y1[i], m_t[i]), state[p], NN, *P_S)
            y_ref[ci * CHUNK:(ci + 1) * CHUNK, lanes(p)] = both[:CHUNK] + y2[i]
            state[p] = both[CHUNK:] + n_t[i]
    for p in range(N_PAIRS):
        z_ref[p] = state[p]
    while fillers:
        fill()


def _wkv_masks():
    ri = lax.broadcasted_iota(jnp.int32, (CHUNK, 4 * CHUNK), 0)
    ci = lax.broadcasted_iota(jnp.int32, (CHUNK, 4 * CHUNK), 1) % CHUNK
    side = [ci < ri, ci <= ri, ri == ci]
    m = 1
    while m < CHUNK:
        side.append(((ri // (2 * m)) == (ci // (2 * m))) & ((ri // m) % 2 == 1) & ((ci // m) % 2 == 0))
        m *= 2
    rq = lax.broadcasted_iota(jnp.int32, (LANES, LANES), 0)
    cq = lax.broadcasted_iota(jnp.int32, (LANES, LANES), 1)
    square = [rq == cq, (rq // HEAD) == (cq // HEAD)]
    tri = (lax.broadcasted_iota(jnp.int32, (CHUNK, CHUNK), 1)
           <= lax.broadcasted_iota(jnp.int32, (CHUNK, CHUNK), 0))
    return tri.astype(BF16), jnp.stack(side).astype(F32), jnp.stack(square).astype(F32)


def _short_pairs(t_s):
    idx = {}
    for kind, inclusive in (("ab", False), ("ak", False), ("rb", True), ("rk", True)):
        for t in range(t_s):
            for s in range(t + 1 if inclusive else t):
                idx[(kind, t, s)] = len(idx)
    return idx


def _short_scale_kernel(r_ref, k_ref, v_ref, lw_ref, a_ref, b_ref,
                        xa_o, xr_o, bh_o, kh_o, vt_o, g4_o, as_o, *, t_s, n_s):
    slab = lambda ref, t: ref[t * n_s:(t + 1) * n_s, :].T
    cs = []
    for t in range(t_s):
        cs.append(slab(lw_ref, t) if t == 0 else cs[-1] + slab(lw_ref, t))
    a_t = [slab(a_ref, t) * (jnp.exp(cs[t - 1]) if t else 1.0) for t in range(t_s)]
    r_t = [slab(r_ref, t) * jnp.exp(cs[t]) for t in range(t_s)]
    b_raw = [slab(b_ref, t) for t in range(t_s)]
    k_raw = [slab(k_ref, t) for t in range(t_s)]
    g_inv = [jnp.exp(-cs[t]) for t in range(t_s)]
    b_t = [b_raw[t] * g_inv[t] for t in range(t_s)]
    k_t = [k_raw[t] * g_inv[t] for t in range(t_s)]
    for t in range(t_s):
        g_tail = jnp.exp(cs[t_s - 1] - cs[t])
        xa_o[t] = a_t[t]
        xr_o[t] = r_t[t]
        bh_o[t] = b_raw[t] * g_tail
        kh_o[t] = k_raw[t] * g_tail
        vt_o[t] = slab(v_ref, t)
    g4_o[...] = jnp.exp(cs[t_s - 1])
    head_sum = lambda x: jnp.sum(x.reshape(N_HEADS, HEAD, n_s), axis=1)
    left = {"ab": a_t, "ak": a_t, "rb": r_t, "rk": r_t}
    right = {"ab": b_t, "ak": k_t, "rb": b_t, "rk": k_t}
    for (kind, t, s), row in _short_pairs(t_s).items():
        as_o[row] = head_sum(left[kind][t] * right[kind][s])


def _short_state_kernel(s_ref, xa_ref, xr_ref, bh_ref, kh_ref, vt_ref, g4_ref, as_ref, so_ref, y_ref, *,
                        t_s):
    idx = _short_pairs(t_s)
    head = pl.program_id(0)
    scal = {key: as_ref[row, pl.ds(head, 1), :] for key, row in idx.items()}

    def value_row(v, carry):
        s_v = s_ref[0, v]
        vv = [vt_ref[t, pl.ds(v, 1), :] for t in range(t_s)]
        u = []
        for t in range(t_s):
            acc = jnp.sum(s_v * xa_ref[t], axis=0, keepdims=True)
            for s in range(t):
                acc = acc + scal[("ab", t, s)] * u[s] + scal[("ak", t, s)] * vv[s]
            u.append(acc)
        for t in range(t_s):
            acc = jnp.sum(s_v * xr_ref[t], axis=0, keepdims=True)
            for s in range(t + 1):
                acc = acc + scal[("rb", t, s)] * u[s] + scal[("rk", t, s)] * vv[s]
            y_ref[t, pl.ds(v, 1), :] = acc
        new = s_v * g4_ref[...]
        for s in range(t_s):
            new = new + u[s] * bh_ref[s] + vv[s] * kh_ref[s]
        so_ref[0, v] = new
        return carry

    lax.fori_loop(0, HEAD, value_row, 0, unroll=2)


def _short_wkv_call(seqs, s0, *, t_s, n_s):
    assert n_s % LANES == 0
    n_scal = len(_short_pairs(t_s))
    full = lambda shape: pl.BlockSpec(shape, lambda: tuple(0 for _ in shape))
    vec = (t_s, D_RWKV, n_s)
    xa, xr, bh, kh, vt, g4, scal = pl.pallas_call(
        functools.partial(_short_scale_kernel, t_s=t_s, n_s=n_s),
        in_specs=[full((t_s * n_s, D_RWKV))] * 6,
        out_specs=[full(vec)] * 5 + [full((D_RWKV, n_s)), full((n_scal, N_HEADS, n_s))],
        out_shape=[jax.ShapeDtypeStruct(vec, F32)] * 5 + [jax.ShapeDtypeStruct((D_RWKV, n_s), F32),
                                                          jax.ShapeDtypeStruct((n_scal, N_HEADS, n_s), F32)],
        compiler_params=pltpu.CompilerParams(vmem_limit_bytes=VMEM_LIMIT),
        name="wkv_short_scale",
    )(*seqs)
    st = pl.BlockSpec((1, HEAD, HEAD, n_s), lambda h: (h, 0, 0, 0))
    per_head = pl.BlockSpec((t_s, HEAD, n_s), lambda h: (0, h, 0))
    s_new, y = pl.pallas_call(
        functools.partial(_short_state_kernel, t_s=t_s),
        grid=(N_HEADS,),
        in_specs=[st] + [per_head] * 5 + [pl.BlockSpec((HEAD, n_s), lambda h: (h, 0)),
                                          pl.BlockSpec((n_scal, N_HEADS, n_s), lambda h: (0, 0, 0))],
        out_specs=[st, per_head],
        out_shape=[jax.ShapeDtypeStruct(s0.shape, F32), jax.ShapeDtypeStruct(vec, F32)],
        compiler_params=pltpu.CompilerParams(
            dimension_semantics=("arbitrary",), vmem_limit_bytes=VMEM_LIMIT),
        name="wkv_short_state",
    )(s0, xa, xr, bh, kh, vt, g4, scal)
    return y, s_new


def _out_kernel(y_ref, bon_ref, sg_ref, op_ref, x_ref, gw_ref, gb_ref, wo_ref, nf_ref, g_ref, o_ref):
    g_ones = g_ref[...]
    y = y_ref[...]
    mu = _seg_sum(y, g_ones) * (1.0 / HEAD)
    d = y - mu
    var = _seg_sum(d * d, g_ones) * (1.0 / HEAD)
    yn = d * lax.rsqrt(var + GN_EPS) * gw_ref[...] + gb_ref[...]
    o_rwkv = (yn + bon_ref[...]) * sg_ref[...]
    out = (jnp.dot(o_rwkv.astype(BF16), wo_ref[0:D_RWKV, :], preferred_element_type=F32)
           + jnp.dot(op_ref[...].astype(BF16), wo_ref[D_RWKV:, :], preferred_element_type=F32))
    res = x_ref[...] + out
    o_ref[...] = res * lax.rsqrt(jnp.mean(res * res, axis=-1, keepdims=True) + NORM_EPS) * nf_ref[...]


def _out_call(y, bon, sg, op, x, gn_w, gn_b, w_out, norm_f, g_ones, *, rows):
    total = x.shape[0]
    row_blk = lambda i: (i, 0)
    const2 = lambda i: (0, 0)
    act = lambda w: pl.BlockSpec((rows, w), row_blk)
    return pl.pallas_call(
        _out_kernel,
        grid=(total // rows,),
        in_specs=[act(D_RWKV)] * 4 + [act(D_MODEL),
                                      pl.BlockSpec((1, D_RWKV), const2),
                                      pl.BlockSpec((1, D_RWKV), const2),
                                      pl.BlockSpec((D_RWKV + D_POOL, D_MODEL), const2),
                                      pl.BlockSpec((1, D_MODEL), const2),
                                      pl.BlockSpec((MXU_DIM, MXU_DIM), const2)],
        out_specs=act(D_MODEL),
        out_shape=jax.ShapeDtypeStruct((total, D_MODEL), F32),
        compiler_params=pltpu.CompilerParams(
            dimension_semantics=("arbitrary",), vmem_limit_bytes=VMEM_LIMIT),
        name="out",
    )(y, bon, sg, op, x, gn_w, gn_b, w_out, norm_f, g_ones)


N_PREP_PARAMS = 12
N_WKV_CONSTS = 3
N_OUT_PARAMS = 5


def _fused_kernel(*refs, n_chunks, t0):
    x_ref, sh_ref, ph_ref, s0_ref, nw_ref, win_ref = refs[:6]
    pos = 4 + N_PREP_PARAMS
    prep_rest = refs[6:pos]
    wkv_consts = refs[pos:pos + N_WKV_CONSTS]
    pos += N_WKV_CONSTS
    gw_ref, gb_ref, wo_ref, nf_ref, g_ref = refs[pos:pos + N_OUT_PARAMS]
    pos += N_OUT_PARAMS
    o_ref, nsh_o, npl_o, so_ref = refs[pos:pos + 4]
    pos += 4
    seq_s = refs[pos:pos + 6]
    bon_s, sg_s, op_s, y_s, p_ext, u_ext, z_ref = refs[pos + 6:]
    j = pl.program_id(1)
    _wkv_load_state(s0_ref, z_ref, j == 0)
    finish_prep = _prep_stages(*_norm_operands(x_ref, nw_ref), win_ref, j, sh_ref, ph_ref, *prep_rest,
                               *seq_s, bon_s, sg_s, op_s, nsh_o, npl_o, p_ext, u_ext,
                               stride=1, t0=t0, hs=sh_ref.shape[1])
    _wkv_tile(*seq_s, *wkv_consts, y_s, z_ref, n_chunks=n_chunks, fillers=finish_prep)
    _out_kernel(y_s, bon_s, sg_s, op_s, x_ref, gw_ref, gb_ref, wo_ref, nf_ref, g_ref, o_ref)
    _wkv_store_state(z_ref, so_ref, j == pl.num_programs(1) - 1)


def _fused_call(x, sh_hist, pool_hist, s0, prep_params, out_params, *, n_seq, n_chunks, t0):
    assert len(prep_params) == N_PREP_PARAMS and len(out_params) == N_OUT_PARAMS
    total = x.shape[0]
    rows = n_chunks * CHUNK
    tiles = total // (n_seq * rows)
    hs = sh_hist.shape[1]
    assert CHUNK == HEAD and 2 * CHUNK == LANES
    wkv_consts = _wkv_masks()
    act = pl.BlockSpec((rows, D_MODEL), lambda b, j: (b * tiles + j, 0))
    per_seq = lambda a: pl.BlockSpec((1,) + a.shape[1:], lambda b, j: (b,) + (0,) * (a.ndim - 1))
    const = lambda a: pl.BlockSpec(a.shape, lambda b, j: (0,) * a.ndim, pipeline_mode=pl.Buffered(1))
    consts = tuple(prep_params) + tuple(wkv_consts) + tuple(out_params)
    seq_scratch = pltpu.VMEM((rows, D_RWKV), F32)
    return pl.pallas_call(
        functools.partial(_fused_kernel, n_chunks=n_chunks, t0=t0),
        grid=(n_seq, tiles),
        in_specs=[act, per_seq(sh_hist), per_seq(pool_hist), per_seq(s0)] + [const(a) for a in consts],
        out_specs=[act, per_seq(sh_hist), per_seq(pool_hist), per_seq(s0)],
        out_shape=[jax.ShapeDtypeStruct((total, D_MODEL), F32),
                   jax.ShapeDtypeStruct(sh_hist.shape, F32),
                   jax.ShapeDtypeStruct(pool_hist.shape, F32),
                   jax.ShapeDtypeStruct(s0.shape, F32)],
        scratch_shapes=[seq_scratch] * 10 + [pltpu.VMEM((hs + rows, D_SHIFT), F32),
                                             pltpu.VMEM((POOL_HIST + rows, D_POOL), F32),
                                             pltpu.VMEM((N_PAIRS, LANES, LANES), F32)],
        compiler_params=pltpu.CompilerParams(
            dimension_semantics=("arbitrary", "arbitrary"), vmem_limit_bytes=VMEM_LIMIT),
        name="layer_long",
    )(x, sh_hist, pool_hist, s0, *consts)


def _head_ones():
    i = lax.broadcasted_iota(jnp.int32, (MXU_DIM, MXU_DIM), 0) // HEAD
    j = lax.broadcasted_iota(jnp.int32, (MXU_DIM, MXU_DIM), 1) // HEAD
    return (i == j).astype(BF16)


def kernel(x_prompt, x_sample, state_shift, state_wkv, state_pool, norm_w, w_in, mu_shift, w_decay_b,
           w0, w_aaa_b, a0, k_k, k_a, r_k, gn_w, gn_b, pool_w, pool_scale, w_out, norm_f):
    depth = norm_w.shape[0]
    n_p, t_p, _ = x_prompt.shape
    n_s, t_s, _ = x_sample.shape
    assert depth == 1, "a stacked trunk needs a residual-only output kernel between layers"
    assert t_p % (WKV_CHUNKS_PER_STEP * CHUNK) == 0
    g_ones = _head_ones()
    row = lambda z: z.reshape(1, -1).astype(F32)

    hp = x_prompt.astype(F32).reshape(n_p * t_p, D_MODEL)
    hs = jnp.transpose(x_sample.astype(F32), (1, 0, 2)).reshape(t_s * n_s, D_MODEL)
    outs = [[] for _ in range(6)]
    for l in range(depth):
        zl = jnp.zeros((D_LORA, D_RWKV), F32)
        w_lora = jnp.concatenate([jnp.concatenate([w_decay_b[l], zl], axis=1),
                                  jnp.concatenate([zl, w_aaa_b[l]], axis=1)], axis=0).astype(BF16)
        prep_params = (row(norm_w[l]), w_in[l].astype(BF16), row(mu_shift[l]), w_lora, row(w0[l]),
                       row(a0[l]), row(k_k[l]), row(k_a[l]), row(r_k[l]), pool_w[l].astype(BF16),
                       row(pool_scale[l]), g_ones)
        out_params = (row(gn_w[l]), row(gn_b[l]), w_out[l].astype(BF16), row(norm_f), g_ones)

        hp, nsh, npl, s_fin = _fused_call(
            hp, jnp.zeros((n_p, SUBLANES, D_SHIFT), F32), jnp.zeros((n_p, POOL_HIST, D_POOL), F32),
            jnp.zeros((n_p, N_HEADS, HEAD, HEAD), F32), prep_params, out_params,
            n_seq=n_p, n_chunks=WKV_CHUNKS_PER_STEP, t0=0)
        outs[0].append(nsh[:, SUBLANES - 1])
        outs[1].append(s_fin)
        outs[2].append(npl[:, 1:])

        sh_hist = state_shift[l].astype(F32)[None]
        pool_hist = jnp.transpose(state_pool[l].astype(F32), (1, 0, 2))
        pool_hist = jnp.pad(pool_hist, ((1, 0), (0, 0), (0, 0))).reshape(1, POOL_HIST * n_s, D_POOL)
        res = _prep_call(hs, sh_hist, pool_hist, prep_params,
                         n_seq=1, rows=t_s * n_s, stride=n_s, t0=PAST_LEN)
        seqs, (bon, sg, op), nsh, npl = res[:6], res[6:9], res[9], res[10]

        y, s_fin = _short_wkv_call(seqs, jnp.transpose(state_wkv[l].astype(F32), (1, 2, 3, 0)),
                                   t_s=t_s, n_s=n_s)
        y = jnp.transpose(y, (0, 2, 1)).reshape(t_s * n_s, D_RWKV)
        s_fin = jnp.transpose(s_fin, (3, 0, 1, 2))
        hs = _out_call(y, bon, sg, op, hs, *out_params, rows=t_s * n_s)
        outs[3].append(nsh[0])
        outs[4].append(s_fin)
        outs[5].append(jnp.transpose(npl[0, n_s:].reshape(POOL_HIST - 1, n_s, D_POOL), (1, 0, 2)))

    y_prompt = hp.reshape(n_p, t_p, D_MODEL).astype(x_prompt.dtype)
    y_sample = jnp.transpose(hs.reshape(t_s, n_s, D_MODEL), (1, 0, 2)).astype(x_sample.dtype)
    return (y_prompt, y_sample) + tuple(jnp.stack(o, axis=0) for o in outs)
```

```python
import functools

import jax
import jax.numpy as jnp
from jax import lax
from jax.experimental import pallas as pl
from jax.experimental.pallas import tpu as pltpu

F32 = jnp.float32
BF16 = jnp.bfloat16

D_MODEL = 1024
D_RWKV = 512
D_POOL = 512
HEAD = 64
N_HEADS = D_RWKV // HEAD
LANES = 128
SUBLANES = 8
MXU_DIM = 256
N_PAIRS = D_RWKV // LANES
D_LORA = 64
POOL_WINDOWS = (2, 4, 8, 16)
POOL_GROUP = D_POOL // len(POOL_WINDOWS)
POOL_HIST = max(POOL_WINDOWS)
D_SHIFT = 3 * D_RWKV + 2 * D_LORA
D_IN = D_SHIFT + D_RWKV + 2 * D_POOL
PAST_LEN = 16384
NORM_EPS = 1e-6
GN_EPS = 64e-5
L2_EPS = 1e-12
CHUNK = 64
VMEM_LIMIT = 56 * 1024 * 1024
P_A = (1, 1)
P_T = (1, 1)
P_U = (1, 1)
P_S = (1, 1)
P_Y = (1, 1)
P_CUMSUM = 2
WKV_CHUNKS_PER_STEP = 8

NN = (((1,), (0,)), ((), ()))
NT = (((1,), (1,)), ((), ()))
TN = (((0,), (0,)), ((), ()))


def _pieces(x, n):
    if x.dtype == BF16:
        return [x]
    out = []
    rem = x
    for i in range(n):
        p = rem.astype(BF16)
        out.append(p)
        if i + 1 < n:
            rem = rem - p.astype(F32)
    return out


def _mm(a, b, dims=NN, na=1, nb=1):
    ap = _pieces(a, na)
    bp = _pieces(b, nb)
    order = max(len(ap), len(bp))
    acc = None
    for i, x in enumerate(ap):
        for j, y in enumerate(bp):
            if i + j >= order:
                continue
            t = lax.dot_general(x, y, dims, preferred_element_type=F32)
            acc = t if acc is None else acc + t
    return acc


def _seg_sum(x, g_ones, pieces=1):
    cols = []
    for s in range(x.shape[1] // MXU_DIM):
        cols.append(_mm(x[:, s * MXU_DIM:(s + 1) * MXU_DIM], g_ones, NN, pieces, 1))
    return cols[0] if len(cols) == 1 else jnp.concatenate(cols, axis=1)


def _sigmoid(x):
    return 1.0 / (1.0 + jnp.exp(-x))


def _norm_operands(x_ref, nw_ref):
    x = x_ref[...]
    rstd = lax.rsqrt(jnp.mean(x * x, axis=-1, keepdims=True) + NORM_EPS)
    return (x * nw_ref[...]).astype(BF16), rstd


def _prep_stages(h, rstd, win_ref, j, sh_ref, ph_ref, mu_ref, wl_ref, w0_ref, a0_ref, kk_ref,
                 ka_ref, rk_ref, pw_ref, ps_ref, g_ref,
                 r_o, k_o, v_o, lw_o, a_o, b_o, bon_o, sg_o, op_o, nsh_o, npl_o,
                 p_ext, u_ext, *, stride, t0, hs):
    rows = h.shape[0]
    ph = POOL_HIST * stride
    width = MXU_DIM

    @pl.when(j == 0)
    def _():
        p_ext[0:hs, :] = sh_ref[0]
        u_ext[0:ph, :] = ph_ref[0]

    def proj_cols(c0, c1):
        return jnp.dot(h, win_ref[:, c0:c1], preferred_element_type=F32) * rstd

    def shifted(c0, c1):
        p = proj_cols(c0, c1)
        p_ext[hs:hs + rows, c0:c1] = p
        prev = p_ext[pl.ds(hs - stride, rows), c0:c1]
        return p + mu_ref[:, c0:c1] * (prev - p)

    xwa = shifted(3 * D_RWKV, D_SHIFT)
    for c in range(0, D_RWKV, width):
        r_o[:, c:c + width] = shifted(c, c + width)
    lane = lax.broadcasted_iota(jnp.int32, xwa.shape, 1)
    lora_in = jnp.where(lane < D_LORA, jnp.tanh(xwa), xwa)
    lora = jnp.dot(lora_in.astype(BF16), wl_ref[...], preferred_element_type=F32)
    lw_o[...] = (-0.6065306597126334) * _sigmoid(w0_ref[...] + lora[:, :D_RWKV])
    alpha = _sigmoid(a0_ref[...] + lora[:, D_RWKV:])
    for c in range(0, D_RWKV, width):
        sl = slice(c, c + width)
        k = shifted(D_RWKV + c, D_RWKV + c + width)
        k_o[:, sl] = k * (1.0 + (alpha[:, sl] - 1.0) * ka_ref[:, sl])
        kk = k * kk_ref[:, sl]
        kk = kk * lax.rsqrt(_seg_sum(kk * kk, g_ref[...]) + L2_EPS)
        a_o[:, sl] = -kk
        b_o[:, sl] = kk * alpha[:, sl]
        v_o[:, sl] = shifted(2 * D_RWKV + c, 2 * D_RWKV + c + width)
    last_p = p_ext[rows:rows + hs, :]
    p_ext[0:hs, :] = last_p
    nsh_o[0] = last_p
    for c in range(0, D_RWKV, width):
        g_rwkv = proj_cols(D_SHIFT + c, D_SHIFT + c + width)
        sg_o[:, c:c + width] = g_rwkv * _sigmoid(g_rwkv)
    for c in range(0, D_POOL, width):
        u_ext[ph:ph + rows, c:c + width] = proj_cols(D_SHIFT + D_RWKV + c, D_SHIFT + D_RWKV + c + width)
        op_o[:, c:c + width] = proj_cols(D_SHIFT + D_RWKV + D_POOL + c,
                                         D_SHIFT + D_RWKV + D_POOL + c + width)

    def bonus():
        bon_o[...] = _seg_sum(r_o[...] * k_o[...] * rk_ref[...], g_ref[...]) * v_o[...]

    def pool(gi, win):
        sl = slice(gi * POOL_GROUP, (gi + 1) * POOL_GROUP)
        row = lax.broadcasted_iota(jnp.int32, (rows, 1), 0)
        pos = t0 + (j * rows + row) // stride
        u = u_ext[ph:ph + rows, sl]
        tot = u
        for back in range(1, win):
            tot = tot + u_ext[pl.ds(ph - back * stride, rows), sl]
        inv_cnt = 1.0 / jnp.minimum(pos + 1, win).astype(F32)
        d = tot * inv_cnt - u
        o = jnp.dot(d.astype(BF16), pw_ref[gi], preferred_element_type=F32)
        gp = op_o[:, sl]
        op_o[:, sl] = o * ps_ref[:, sl] * (gp * _sigmoid(gp))

    def carry_pool():
        last_u = u_ext[rows:rows + ph, :]
        u_ext[0:ph, :] = last_u
        npl_o[0] = last_u

    pools = [functools.partial(pool, gi, win) for gi, win in enumerate(POOL_WINDOWS)]
    return [bonus] + pools + [carry_pool]


def _prep_kernel(x_ref, sh_ref, ph_ref, nw_ref, win_ref, *rest, **static):
    h, rstd = _norm_operands(x_ref, nw_ref)
    for finish in _prep_stages(h, rstd, win_ref, pl.program_id(1), sh_ref, ph_ref, *rest, **static):
        finish()


def _prep_call(x, sh_hist, pool_hist, params, *, n_seq, rows, stride, t0):
    total = x.shape[0]
    tiles = total // (n_seq * rows)
    hs = sh_hist.shape[1]
    ph = POOL_HIST * stride
    row_blk = lambda b, j: (b * tiles + j, 0)
    const2 = lambda b, j: (0, 0)
    seq3 = lambda b, j: (b, 0, 0)
    act = lambda w: pl.BlockSpec((rows, w), row_blk)
    in_specs = [
        act(D_MODEL),
        pl.BlockSpec((1, hs, D_SHIFT), seq3),
        pl.BlockSpec((1, ph, D_POOL), seq3),
        pl.BlockSpec((1, D_MODEL), const2),
        pl.BlockSpec((D_MODEL, D_IN), const2),
        pl.BlockSpec((1, D_SHIFT), const2),
        pl.BlockSpec((2 * D_LORA, 2 * D_RWKV), const2),
        pl.BlockSpec((1, D_RWKV), const2),
        pl.BlockSpec((1, D_RWKV), const2),
        pl.BlockSpec((1, D_RWKV), const2),
        pl.BlockSpec((1, D_RWKV), const2),
        pl.BlockSpec((1, D_RWKV), const2),
        pl.BlockSpec((len(POOL_WINDOWS), POOL_GROUP, POOL_GROUP), lambda b, j: (0, 0, 0)),
        pl.BlockSpec((1, D_POOL), const2),
        pl.BlockSpec((MXU_DIM, MXU_DIM), const2),
    ]
    out_shape = [jax.ShapeDtypeStruct((total, D_RWKV), F32)] * 9 + [
        jax.ShapeDtypeStruct((n_seq, hs, D_SHIFT), F32),
        jax.ShapeDtypeStruct((n_seq, ph, D_POOL), F32),
    ]
    out_specs = [act(D_RWKV)] * 9 + [
        pl.BlockSpec((1, hs, D_SHIFT), seq3),
        pl.BlockSpec((1, ph, D_POOL), seq3),
    ]
    return pl.pallas_call(
        functools.partial(_prep_kernel, stride=stride, t0=t0, hs=hs),
        grid=(n_seq, tiles),
        in_specs=in_specs,
        out_specs=out_specs,
        out_shape=out_shape,
        scratch_shapes=[pltpu.VMEM((hs + rows, D_SHIFT), F32), pltpu.VMEM((ph + rows, D_POOL), F32)],
        compiler_params=pltpu.CompilerParams(
            dimension_semantics=("arbitrary", "arbitrary"), vmem_limit_bytes=VMEM_LIMIT),
        name="prep",
    )(x, sh_hist, pool_hist, *params)


def _stack(x, m0):
    return jnp.concatenate([jnp.where(m0, x, 0.0), jnp.where(m0, 0.0, x)], axis=0)


def _wkv_load_state(s0_ref, z_ref, is_first):
    zeros = jnp.zeros((HEAD, HEAD), F32)

    @pl.when(is_first)
    def _():
        for p in range(N_PAIRS):
            top = jnp.concatenate([s0_ref[0, 2 * p], zeros], axis=1)
            bot = jnp.concatenate([zeros, s0_ref[0, 2 * p + 1]], axis=1)
            z_ref[p] = jnp.concatenate([top, bot], axis=0).T


def _wkv_store_state(z_ref, so_ref, is_last):
    @pl.when(is_last)
    def _():
        for p in range(N_PAIRS):
            s2 = z_ref[p].T
            so_ref[0, 2 * p] = s2[:HEAD, :HEAD]
            so_ref[0, 2 * p + 1] = s2[HEAD:, HEAD:]


def _wkv_tile(r_ref, k_ref, v_ref, lw_ref, a_ref, b_ref, tri_ref, mss_ref, msq_ref, y_ref, z_ref, *,
              n_chunks, fillers=()):
    fillers = list(fillers)
    fill = lambda: fillers.pop(0)() if fillers else None
    lane = lax.broadcasted_iota(jnp.int32, (CHUNK, LANES), 1)
    m0 = lane < HEAD
    strict = mss_ref[0][:, :LANES] > 0.5
    incl2 = mss_ref[1] > 0.5
    n_levels = mss_ref.shape[0] - 3
    lane_head4 = lax.broadcasted_iota(jnp.int32, (CHUNK, 2 * LANES), 1) // HEAD
    eye = msq_ref[0]
    same_head = msq_ref[1] > 0.5
    zblk = jnp.zeros((CHUNK, LANES), BF16)
    zblk2 = jnp.zeros((2 * CHUNK, LANES), BF16)
    bf = lambda x: x.astype(BF16)

    scaled = []
    for ci in range(n_chunks):
        rows = slice(ci * CHUNK, (ci + 1) * CHUNK)
        lw = lw_ref[rows, :]
        cs = _mm(tri_ref[...], lw, NN, 1, P_CUMSUM)
        cs_end = cs[CHUNK - 1:CHUNK, :]
        g_inv = jnp.exp(-cs)
        g_tail = jnp.exp(cs_end - cs)
        r_t = r_ref[rows, :] * jnp.exp(cs)
        scaled.append(dict(
            g_end=jnp.exp(cs_end), r_f32=r_t, r_t=bf(r_t),
            a_t=bf(a_ref[rows, :] * jnp.exp(cs - lw)),
            b_t=bf(b_ref[rows, :] * g_inv), k_t=bf(k_ref[rows, :] * g_inv),
            b_h=bf(b_ref[rows, :] * g_tail), k_h=bf(k_ref[rows, :] * g_tail), v=bf(v_ref[rows, :])))
    chains = [(ci, p) for ci in range(n_chunks) for p in range(N_PAIRS)]
    lanes = lambda p: slice(p * LANES, (p + 1) * LANES)
    pair = lambda name: [scaled[ci][name][:, lanes(p)] for ci, p in chains]
    each = lambda fn, *lists: [fn(*args) for args in zip(*lists)]
    cat0 = lambda *xs: jnp.concatenate(xs, axis=0)
    cat1 = lambda *xs: jnp.concatenate(xs, axis=1)
    at, rt, bt, kt, bh, kh, v, r_f32 = (pair(n) for n in ("a_t", "r_t", "b_t", "k_t", "b_h", "k_h", "v",
                                                          "r_f32"))
    v2 = each(lambda x: _stack(x, m0), v)
    scores = each(lambda a, r, b, k: _mm(cat0(a, r), cat0(_stack(b, m0), _stack(k, m0)), NT, *P_A),
                  at, rt, bt, kt)
    a_ab = each(lambda s: jnp.where(strict, s[:CHUNK, :LANES], 0.0), scores)
    a_ak = each(lambda s: jnp.where(strict, s[:CHUNK, LANES:], 0.0), scores)
    a_r = each(lambda s: jnp.where(incl2, s[CHUNK:, :], 0.0), scores)
    fill()
    stack4 = lambda x: cat0(*[jnp.where(lane_head4 == h, x, 0.0) for h in range(4)])
    join = lambda xs: [cat1(xs[i], xs[i + 1]) for i in range(0, len(xs), 2)]
    split = lambda xs: [x[:, s] for x in xs for s in (slice(0, LANES), slice(LANES, 2 * LANES))]
    a_ab4 = join(a_ab)
    t_inv4 = each(lambda x: mss_ref[2] + x * mss_ref[3], a_ab4)
    for lvl in range(1, n_levels):
        t_b = each(bf, t_inv4)
        inner = each(lambda x, t: _mm(x * mss_ref[3 + lvl], stack4(t), NN, *P_T), a_ab4, t_b)
        t_inv4 = each(lambda t, tb, w: t + _mm(tb, stack4(bf(w)), NN, *P_T), t_inv4, t_b, inner)
        fill()
    t_inv = split(each(bf, t_inv4))
    akv = split(each(lambda x, y: _mm(x, stack4(y), NN, *P_U), join(a_ak), join(v)))
    pq = each(lambda t, x, y: _mm(t, cat1(_stack(x, m0), _stack(bf(y), m0)), NN, *P_U), t_inv, at, akv)
    pq_b = each(bf, pq)
    bpq = each(lambda b, k, x, y: _mm(cat0(b, k), cat0(x, cat1(zblk, y)), TN, *P_S), bh, kh, pq_b, v)
    g_end = [scaled[ci]["g_end"][:, lanes(p)] for ci, p in chains]
    m_t = each(lambda g, x: eye * g + jnp.where(same_head, x[:, :LANES], 0.0), g_end, bpq)
    n_t = each(lambda x: jnp.where(same_head, x[:, LANES:], 0.0), bpq)
    yy = each(lambda x, y, z: _mm(x, cat0(cat1(_stack(y[:, :LANES], m0), _stack(y[:, LANES:], m0)),
                                         cat1(zblk2, z)), NN, *P_Y), a_r, pq_b, v2)
    y1 = each(lambda r, x: r + x[:, :LANES], r_f32, yy)
    y2 = each(lambda x: x[:, LANES:], yy)

    while fillers:
        fill()
    state = [z_ref[p] for p in range(N_PAIRS)]
    for ci in range(n_chunks):
        for p in range(N_PAIRS):
            i = ci * N_PAIRS + p
            both = _mm(cat0(y1[i], m_t[i]), state[p], NN, *P_S)
            y_ref[ci * CHUNK:(ci + 1) * CHUNK, lanes(p)] = both[:CHUNK] + y2[i]
            state[p] = both[CHUNK:] + n_t[i]
    for p in range(N_PAIRS):
        z_ref[p] = state[p]


def _wkv_masks():
    ri = lax.broadcasted_iota(jnp.int32, (CHUNK, 4 * CHUNK), 0)
    ci = lax.broadcasted_iota(jnp.int32, (CHUNK, 4 * CHUNK), 1) % CHUNK
    side = [ci < ri, ci <= ri, ri == ci]
    m = 1
    while m < CHUNK:
        side.append(((ri // (2 * m)) == (ci // (2 * m))) & ((ri // m) % 2 == 1) & ((ci // m) % 2 == 0))
        m *= 2
    rq = lax.broadcasted_iota(jnp.int32, (LANES, LANES), 0)
    cq = lax.broadcasted_iota(jnp.int32, (LANES, LANES), 1)
    square = [rq == cq, (rq // HEAD) == (cq // HEAD)]
    tri = (lax.broadcasted_iota(jnp.int32, (CHUNK, CHUNK), 1)
           <= lax.broadcasted_iota(jnp.int32, (CHUNK, CHUNK), 0))
    return tri.astype(BF16), jnp.stack(side).astype(F32), jnp.stack(square).astype(F32)


def _short_pairs(t_s):
    idx = {}
    for kind, inclusive in (("ab", False), ("ak", False), ("rb", True), ("rk", True)):
        for t in range(t_s):
            for s in range(t + 1 if inclusive else t):
                idx[(kind, t, s)] = len(idx)
    return idx


def _short_scale_kernel(r_ref, k_ref, v_ref, lw_ref, a_ref, b_ref,
                        xa_o, xr_o, bh_o, kh_o, vt_o, g4_o, as_o, *, t_s, n_s):
    slab = lambda ref, t: ref[t * n_s:(t + 1) * n_s, :].T
    cs = []
    for t in range(t_s):
        cs.append(slab(lw_ref, t) if t == 0 else cs[-1] + slab(lw_ref, t))
    a_t = [slab(a_ref, t) * (jnp.exp(cs[t - 1]) if t else 1.0) for t in range(t_s)]
    r_t = [slab(r_ref, t) * jnp.exp(cs[t]) for t in range(t_s)]
    b_raw = [slab(b_ref, t) for t in range(t_s)]
    k_raw = [slab(k_ref, t) for t in range(t_s)]
    g_inv = [jnp.exp(-cs[t]) for t in range(t_s)]
    b_t = [b_raw[t] * g_inv[t] for t in range(t_s)]
    k_t = [k_raw[t] * g_inv[t] for t in range(t_s)]
    for t in range(t_s):
        g_tail = jnp.exp(cs[t_s - 1] - cs[t])
        xa_o[t] = a_t[t]
        xr_o[t] = r_t[t]
        bh_o[t] = b_raw[t] * g_tail
        kh_o[t] = k_raw[t] * g_tail
        vt_o[t] = slab(v_ref, t)
    g4_o[...] = jnp.exp(cs[t_s - 1])
    head_sum = lambda x: jnp.sum(x.reshape(N_HEADS, HEAD, n_s), axis=1)
    left = {"ab": a_t, "ak": a_t, "rb": r_t, "rk": r_t}
    right = {"ab": b_t, "ak": k_t, "rb": b_t, "rk": k_t}
    for (kind, t, s), row in _short_pairs(t_s).items():
        as_o[row] = head_sum(left[kind][t] * right[kind][s])


def _short_state_kernel(s_ref, xa_ref, xr_ref, bh_ref, kh_ref, vt_ref, g4_ref, as_ref, so_ref, y_ref, *,
                        t_s):
    idx = _short_pairs(t_s)
    head = pl.program_id(0)
    scal = {key: as_ref[row, pl.ds(head, 1), :] for key, row in idx.items()}

    def value_row(v, carry):
        s_v = s_ref[0, v]
        vv = [vt_ref[t, pl.ds(v, 1), :] for t in range(t_s)]
        u = []
        for t in range(t_s):
            acc = jnp.sum(s_v * xa_ref[t], axis=0, keepdims=True)
            for s in range(t):
                acc = acc + scal[("ab", t, s)] * u[s] + scal[("ak", t, s)] * vv[s]
            u.append(acc)
        for t in range(t_s):
            acc = jnp.sum(s_v * xr_ref[t], axis=0, keepdims=True)
            for s in range(t + 1):
                acc = acc + scal[("rb", t, s)] * u[s] + scal[("rk", t, s)] * vv[s]
            y_ref[t, pl.ds(v, 1), :] = acc
        new = s_v * g4_ref[...]
        for s in range(t_s):
            new = new + u[s] * bh_ref[s] + vv[s] * kh_ref[s]
        so_ref[0, v] = new
        return carry

    lax.fori_loop(0, HEAD, value_row, 0, unroll=2)


def _short_wkv_call(seqs, s0, *, t_s, n_s):
    assert n_s % LANES == 0
    n_scal = len(_short_pairs(t_s))
    full = lambda shape: pl.BlockSpec(shape, lambda: tuple(0 for _ in shape))
    vec = (t_s, D_RWKV, n_s)
    xa, xr, bh, kh, vt, g4, scal = pl.pallas_call(
        functools.partial(_short_scale_kernel, t_s=t_s, n_s=n_s),
        in_specs=[full((t_s * n_s, D_RWKV))] * 6,
        out_specs=[full(vec)] * 5 + [full((D_RWKV, n_s)), full((n_scal, N_HEADS, n_s))],
        out_shape=[jax.ShapeDtypeStruct(vec, F32)] * 5 + [jax.ShapeDtypeStruct((D_RWKV, n_s), F32),
                                                          jax.ShapeDtypeStruct((n_scal, N_HEADS, n_s), F32)],
        compiler_params=pltpu.CompilerParams(vmem_limit_bytes=VMEM_LIMIT),
        name="wkv_short_scale",
    )(*seqs)
    st = pl.BlockSpec((1, HEAD, HEAD, n_s), lambda h: (h, 0, 0, 0))
    per_head = pl.BlockSpec((t_s, HEAD, n_s), lambda h: (0, h, 0))
    s_new, y = pl.pallas_call(
        functools.partial(_short_state_kernel, t_s=t_s),
        grid=(N_HEADS,),
        in_specs=[st] + [per_head] * 5 + [pl.BlockSpec((HEAD, n_s), lambda h: (h, 0)),
                                          pl.BlockSpec((n_scal, N_HEADS, n_s), lambda h: (0, 0, 0))],
        out_specs=[st, per_head],
        out_shape=[jax.ShapeDtypeStruct(s0.shape, F32), jax.ShapeDtypeStruct(vec, F32)],
        compiler_params=pltpu.CompilerParams(
            dimension_semantics=("arbitrary",), vmem_limit_bytes=VMEM_LIMIT),
        name="wkv_short_state",
    )(s0, xa, xr, bh, kh, vt, g4, scal)
    return y, s_new


def _gated_rwkv(y, bon, sg, gw_ref, gb_ref, g_ref):
    g_ones = g_ref[...]
    mu = _seg_sum(y, g_ones) * (1.0 / HEAD)
    d = y - mu
    var = _seg_sum(d * d, g_ones) * (1.0 / HEAD)
    yn = d * lax.rsqrt(var + GN_EPS) * gw_ref[...] + gb_ref[...]
    return (yn + bon) * sg


def _out_project(o_rwkv, o_pool, x, wo_ref, nf_ref):
    out = (jnp.dot(o_rwkv.astype(BF16), wo_ref[0:D_RWKV, :], preferred_element_type=F32)
           + jnp.dot(o_pool.astype(BF16), wo_ref[D_RWKV:, :], preferred_element_type=F32))
    res = x + out
    return res * lax.rsqrt(jnp.mean(res * res, axis=-1, keepdims=True) + NORM_EPS) * nf_ref[...]


def _out_kernel(y_ref, bon_ref, sg_ref, op_ref, x_ref, gw_ref, gb_ref, wo_ref, nf_ref, g_ref, o_ref):
    o_rwkv = _gated_rwkv(y_ref[...], bon_ref[...], sg_ref[...], gw_ref, gb_ref, g_ref)
    o_ref[...] = _out_project(o_rwkv, op_ref[...], x_ref[...], wo_ref, nf_ref)


def _out_call(y, bon, sg, op, x, gn_w, gn_b, w_out, norm_f, g_ones, *, rows):
    total = x.shape[0]
    row_blk = lambda i: (i, 0)
    const2 = lambda i: (0, 0)
    act = lambda w: pl.BlockSpec((rows, w), row_blk)
    return pl.pallas_call(
        _out_kernel,
        grid=(total // rows,),
        in_specs=[act(D_RWKV)] * 4 + [act(D_MODEL),
                                      pl.BlockSpec((1, D_RWKV), const2),
                                      pl.BlockSpec((1, D_RWKV), const2),
                                      pl.BlockSpec((D_RWKV + D_POOL, D_MODEL), const2),
                                      pl.BlockSpec((1, D_MODEL), const2),
                                      pl.BlockSpec((MXU_DIM, MXU_DIM), const2)],
        out_specs=act(D_MODEL),
        out_shape=jax.ShapeDtypeStruct((total, D_MODEL), F32),
        compiler_params=pltpu.CompilerParams(
            dimension_semantics=("arbitrary",), vmem_limit_bytes=VMEM_LIMIT),
        name="out",
    )(y, bon, sg, op, x, gn_w, gn_b, w_out, norm_f, g_ones)


N_PREP_PARAMS = 12
N_WKV_CONSTS = 3
N_OUT_PARAMS = 5


def _fused_kernel(*refs, n_chunks, t0):
    x_ref, sh_ref, ph_ref, s0_ref, nw_ref, win_ref = refs[:6]
    pos = 4 + N_PREP_PARAMS
    prep_rest = refs[6:pos]
    wkv_consts = refs[pos:pos + N_WKV_CONSTS]
    pos += N_WKV_CONSTS
    gw_ref, gb_ref, wo_ref, nf_ref, g_ref = refs[pos:pos + N_OUT_PARAMS]
    pos += N_OUT_PARAMS
    o_ref, nsh_o, npl_o, so_ref = refs[pos:pos + 4]
    pos += 4
    seq_s = refs[pos:pos + 6]
    bon_s, sg_s, op_s, y_s, p_ext, u_ext, z_ref = refs[pos + 6:]
    j = pl.program_id(1)
    _wkv_load_state(s0_ref, z_ref, j == 0)
    finish_prep = _prep_stages(*_norm_operands(x_ref, nw_ref), win_ref, j, sh_ref, ph_ref, *prep_rest,
                               *seq_s, bon_s, sg_s, op_s, nsh_o, npl_o, p_ext, u_ext,
                               stride=1, t0=t0, hs=sh_ref.shape[1])
    _wkv_tile(*seq_s, *wkv_consts, y_s, z_ref, n_chunks=n_chunks, fillers=finish_prep)
    _out_kernel(y_s, bon_s, sg_s, op_s, x_ref, gw_ref, gb_ref, wo_ref, nf_ref, g_ref, o_ref)
    _wkv_store_state(z_ref, so_ref, j == pl.num_programs(1) - 1)


def _fused_call(x, sh_hist, pool_hist, s0, prep_params, out_params, *, n_seq, n_chunks, t0):
    assert len(prep_params) == N_PREP_PARAMS and len(out_params) == N_OUT_PARAMS
    total = x.shape[0]
    rows = n_chunks * CHUNK
    tiles = total // (n_seq * rows)
    hs = sh_hist.shape[1]
    assert CHUNK == HEAD and 2 * CHUNK == LANES
    wkv_consts = _wkv_masks()
    act = pl.BlockSpec((rows, D_MODEL), lambda b, j: (b * tiles + j, 0))
    per_seq = lambda a: pl.BlockSpec((1,) + a.shape[1:], lambda b, j: (b,) + (0,) * (a.ndim - 1))
    const = lambda a: pl.BlockSpec(a.shape, lambda b, j: (0,) * a.ndim, pipeline_mode=pl.Buffered(1))
    consts = tuple(prep_params) + tuple(wkv_consts) + tuple(out_params)
    seq_scratch = pltpu.VMEM((rows, D_RWKV), F32)
    return pl.pallas_call(
        functools.partial(_fused_kernel, n_chunks=n_chunks, t0=t0),
        grid=(n_seq, tiles),
        in_specs=[act, per_seq(sh_hist), per_seq(pool_hist), per_seq(s0)] + [const(a) for a in consts],
        out_specs=[act, per_seq(sh_hist), per_seq(pool_hist), per_seq(s0)],
        out_shape=[jax.ShapeDtypeStruct((total, D_MODEL), F32),
                   jax.ShapeDtypeStruct(sh_hist.shape, F32),
                   jax.ShapeDtypeStruct(pool_hist.shape, F32),
                   jax.ShapeDtypeStruct(s0.shape, F32)],
        scratch_shapes=[seq_scratch] * 10 + [pltpu.VMEM((hs + rows, D_SHIFT), F32),
                                             pltpu.VMEM((POOL_HIST + rows, D_POOL), F32),
                                             pltpu.VMEM((N_PAIRS, LANES, LANES), F32)],
        compiler_params=pltpu.CompilerParams(
            dimension_semantics=("arbitrary", "arbitrary"), vmem_limit_bytes=VMEM_LIMIT),
        name="layer_long",
    )(x, sh_hist, pool_hist, s0, *consts)


def _head_ones():
    i = lax.broadcasted_iota(jnp.int32, (MXU_DIM, MXU_DIM), 0) // HEAD
    j = lax.broadcasted_iota(jnp.int32, (MXU_DIM, MXU_DIM), 1) // HEAD
    return (i == j).astype(BF16)


def kernel(x_prompt, x_sample, state_shift, state_wkv, state_pool, norm_w, w_in, mu_shift, w_decay_b,
           w0, w_aaa_b, a0, k_k, k_a, r_k, gn_w, gn_b, pool_w, pool_scale, w_out, norm_f):
    depth = norm_w.shape[0]
    n_p, t_p, _ = x_prompt.shape
    n_s, t_s, _ = x_sample.shape
    assert depth == 1, "a stacked trunk needs a residual-only output kernel between layers"
    assert t_p % (WKV_CHUNKS_PER_STEP * CHUNK) == 0
    g_ones = _head_ones()
    row = lambda z: z.reshape(1, -1).astype(F32)

    hp = x_prompt.astype(F32).reshape(n_p * t_p, D_MODEL)
    hs = jnp.transpose(x_sample.astype(F32), (1, 0, 2)).reshape(t_s * n_s, D_MODEL)
    outs = [[] for _ in range(6)]
    for l in range(depth):
        zl = jnp.zeros((D_LORA, D_RWKV), F32)
        w_lora = jnp.concatenate([jnp.concatenate([w_decay_b[l], zl], axis=1),
                                  jnp.concatenate([zl, w_aaa_b[l]], axis=1)], axis=0).astype(BF16)
        prep_params = (row(norm_w[l]), w_in[l].astype(BF16), row(mu_shift[l]), w_lora, row(w0[l]),
                       row(a0[l]), row(k_k[l]), row(k_a[l]), row(r_k[l]), pool_w[l].astype(BF16),
                       row(pool_scale[l]), g_ones)
        out_params = (row(gn_w[l]), row(gn_b[l]), w_out[l].astype(BF16), row(norm_f), g_ones)

        hp, nsh, npl, s_fin = _fused_call(
            hp, jnp.zeros((n_p, SUBLANES, D_SHIFT), F32), jnp.zeros((n_p, POOL_HIST, D_POOL), F32),
            jnp.zeros((n_p, N_HEADS, HEAD, HEAD), F32), prep_params, out_params,
            n_seq=n_p, n_chunks=WKV_CHUNKS_PER_STEP, t0=0)
        outs[0].append(nsh[:, SUBLANES - 1])
        outs[1].append(s_fin)
        outs[2].append(npl[:, 1:])

        sh_hist = state_shift[l].astype(F32)[None]
        pool_hist = jnp.transpose(state_pool[l].astype(F32), (1, 0, 2))
        pool_hist = jnp.pad(pool_hist, ((1, 0), (0, 0), (0, 0))).reshape(1, POOL_HIST * n_s, D_POOL)
        res = _prep_call(hs, sh_hist, pool_hist, prep_params,
                         n_seq=1, rows=t_s * n_s, stride=n_s, t0=PAST_LEN)
        seqs, (bon, sg, op), nsh, npl = res[:6], res[6:9], res[9], res[10]

        y, s_fin = _short_wkv_call(seqs, jnp.transpose(state_wkv[l].astype(F32), (1, 2, 3, 0)),
                                   t_s=t_s, n_s=n_s)
        y = jnp.transpose(y, (0, 2, 1)).reshape(t_s * n_s, D_RWKV)
        s_fin = jnp.transpose(s_fin, (3, 0, 1, 2))
        hs = _out_call(y, bon, sg, op, hs, *out_params, rows=t_s * n_s)
        outs[3].append(nsh[0])
        outs[4].append(s_fin)
        outs[5].append(jnp.transpose(npl[0, n_s:].reshape(POOL_HIST - 1, n_s, D_POOL), (1, 0, 2)))

    y_prompt = hp.reshape(n_p, t_p, D_MODEL).astype(x_prompt.dtype)
    y_sample = jnp.transpose(hs.reshape(t_s, n_s, D_MODEL), (1, 0, 2)).astype(x_sample.dtype)
    return (y_prompt, y_sample) + tuple(jnp.stack(o, axis=0) for o in outs)
```

```python
import functools

import jax
import jax.numpy as jnp
import numpy as np
from jax import lax
from jax.experimental import pallas as pl
from jax.experimental.pallas import tpu as pltpu

F32 = jnp.float32
BF16 = jnp.bfloat16

D_MODEL = 1024
D_RWKV = 512
D_POOL = 512
HEAD = 64
N_HEADS = D_RWKV // HEAD
LANES = 128
SUBLANES = 8
MXU_DIM = 256
N_PAIRS = D_RWKV // LANES
D_LORA = 64
POOL_WINDOWS = (2, 4, 8, 16)
POOL_GROUP = D_POOL // len(POOL_WINDOWS)
POOL_KEEP = max(POOL_WINDOWS) - 1
D_SHIFT = 3 * D_RWKV + 2 * D_LORA
D_IN = D_SHIFT + D_RWKV + 2 * D_POOL
PAST_LEN = 16384
NORM_EPS = 1e-6
GN_EPS = 64e-5
L2_EPS = 1e-12
CHUNK = 64
VMEM_LIMIT = 56 * 1024 * 1024
P_A = (1, 1)
P_T = (1, 1)
P_U = (1, 1)
P_S = (1, 1)
P_Y = (1, 1)
P_CUMSUM = 2
WKV_CHUNKS_PER_STEP = 8

NN = (((1,), (0,)), ((), ()))
NT = (((1,), (1,)), ((), ()))
TN = (((0,), (0,)), ((), ()))


def _pieces(x, n):
    if x.dtype == BF16:
        return [x]
    out = []
    rem = x
    for i in range(n):
        p = rem.astype(BF16)
        out.append(p)
        if i + 1 < n:
            rem = rem - p.astype(F32)
    return out


def _mm(a, b, dims=NN, na=1, nb=1):
    ap = _pieces(a, na)
    bp = _pieces(b, nb)
    order = max(len(ap), len(bp))
    acc = None
    for i, x in enumerate(ap):
        for j, y in enumerate(bp):
            if i + j >= order:
                continue
            t = lax.dot_general(x, y, dims, preferred_element_type=F32)
            acc = t if acc is None else acc + t
    return acc


def _seg_sum(x, g_ones, pieces=1):
    cols = []
    for s in range(x.shape[1] // MXU_DIM):
        cols.append(_mm(x[:, s * MXU_DIM:(s + 1) * MXU_DIM], g_ones, NN, pieces, 1))
    return cols[0] if len(cols) == 1 else jnp.concatenate(cols, axis=1)


def _sigmoid(x):
    return 1.0 / (1.0 + jnp.exp(-x))


def _pool_hist_rows(stride):
    return -(-POOL_KEEP * stride // SUBLANES) * SUBLANES


def _norm_operands(x_ref, nw_ref):
    x = x_ref[...]
    rstd = lax.rsqrt(jnp.mean(x * x, axis=-1, keepdims=True) + NORM_EPS)
    return (x * nw_ref[...]).astype(BF16), rstd


def _prep_stages(h, rstd, win_ref, j, sh_ref, ph_ref, mu_ref, wl_ref, w0_ref, a0_ref, kk_ref,
                 ka_ref, rk_ref, pw_ref, ps_ref, g_ref,
                 r_o, k_o, v_o, lw_o, a_o, b_o, bon_o, sg_o, op_o, nsh_o, npl_o,
                 p_ext, u_ext, *, stride, t0, hs):
    rows = h.shape[0]
    ph = _pool_hist_rows(stride)
    width = MXU_DIM

    @pl.when(j == 0)
    def _():
        p_ext[0:hs, :] = jnp.zeros((hs, D_SHIFT), F32) if sh_ref is None else sh_ref[0]
        u_ext[0:ph, :] = jnp.zeros((ph, D_POOL), F32) if ph_ref is None else ph_ref[0]

    def proj_cols(c0, c1):
        return jnp.dot(h, win_ref[:, c0:c1], preferred_element_type=F32) * rstd

    def shifted(c0, c1):
        p = proj_cols(c0, c1)
        p_ext[hs:hs + rows, c0:c1] = p
        prev = p_ext[pl.ds(hs - stride, rows), c0:c1]
        return p + mu_ref[:, c0:c1] * (prev - p)

    xwa = shifted(3 * D_RWKV, D_SHIFT)
    for c in range(0, D_RWKV, width):
        r_o[:, c:c + width] = shifted(c, c + width)
    lane = lax.broadcasted_iota(jnp.int32, xwa.shape, 1)
    lora_in = jnp.where(lane < D_LORA, jnp.tanh(xwa), xwa)
    lora = jnp.dot(lora_in.astype(BF16), wl_ref[...], preferred_element_type=F32)
    lw_o[...] = (-0.6065306597126334) * _sigmoid(w0_ref[...] + lora[:, :D_RWKV])
    alpha = _sigmoid(a0_ref[...] + lora[:, D_RWKV:])
    for c in range(0, D_RWKV, width):
        sl = slice(c, c + width)
        k = shifted(D_RWKV + c, D_RWKV + c + width)
        k_o[:, sl] = k * (1.0 + (alpha[:, sl] - 1.0) * ka_ref[:, sl])
        kk = k * kk_ref[:, sl]
        kk = kk * lax.rsqrt(_seg_sum(kk * kk, g_ref[...]) + L2_EPS)
        a_o[:, sl] = -kk
        b_o[:, sl] = kk * alpha[:, sl]
        v_o[:, sl] = shifted(2 * D_RWKV + c, 2 * D_RWKV + c + width)
    last_p = p_ext[rows:rows + hs, :]
    p_ext[0:hs, :] = last_p
    nsh_o[0] = last_p
    for c in range(0, D_RWKV, width):
        g_rwkv = proj_cols(D_SHIFT + c, D_SHIFT + c + width)
        sg_o[:, c:c + width] = g_rwkv * _sigmoid(g_rwkv)
    for c in range(0, D_POOL, width):
        u_ext[ph:ph + rows, c:c + width] = proj_cols(D_SHIFT + D_RWKV + c, D_SHIFT + D_RWKV + c + width)
        op_o[:, c:c + width] = proj_cols(D_SHIFT + D_RWKV + D_POOL + c,
                                         D_SHIFT + D_RWKV + D_POOL + c + width)

    def bonus():
        bon_o[...] = _seg_sum(r_o[...] * k_o[...] * rk_ref[...], g_ref[...]) * v_o[...]

    def pool(gi, win):
        sl = slice(gi * POOL_GROUP, (gi + 1) * POOL_GROUP)
        row = lax.broadcasted_iota(jnp.int32, (rows, 1), 0)
        pos = t0 + (j * rows + row) // stride
        u = u_ext[ph:ph + rows, sl]
        tot = u
        for back in range(1, win):
            tot = tot + u_ext[pl.ds(ph - back * stride, rows), sl]
        inv_cnt = 1.0 / jnp.minimum(pos + 1, win).astype(F32)
        d = tot * inv_cnt - u
        o = jnp.dot(d.astype(BF16), pw_ref[gi], preferred_element_type=F32)
        gp = op_o[:, sl]
        op_o[:, sl] = o * ps_ref[:, sl] * (gp * _sigmoid(gp))

    def carry_pool():
        last_u = u_ext[rows:rows + ph, :]
        u_ext[0:ph, :] = last_u
        npl_o[0] = last_u

    pools = [functools.partial(pool, gi, win) for gi, win in enumerate(POOL_WINDOWS)]
    return [bonus] + pools + [carry_pool]


def _prep_kernel(x_ref, sh_ref, ph_ref, nw_ref, win_ref, *rest, **static):
    h, rstd = _norm_operands(x_ref, nw_ref)
    for finish in _prep_stages(h, rstd, win_ref, pl.program_id(1), sh_ref, ph_ref, *rest, **static):
        finish()


def _prep_call(x, sh_hist, pool_hist, params, *, n_seq, rows, stride, t0):
    total = x.shape[0]
    tiles = total // (n_seq * rows)
    hs = sh_hist.shape[1]
    ph = _pool_hist_rows(stride)
    assert pool_hist.shape[1] == ph
    row_blk = lambda b, j: (b * tiles + j, 0)
    const2 = lambda b, j: (0, 0)
    seq3 = lambda b, j: (b, 0, 0)
    act = lambda w: pl.BlockSpec((rows, w), row_blk)
    in_specs = [
        act(D_MODEL),
        pl.BlockSpec((1, hs, D_SHIFT), seq3),
        pl.BlockSpec((1, ph, D_POOL), seq3),
        pl.BlockSpec((1, D_MODEL), const2),
        pl.BlockSpec((D_MODEL, D_IN), const2),
        pl.BlockSpec((1, D_SHIFT), const2),
        pl.BlockSpec((2 * D_LORA, 2 * D_RWKV), const2),
        pl.BlockSpec((1, D_RWKV), const2),
        pl.BlockSpec((1, D_RWKV), const2),
        pl.BlockSpec((1, D_RWKV), const2),
        pl.BlockSpec((1, D_RWKV), const2),
        pl.BlockSpec((1, D_RWKV), const2),
        pl.BlockSpec((len(POOL_WINDOWS), POOL_GROUP, POOL_GROUP), lambda b, j: (0, 0, 0)),
        pl.BlockSpec((1, D_POOL), const2),
        pl.BlockSpec((MXU_DIM, MXU_DIM), const2),
    ]
    out_shape = [jax.ShapeDtypeStruct((total, D_RWKV), F32)] * 9 + [
        jax.ShapeDtypeStruct((n_seq, hs, D_SHIFT), F32),
        jax.ShapeDtypeStruct((n_seq, ph, D_POOL), F32),
    ]
    out_specs = [act(D_RWKV)] * 9 + [
        pl.BlockSpec((1, hs, D_SHIFT), seq3),
        pl.BlockSpec((1, ph, D_POOL), seq3),
    ]
    return pl.pallas_call(
        functools.partial(_prep_kernel, stride=stride, t0=t0, hs=hs),
        grid=(n_seq, tiles),
        in_specs=in_specs,
        out_specs=out_specs,
        out_shape=out_shape,
        scratch_shapes=[pltpu.VMEM((hs + rows, D_SHIFT), F32), pltpu.VMEM((ph + rows, D_POOL), F32)],
        compiler_params=pltpu.CompilerParams(
            dimension_semantics=("arbitrary", "arbitrary"), vmem_limit_bytes=VMEM_LIMIT),
        name="prep",
    )(x, sh_hist, pool_hist, *params)


def _stack(x, m0):
    return jnp.concatenate([jnp.where(m0, x, 0.0), jnp.where(m0, 0.0, x)], axis=0)


def _wkv_zero_state(z_ref, is_first):
    @pl.when(is_first)
    def _():
        z_ref[...] = jnp.zeros(z_ref.shape, F32)


def _wkv_store_state(z_ref, so_ref, is_last):
    @pl.when(is_last)
    def _():
        for p in range(N_PAIRS):
            s2 = z_ref[p].T
            so_ref[0, 2 * p] = s2[:HEAD, :HEAD]
            so_ref[0, 2 * p + 1] = s2[HEAD:, HEAD:]


def _wkv_tile(r_ref, k_ref, v_ref, lw_ref, a_ref, b_ref, tri_ref, mss_ref, msq_ref, y_ref, z_ref, *,
              n_chunks, fillers=()):
    fillers = list(fillers)
    fill = lambda: fillers.pop(0)() if fillers else None
    lane = lax.broadcasted_iota(jnp.int32, (CHUNK, LANES), 1)
    m0 = lane < HEAD
    strict = mss_ref[0][:, :LANES] > 0.5
    incl2 = mss_ref[1] > 0.5
    n_levels = mss_ref.shape[0] - 3
    lane_head4 = lax.broadcasted_iota(jnp.int32, (CHUNK, 2 * LANES), 1) // HEAD
    eye = msq_ref[0]
    same_head = msq_ref[1] > 0.5
    zblk = jnp.zeros((CHUNK, LANES), BF16)
    zblk2 = jnp.zeros((2 * CHUNK, LANES), BF16)
    bf = lambda x: x.astype(BF16)
    lanes = lambda p: slice(p * LANES, (p + 1) * LANES)
    each = lambda fn, *lists: [fn(*args) for args in zip(*lists)]
    cat0 = lambda *xs: jnp.concatenate(xs, axis=0)
    cat1 = lambda *xs: jnp.concatenate(xs, axis=1)

    scaled = []
    for ci in range(n_chunks):
        rows = slice(ci * CHUNK, (ci + 1) * CHUNK)
        lw = lw_ref[rows, :]
        cs = _mm(tri_ref[...], lw, NN, 1, P_CUMSUM)
        cs_end = cs[CHUNK - 1:CHUNK, :]
        g_inv = jnp.exp(-cs)
        g_tail = jnp.exp(cs_end - cs)
        r_t = r_ref[rows, :] * jnp.exp(cs)
        scaled.append(dict(
            g_end=jnp.exp(cs_end), r_f32=r_t, r_t=bf(r_t),
            a_t=bf(a_ref[rows, :] * jnp.exp(cs - lw)),
            b_t=bf(b_ref[rows, :] * g_inv), k_t=bf(k_ref[rows, :] * g_inv),
            b_h=bf(b_ref[rows, :] * g_tail), k_h=bf(k_ref[rows, :] * g_tail), v=bf(v_ref[rows, :])))
    chains = [(ci, p) for ci in range(n_chunks) for p in range(N_PAIRS)]
    pair = lambda name: [scaled[ci][name][:, lanes(p)] for ci, p in chains]
    at, rt, bt, kt, bh, kh, v, r_f32 = (pair(n) for n in ("a_t", "r_t", "b_t", "k_t", "b_h", "k_h", "v",
                                                          "r_f32"))
    v2 = each(lambda x: _stack(x, m0), v)
    scores = each(lambda a, r, b, k: _mm(cat0(a, r), cat0(_stack(b, m0), _stack(k, m0)), NT, *P_A),
                  at, rt, bt, kt)
    a_ab = each(lambda s: jnp.where(strict, s[:CHUNK, :LANES], 0.0), scores)
    a_ak = each(lambda s: jnp.where(strict, s[:CHUNK, LANES:], 0.0), scores)
    a_r = each(lambda s: jnp.where(incl2, s[CHUNK:, :], 0.0), scores)
    fill()
    stack4 = lambda x: cat0(*[jnp.where(lane_head4 == h, x, 0.0) for h in range(4)])
    join = lambda xs: [cat1(xs[i], xs[i + 1]) for i in range(0, len(xs), 2)]
    split = lambda xs: [x[:, s] for x in xs for s in (slice(0, LANES), slice(LANES, 2 * LANES))]
    a_ab4 = join(a_ab)
    t_inv4 = each(lambda x: mss_ref[2] + x * mss_ref[3], a_ab4)
    for lvl in range(1, n_levels):
        t_b = each(bf, t_inv4)
        inner = each(lambda x, t: _mm(x * mss_ref[3 + lvl], stack4(t), NN, *P_T), a_ab4, t_b)
        t_inv4 = each(lambda t, tb, w: t + _mm(tb, stack4(bf(w)), NN, *P_T), t_inv4, t_b, inner)
        fill()
    t_inv = split(each(bf, t_inv4))
    akv = split(each(lambda x, y: _mm(x, stack4(y), NN, *P_U), join(a_ak), join(v)))
    pq = each(lambda t, x, y: _mm(t, cat1(_stack(x, m0), _stack(bf(y), m0)), NN, *P_U), t_inv, at, akv)
    pq_b = each(bf, pq)
    bpq = each(lambda b, k, x, y: _mm(cat0(b, k), cat0(x, cat1(zblk, y)), TN, *P_S), bh, kh, pq_b, v)
    g_end = [scaled[ci]["g_end"][:, lanes(p)] for ci, p in chains]
    m_t = each(lambda g, x: eye * g + jnp.where(same_head, x[:, :LANES], 0.0), g_end, bpq)
    n_t = each(lambda x: jnp.where(same_head, x[:, LANES:], 0.0), bpq)
    yy = each(lambda x, y, z: _mm(x, cat0(cat1(_stack(y[:, :LANES], m0), _stack(y[:, LANES:], m0)),
                                         cat1(zblk2, z)), NN, *P_Y), a_r, pq_b, v2)
    y1 = each(lambda r, x: r + x[:, :LANES], r_f32, yy)
    y2 = each(lambda x: x[:, LANES:], yy)

    while fillers:
        fill()
    state = [z_ref[p] for p in range(N_PAIRS)]
    for ci in range(n_chunks):
        for p in range(N_PAIRS):
            i = ci * N_PAIRS + p
            both = _mm(cat0(y1[i], m_t[i]), state[p], NN, *P_S)
            y_ref[ci * CHUNK:(ci + 1) * CHUNK, lanes(p)] = both[:CHUNK] + y2[i]
            state[p] = both[CHUNK:] + n_t[i]
    for p in range(N_PAIRS):
        z_ref[p] = state[p]


def _wkv_masks():
    ri, ci = np.indices((CHUNK, 4 * CHUNK))
    ci = ci % CHUNK
    side = [ci < ri, ci <= ri, ri == ci]
    m = 1
    while m < CHUNK:
        side.append(((ri // (2 * m)) == (ci // (2 * m))) & ((ri // m) % 2 == 1) & ((ci // m) % 2 == 0))
        m *= 2
    rq, cq = np.indices((LANES, LANES))
    square = [rq == cq, (rq // HEAD) == (cq // HEAD)]
    rt, ct = np.indices((CHUNK, CHUNK))
    return (jnp.asarray(ct <= rt, BF16), jnp.asarray(np.stack(side), F32),
            jnp.asarray(np.stack(square), F32))


def _short_pairs(t_s):
    idx = {}
    for kind, inclusive in (("ab", False), ("ak", False), ("rb", True), ("rk", True)):
        for t in range(t_s):
            for s in range(t + 1 if inclusive else t):
                idx[(kind, t, s)] = len(idx)
    return idx


def _short_scale_kernel(r_ref, k_ref, v_ref, lw_ref, a_ref, b_ref,
                        xa_o, xr_o, bh_o, kh_o, vt_o, g4_o, as_o, *, t_s, n_s):
    slab = lambda ref, t: ref[t * n_s:(t + 1) * n_s, :].T
    cs = []
    for t in range(t_s):
        cs.append(slab(lw_ref, t) if t == 0 else cs[-1] + slab(lw_ref, t))
    a_t = [slab(a_ref, t) * (jnp.exp(cs[t - 1]) if t else 1.0) for t in range(t_s)]
    r_t = [slab(r_ref, t) * jnp.exp(cs[t]) for t in range(t_s)]
    b_raw = [slab(b_ref, t) for t in range(t_s)]
    k_raw = [slab(k_ref, t) for t in range(t_s)]
    g_inv = [jnp.exp(-cs[t]) for t in range(t_s)]
    b_t = [b_raw[t] * g_inv[t] for t in range(t_s)]
    k_t = [k_raw[t] * g_inv[t] for t in range(t_s)]
    for t in range(t_s):
        g_tail = jnp.exp(cs[t_s - 1] - cs[t])
        xa_o[t] = a_t[t]
        xr_o[t] = r_t[t]
        bh_o[t] = b_raw[t] * g_tail
        kh_o[t] = k_raw[t] * g_tail
        vt_o[t] = slab(v_ref, t)
    g4_o[...] = jnp.exp(cs[t_s - 1])
    head_sum = lambda x: jnp.sum(x.reshape(N_HEADS, HEAD, n_s), axis=1)
    left = {"ab": a_t, "ak": a_t, "rb": r_t, "rk": r_t}
    right = {"ab": b_t, "ak": k_t, "rb": b_t, "rk": k_t}
    for (kind, t, s), row in _short_pairs(t_s).items():
        as_o[row] = head_sum(left[kind][t] * right[kind][s])


def _short_state_kernel(s_ref, xa_ref, xr_ref, bh_ref, kh_ref, vt_ref, g4_ref, as_ref, so_ref, y_ref, *,
                        t_s):
    idx = _short_pairs(t_s)
    head = pl.program_id(0)
    scal = {key: as_ref[row, pl.ds(head, 1), :] for key, row in idx.items()}

    def value_row(v, carry):
        s_v = s_ref[0, v]
        vv = [vt_ref[t, pl.ds(v, 1), :] for t in range(t_s)]
        u = []
        for t in range(t_s):
            acc = jnp.sum(s_v * xa_ref[t], axis=0, keepdims=True)
            for s in range(t):
                acc = acc + scal[("ab", t, s)] * u[s] + scal[("ak", t, s)] * vv[s]
            u.append(acc)
        for t in range(t_s):
            acc = jnp.sum(s_v * xr_ref[t], axis=0, keepdims=True)
            for s in range(t + 1):
                acc = acc + scal[("rb", t, s)] * u[s] + scal[("rk", t, s)] * vv[s]
            y_ref[t, pl.ds(v, 1), :] = acc
        new = s_v * g4_ref[...]
        for s in range(t_s):
            new = new + u[s] * bh_ref[s] + vv[s] * kh_ref[s]
        so_ref[0, v] = new
        return carry

    lax.fori_loop(0, HEAD, value_row, 0, unroll=2)


def _short_wkv_call(seqs, s0, *, t_s, n_s):
    assert n_s % LANES == 0
    n_scal = len(_short_pairs(t_s))
    full = lambda shape: pl.BlockSpec(shape, lambda: tuple(0 for _ in shape))
    vec = (t_s, D_RWKV, n_s)
    xa, xr, bh, kh, vt, g4, scal = pl.pallas_call(
        functools.partial(_short_scale_kernel, t_s=t_s, n_s=n_s),
        in_specs=[full((t_s * n_s, D_RWKV))] * 6,
        out_specs=[full(vec)] * 5 + [full((D_RWKV, n_s)), full((n_scal, N_HEADS, n_s))],
        out_shape=[jax.ShapeDtypeStruct(vec, F32)] * 5 + [jax.ShapeDtypeStruct((D_RWKV, n_s), F32),
                                                          jax.ShapeDtypeStruct((n_scal, N_HEADS, n_s), F32)],
        compiler_params=pltpu.CompilerParams(vmem_limit_bytes=VMEM_LIMIT),
        name="wkv_short_scale",
    )(*seqs)
    st = pl.BlockSpec((1, HEAD, HEAD, n_s), lambda h: (h, 0, 0, 0))
    per_head = pl.BlockSpec((t_s, HEAD, n_s), lambda h: (0, h, 0))
    s_new, y = pl.pallas_call(
        functools.partial(_short_state_kernel, t_s=t_s),
        grid=(N_HEADS,),
        in_specs=[st] + [per_head] * 5 + [pl.BlockSpec((HEAD, n_s), lambda h: (h, 0)),
                                          pl.BlockSpec((n_scal, N_HEADS, n_s), lambda h: (0, 0, 0))],
        out_specs=[st, per_head],
        out_shape=[jax.ShapeDtypeStruct(s0.shape, F32), jax.ShapeDtypeStruct(vec, F32)],
        compiler_params=pltpu.CompilerParams(
            dimension_semantics=("arbitrary",), vmem_limit_bytes=VMEM_LIMIT),
        name="wkv_short_state",
    )(s0, xa, xr, bh, kh, vt, g4, scal)
    return y, s_new


def _gated_rwkv(y, bon, sg, gw_ref, gb_ref, g_ref):
    g_ones = g_ref[...]
    mu = _seg_sum(y, g_ones) * (1.0 / HEAD)
    d = y - mu
    var = _seg_sum(d * d, g_ones) * (1.0 / HEAD)
    yn = d * lax.rsqrt(var + GN_EPS) * gw_ref[...] + gb_ref[...]
    return (yn + bon) * sg


def _out_project(o_rwkv, o_pool, x, wo_ref, nf_ref):
    out = (jnp.dot(o_rwkv.astype(BF16), wo_ref[0:D_RWKV, :], preferred_element_type=F32)
           + jnp.dot(o_pool.astype(BF16), wo_ref[D_RWKV:, :], preferred_element_type=F32))
    res = x + out
    return res * lax.rsqrt(jnp.mean(res * res, axis=-1, keepdims=True) + NORM_EPS) * nf_ref[...]


def _out_kernel(y_ref, bon_ref, sg_ref, op_ref, x_ref, gw_ref, gb_ref, wo_ref, nf_ref, g_ref, o_ref):
    if len(y_ref.shape) == 3:
        y = jnp.concatenate([y_ref[t].T for t in range(y_ref.shape[0])], axis=0)
    else:
        y = y_ref[...]
    o_rwkv = _gated_rwkv(y, bon_ref[...], sg_ref[...], gw_ref, gb_ref, g_ref)
    o_ref[...] = _out_project(o_rwkv, op_ref[...], x_ref[...], wo_ref, nf_ref)


def _out_call(y, bon, sg, op, x, gn_w, gn_b, w_out, norm_f, g_ones, *, rows):
    total = x.shape[0]
    row_blk = lambda i: (i, 0)
    const2 = lambda i: (0, 0)
    act = lambda w: pl.BlockSpec((rows, w), row_blk)
    if y.ndim == 3:
        assert total == rows == y.shape[0] * y.shape[2]
        y_spec = pl.BlockSpec(y.shape, lambda i: (0, 0, 0))
    else:
        y_spec = act(D_RWKV)
    return pl.pallas_call(
        _out_kernel,
        grid=(total // rows,),
        in_specs=[y_spec] + [act(D_RWKV)] * 3 + [act(D_MODEL),
                                                 pl.BlockSpec((1, D_RWKV), const2),
                                                 pl.BlockSpec((1, D_RWKV), const2),
                                                 pl.BlockSpec((D_RWKV + D_POOL, D_MODEL), const2),
                                                 pl.BlockSpec((1, D_MODEL), const2),
                                                 pl.BlockSpec((MXU_DIM, MXU_DIM), const2)],
        out_specs=act(D_MODEL),
        out_shape=jax.ShapeDtypeStruct((total, D_MODEL), F32),
        compiler_params=pltpu.CompilerParams(
            dimension_semantics=("arbitrary",), vmem_limit_bytes=VMEM_LIMIT),
        name="out",
    )(y, bon, sg, op, x, gn_w, gn_b, w_out, norm_f, g_ones)


N_PREP_PARAMS = 12
N_WKV_CONSTS = 3
N_OUT_PARAMS = 5


def _fused_kernel(*refs, n_chunks, t0):
    x_ref, nw_ref, win_ref = refs[:3]
    pos = 1 + N_PREP_PARAMS
    prep_rest = refs[3:pos]
    wkv_consts = refs[pos:pos + N_WKV_CONSTS]
    pos += N_WKV_CONSTS
    gw_ref, gb_ref, wo_ref, nf_ref, g_ref = refs[pos:pos + N_OUT_PARAMS]
    pos += N_OUT_PARAMS
    o_ref, nsh_o, npl_o, so_ref = refs[pos:pos + 4]
    pos += 4
    seq_s = refs[pos:pos + 6]
    bon_s, sg_s, op_s, y_s, p_ext, u_ext, z_ref = refs[pos + 6:]
    j = pl.program_id(1)
    _wkv_zero_state(z_ref, j == 0)
    finish_prep = _prep_stages(*_norm_operands(x_ref, nw_ref), win_ref, j, None, None, *prep_rest,
                               *seq_s, bon_s, sg_s, op_s, nsh_o, npl_o, p_ext, u_ext,
                               stride=1, t0=t0, hs=nsh_o.shape[1])
    _wkv_tile(*seq_s, *wkv_consts, y_s, z_ref, n_chunks=n_chunks, fillers=finish_prep)
    _out_kernel(y_s, bon_s, sg_s, op_s, x_ref, gw_ref, gb_ref, wo_ref, nf_ref, g_ref, o_ref)
    _wkv_store_state(z_ref, so_ref, j == pl.num_programs(1) - 1)


def _fused_call(x, prep_params, out_params, *, n_seq, n_chunks, t0):
    assert len(prep_params) == N_PREP_PARAMS and len(out_params) == N_OUT_PARAMS
    total = x.shape[0]
    rows = n_chunks * CHUNK
    tiles = total // (n_seq * rows)
    hs, ph = SUBLANES, _pool_hist_rows(1)
    assert CHUNK == HEAD and 2 * CHUNK == LANES
    wkv_consts = _wkv_masks()
    act = pl.BlockSpec((rows, D_MODEL), lambda b, j: (b * tiles + j, 0))
    per_seq = lambda shape: pl.BlockSpec((1,) + shape[1:], lambda b, j: (b,) + (0,) * (len(shape) - 1))
    const = lambda a: pl.BlockSpec(a.shape, lambda b, j: (0,) * a.ndim, pipeline_mode=pl.Buffered(1))
    consts = tuple(prep_params) + tuple(wkv_consts) + tuple(out_params)
    seq_shapes = [(n_seq, hs, D_SHIFT), (n_seq, ph, D_POOL), (n_seq, N_HEADS, HEAD, HEAD)]
    seq_scratch = pltpu.VMEM((rows, D_RWKV), F32)
    return pl.pallas_call(
        functools.partial(_fused_kernel, n_chunks=n_chunks, t0=t0),
        grid=(n_seq, tiles),
        in_specs=[act] + [const(a) for a in consts],
        out_specs=[act] + [per_seq(s) for s in seq_shapes],
        out_shape=[jax.ShapeDtypeStruct((total, D_MODEL), F32)]
        + [jax.ShapeDtypeStruct(s, F32) for s in seq_shapes],
        scratch_shapes=[seq_scratch] * 10 + [pltpu.VMEM((hs + rows, D_SHIFT), F32),
                                             pltpu.VMEM((ph + rows, D_POOL), F32),
                                             pltpu.VMEM((N_PAIRS, LANES, LANES), F32)],
        compiler_params=pltpu.CompilerParams(
            dimension_semantics=("arbitrary", "arbitrary"), vmem_limit_bytes=VMEM_LIMIT),
        name="layer_long",
    )(x, *consts)


def _head_ones():
    i, j = np.indices((MXU_DIM, MXU_DIM))
    return jnp.asarray((i // HEAD) == (j // HEAD), BF16)


def kernel(x_prompt, x_sample, state_shift, state_wkv, state_pool, norm_w, w_in, mu_shift, w_decay_b,
           w0, w_aaa_b, a0, k_k, k_a, r_k, gn_w, gn_b, pool_w, pool_scale, w_out, norm_f):
    depth = norm_w.shape[0]
    n_p, t_p, _ = x_prompt.shape
    n_s, t_s, _ = x_sample.shape
    assert depth == 1, "a stacked trunk needs a residual-only output kernel between layers"
    assert t_p % (WKV_CHUNKS_PER_STEP * CHUNK) == 0
    g_ones = _head_ones()
    row = lambda z: z.reshape(1, -1).astype(F32)

    hp = x_prompt.astype(F32).reshape(n_p * t_p, D_MODEL)
    hs = jnp.transpose(x_sample.astype(F32), (1, 0, 2)).reshape(t_s * n_s, D_MODEL)
    outs = [[] for _ in range(6)]
    for l in range(depth):
        zl = jnp.zeros((D_LORA, D_RWKV), F32)
        w_lora = jnp.concatenate([jnp.concatenate([w_decay_b[l], zl], axis=1),
                                  jnp.concatenate([zl, w_aaa_b[l]], axis=1)], axis=0).astype(BF16)
        prep_params = (row(norm_w[l]), w_in[l].astype(BF16), row(mu_shift[l]), w_lora, row(w0[l]),
                       row(a0[l]), row(k_k[l]), row(k_a[l]), row(r_k[l]), pool_w[l].astype(BF16),
                       row(pool_scale[l]), g_ones)
        out_params = (row(gn_w[l]), row(gn_b[l]), w_out[l].astype(BF16), row(norm_f), g_ones)

        hp, nsh, npl, s_fin = _fused_call(hp, prep_params, out_params,
                                          n_seq=n_p, n_chunks=WKV_CHUNKS_PER_STEP, t0=0)
        outs[0].append(nsh[:, SUBLANES - 1])
        outs[1].append(s_fin)
        outs[2].append(npl[:, -POOL_KEEP:])

        sh_hist = state_shift[l].astype(F32)[None]
        pool_hist = jnp.transpose(state_pool[l].astype(F32), (1, 0, 2)).reshape(1, POOL_KEEP * n_s, D_POOL)
        res = _prep_call(hs, sh_hist, pool_hist, prep_params,
                         n_seq=1, rows=t_s * n_s, stride=n_s, t0=PAST_LEN)
        seqs, (bon, sg, op), nsh, npl = res[:6], res[6:9], res[9], res[10]

        y, s_fin = _short_wkv_call(seqs, jnp.transpose(state_wkv[l].astype(F32), (1, 2, 3, 0)),
                                   t_s=t_s, n_s=n_s)
        s_fin = jnp.transpose(s_fin, (3, 0, 1, 2))
        hs = _out_call(y, bon, sg, op, hs, *out_params, rows=t_s * n_s)
        outs[3].append(nsh[0])
        outs[4].append(s_fin)
        outs[5].append(jnp.transpose(npl[0].reshape(POOL_KEEP, n_s, D_POOL), (1, 0, 2)))

    y_prompt = hp.reshape(n_p, t_p, D_MODEL).astype(x_prompt.dtype)
    y_sample = jnp.transpose(hs.reshape(t_s, n_s, D_MODEL), (1, 0, 2)).astype(x_sample.dtype)
    return (y_prompt, y_sample) + tuple(jnp.stack(o, axis=0) for o in outs)
```

```python
import functools

import jax
import jax.numpy as jnp
import numpy as np
from jax import lax
from jax.experimental import pallas as pl
from jax.experimental.pallas import tpu as pltpu

F32 = jnp.float32
BF16 = jnp.bfloat16

D_MODEL = 1024
D_RWKV = 512
D_POOL = 512
HEAD = 64
N_HEADS = D_RWKV // HEAD
LANES = 128
SUBLANES = 8
MXU_DIM = 256
N_PAIRS = D_RWKV // LANES
D_LORA = 64
POOL_WINDOWS = (2, 4, 8, 16)
POOL_GROUP = D_POOL // len(POOL_WINDOWS)
POOL_KEEP = max(POOL_WINDOWS) - 1
D_SHIFT = 3 * D_RWKV + 2 * D_LORA
D_IN = D_SHIFT + D_RWKV + 2 * D_POOL
PAST_LEN = 16384
NORM_EPS = 1e-6
GN_EPS = 64e-5
L2_EPS = 1e-12
CHUNK = 64
VMEM_LIMIT = 56 * 1024 * 1024
P_A = (1, 1)
P_T = (1, 1)
P_U = (1, 1)
P_S = (1, 1)
P_Y = (1, 1)
P_CUMSUM = 2
WKV_CHUNKS_PER_STEP = 8

NN = (((1,), (0,)), ((), ()))
NT = (((1,), (1,)), ((), ()))
TN = (((0,), (0,)), ((), ()))


def _pieces(x, n):
    if x.dtype == BF16:
        return [x]
    out = []
    rem = x
    for i in range(n):
        p = rem.astype(BF16)
        out.append(p)
        if i + 1 < n:
            rem = rem - p.astype(F32)
    return out


def _mm(a, b, dims=NN, na=1, nb=1):
    ap = _pieces(a, na)
    bp = _pieces(b, nb)
    order = max(len(ap), len(bp))
    acc = None
    for i, x in enumerate(ap):
        for j, y in enumerate(bp):
            if i + j >= order:
                continue
            t = lax.dot_general(x, y, dims, preferred_element_type=F32)
            acc = t if acc is None else acc + t
    return acc


def _seg_sum(x, g_ones, pieces=1):
    cols = []
    for s in range(x.shape[1] // MXU_DIM):
        cols.append(_mm(x[:, s * MXU_DIM:(s + 1) * MXU_DIM], g_ones, NN, pieces, 1))
    return cols[0] if len(cols) == 1 else jnp.concatenate(cols, axis=1)


def _sigmoid(x):
    return 1.0 / (1.0 + jnp.exp(-x))


def _pool_hist_rows(stride):
    return -(-POOL_KEEP * stride // SUBLANES) * SUBLANES


def _norm_operands(x_ref, nw_ref):
    x = x_ref[...]
    rstd = lax.rsqrt(jnp.mean(x * x, axis=-1, keepdims=True) + NORM_EPS)
    return (x * nw_ref[...]).astype(BF16), rstd


def _prep_stages(h, rstd, win_ref, j, sh_ref, ph_ref, mu_ref, wl_ref, w0_ref, a0_ref, kk_ref,
                 ka_ref, rk_ref, pw_ref, ps_ref, g_ref,
                 r_o, k_o, v_o, lw_o, a_o, b_o, bon_o, sg_o, op_o, nsh_o, npl_o,
                 p_ext, u_ext, *, stride, t0, hs, scan=None):
    rows = h.shape[0]
    ph = _pool_hist_rows(stride)
    width = MXU_DIM

    @pl.when(j == 0)
    def _():
        p_ext[0:hs, :] = jnp.zeros((hs, D_SHIFT), F32) if sh_ref is None else sh_ref[0]
        u_ext[0:ph, :] = jnp.zeros((ph, D_POOL), F32) if ph_ref is None else ph_ref[0]

    def proj_cols(c0, c1):
        return jnp.dot(h, win_ref[:, c0:c1], preferred_element_type=F32) * rstd

    def shifted(c0, c1):
        p = proj_cols(c0, c1)
        p_ext[hs:hs + rows, c0:c1] = p
        prev = p_ext[pl.ds(hs - stride, rows), c0:c1]
        return p + mu_ref[:, c0:c1] * (prev - p)

    xwa = shifted(3 * D_RWKV, D_SHIFT)
    r_blocks = [shifted(c, c + width) for c in range(0, D_RWKV, width)]
    lane = lax.broadcasted_iota(jnp.int32, xwa.shape, 1)
    lora_in = jnp.where(lane < D_LORA, jnp.tanh(xwa), xwa)
    lora = jnp.dot(lora_in.astype(BF16), wl_ref[...], preferred_element_type=F32)
    lw = (-0.6065306597126334) * _sigmoid(w0_ref[...] + lora[:, :D_RWKV])
    alpha = _sigmoid(a0_ref[...] + lora[:, D_RWKV:])
    if scan is None:
        lw_o[...] = lw
    else:
        cs, cs_end = [], []
        for ci in range(rows // CHUNK):
            cs.append(_mm(scan["tri"][...], lw[ci * CHUNK:(ci + 1) * CHUNK, :], NN, 1, P_CUMSUM))
            cs_end.append(jnp.broadcast_to(cs[-1][CHUNK - 1:CHUNK, :], (CHUNK, D_RWKV)))
            scan["g_end"][ci:ci + 1, :] = jnp.exp(cs[-1][CHUNK - 1:CHUNK, :])
        cs = jnp.concatenate(cs, axis=0)
        e_cs, e_prev, g_inv = jnp.exp(cs), jnp.exp(cs - lw), jnp.exp(-cs)
        g_tail = jnp.exp(jnp.concatenate(cs_end, axis=0) - cs)
    for i, c in enumerate(range(0, D_RWKV, width)):
        sl = slice(c, c + width)
        r = r_blocks[i]
        r_o[:, sl] = r
        k = shifted(D_RWKV + c, D_RWKV + c + width)
        k2 = k * (1.0 + (alpha[:, sl] - 1.0) * ka_ref[:, sl])
        k_o[:, sl] = k2
        kk = k * kk_ref[:, sl]
        kk = kk * lax.rsqrt(_seg_sum(kk * kk, g_ref[...]) + L2_EPS)
        b = kk * alpha[:, sl]
        v = shifted(2 * D_RWKV + c, 2 * D_RWKV + c + width)
        v_o[:, sl] = v
        if scan is None:
            a_o[:, sl] = -kk
            b_o[:, sl] = b
        else:
            r_t = r * e_cs[:, sl]
            scan["r_f32"][:, sl] = r_t
            scan["r_t"][:, sl] = r_t.astype(BF16)
            scan["a_t"][:, sl] = (-kk * e_prev[:, sl]).astype(BF16)
            scan["b_t"][:, sl] = (b * g_inv[:, sl]).astype(BF16)
            scan["k_t"][:, sl] = (k2 * g_inv[:, sl]).astype(BF16)
            scan["b_h"][:, sl] = (b * g_tail[:, sl]).astype(BF16)
            scan["k_h"][:, sl] = (k2 * g_tail[:, sl]).astype(BF16)
            scan["v"][:, sl] = v.astype(BF16)
    last_p = p_ext[rows:rows + hs, :]
    p_ext[0:hs, :] = last_p
    nsh_o[0] = last_p
    for c in range(0, D_RWKV, width):
        g_rwkv = proj_cols(D_SHIFT + c, D_SHIFT + c + width)
        sg_o[:, c:c + width] = g_rwkv * _sigmoid(g_rwkv)
    for c in range(0, D_POOL, width):
        u_ext[ph:ph + rows, c:c + width] = proj_cols(D_SHIFT + D_RWKV + c, D_SHIFT + D_RWKV + c + width)
        op_o[:, c:c + width] = proj_cols(D_SHIFT + D_RWKV + D_POOL + c,
                                         D_SHIFT + D_RWKV + D_POOL + c + width)

    def bonus():
        bon_o[...] = _seg_sum(r_o[...] * k_o[...] * rk_ref[...], g_ref[...]) * v_o[...]

    def pool(gi, win):
        sl = slice(gi * POOL_GROUP, (gi + 1) * POOL_GROUP)
        row = lax.broadcasted_iota(jnp.int32, (rows, 1), 0)
        pos = t0 + (j * rows + row) // stride
        u = u_ext[ph:ph + rows, sl]
        tot = u
        for back in range(1, win):
            tot = tot + u_ext[pl.ds(ph - back * stride, rows), sl]
        inv_cnt = 1.0 / jnp.minimum(pos + 1, win).astype(F32)
        d = tot * inv_cnt - u
        o = jnp.dot(d.astype(BF16), pw_ref[gi], preferred_element_type=F32)
        gp = op_o[:, sl]
        op_o[:, sl] = o * ps_ref[:, sl] * (gp * _sigmoid(gp))

    def carry_pool():
        last_u = u_ext[rows:rows + ph, :]
        u_ext[0:ph, :] = last_u
        npl_o[0] = last_u

    pools = [functools.partial(pool, gi, win) for gi, win in enumerate(POOL_WINDOWS)]
    return [bonus] + pools + [carry_pool]


def _prep_kernel(x_ref, sh_ref, ph_ref, nw_ref, win_ref, *rest, **static):
    h, rstd = _norm_operands(x_ref, nw_ref)
    for finish in _prep_stages(h, rstd, win_ref, pl.program_id(1), sh_ref, ph_ref, *rest, **static):
        finish()


def _prep_call(x, sh_hist, pool_hist, params, *, n_seq, rows, stride, t0):
    total = x.shape[0]
    tiles = total // (n_seq * rows)
    hs = sh_hist.shape[1]
    ph = _pool_hist_rows(stride)
    assert pool_hist.shape[1] == ph
    row_blk = lambda b, j: (b * tiles + j, 0)
    const2 = lambda b, j: (0, 0)
    seq3 = lambda b, j: (b, 0, 0)
    act = lambda w: pl.BlockSpec((rows, w), row_blk)
    in_specs = [
        act(D_MODEL),
        pl.BlockSpec((1, hs, D_SHIFT), seq3),
        pl.BlockSpec((1, ph, D_POOL), seq3),
        pl.BlockSpec((1, D_MODEL), const2),
        pl.BlockSpec((D_MODEL, D_IN), const2),
        pl.BlockSpec((1, D_SHIFT), const2),
        pl.BlockSpec((2 * D_LORA, 2 * D_RWKV), const2),
        pl.BlockSpec((1, D_RWKV), const2),
        pl.BlockSpec((1, D_RWKV), const2),
        pl.BlockSpec((1, D_RWKV), const2),
        pl.BlockSpec((1, D_RWKV), const2),
        pl.BlockSpec((1, D_RWKV), const2),
        pl.BlockSpec((len(POOL_WINDOWS), POOL_GROUP, POOL_GROUP), lambda b, j: (0, 0, 0)),
        pl.BlockSpec((1, D_POOL), const2),
        pl.BlockSpec((MXU_DIM, MXU_DIM), const2),
    ]
    out_shape = [jax.ShapeDtypeStruct((total, D_RWKV), F32)] * 9 + [
        jax.ShapeDtypeStruct((n_seq, hs, D_SHIFT), F32),
        jax.ShapeDtypeStruct((n_seq, ph, D_POOL), F32),
    ]
    out_specs = [act(D_RWKV)] * 9 + [
        pl.BlockSpec((1, hs, D_SHIFT), seq3),
        pl.BlockSpec((1, ph, D_POOL), seq3),
    ]
    return pl.pallas_call(
        functools.partial(_prep_kernel, stride=stride, t0=t0, hs=hs),
        grid=(n_seq, tiles),
        in_specs=in_specs,
        out_specs=out_specs,
        out_shape=out_shape,
        scratch_shapes=[pltpu.VMEM((hs + rows, D_SHIFT), F32), pltpu.VMEM((ph + rows, D_POOL), F32)],
        compiler_params=pltpu.CompilerParams(
            dimension_semantics=("arbitrary", "arbitrary"), vmem_limit_bytes=VMEM_LIMIT),
        name="prep",
    )(x, sh_hist, pool_hist, *params)


def _stack(x, m0):
    return jnp.concatenate([jnp.where(m0, x, 0.0), jnp.where(m0, 0.0, x)], axis=0)


def _wkv_zero_state(z_ref, is_first):
    @pl.when(is_first)
    def _():
        z_ref[...] = jnp.zeros(z_ref.shape, F32)


def _wkv_store_state(z_ref, so_ref, is_last):
    @pl.when(is_last)
    def _():
        for p in range(N_PAIRS):
            s2 = z_ref[p].T
            so_ref[0, 2 * p] = s2[:HEAD, :HEAD]
            so_ref[0, 2 * p + 1] = s2[HEAD:, HEAD:]


def _wkv_tile(scan, mss_ref, msq_ref, y_ref, z_ref, *, n_chunks, fillers=()):
    fillers = list(fillers)
    fill = lambda: fillers.pop(0)() if fillers else None
    lane = lax.broadcasted_iota(jnp.int32, (CHUNK, LANES), 1)
    m0 = lane < HEAD
    strict = mss_ref[0][:, :LANES] > 0.5
    incl2 = mss_ref[1] > 0.5
    n_levels = mss_ref.shape[0] - 3
    lane_head4 = lax.broadcasted_iota(jnp.int32, (CHUNK, 2 * LANES), 1) // HEAD
    eye = msq_ref[0]
    same_head = msq_ref[1] > 0.5
    zblk = jnp.zeros((CHUNK, LANES), BF16)
    zblk2 = jnp.zeros((2 * CHUNK, LANES), BF16)
    bf = lambda x: x.astype(BF16)
    lanes = lambda p: slice(p * LANES, (p + 1) * LANES)
    each = lambda fn, *lists: [fn(*args) for args in zip(*lists)]
    cat0 = lambda *xs: jnp.concatenate(xs, axis=0)
    cat1 = lambda *xs: jnp.concatenate(xs, axis=1)

    chains = [(ci, p) for ci in range(n_chunks) for p in range(N_PAIRS)]
    pair = lambda name: [scan[name][ci * CHUNK:(ci + 1) * CHUNK, lanes(p)] for ci, p in chains]
    at, rt, bt, kt, bh, kh, v, r_f32 = (pair(n) for n in ("a_t", "r_t", "b_t", "k_t", "b_h", "k_h", "v",
                                                          "r_f32"))
    v2 = each(lambda x: _stack(x, m0), v)
    scores = each(lambda a, r, b, k: _mm(cat0(a, r), cat0(_stack(b, m0), _stack(k, m0)), NT, *P_A),
                  at, rt, bt, kt)
    a_ab = each(lambda s: jnp.where(strict, s[:CHUNK, :LANES], 0.0), scores)
    a_ak = each(lambda s: jnp.where(strict, s[:CHUNK, LANES:], 0.0), scores)
    a_r = each(lambda s: jnp.where(incl2, s[CHUNK:, :], 0.0), scores)
    fill()
    stack4 = lambda x: cat0(*[jnp.where(lane_head4 == h, x, 0.0) for h in range(4)])
    join = lambda xs: [cat1(xs[i], xs[i + 1]) for i in range(0, len(xs), 2)]
    split = lambda xs: [x[:, s] for x in xs for s in (slice(0, LANES), slice(LANES, 2 * LANES))]
    a_ab4 = join(a_ab)
    t_inv4 = each(lambda x: mss_ref[2] + x * mss_ref[3], a_ab4)
    for lvl in range(1, n_levels):
        t_b = each(bf, t_inv4)
        inner = each(lambda x, t: _mm(x * mss_ref[3 + lvl], stack4(t), NN, *P_T), a_ab4, t_b)
        t_inv4 = each(lambda t, tb, w: t + _mm(tb, stack4(bf(w)), NN, *P_T), t_inv4, t_b, inner)
        fill()
    t_inv = split(each(bf, t_inv4))
    akv = split(each(lambda x, y: _mm(x, stack4(y), NN, *P_U), join(a_ak), join(v)))
    pq = each(lambda t, x, y: _mm(t, cat1(_stack(x, m0), _stack(bf(y), m0)), NN, *P_U), t_inv, at, akv)
    pq_b = each(bf, pq)
    bpq = each(lambda b, k, x, y: _mm(cat0(b, k), cat0(x, cat1(zblk, y)), TN, *P_S), bh, kh, pq_b, v)
    g_end = [scan["g_end"][ci:ci + 1, lanes(p)] for ci, p in chains]
    m_t = each(lambda g, x: eye * g + jnp.where(same_head, x[:, :LANES], 0.0), g_end, bpq)
    n_t = each(lambda x: jnp.where(same_head, x[:, LANES:], 0.0), bpq)
    yy = each(lambda x, y, z: _mm(x, cat0(cat1(_stack(y[:, :LANES], m0), _stack(y[:, LANES:], m0)),
                                         cat1(zblk2, z)), NN, *P_Y), a_r, pq_b, v2)
    y1 = each(lambda r, x: r + x[:, :LANES], r_f32, yy)
    y2 = each(lambda x: x[:, LANES:], yy)

    while fillers:
        fill()
    state = [z_ref[p] for p in range(N_PAIRS)]
    for ci in range(n_chunks):
        for p in range(N_PAIRS):
            i = ci * N_PAIRS + p
            both = _mm(cat0(y1[i], m_t[i]), state[p], NN, *P_S)
            y_ref[ci * CHUNK:(ci + 1) * CHUNK, lanes(p)] = both[:CHUNK] + y2[i]
            state[p] = both[CHUNK:] + n_t[i]
    for p in range(N_PAIRS):
        z_ref[p] = state[p]


def _wkv_masks():
    ri, ci = np.indices((CHUNK, 4 * CHUNK))
    ci = ci % CHUNK
    side = [ci < ri, ci <= ri, ri == ci]
    m = 1
    while m < CHUNK:
        side.append(((ri // (2 * m)) == (ci // (2 * m))) & ((ri // m) % 2 == 1) & ((ci // m) % 2 == 0))
        m *= 2
    rq, cq = np.indices((LANES, LANES))
    square = [rq == cq, (rq // HEAD) == (cq // HEAD)]
    rt, ct = np.indices((CHUNK, CHUNK))
    return (jnp.asarray(ct <= rt, BF16), jnp.asarray(np.stack(side), F32),
            jnp.asarray(np.stack(square), F32))


def _short_pairs(t_s):
    idx = {}
    for kind, inclusive in (("ab", False), ("ak", False), ("rb", True), ("rk", True)):
        for t in range(t_s):
            for s in range(t + 1 if inclusive else t):
                idx[(kind, t, s)] = len(idx)
    return idx


def _short_scale_kernel(r_ref, k_ref, v_ref, lw_ref, a_ref, b_ref,
                        xa_o, xr_o, bh_o, kh_o, vt_o, g4_o, as_o, *, t_s, n_s):
    slab = lambda ref, t: ref[t * n_s:(t + 1) * n_s, :].T
    cs = []
    for t in range(t_s):
        cs.append(slab(lw_ref, t) if t == 0 else cs[-1] + slab(lw_ref, t))
    a_t = [slab(a_ref, t) * (jnp.exp(cs[t - 1]) if t else 1.0) for t in range(t_s)]
    r_t = [slab(r_ref, t) * jnp.exp(cs[t]) for t in range(t_s)]
    b_raw = [slab(b_ref, t) for t in range(t_s)]
    k_raw = [slab(k_ref, t) for t in range(t_s)]
    g_inv = [jnp.exp(-cs[t]) for t in range(t_s)]
    b_t = [b_raw[t] * g_inv[t] for t in range(t_s)]
    k_t = [k_raw[t] * g_inv[t] for t in range(t_s)]
    for t in range(t_s):
        g_tail = jnp.exp(cs[t_s - 1] - cs[t])
        xa_o[t] = a_t[t]
        xr_o[t] = r_t[t]
        bh_o[t] = b_raw[t] * g_tail
        kh_o[t] = k_raw[t] * g_tail
        vt_o[t] = slab(v_ref, t)
    g4_o[...] = jnp.exp(cs[t_s - 1])
    head_sum = lambda x: jnp.sum(x.reshape(N_HEADS, HEAD, n_s), axis=1)
    left = {"ab": a_t, "ak": a_t, "rb": r_t, "rk": r_t}
    right = {"ab": b_t, "ak": k_t, "rb": b_t, "rk": k_t}
    for (kind, t, s), row in _short_pairs(t_s).items():
        as_o[row] = head_sum(left[kind][t] * right[kind][s])


def _short_state_kernel(s_ref, xa_ref, xr_ref, bh_ref, kh_ref, vt_ref, g4_ref, as_ref, so_ref, y_ref, *,
                        t_s):
    idx = _short_pairs(t_s)
    head = pl.program_id(0)
    scal = {key: as_ref[row, pl.ds(head, 1), :] for key, row in idx.items()}

    def value_row(v, carry):
        s_v = s_ref[0, v]
        vv = [vt_ref[t, pl.ds(v, 1), :] for t in range(t_s)]
        u = []
        for t in range(t_s):
            acc = jnp.sum(s_v * xa_ref[t], axis=0, keepdims=True)
            for s in range(t):
                acc = acc + scal[("ab", t, s)] * u[s] + scal[("ak", t, s)] * vv[s]
            u.append(acc)
        for t in range(t_s):
            acc = jnp.sum(s_v * xr_ref[t], axis=0, keepdims=True)
            for s in range(t + 1):
                acc = acc + scal[("rb", t, s)] * u[s] + scal[("rk", t, s)] * vv[s]
            y_ref[t, pl.ds(v, 1), :] = acc
        new = s_v * g4_ref[...]
        for s in range(t_s):
            new = new + u[s] * bh_ref[s] + vv[s] * kh_ref[s]
        so_ref[0, v] = new
        return carry

    lax.fori_loop(0, HEAD, value_row, 0, unroll=2)


def _short_wkv_call(seqs, s0, *, t_s, n_s):
    assert n_s % LANES == 0
    n_scal = len(_short_pairs(t_s))
    full = lambda shape: pl.BlockSpec(shape, lambda: tuple(0 for _ in shape))
    vec = (t_s, D_RWKV, n_s)
    xa, xr, bh, kh, vt, g4, scal = pl.pallas_call(
        functools.partial(_short_scale_kernel, t_s=t_s, n_s=n_s),
        in_specs=[full((t_s * n_s, D_RWKV))] * 6,
        out_specs=[full(vec)] * 5 + [full((D_RWKV, n_s)), full((n_scal, N_HEADS, n_s))],
        out_shape=[jax.ShapeDtypeStruct(vec, F32)] * 5 + [jax.ShapeDtypeStruct((D_RWKV, n_s), F32),
                                                          jax.ShapeDtypeStruct((n_scal, N_HEADS, n_s), F32)],
        compiler_params=pltpu.CompilerParams(vmem_limit_bytes=VMEM_LIMIT),
        name="wkv_short_scale",
    )(*seqs)
    st = pl.BlockSpec((1, HEAD, HEAD, n_s), lambda h: (h, 0, 0, 0))
    per_head = pl.BlockSpec((t_s, HEAD, n_s), lambda h: (0, h, 0))
    s_new, y = pl.pallas_call(
        functools.partial(_short_state_kernel, t_s=t_s),
        grid=(N_HEADS,),
        in_specs=[st] + [per_head] * 5 + [pl.BlockSpec((HEAD, n_s), lambda h: (h, 0)),
                                          pl.BlockSpec((n_scal, N_HEADS, n_s), lambda h: (0, 0, 0))],
        out_specs=[st, per_head],
        out_shape=[jax.ShapeDtypeStruct(s0.shape, F32), jax.ShapeDtypeStruct(vec, F32)],
        compiler_params=pltpu.CompilerParams(
            dimension_semantics=("arbitrary",), vmem_limit_bytes=VMEM_LIMIT),
        name="wkv_short_state",
    )(s0, xa, xr, bh, kh, vt, g4, scal)
    return y, s_new


def _gated_rwkv(y, bon, sg, gw_ref, gb_ref, g_ref):
    g_ones = g_ref[...]
    mu = _seg_sum(y, g_ones) * (1.0 / HEAD)
    d = y - mu
    var = _seg_sum(d * d, g_ones) * (1.0 / HEAD)
    yn = d * lax.rsqrt(var + GN_EPS) * gw_ref[...] + gb_ref[...]
    return (yn + bon) * sg


def _out_project(o_rwkv, o_pool, x, wo_ref, nf_ref):
    out = (jnp.dot(o_rwkv.astype(BF16), wo_ref[0:D_RWKV, :], preferred_element_type=F32)
           + jnp.dot(o_pool.astype(BF16), wo_ref[D_RWKV:, :], preferred_element_type=F32))
    res = x + out
    return res * lax.rsqrt(jnp.mean(res * res, axis=-1, keepdims=True) + NORM_EPS) * nf_ref[...]


def _out_kernel(y_ref, bon_ref, sg_ref, op_ref, x_ref, gw_ref, gb_ref, wo_ref, nf_ref, g_ref, o_ref):
    if len(y_ref.shape) == 3:
        y = jnp.concatenate([y_ref[t].T for t in range(y_ref.shape[0])], axis=0)
    else:
        y = y_ref[...]
    o_rwkv = _gated_rwkv(y, bon_ref[...], sg_ref[...], gw_ref, gb_ref, g_ref)
    o_ref[...] = _out_project(o_rwkv, op_ref[...], x_ref[...], wo_ref, nf_ref)


def _out_call(y, bon, sg, op, x, gn_w, gn_b, w_out, norm_f, g_ones, *, rows):
    total = x.shape[0]
    row_blk = lambda i: (i, 0)
    const2 = lambda i: (0, 0)
    act = lambda w: pl.BlockSpec((rows, w), row_blk)
    if y.ndim == 3:
        assert total == rows == y.shape[0] * y.shape[2]
        y_spec = pl.BlockSpec(y.shape, lambda i: (0, 0, 0))
    else:
        y_spec = act(D_RWKV)
    return pl.pallas_call(
        _out_kernel,
        grid=(total // rows,),
        in_specs=[y_spec] + [act(D_RWKV)] * 3 + [act(D_MODEL),
                                                 pl.BlockSpec((1, D_RWKV), const2),
                                                 pl.BlockSpec((1, D_RWKV), const2),
                                                 pl.BlockSpec((D_RWKV + D_POOL, D_MODEL), const2),
                                                 pl.BlockSpec((1, D_MODEL), const2),
                                                 pl.BlockSpec((MXU_DIM, MXU_DIM), const2)],
        out_specs=act(D_MODEL),
        out_shape=jax.ShapeDtypeStruct((total, D_MODEL), F32),
        compiler_params=pltpu.CompilerParams(
            dimension_semantics=("arbitrary",), vmem_limit_bytes=VMEM_LIMIT),
        name="out",
    )(y, bon, sg, op, x, gn_w, gn_b, w_out, norm_f, g_ones)


N_PREP_PARAMS = 12
N_WKV_CONSTS = 3
N_OUT_PARAMS = 5
SCAN_OPERANDS = ("a_t", "r_t", "b_t", "k_t", "b_h", "k_h", "v", "r_f32", "g_end")


def _fused_kernel(*refs, n_chunks, t0):
    x_ref, nw_ref, win_ref = refs[:3]
    pos = 1 + N_PREP_PARAMS
    prep_rest = refs[3:pos]
    wkv_consts = refs[pos:pos + N_WKV_CONSTS]
    pos += N_WKV_CONSTS
    gw_ref, gb_ref, wo_ref, nf_ref, g_ref = refs[pos:pos + N_OUT_PARAMS]
    pos += N_OUT_PARAMS
    o_ref, nsh_o, npl_o, so_ref = refs[pos:pos + 4]
    pos += 4
    r_s, k_s, v_s, bon_s, sg_s, op_s, y_s, p_ext, u_ext, z_ref = refs[pos:pos + 10]
    tri_ref, mss_ref, msq_ref = wkv_consts
    scan = dict(zip(SCAN_OPERANDS, refs[pos + 10:]), tri=tri_ref)
    j = pl.program_id(1)
    _wkv_zero_state(z_ref, j == 0)
    finish_prep = _prep_stages(*_norm_operands(x_ref, nw_ref), win_ref, j, None, None, *prep_rest,
                               r_s, k_s, v_s, None, None, None, bon_s, sg_s, op_s, nsh_o, npl_o,
                               p_ext, u_ext, stride=1, t0=t0, hs=nsh_o.shape[1], scan=scan)
    _wkv_tile(scan, mss_ref, msq_ref, y_s, z_ref, n_chunks=n_chunks, fillers=finish_prep)
    _out_kernel(y_s, bon_s, sg_s, op_s, x_ref, gw_ref, gb_ref, wo_ref, nf_ref, g_ref, o_ref)
    _wkv_store_state(z_ref, so_ref, j == pl.num_programs(1) - 1)


def _fused_call(x, prep_params, out_params, *, n_seq, n_chunks, t0):
    assert len(prep_params) == N_PREP_PARAMS and len(out_params) == N_OUT_PARAMS
    total = x.shape[0]
    rows = n_chunks * CHUNK
    tiles = total // (n_seq * rows)
    hs, ph = SUBLANES, _pool_hist_rows(1)
    assert CHUNK == HEAD and 2 * CHUNK == LANES
    wkv_consts = _wkv_masks()
    act = pl.BlockSpec((rows, D_MODEL), lambda b, j: (b * tiles + j, 0))
    per_seq = lambda shape: pl.BlockSpec((1,) + shape[1:], lambda b, j: (b,) + (0,) * (len(shape) - 1))
    const = lambda a: pl.BlockSpec(a.shape, lambda b, j: (0,) * a.ndim, pipeline_mode=pl.Buffered(1))
    consts = tuple(prep_params) + tuple(wkv_consts) + tuple(out_params)
    seq_shapes = [(n_seq, hs, D_SHIFT), (n_seq, ph, D_POOL), (n_seq, N_HEADS, HEAD, HEAD)]
    seq_scratch = pltpu.VMEM((rows, D_RWKV), F32)
    scan_scratch = [pltpu.VMEM((rows, D_RWKV), BF16)] * 7 + [seq_scratch, pltpu.VMEM((n_chunks, D_RWKV), F32)]
    return pl.pallas_call(
        functools.partial(_fused_kernel, n_chunks=n_chunks, t0=t0),
        grid=(n_seq, tiles),
        in_specs=[act] + [const(a) for a in consts],
        out_specs=[act] + [per_seq(s) for s in seq_shapes],
        out_shape=[jax.ShapeDtypeStruct((total, D_MODEL), F32)]
        + [jax.ShapeDtypeStruct(s, F32) for s in seq_shapes],
        scratch_shapes=[seq_scratch] * 7 + [pltpu.VMEM((hs + rows, D_SHIFT), F32),
                                            pltpu.VMEM((ph + rows, D_POOL), F32),
                                            pltpu.VMEM((N_PAIRS, LANES, LANES), F32)] + scan_scratch,
        compiler_params=pltpu.CompilerParams(
            dimension_semantics=("arbitrary", "arbitrary"), vmem_limit_bytes=VMEM_LIMIT),
        name="layer_long",
    )(x, *consts)


def _head_ones():
    i, j = np.indices((MXU_DIM, MXU_DIM))
    return jnp.asarray((i // HEAD) == (j // HEAD), BF16)


def kernel(x_prompt, x_sample, state_shift, state_wkv, state_pool, norm_w, w_in, mu_shift, w_decay_b,
           w0, w_aaa_b, a0, k_k, k_a, r_k, gn_w, gn_b, pool_w, pool_scale, w_out, norm_f):
    depth = norm_w.shape[0]
    n_p, t_p, _ = x_prompt.shape
    n_s, t_s, _ = x_sample.shape
    assert depth == 1, "a stacked trunk needs a residual-only output kernel between layers"
    assert t_p % (WKV_CHUNKS_PER_STEP * CHUNK) == 0
    g_ones = _head_ones()
    row = lambda z: z.reshape(1, -1).astype(F32)

    hp = x_prompt.astype(F32).reshape(n_p * t_p, D_MODEL)
    hs = jnp.transpose(x_sample.astype(F32), (1, 0, 2)).reshape(t_s * n_s, D_MODEL)
    outs = [[] for _ in range(6)]
    for l in range(depth):
        zl = jnp.zeros((D_LORA, D_RWKV), F32)
        w_lora = jnp.concatenate([jnp.concatenate([w_decay_b[l], zl], axis=1),
                                  jnp.concatenate([zl, w_aaa_b[l]], axis=1)], axis=0).astype(BF16)
        prep_params = (row(norm_w[l]), w_in[l].astype(BF16), row(mu_shift[l]), w_lora, row(w0[l]),
                       row(a0[l]), row(k_k[l]), row(k_a[l]), row(r_k[l]), pool_w[l].astype(BF16),
                       row(pool_scale[l]), g_ones)
        out_params = (row(gn_w[l]), row(gn_b[l]), w_out[l].astype(BF16), row(norm_f), g_ones)

        hp, nsh, npl, s_fin = _fused_call(hp, prep_params, out_params,
                                          n_seq=n_p, n_chunks=WKV_CHUNKS_PER_STEP, t0=0)
        outs[0].append(nsh[:, SUBLANES - 1])
        outs[1].append(s_fin)
        outs[2].append(npl[:, -POOL_KEEP:])

        sh_hist = state_shift[l].astype(F32)[None]
        pool_hist = jnp.transpose(state_pool[l].astype(F32), (1, 0, 2)).reshape(1, POOL_KEEP * n_s, D_POOL)
        res = _prep_call(hs, sh_hist, pool_hist, prep_params,
                         n_seq=1, rows=t_s * n_s, stride=n_s, t0=PAST_LEN)
        seqs, (bon, sg, op), nsh, npl = res[:6], res[6:9], res[9], res[10]

        y, s_fin = _short_wkv_call(seqs, jnp.transpose(state_wkv[l].astype(F32), (1, 2, 3, 0)),
                                   t_s=t_s, n_s=n_s)
        s_fin = jnp.transpose(s_fin, (3, 0, 1, 2))
        hs = _out_call(y, bon, sg, op, hs, *out_params, rows=t_s * n_s)
        outs[3].append(nsh[0])
        outs[4].append(s_fin)
        outs[5].append(jnp.transpose(npl[0].reshape(POOL_KEEP, n_s, D_POOL), (1, 0, 2)))

    y_prompt = hp.reshape(n_p, t_p, D_MODEL).astype(x_prompt.dtype)
    y_sample = jnp.transpose(hs.reshape(t_s, n_s, D_MODEL), (1, 0, 2)).astype(x_sample.dtype)
    return (y_prompt, y_sample) + tuple(jnp.stack(o, axis=0) for o in outs)
```

```python
import functools

import jax
import jax.numpy as jnp
import numpy as np
from jax import lax
from jax.experimental import pallas as pl
from jax.experimental.pallas import tpu as pltpu

F32 = jnp.float32
BF16 = jnp.bfloat16

D_MODEL = 1024
D_RWKV = 512
D_POOL = 512
HEAD = 64
N_HEADS = D_RWKV // HEAD
LANES = 128
SUBLANES = 8
MXU_DIM = 256
N_PAIRS = D_RWKV // LANES
D_LORA = 64
POOL_WINDOWS = (2, 4, 8, 16)
POOL_GROUP = D_POOL // len(POOL_WINDOWS)
POOL_KEEP = max(POOL_WINDOWS) - 1
D_SHIFT = 3 * D_RWKV + 2 * D_LORA
D_IN = D_SHIFT + D_RWKV + 2 * D_POOL
PAST_LEN = 16384
NORM_EPS = 1e-6
GN_EPS = 64e-5
L2_EPS = 1e-12
CHUNK = 64
VMEM_LIMIT = 56 * 1024 * 1024
P_A = (1, 1)
P_T = (1, 1)
P_U = (1, 1)
P_S = (1, 1)
P_Y = (1, 1)
P_CUMSUM = 2
WKV_CHUNKS_PER_STEP = 8

NN = (((1,), (0,)), ((), ()))
NT = (((1,), (1,)), ((), ()))
TN = (((0,), (0,)), ((), ()))


def _pieces(x, n):
    if x.dtype == BF16:
        return [x]
    out = []
    rem = x
    for i in range(n):
        p = rem.astype(BF16)
        out.append(p)
        if i + 1 < n:
            rem = rem - p.astype(F32)
    return out


def _mm(a, b, dims=NN, na=1, nb=1):
    ap = _pieces(a, na)
    bp = _pieces(b, nb)
    order = max(len(ap), len(bp))
    acc = None
    for i, x in enumerate(ap):
        for j, y in enumerate(bp):
            if i + j >= order:
                continue
            t = lax.dot_general(x, y, dims, preferred_element_type=F32)
            acc = t if acc is None else acc + t
    return acc


def _seg_sum(x, g_ones, pieces=1):
    cols = []
    for s in range(x.shape[1] // MXU_DIM):
        cols.append(_mm(x[:, s * MXU_DIM:(s + 1) * MXU_DIM], g_ones, NN, pieces, 1))
    return cols[0] if len(cols) == 1 else jnp.concatenate(cols, axis=1)


def _sigmoid(x):
    return 1.0 / (1.0 + jnp.exp(-x))


def _pool_hist_rows(stride):
    return -(-POOL_KEEP * stride // SUBLANES) * SUBLANES


def _norm_operands(x_ref, nw_ref):
    x = x_ref[...]
    rstd = lax.rsqrt(jnp.mean(x * x, axis=-1, keepdims=True) + NORM_EPS)
    return (x * nw_ref[...]).astype(BF16), rstd


def _prep_stages(h, rstd, win_ref, j, sh_ref, ph_ref, mu_ref, wl_ref, w0_ref, a0_ref, kk_ref,
                 ka_ref, rk_ref, pw_ref, ps_ref, g_ref,
                 r_o, k_o, v_o, lw_o, a_o, b_o, bon_o, sg_o, op_o, nsh_o, npl_o,
                 p_ext, u_ext, *, stride, t0, hs):
    rows = h.shape[0]
    ph = _pool_hist_rows(stride)
    width = MXU_DIM

    @pl.when(j == 0)
    def _():
        p_ext[0:hs, :] = jnp.zeros((hs, D_SHIFT), F32) if sh_ref is None else sh_ref[0]
        u_ext[0:ph, :] = jnp.zeros((ph, D_POOL), F32) if ph_ref is None else ph_ref[0]

    def proj_cols(c0, c1):
        return jnp.dot(h, win_ref[:, c0:c1], preferred_element_type=F32) * rstd

    def shifted(c0, c1):
        p = proj_cols(c0, c1)
        p_ext[hs:hs + rows, c0:c1] = p
        prev = p_ext[pl.ds(hs - stride, rows), c0:c1]
        return p + mu_ref[:, c0:c1] * (prev - p)

    xwa = shifted(3 * D_RWKV, D_SHIFT)
    for c in range(0, D_RWKV, width):
        r_o[:, c:c + width] = shifted(c, c + width)
    lane = lax.broadcasted_iota(jnp.int32, xwa.shape, 1)
    lora_in = jnp.where(lane < D_LORA, jnp.tanh(xwa), xwa)
    lora = jnp.dot(lora_in.astype(BF16), wl_ref[...], preferred_element_type=F32)
    lw_o[...] = (-0.6065306597126334) * _sigmoid(w0_ref[...] + lora[:, :D_RWKV])
    alpha = _sigmoid(a0_ref[...] + lora[:, D_RWKV:])
    for c in range(0, D_RWKV, width):
        sl = slice(c, c + width)
        k = shifted(D_RWKV + c, D_RWKV + c + width)
        k_o[:, sl] = k * (1.0 + (alpha[:, sl] - 1.0) * ka_ref[:, sl])
        kk = k * kk_ref[:, sl]
        kk = kk * lax.rsqrt(_seg_sum(kk * kk, g_ref[...]) + L2_EPS)
        a_o[:, sl] = -kk
        b_o[:, sl] = kk * alpha[:, sl]
        v_o[:, sl] = shifted(2 * D_RWKV + c, 2 * D_RWKV + c + width)
    last_p = p_ext[rows:rows + hs, :]
    p_ext[0:hs, :] = last_p
    nsh_o[0] = last_p
    for c in range(0, D_RWKV, width):
        g_rwkv = proj_cols(D_SHIFT + c, D_SHIFT + c + width)
        sg_o[:, c:c + width] = g_rwkv * _sigmoid(g_rwkv)
    for c in range(0, D_POOL, width):
        u_ext[ph:ph + rows, c:c + width] = proj_cols(D_SHIFT + D_RWKV + c, D_SHIFT + D_RWKV + c + width)
        op_o[:, c:c + width] = proj_cols(D_SHIFT + D_RWKV + D_POOL + c,
                                         D_SHIFT + D_RWKV + D_POOL + c + width)

    def bonus():
        bon_o[...] = _seg_sum(r_o[...] * k_o[...] * rk_ref[...], g_ref[...]) * v_o[...]

    def pool(gi, win):
        sl = slice(gi * POOL_GROUP, (gi + 1) * POOL_GROUP)
        row = lax.broadcasted_iota(jnp.int32, (rows, 1), 0)
        pos = t0 + (j * rows + row) // stride
        u = u_ext[ph:ph + rows, sl]
        tot = u
        for back in range(1, win):
            tot = tot + u_ext[pl.ds(ph - back * stride, rows), sl]
        inv_cnt = 1.0 / jnp.minimum(pos + 1, win).astype(F32)
        d = tot * inv_cnt - u
        o = jnp.dot(d.astype(BF16), pw_ref[gi], preferred_element_type=F32)
        gp = op_o[:, sl]
        op_o[:, sl] = o * ps_ref[:, sl] * (gp * _sigmoid(gp))

    def carry_pool():
        last_u = u_ext[rows:rows + ph, :]
        u_ext[0:ph, :] = last_u
        npl_o[0] = last_u

    pools = [functools.partial(pool, gi, win) for gi, win in enumerate(POOL_WINDOWS)]
    return [bonus] + pools + [carry_pool]


def _stack(x, m0):
    return jnp.concatenate([jnp.where(m0, x, 0.0), jnp.where(m0, 0.0, x)], axis=0)


def _wkv_zero_state(z_ref, is_first):
    @pl.when(is_first)
    def _():
        z_ref[...] = jnp.zeros(z_ref.shape, F32)


def _wkv_store_state(z_ref, so_ref, is_last):
    @pl.when(is_last)
    def _():
        for p in range(N_PAIRS):
            s2 = z_ref[p].T
            so_ref[0, 2 * p] = s2[:HEAD, :HEAD]
            so_ref[0, 2 * p + 1] = s2[HEAD:, HEAD:]


def _wkv_tile(r_ref, k_ref, v_ref, lw_ref, a_ref, b_ref, tri_ref, mss_ref, msq_ref, y_ref, z_ref, *,
              n_chunks, fillers=()):
    fillers = list(fillers)
    fill = lambda: fillers.pop(0)() if fillers else None
    lane = lax.broadcasted_iota(jnp.int32, (CHUNK, LANES), 1)
    m0 = lane < HEAD
    strict = mss_ref[0][:, :LANES] > 0.5
    incl2 = mss_ref[1] > 0.5
    n_levels = mss_ref.shape[0] - 3
    lane_head4 = lax.broadcasted_iota(jnp.int32, (CHUNK, 2 * LANES), 1) // HEAD
    eye = msq_ref[0]
    same_head = msq_ref[1] > 0.5
    zblk = jnp.zeros((CHUNK, LANES), BF16)
    zblk2 = jnp.zeros((2 * CHUNK, LANES), BF16)
    bf = lambda x: x.astype(BF16)
    lanes = lambda p: slice(p * LANES, (p + 1) * LANES)
    each = lambda fn, *lists: [fn(*args) for args in zip(*lists)]
    cat0 = lambda *xs: jnp.concatenate(xs, axis=0)
    cat1 = lambda *xs: jnp.concatenate(xs, axis=1)

    scaled = []
    for ci in range(n_chunks):
        rows = slice(ci * CHUNK, (ci + 1) * CHUNK)
        lw = lw_ref[rows, :]
        cs = _mm(tri_ref[...], lw, NN, 1, P_CUMSUM)
        cs_end = cs[CHUNK - 1:CHUNK, :]
        g_inv = jnp.exp(-cs)
        g_tail = jnp.exp(cs_end - cs)
        r_t = r_ref[rows, :] * jnp.exp(cs)
        scaled.append(dict(
            g_end=jnp.exp(cs_end), r_f32=r_t, r_t=bf(r_t),
            a_t=bf(a_ref[rows, :] * jnp.exp(cs - lw)),
            b_t=bf(b_ref[rows, :] * g_inv), k_t=bf(k_ref[rows, :] * g_inv),
            b_h=bf(b_ref[rows, :] * g_tail), k_h=bf(k_ref[rows, :] * g_tail), v=bf(v_ref[rows, :])))
    chains = [(ci, p) for ci in range(n_chunks) for p in range(N_PAIRS)]
    pair = lambda name: [scaled[ci][name][:, lanes(p)] for ci, p in chains]
    at, rt, bt, kt, bh, kh, v, r_f32 = (pair(n) for n in ("a_t", "r_t", "b_t", "k_t", "b_h", "k_h", "v",
                                                          "r_f32"))
    v2 = each(lambda x: _stack(x, m0), v)
    scores = each(lambda a, r, b, k: _mm(cat0(a, r), cat0(_stack(b, m0), _stack(k, m0)), NT, *P_A),
                  at, rt, bt, kt)
    a_ab = each(lambda s: jnp.where(strict, s[:CHUNK, :LANES], 0.0), scores)
    a_ak = each(lambda s: jnp.where(strict, s[:CHUNK, LANES:], 0.0), scores)
    a_r = each(lambda s: jnp.where(incl2, s[CHUNK:, :], 0.0), scores)
    fill()
    stack4 = lambda x: cat0(*[jnp.where(lane_head4 == h, x, 0.0) for h in range(4)])
    join = lambda xs: [cat1(xs[i], xs[i + 1]) for i in range(0, len(xs), 2)]
    split = lambda xs: [x[:, s] for x in xs for s in (slice(0, LANES), slice(LANES, 2 * LANES))]
    a_ab4 = join(a_ab)
    t_inv4 = each(lambda x: mss_ref[2] + x * mss_ref[3], a_ab4)
    for lvl in range(1, n_levels):
        t_b = each(bf, t_inv4)
        inner = each(lambda x, t: _mm(x * mss_ref[3 + lvl], stack4(t), NN, *P_T), a_ab4, t_b)
        t_inv4 = each(lambda t, tb, w: t + _mm(tb, stack4(bf(w)), NN, *P_T), t_inv4, t_b, inner)
        fill()
    t_inv = split(each(bf, t_inv4))
    akv = split(each(lambda x, y: _mm(x, stack4(y), NN, *P_U), join(a_ak), join(v)))
    pq = each(lambda t, x, y: _mm(t, cat1(_stack(x, m0), _stack(bf(y), m0)), NN, *P_U), t_inv, at, akv)
    pq_b = each(bf, pq)
    bpq = each(lambda b, k, x, y: _mm(cat0(b, k), cat0(x, cat1(zblk, y)), TN, *P_S), bh, kh, pq_b, v)
    g_end = [scaled[ci]["g_end"][:, lanes(p)] for ci, p in chains]
    m_t = each(lambda g, x: eye * g + jnp.where(same_head, x[:, :LANES], 0.0), g_end, bpq)
    n_t = each(lambda x: jnp.where(same_head, x[:, LANES:], 0.0), bpq)
    yy = each(lambda x, y, z: _mm(x, cat0(cat1(_stack(y[:, :LANES], m0), _stack(y[:, LANES:], m0)),
                                         cat1(zblk2, z)), NN, *P_Y), a_r, pq_b, v2)
    y1 = each(lambda r, x: r + x[:, :LANES], r_f32, yy)
    y2 = each(lambda x: x[:, LANES:], yy)

    while fillers:
        fill()
    state = [z_ref[p] for p in range(N_PAIRS)]
    for ci in range(n_chunks):
        for p in range(N_PAIRS):
            i = ci * N_PAIRS + p
            both = _mm(cat0(y1[i], m_t[i]), state[p], NN, *P_S)
            y_ref[ci * CHUNK:(ci + 1) * CHUNK, lanes(p)] = both[:CHUNK] + y2[i]
            state[p] = both[CHUNK:] + n_t[i]
    for p in range(N_PAIRS):
        z_ref[p] = state[p]


def _wkv_masks():
    ri, ci = np.indices((CHUNK, 4 * CHUNK))
    ci = ci % CHUNK
    side = [ci < ri, ci <= ri, ri == ci]
    m = 1
    while m < CHUNK:
        side.append(((ri // (2 * m)) == (ci // (2 * m))) & ((ri // m) % 2 == 1) & ((ci // m) % 2 == 0))
        m *= 2
    rq, cq = np.indices((LANES, LANES))
    square = [rq == cq, (rq // HEAD) == (cq // HEAD)]
    rt, ct = np.indices((CHUNK, CHUNK))
    return (jnp.asarray(ct <= rt, BF16), jnp.asarray(np.stack(side), F32),
            jnp.asarray(np.stack(square), F32))


def _short_pairs(t_s):
    idx = {}
    for kind, inclusive in (("ab", False), ("ak", False), ("rb", True), ("rk", True)):
        for t in range(t_s):
            for s in range(t + 1 if inclusive else t):
                idx[(kind, t, s)] = len(idx)
    return idx


def _short_scale_kernel(r_ref, k_ref, v_ref, lw_ref, a_ref, b_ref,
                        xa_o, xr_o, bh_o, kh_o, vt_o, g4_o, as_o, *, t_s, n_s):
    slab = lambda ref, t: ref[t * n_s:(t + 1) * n_s, :].T
    cs = []
    for t in range(t_s):
        cs.append(slab(lw_ref, t) if t == 0 else cs[-1] + slab(lw_ref, t))
    a_t = [slab(a_ref, t) * (jnp.exp(cs[t - 1]) if t else 1.0) for t in range(t_s)]
    r_t = [slab(r_ref, t) * jnp.exp(cs[t]) for t in range(t_s)]
    b_raw = [slab(b_ref, t) for t in range(t_s)]
    k_raw = [slab(k_ref, t) for t in range(t_s)]
    g_inv = [jnp.exp(-cs[t]) for t in range(t_s)]
    b_t = [b_raw[t] * g_inv[t] for t in range(t_s)]
    k_t = [k_raw[t] * g_inv[t] for t in range(t_s)]
    for t in range(t_s):
        g_tail = jnp.exp(cs[t_s - 1] - cs[t])
        xa_o[t] = a_t[t]
        xr_o[t] = r_t[t]
        bh_o[t] = b_raw[t] * g_tail
        kh_o[t] = k_raw[t] * g_tail
        vt_o[t] = slab(v_ref, t)
    g4_o[...] = jnp.exp(cs[t_s - 1])
    head_sum = lambda x: jnp.sum(x.reshape(N_HEADS, HEAD, n_s), axis=1)
    left = {"ab": a_t, "ak": a_t, "rb": r_t, "rk": r_t}
    right = {"ab": b_t, "ak": k_t, "rb": b_t, "rk": k_t}
    for (kind, t, s), row in _short_pairs(t_s).items():
        as_o[row] = head_sum(left[kind][t] * right[kind][s])


def _short_state_kernel(s_ref, xa_ref, xr_ref, bh_ref, kh_ref, vt_ref, g4_ref, as_ref, so_ref, y_ref, *,
                        t_s):
    idx = _short_pairs(t_s)
    head = pl.program_id(0)
    scal = {key: as_ref[row, pl.ds(head, 1), :] for key, row in idx.items()}

    def value_row(v, carry):
        s_v = s_ref[0, v]
        vv = [vt_ref[t, pl.ds(v, 1), :] for t in range(t_s)]
        u = []
        for t in range(t_s):
            acc = jnp.sum(s_v * xa_ref[t], axis=0, keepdims=True)
            for s in range(t):
                acc = acc + scal[("ab", t, s)] * u[s] + scal[("ak", t, s)] * vv[s]
            u.append(acc)
        for t in range(t_s):
            acc = jnp.sum(s_v * xr_ref[t], axis=0, keepdims=True)
            for s in range(t + 1):
                acc = acc + scal[("rb", t, s)] * u[s] + scal[("rk", t, s)] * vv[s]
            y_ref[t, pl.ds(v, 1), :] = acc
        new = s_v * g4_ref[...]
        for s in range(t_s):
            new = new + u[s] * bh_ref[s] + vv[s] * kh_ref[s]
        so_ref[0, v] = new
        return carry

    lax.fori_loop(0, HEAD, value_row, 0, unroll=2)


def _short_prep_kernel(x_ref, sh_ref, ph_ref, nw_ref, win_ref, *refs, t_s, n_s, t0):
    prep_rest, refs = refs[:N_PREP_PARAMS - 2], refs[N_PREP_PARAMS - 2:]
    bon_o, sg_o, op_o, nsh_o, npl_o = refs[:5]
    scale_outs, seq_s, (p_ext, u_ext) = refs[5:12], refs[12:18], refs[18:]
    h, rstd = _norm_operands(x_ref, nw_ref)
    for finish in _prep_stages(h, rstd, win_ref, pl.program_id(0), sh_ref, ph_ref, *prep_rest, *seq_s,
                               bon_o, sg_o, op_o, nsh_o, npl_o, p_ext, u_ext,
                               stride=n_s, t0=t0, hs=sh_ref.shape[1]):
        finish()
    _short_scale_kernel(*seq_s, *scale_outs, t_s=t_s, n_s=n_s)


def _short_prep_call(x, sh_hist, pool_hist, params, *, t_s, n_s, t0):
    assert n_s % LANES == 0 and len(params) == N_PREP_PARAMS
    rows = t_s * n_s
    hs, ph = sh_hist.shape[1], _pool_hist_rows(n_s)
    assert pool_hist.shape[1] == ph
    n_scal = len(_short_pairs(t_s))
    full = lambda shape: pl.BlockSpec(shape, lambda i: tuple(0 for _ in shape))
    vec = (t_s, D_RWKV, n_s)
    out_shapes = ([(rows, D_RWKV)] * 3 + [(1, hs, D_SHIFT), (1, ph, D_POOL)] + [vec] * 5
                  + [(D_RWKV, n_s), (n_scal, N_HEADS, n_s)])
    return pl.pallas_call(
        functools.partial(_short_prep_kernel, t_s=t_s, n_s=n_s, t0=t0),
        grid=(1,),
        in_specs=[full(a.shape) for a in (x, sh_hist, pool_hist) + tuple(params)],
        out_specs=[full(s) for s in out_shapes],
        out_shape=[jax.ShapeDtypeStruct(s, F32) for s in out_shapes],
        scratch_shapes=[pltpu.VMEM((rows, D_RWKV), F32)] * 6 + [pltpu.VMEM((hs + rows, D_SHIFT), F32),
                                                                pltpu.VMEM((ph + rows, D_POOL), F32)],
        compiler_params=pltpu.CompilerParams(
            dimension_semantics=("arbitrary",), vmem_limit_bytes=VMEM_LIMIT),
        name="prep_short",
    )(x, sh_hist, pool_hist, *params)


def _short_state_call(s0, xa, xr, bh, kh, vt, g4, scal):
    t_s, _, n_s = xa.shape
    st = pl.BlockSpec((1, HEAD, HEAD, n_s), lambda h: (h, 0, 0, 0))
    per_head = pl.BlockSpec((t_s, HEAD, n_s), lambda h: (0, h, 0))
    return pl.pallas_call(
        functools.partial(_short_state_kernel, t_s=t_s),
        grid=(N_HEADS,),
        in_specs=[st] + [per_head] * 5 + [pl.BlockSpec((HEAD, n_s), lambda h: (h, 0)),
                                          pl.BlockSpec(scal.shape, lambda h: (0, 0, 0))],
        out_specs=[st, per_head],
        out_shape=[jax.ShapeDtypeStruct(s0.shape, F32), jax.ShapeDtypeStruct(xa.shape, F32)],
        compiler_params=pltpu.CompilerParams(
            dimension_semantics=("arbitrary",), vmem_limit_bytes=VMEM_LIMIT),
        name="wkv_short_state",
    )(s0, xa, xr, bh, kh, vt, g4, scal)


def _gated_rwkv(y, bon, sg, gw_ref, gb_ref, g_ref):
    g_ones = g_ref[...]
    mu = _seg_sum(y, g_ones) * (1.0 / HEAD)
    d = y - mu
    var = _seg_sum(d * d, g_ones) * (1.0 / HEAD)
    yn = d * lax.rsqrt(var + GN_EPS) * gw_ref[...] + gb_ref[...]
    return (yn + bon) * sg


def _out_project(o_rwkv, o_pool, x, wo_ref, nf_ref):
    out = (jnp.dot(o_rwkv.astype(BF16), wo_ref[0:D_RWKV, :], preferred_element_type=F32)
           + jnp.dot(o_pool.astype(BF16), wo_ref[D_RWKV:, :], preferred_element_type=F32))
    res = x + out
    return res * lax.rsqrt(jnp.mean(res * res, axis=-1, keepdims=True) + NORM_EPS) * nf_ref[...]


def _out_kernel(y_ref, bon_ref, sg_ref, op_ref, x_ref, gw_ref, gb_ref, wo_ref, nf_ref, g_ref, o_ref):
    if len(y_ref.shape) == 3:
        y = jnp.concatenate([y_ref[t].T for t in range(y_ref.shape[0])], axis=0)
    else:
        y = y_ref[...]
    o_rwkv = _gated_rwkv(y, bon_ref[...], sg_ref[...], gw_ref, gb_ref, g_ref)
    o_ref[...] = _out_project(o_rwkv, op_ref[...], x_ref[...], wo_ref, nf_ref)


def _out_call(y, bon, sg, op, x, gn_w, gn_b, w_out, norm_f, g_ones, *, rows):
    total = x.shape[0]
    row_blk = lambda i: (i, 0)
    const2 = lambda i: (0, 0)
    act = lambda w: pl.BlockSpec((rows, w), row_blk)
    if y.ndim == 3:
        assert total == rows == y.shape[0] * y.shape[2]
        y_spec = pl.BlockSpec(y.shape, lambda i: (0, 0, 0))
    else:
        y_spec = act(D_RWKV)
    return pl.pallas_call(
        _out_kernel,
        grid=(total // rows,),
        in_specs=[y_spec] + [act(D_RWKV)] * 3 + [act(D_MODEL),
                                                 pl.BlockSpec((1, D_RWKV), const2),
                                                 pl.BlockSpec((1, D_RWKV), const2),
                                                 pl.BlockSpec((D_RWKV + D_POOL, D_MODEL), const2),
                                                 pl.BlockSpec((1, D_MODEL), const2),
                                                 pl.BlockSpec((MXU_DIM, MXU_DIM), const2)],
        out_specs=act(D_MODEL),
        out_shape=jax.ShapeDtypeStruct((total, D_MODEL), F32),
        compiler_params=pltpu.CompilerParams(
            dimension_semantics=("arbitrary",), vmem_limit_bytes=VMEM_LIMIT),
        name="out",
    )(y, bon, sg, op, x, gn_w, gn_b, w_out, norm_f, g_ones)


N_PREP_PARAMS = 12
N_WKV_CONSTS = 3
N_OUT_PARAMS = 5


def _fused_kernel(*refs, n_chunks, t0):
    x_ref, nw_ref, win_ref = refs[:3]
    pos = 1 + N_PREP_PARAMS
    prep_rest = refs[3:pos]
    wkv_consts = refs[pos:pos + N_WKV_CONSTS]
    pos += N_WKV_CONSTS
    gw_ref, gb_ref, wo_ref, nf_ref, g_ref = refs[pos:pos + N_OUT_PARAMS]
    pos += N_OUT_PARAMS
    o_ref, nsh_o, npl_o, so_ref = refs[pos:pos + 4]
    pos += 4
    seq_s = refs[pos:pos + 6]
    bon_s, sg_s, op_s, y_s, p_ext, u_ext, z_ref = refs[pos + 6:]
    j = pl.program_id(1)
    _wkv_zero_state(z_ref, j == 0)
    finish_prep = _prep_stages(*_norm_operands(x_ref, nw_ref), win_ref, j, None, None, *prep_rest,
                               *seq_s, bon_s, sg_s, op_s, nsh_o, npl_o, p_ext, u_ext,
                               stride=1, t0=t0, hs=nsh_o.shape[1])
    _wkv_tile(*seq_s, *wkv_consts, y_s, z_ref, n_chunks=n_chunks, fillers=finish_prep)
    _out_kernel(y_s, bon_s, sg_s, op_s, x_ref, gw_ref, gb_ref, wo_ref, nf_ref, g_ref, o_ref)
    _wkv_store_state(z_ref, so_ref, j == pl.num_programs(1) - 1)


def _fused_call(x, prep_params, out_params, *, n_seq, n_chunks, t0):
    assert len(prep_params) == N_PREP_PARAMS and len(out_params) == N_OUT_PARAMS
    total = x.shape[0]
    rows = n_chunks * CHUNK
    tiles = total // (n_seq * rows)
    hs, ph = SUBLANES, _pool_hist_rows(1)
    assert CHUNK == HEAD and 2 * CHUNK == LANES
    wkv_consts = _wkv_masks()
    act = pl.BlockSpec((rows, D_MODEL), lambda b, j: (b * tiles + j, 0))
    per_seq = lambda shape: pl.BlockSpec((1,) + shape[1:], lambda b, j: (b,) + (0,) * (len(shape) - 1))
    const = lambda a: pl.BlockSpec(a.shape, lambda b, j: (0,) * a.ndim, pipeline_mode=pl.Buffered(1))
    consts = tuple(prep_params) + tuple(wkv_consts) + tuple(out_params)
    seq_shapes = [(n_seq, hs, D_SHIFT), (n_seq, ph, D_POOL), (n_seq, N_HEADS, HEAD, HEAD)]
    seq_scratch = pltpu.VMEM((rows, D_RWKV), F32)
    return pl.pallas_call(
        functools.partial(_fused_kernel, n_chunks=n_chunks, t0=t0),
        grid=(n_seq, tiles),
        in_specs=[act] + [const(a) for a in consts],
        out_specs=[act] + [per_seq(s) for s in seq_shapes],
        out_shape=[jax.ShapeDtypeStruct((total, D_MODEL), F32)]
        + [jax.ShapeDtypeStruct(s, F32) for s in seq_shapes],
        scratch_shapes=[seq_scratch] * 10 + [pltpu.VMEM((hs + rows, D_SHIFT), F32),
                                             pltpu.VMEM((ph + rows, D_POOL), F32),
                                             pltpu.VMEM((N_PAIRS, LANES, LANES), F32)],
        compiler_params=pltpu.CompilerParams(
            dimension_semantics=("arbitrary", "arbitrary"), vmem_limit_bytes=VMEM_LIMIT),
        name="layer_long",
    )(x, *consts)


def _head_ones():
    i, j = np.indices((MXU_DIM, MXU_DIM))
    return jnp.asarray((i // HEAD) == (j // HEAD), BF16)


def kernel(x_prompt, x_sample, state_shift, state_wkv, state_pool, norm_w, w_in, mu_shift, w_decay_b,
           w0, w_aaa_b, a0, k_k, k_a, r_k, gn_w, gn_b, pool_w, pool_scale, w_out, norm_f):
    depth = norm_w.shape[0]
    n_p, t_p, _ = x_prompt.shape
    n_s, t_s, _ = x_sample.shape
    assert depth == 1, "a stacked trunk needs a residual-only output kernel between layers"
    assert t_p % (WKV_CHUNKS_PER_STEP * CHUNK) == 0
    g_ones = _head_ones()
    row = lambda z: z.reshape(1, -1).astype(F32)

    hp = x_prompt.astype(F32).reshape(n_p * t_p, D_MODEL)
    hs = jnp.transpose(x_sample.astype(F32), (1, 0, 2)).reshape(t_s * n_s, D_MODEL)
    outs = [[] for _ in range(6)]
    for l in range(depth):
        zl = jnp.zeros((D_LORA, D_RWKV), F32)
        w_lora = jnp.concatenate([jnp.concatenate([w_decay_b[l], zl], axis=1),
                                  jnp.concatenate([zl, w_aaa_b[l]], axis=1)], axis=0).astype(BF16)
        prep_params = (row(norm_w[l]), w_in[l].astype(BF16), row(mu_shift[l]), w_lora, row(w0[l]),
                       row(a0[l]), row(k_k[l]), row(k_a[l]), row(r_k[l]), pool_w[l].astype(BF16),
                       row(pool_scale[l]), g_ones)
        out_params = (row(gn_w[l]), row(gn_b[l]), w_out[l].astype(BF16), row(norm_f), g_ones)

        hp, nsh, npl, s_fin = _fused_call(hp, prep_params, out_params,
                                          n_seq=n_p, n_chunks=WKV_CHUNKS_PER_STEP, t0=0)
        outs[0].append(nsh[:, SUBLANES - 1])
        outs[1].append(s_fin)
        outs[2].append(npl[:, -POOL_KEEP:])

        sh_hist = state_shift[l].astype(F32)[None]
        pool_hist = jnp.transpose(state_pool[l].astype(F32), (1, 0, 2)).reshape(1, POOL_KEEP * n_s, D_POOL)
        bon, sg, op, nsh, npl, *scan_ops = _short_prep_call(hs, sh_hist, pool_hist, prep_params,
                                                            t_s=t_s, n_s=n_s, t0=PAST_LEN)
        s_fin, y = _short_state_call(jnp.transpose(state_wkv[l].astype(F32), (1, 2, 3, 0)), *scan_ops)
        s_fin = jnp.transpose(s_fin, (3, 0, 1, 2))
        hs = _out_call(y, bon, sg, op, hs, *out_params, rows=t_s * n_s)
        outs[3].append(nsh[0])
        outs[4].append(s_fin)
        outs[5].append(jnp.transpose(npl[0].reshape(POOL_KEEP, n_s, D_POOL), (1, 0, 2)))

    y_prompt = hp.reshape(n_p, t_p, D_MODEL).astype(x_prompt.dtype)
    y_sample = jnp.transpose(hs.reshape(t_s, n_s, D_MODEL), (1, 0, 2)).astype(x_sample.dtype)
    return (y_prompt, y_sample) + tuple(jnp.stack(o, axis=0) for o in outs)
```

```python
import functools

import jax
import jax.numpy as jnp
import numpy as np
from jax import lax
from jax.experimental import pallas as pl
from jax.experimental.pallas import tpu as pltpu

F32 = jnp.float32
BF16 = jnp.bfloat16

D_MODEL = 1024
D_RWKV = 512
D_POOL = 512
HEAD = 64
N_HEADS = D_RWKV // HEAD
LANES = 128
SUBLANES = 8
MXU_DIM = 256
N_PAIRS = D_RWKV // LANES
D_LORA = 64
POOL_WINDOWS = (2, 4, 8, 16)
POOL_GROUP = D_POOL // len(POOL_WINDOWS)
POOL_KEEP = max(POOL_WINDOWS) - 1
D_SHIFT = 3 * D_RWKV + 2 * D_LORA
D_IN = D_SHIFT + D_RWKV + 2 * D_POOL
PAST_LEN = 16384
NORM_EPS = 1e-6
GN_EPS = 64e-5
L2_EPS = 1e-12
CHUNK = 64
VMEM_LIMIT = 56 * 1024 * 1024
P_A = (1, 1)
P_T = (1, 1)
P_U = (1, 1)
P_S = (1, 1)
P_Y = (1, 1)
P_CUMSUM = 2
WKV_CHUNKS_PER_STEP = 8

NN = (((1,), (0,)), ((), ()))
NT = (((1,), (1,)), ((), ()))
TN = (((0,), (0,)), ((), ()))


def _pieces(x, n):
    if x.dtype == BF16:
        return [x]
    out = []
    rem = x
    for i in range(n):
        p = rem.astype(BF16)
        out.append(p)
        if i + 1 < n:
            rem = rem - p.astype(F32)
    return out


def _mm(a, b, dims=NN, na=1, nb=1):
    ap = _pieces(a, na)
    bp = _pieces(b, nb)
    order = max(len(ap), len(bp))
    acc = None
    for i, x in enumerate(ap):
        for j, y in enumerate(bp):
            if i + j >= order:
                continue
            t = lax.dot_general(x, y, dims, preferred_element_type=F32)
            acc = t if acc is None else acc + t
    return acc


def _seg_sum(x, g_ones, pieces=1):
    cols = []
    for s in range(x.shape[1] // MXU_DIM):
        cols.append(_mm(x[:, s * MXU_DIM:(s + 1) * MXU_DIM], g_ones, NN, pieces, 1))
    return cols[0] if len(cols) == 1 else jnp.concatenate(cols, axis=1)


def _sigmoid(x):
    return 1.0 / (1.0 + jnp.exp(-x))


def _pool_hist_rows(stride):
    return -(-POOL_KEEP * stride // SUBLANES) * SUBLANES


def _norm_operands(x_ref, nw_ref):
    x = x_ref[...]
    rstd = lax.rsqrt(jnp.mean(x * x, axis=-1, keepdims=True) + NORM_EPS)
    return (x * nw_ref[...]).astype(BF16), rstd


def _prep_stages(h, rstd, win_ref, j, sh_ref, ph_ref, mu_ref, wl_ref, w0_ref, a0_ref, kk_ref,
                 ka_ref, rk_ref, pw_ref, ps_ref, g_ref,
                 r_o, k_o, v_o, lw_o, a_o, b_o, bon_o, sg_o, op_o, nsh_o, npl_o,
                 p_ext, u_ext, *, stride, t0, hs):
    rows = h.shape[0]
    ph = _pool_hist_rows(stride)
    width = MXU_DIM

    @pl.when(j == 0)
    def _():
        p_ext[0:hs, :] = jnp.zeros((hs, D_SHIFT), F32) if sh_ref is None else sh_ref[0]
        u_ext[0:ph, :] = jnp.zeros((ph, D_POOL), F32) if ph_ref is None else ph_ref[0]

    def proj_cols(c0, c1):
        return jnp.dot(h, win_ref[:, c0:c1], preferred_element_type=F32) * rstd

    def shifted(c0, c1):
        p = proj_cols(c0, c1)
        p_ext[hs:hs + rows, c0:c1] = p
        prev = p_ext[pl.ds(hs - stride, rows), c0:c1]
        return p + mu_ref[:, c0:c1] * (prev - p)

    xwa = shifted(3 * D_RWKV, D_SHIFT)
    for c in range(0, D_RWKV, width):
        r_o[:, c:c + width] = shifted(c, c + width)
    lane = lax.broadcasted_iota(jnp.int32, xwa.shape, 1)
    lora_in = jnp.where(lane < D_LORA, jnp.tanh(xwa), xwa)
    lora = jnp.dot(lora_in.astype(BF16), wl_ref[...], preferred_element_type=F32)
    lw_o[...] = (-0.6065306597126334) * _sigmoid(w0_ref[...] + lora[:, :D_RWKV])
    alpha = _sigmoid(a0_ref[...] + lora[:, D_RWKV:])
    for c in range(0, D_RWKV, width):
        sl = slice(c, c + width)
        k = shifted(D_RWKV + c, D_RWKV + c + width)
        k_o[:, sl] = k * (1.0 + (alpha[:, sl] - 1.0) * ka_ref[:, sl])
        kk = k * kk_ref[:, sl]
        kk = kk * lax.rsqrt(_seg_sum(kk * kk, g_ref[...]) + L2_EPS)
        a_o[:, sl] = -kk
        b_o[:, sl] = kk * alpha[:, sl]
        v_o[:, sl] = shifted(2 * D_RWKV + c, 2 * D_RWKV + c + width)
    last_p = p_ext[rows:rows + hs, :]
    p_ext[0:hs, :] = last_p
    nsh_o[0] = last_p
    for c in range(0, D_RWKV, width):
        g_rwkv = proj_cols(D_SHIFT + c, D_SHIFT + c + width)
        sg_o[:, c:c + width] = g_rwkv * _sigmoid(g_rwkv)
    for c in range(0, D_POOL, width):
        u_ext[ph:ph + rows, c:c + width] = proj_cols(D_SHIFT + D_RWKV + c, D_SHIFT + D_RWKV + c + width)
        op_o[:, c:c + width] = proj_cols(D_SHIFT + D_RWKV + D_POOL + c,
                                         D_SHIFT + D_RWKV + D_POOL + c + width)

    def bonus():
        bon_o[...] = _seg_sum(r_o[...] * k_o[...] * rk_ref[...], g_ref[...]) * v_o[...]

    def pool(gi, win):
        sl = slice(gi * POOL_GROUP, (gi + 1) * POOL_GROUP)
        row = lax.broadcasted_iota(jnp.int32, (rows, 1), 0)
        pos = t0 + (j * rows + row) // stride
        u = u_ext[ph:ph + rows, sl]
        tot = u
        for back in range(1, win):
            tot = tot + u_ext[pl.ds(ph - back * stride, rows), sl]
        inv_cnt = 1.0 / jnp.minimum(pos + 1, win).astype(F32)
        d = tot * inv_cnt - u
        o = jnp.dot(d.astype(BF16), pw_ref[gi], preferred_element_type=F32)
        gp = op_o[:, sl]
        op_o[:, sl] = o * ps_ref[:, sl] * (gp * _sigmoid(gp))

    def carry_pool():
        last_u = u_ext[rows:rows + ph, :]
        u_ext[0:ph, :] = last_u
        npl_o[0] = last_u

    pools = [functools.partial(pool, gi, win) for gi, win in enumerate(POOL_WINDOWS)]
    return [bonus] + pools + [carry_pool]


def _stack(x, m0):
    return jnp.concatenate([jnp.where(m0, x, 0.0), jnp.where(m0, 0.0, x)], axis=0)


def _wkv_zero_state(z_ref, is_first):
    @pl.when(is_first)
    def _():
        z_ref[...] = jnp.zeros(z_ref.shape, F32)


def _wkv_store_state(z_ref, so_ref, is_last):
    @pl.when(is_last)
    def _():
        for p in range(N_PAIRS):
            s2 = z_ref[p].T
            so_ref[0, 2 * p] = s2[:HEAD, :HEAD]
            so_ref[0, 2 * p + 1] = s2[HEAD:, HEAD:]


def _wkv_tile(r_ref, k_ref, v_ref, lw_ref, a_ref, b_ref, tri_ref, mss_ref, msq_ref, y_ref, z_ref, *,
              n_chunks, fillers=()):
    fillers = list(fillers)
    fill = lambda: fillers.pop(0)() if fillers else None
    lane = lax.broadcasted_iota(jnp.int32, (CHUNK, LANES), 1)
    m0 = lane < HEAD
    strict = mss_ref[0][:, :LANES] > 0.5
    incl2 = mss_ref[1] > 0.5
    n_levels = mss_ref.shape[0] - 3
    lane_head4 = lax.broadcasted_iota(jnp.int32, (CHUNK, 2 * LANES), 1) // HEAD
    eye = msq_ref[0]
    same_head = msq_ref[1] > 0.5
    zblk = jnp.zeros((CHUNK, LANES), BF16)
    zblk2 = jnp.zeros((2 * CHUNK, LANES), BF16)
    bf = lambda x: x.astype(BF16)
    lanes = lambda p: slice(p * LANES, (p + 1) * LANES)
    each = lambda fn, *lists: [fn(*args) for args in zip(*lists)]
    cat0 = lambda *xs: jnp.concatenate(xs, axis=0)
    cat1 = lambda *xs: jnp.concatenate(xs, axis=1)

    scaled = []
    for ci in range(n_chunks):
        rows = slice(ci * CHUNK, (ci + 1) * CHUNK)
        lw = lw_ref[rows, :]
        cs = _mm(tri_ref[...], lw, NN, 1, P_CUMSUM)
        cs_end = cs[CHUNK - 1:CHUNK, :]
        g_inv = jnp.exp(-cs)
        g_tail = jnp.exp(cs_end - cs)
        r_t = r_ref[rows, :] * jnp.exp(cs)
        scaled.append(dict(
            g_end=jnp.exp(cs_end), r_f32=r_t, r_t=bf(r_t),
            a_t=bf(a_ref[rows, :] * jnp.exp(cs - lw)),
            b_t=bf(b_ref[rows, :] * g_inv), k_t=bf(k_ref[rows, :] * g_inv),
            b_h=bf(b_ref[rows, :] * g_tail), k_h=bf(k_ref[rows, :] * g_tail), v=bf(v_ref[rows, :])))
    chains = [(ci, p) for ci in range(n_chunks) for p in range(N_PAIRS)]
    pair = lambda name: [scaled[ci][name][:, lanes(p)] for ci, p in chains]
    at, rt, bt, kt, bh, kh, v, r_f32 = (pair(n) for n in ("a_t", "r_t", "b_t", "k_t", "b_h", "k_h", "v",
                                                          "r_f32"))
    v2 = each(lambda x: _stack(x, m0), v)
    scores = each(lambda a, r, b, k: _mm(cat0(a, r), cat0(_stack(b, m0), _stack(k, m0)), NT, *P_A),
                  at, rt, bt, kt)
    a_ab = each(lambda s: jnp.where(strict, s[:CHUNK, :LANES], 0.0), scores)
    a_ak = each(lambda s: jnp.where(strict, s[:CHUNK, LANES:], 0.0), scores)
    a_r = each(lambda s: jnp.where(incl2, s[CHUNK:, :], 0.0), scores)
    fill()
    stack4 = lambda x: cat0(*[jnp.where(lane_head4 == h, x, 0.0) for h in range(4)])
    join = lambda xs: [cat1(xs[i], xs[i + 1]) for i in range(0, len(xs), 2)]
    split = lambda xs: [x[:, s] for x in xs for s in (slice(0, LANES), slice(LANES, 2 * LANES))]
    a_ab4 = join(a_ab)
    t_inv4 = each(lambda x: mss_ref[2] + x * mss_ref[3], a_ab4)
    for lvl in range(1, n_levels):
        t_b = each(bf, t_inv4)
        inner = each(lambda x, t: _mm(x * mss_ref[3 + lvl], stack4(t), NN, *P_T), a_ab4, t_b)
        t_inv4 = each(lambda t, tb, w: t + _mm(tb, stack4(bf(w)), NN, *P_T), t_inv4, t_b, inner)
        fill()
    t_inv = split(each(bf, t_inv4))
    akv = split(each(lambda x, y: _mm(x, stack4(y), NN, *P_U), join(a_ak), join(v)))
    pq = each(lambda t, x, y: _mm(t, cat1(_stack(x, m0), _stack(bf(y), m0)), NN, *P_U), t_inv, at, akv)
    pq_b = each(bf, pq)
    bpq = each(lambda b, k, x, y: _mm(cat0(b, k), cat0(x, cat1(zblk, y)), TN, *P_S), bh, kh, pq_b, v)
    g_end = [scaled[ci]["g_end"][:, lanes(p)] for ci, p in chains]
    m_t = each(lambda g, x: eye * g + jnp.where(same_head, x[:, :LANES], 0.0), g_end, bpq)
    n_t = each(lambda x: jnp.where(same_head, x[:, LANES:], 0.0), bpq)
    yy = each(lambda x, y, z: _mm(x, cat0(cat1(_stack(y[:, :LANES], m0), _stack(y[:, LANES:], m0)),
                                         cat1(zblk2, z)), NN, *P_Y), a_r, pq_b, v2)
    y1 = each(lambda r, x: r + x[:, :LANES], r_f32, yy)
    y2 = each(lambda x: x[:, LANES:], yy)

    while fillers:
        fill()
    state = [z_ref[p] for p in range(N_PAIRS)]
    for ci in range(n_chunks):
        for p in range(N_PAIRS):
            i = ci * N_PAIRS + p
            both = _mm(cat0(y1[i], m_t[i]), state[p], NN, *P_S)
            y_ref[ci * CHUNK:(ci + 1) * CHUNK, lanes(p)] = both[:CHUNK] + y2[i]
            state[p] = both[CHUNK:] + n_t[i]
    for p in range(N_PAIRS):
        z_ref[p] = state[p]


def _wkv_masks():
    ri, ci = np.indices((CHUNK, 4 * CHUNK))
    ci = ci % CHUNK
    side = [ci < ri, ci <= ri, ri == ci]
    m = 1
    while m < CHUNK:
        side.append(((ri // (2 * m)) == (ci // (2 * m))) & ((ri // m) % 2 == 1) & ((ci // m) % 2 == 0))
        m *= 2
    rq, cq = np.indices((LANES, LANES))
    square = [rq == cq, (rq // HEAD) == (cq // HEAD)]
    rt, ct = np.indices((CHUNK, CHUNK))
    return (jnp.asarray(ct <= rt, BF16), jnp.asarray(np.stack(side), F32),
            jnp.asarray(np.stack(square), F32))


def _short_pairs(t_s):
    idx = {}
    for kind, inclusive in (("ab", False), ("ak", False), ("rb", True), ("rk", True)):
        for t in range(t_s):
            for s in range(t + 1 if inclusive else t):
                idx[(kind, t, s)] = len(idx)
    return idx


def _short_scale_kernel(r_ref, k_ref, v_ref, lw_ref, a_ref, b_ref,
                        xa_o, xr_o, bh_o, kh_o, vt_o, g4_o, as_o, *, t_s, n_s):
    slab = lambda ref, t: ref[t * n_s:(t + 1) * n_s, :].T
    cs = []
    for t in range(t_s):
        cs.append(slab(lw_ref, t) if t == 0 else cs[-1] + slab(lw_ref, t))
    a_t = [slab(a_ref, t) * (jnp.exp(cs[t - 1]) if t else 1.0) for t in range(t_s)]
    r_t = [slab(r_ref, t) * jnp.exp(cs[t]) for t in range(t_s)]
    b_raw = [slab(b_ref, t) for t in range(t_s)]
    k_raw = [slab(k_ref, t) for t in range(t_s)]
    g_inv = [jnp.exp(-cs[t]) for t in range(t_s)]
    b_t = [b_raw[t] * g_inv[t] for t in range(t_s)]
    k_t = [k_raw[t] * g_inv[t] for t in range(t_s)]
    for t in range(t_s):
        g_tail = jnp.exp(cs[t_s - 1] - cs[t])
        xa_o[t] = a_t[t]
        xr_o[t] = r_t[t]
        bh_o[t] = b_raw[t] * g_tail
        kh_o[t] = k_raw[t] * g_tail
        vt_o[t] = slab(v_ref, t)
    g4_o[...] = jnp.exp(cs[t_s - 1])
    head_sum = lambda x: jnp.sum(x.reshape(N_HEADS, HEAD, n_s), axis=1)
    left = {"ab": a_t, "ak": a_t, "rb": r_t, "rk": r_t}
    right = {"ab": b_t, "ak": k_t, "rb": b_t, "rk": k_t}
    for (kind, t, s), row in _short_pairs(t_s).items():
        as_o[row] = head_sum(left[kind][t] * right[kind][s])


def _short_state_kernel(s_ref, xa_ref, xr_ref, bh_ref, kh_ref, vt_ref, g4_ref, as_ref, so_ref, y_ref, *,
                        t_s):
    idx = _short_pairs(t_s)
    head = pl.program_id(0)
    scal = {key: as_ref[row, pl.ds(head, 1), :] for key, row in idx.items()}

    def value_row(v, carry):
        s_v = s_ref[0, v]
        vv = [vt_ref[t, pl.ds(v, 1), :] for t in range(t_s)]
        u = []
        for t in range(t_s):
            acc = jnp.sum(s_v * xa_ref[t], axis=0, keepdims=True)
            for s in range(t):
                acc = acc + scal[("ab", t, s)] * u[s] + scal[("ak", t, s)] * vv[s]
            u.append(acc)
        for t in range(t_s):
            acc = jnp.sum(s_v * xr_ref[t], axis=0, keepdims=True)
            for s in range(t + 1):
                acc = acc + scal[("rb", t, s)] * u[s] + scal[("rk", t, s)] * vv[s]
            y_ref[t, pl.ds(v, 1), :] = acc
        new = s_v * g4_ref[...]
        for s in range(t_s):
            new = new + u[s] * bh_ref[s] + vv[s] * kh_ref[s]
        so_ref[0, v] = new
        return carry

    lax.fori_loop(0, HEAD, value_row, 0, unroll=8)


def _short_prep_kernel(x_ref, sh_ref, ph_ref, nw_ref, win_ref, *refs, t_s, n_s, t0):
    prep_rest, refs = refs[:N_PREP_PARAMS - 2], refs[N_PREP_PARAMS - 2:]
    bon_o, sg_o, op_o, nsh_o, npl_o = refs[:5]
    scale_outs, seq_s, (p_ext, u_ext) = refs[5:12], refs[12:18], refs[18:]
    h, rstd = _norm_operands(x_ref, nw_ref)
    for finish in _prep_stages(h, rstd, win_ref, pl.program_id(0), sh_ref, ph_ref, *prep_rest, *seq_s,
                               bon_o, sg_o, op_o, nsh_o, npl_o, p_ext, u_ext,
                               stride=n_s, t0=t0, hs=sh_ref.shape[1]):
        finish()
    _short_scale_kernel(*seq_s, *scale_outs, t_s=t_s, n_s=n_s)


def _short_prep_call(x, sh_hist, pool_hist, params, *, t_s, n_s, t0):
    assert n_s % LANES == 0 and len(params) == N_PREP_PARAMS
    rows = t_s * n_s
    hs, ph = sh_hist.shape[1], _pool_hist_rows(n_s)
    assert pool_hist.shape[1] == ph
    n_scal = len(_short_pairs(t_s))
    full = lambda shape: pl.BlockSpec(shape, lambda i: tuple(0 for _ in shape))
    vec = (t_s, D_RWKV, n_s)
    out_shapes = ([(rows, D_RWKV)] * 3 + [(1, hs, D_SHIFT), (1, ph, D_POOL)] + [vec] * 5
                  + [(D_RWKV, n_s), (n_scal, N_HEADS, n_s)])
    return pl.pallas_call(
        functools.partial(_short_prep_kernel, t_s=t_s, n_s=n_s, t0=t0),
        grid=(1,),
        in_specs=[full(a.shape) for a in (x, sh_hist, pool_hist) + tuple(params)],
        out_specs=[full(s) for s in out_shapes],
        out_shape=[jax.ShapeDtypeStruct(s, F32) for s in out_shapes],
        scratch_shapes=[pltpu.VMEM((rows, D_RWKV), F32)] * 6 + [pltpu.VMEM((hs + rows, D_SHIFT), F32),
                                                                pltpu.VMEM((ph + rows, D_POOL), F32)],
        compiler_params=pltpu.CompilerParams(
            dimension_semantics=("arbitrary",), vmem_limit_bytes=VMEM_LIMIT),
        name="prep_short",
    )(x, sh_hist, pool_hist, *params)


def _short_state_call(s0, xa, xr, bh, kh, vt, g4, scal):
    t_s, _, n_s = xa.shape
    st = pl.BlockSpec((1, HEAD, HEAD, n_s), lambda h: (h, 0, 0, 0))
    per_head = pl.BlockSpec((t_s, HEAD, n_s), lambda h: (0, h, 0))
    return pl.pallas_call(
        functools.partial(_short_state_kernel, t_s=t_s),
        grid=(N_HEADS,),
        in_specs=[st] + [per_head] * 5 + [pl.BlockSpec((HEAD, n_s), lambda h: (h, 0)),
                                          pl.BlockSpec(scal.shape, lambda h: (0, 0, 0))],
        out_specs=[st, per_head],
        out_shape=[jax.ShapeDtypeStruct(s0.shape, F32), jax.ShapeDtypeStruct(xa.shape, F32)],
        compiler_params=pltpu.CompilerParams(
            dimension_semantics=("arbitrary",), vmem_limit_bytes=VMEM_LIMIT),
        name="wkv_short_state",
    )(s0, xa, xr, bh, kh, vt, g4, scal)


def _gated_rwkv(y, bon, sg, gw_ref, gb_ref, g_ref):
    g_ones = g_ref[...]
    mu = _seg_sum(y, g_ones) * (1.0 / HEAD)
    d = y - mu
    var = _seg_sum(d * d, g_ones) * (1.0 / HEAD)
    yn = d * lax.rsqrt(var + GN_EPS) * gw_ref[...] + gb_ref[...]
    return (yn + bon) * sg


def _out_project(o_rwkv, o_pool, x, wo_ref, nf_ref):
    out = (jnp.dot(o_rwkv.astype(BF16), wo_ref[0:D_RWKV, :], preferred_element_type=F32)
           + jnp.dot(o_pool.astype(BF16), wo_ref[D_RWKV:, :], preferred_element_type=F32))
    res = x + out
    return res * lax.rsqrt(jnp.mean(res * res, axis=-1, keepdims=True) + NORM_EPS) * nf_ref[...]


def _out_kernel(y_ref, bon_ref, sg_ref, op_ref, x_ref, gw_ref, gb_ref, wo_ref, nf_ref, g_ref, o_ref):
    if len(y_ref.shape) == 3:
        y = jnp.concatenate([y_ref[t].T for t in range(y_ref.shape[0])], axis=0)
    else:
        y = y_ref[...]
    o_rwkv = _gated_rwkv(y, bon_ref[...], sg_ref[...], gw_ref, gb_ref, g_ref)
    o_ref[...] = _out_project(o_rwkv, op_ref[...], x_ref[...], wo_ref, nf_ref)


def _out_call(y, bon, sg, op, x, gn_w, gn_b, w_out, norm_f, g_ones, *, rows):
    total = x.shape[0]
    row_blk = lambda i: (i, 0)
    const2 = lambda i: (0, 0)
    act = lambda w: pl.BlockSpec((rows, w), row_blk)
    if y.ndim == 3:
        assert total == rows == y.shape[0] * y.shape[2]
        y_spec = pl.BlockSpec(y.shape, lambda i: (0, 0, 0))
    else:
        y_spec = act(D_RWKV)
    return pl.pallas_call(
        _out_kernel,
        grid=(total // rows,),
        in_specs=[y_spec] + [act(D_RWKV)] * 3 + [act(D_MODEL),
                                                 pl.BlockSpec((1, D_RWKV), const2),
                                                 pl.BlockSpec((1, D_RWKV), const2),
                                                 pl.BlockSpec((D_RWKV + D_POOL, D_MODEL), const2),
                                                 pl.BlockSpec((1, D_MODEL), const2),
                                                 pl.BlockSpec((MXU_DIM, MXU_DIM), const2)],
        out_specs=act(D_MODEL),
        out_shape=jax.ShapeDtypeStruct((total, D_MODEL), F32),
        compiler_params=pltpu.CompilerParams(
            dimension_semantics=("arbitrary",), vmem_limit_bytes=VMEM_LIMIT),
        name="out",
    )(y, bon, sg, op, x, gn_w, gn_b, w_out, norm_f, g_ones)


N_PREP_PARAMS = 12
N_WKV_CONSTS = 3
N_OUT_PARAMS = 5


def _fused_kernel(*refs, n_chunks, t0):
    x_ref, nw_ref, win_ref = refs[:3]
    pos = 1 + N_PREP_PARAMS
    prep_rest = refs[3:pos]
    wkv_consts = refs[pos:pos + N_WKV_CONSTS]
    pos += N_WKV_CONSTS
    gw_ref, gb_ref, wo_ref, nf_ref, g_ref = refs[pos:pos + N_OUT_PARAMS]
    pos += N_OUT_PARAMS
    o_ref, nsh_o, npl_o, so_ref = refs[pos:pos + 4]
    pos += 4
    seq_s = refs[pos:pos + 6]
    bon_s, sg_s, op_s, y_s, p_ext, u_ext, z_ref = refs[pos + 6:]
    j = pl.program_id(1)
    _wkv_zero_state(z_ref, j == 0)
    finish_prep = _prep_stages(*_norm_operands(x_ref, nw_ref), win_ref, j, None, None, *prep_rest,
                               *seq_s, bon_s, sg_s, op_s, nsh_o, npl_o, p_ext, u_ext,
                               stride=1, t0=t0, hs=nsh_o.shape[1])
    _wkv_tile(*seq_s, *wkv_consts, y_s, z_ref, n_chunks=n_chunks, fillers=finish_prep)
    _out_kernel(y_s, bon_s, sg_s, op_s, x_ref, gw_ref, gb_ref, wo_ref, nf_ref, g_ref, o_ref)
    _wkv_store_state(z_ref, so_ref, j == pl.num_programs(1) - 1)


def _fused_call(x, prep_params, out_params, *, n_seq, n_chunks, t0):
    assert len(prep_params) == N_PREP_PARAMS and len(out_params) == N_OUT_PARAMS
    total = x.shape[0]
    rows = n_chunks * CHUNK
    tiles = total // (n_seq * rows)
    hs, ph = SUBLANES, _pool_hist_rows(1)
    assert CHUNK == HEAD and 2 * CHUNK == LANES
    wkv_consts = _wkv_masks()
    act = pl.BlockSpec((rows, D_MODEL), lambda b, j: (b * tiles + j, 0))
    per_seq = lambda shape: pl.BlockSpec((1,) + shape[1:], lambda b, j: (b,) + (0,) * (len(shape) - 1))
    const = lambda a: pl.BlockSpec(a.shape, lambda b, j: (0,) * a.ndim, pipeline_mode=pl.Buffered(1))
    consts = tuple(prep_params) + tuple(wkv_consts) + tuple(out_params)
    seq_shapes = [(n_seq, hs, D_SHIFT), (n_seq, ph, D_POOL), (n_seq, N_HEADS, HEAD, HEAD)]
    seq_scratch = pltpu.VMEM((rows, D_RWKV), F32)
    return pl.pallas_call(
        functools.partial(_fused_kernel, n_chunks=n_chunks, t0=t0),
        grid=(n_seq, tiles),
        in_specs=[act] + [const(a) for a in consts],
        out_specs=[act] + [per_seq(s) for s in seq_shapes],
        out_shape=[jax.ShapeDtypeStruct((total, D_MODEL), F32)]
        + [jax.ShapeDtypeStruct(s, F32) for s in seq_shapes],
        scratch_shapes=[seq_scratch] * 10 + [pltpu.VMEM((hs + rows, D_SHIFT), F32),
                                             pltpu.VMEM((ph + rows, D_POOL), F32),
                                             pltpu.VMEM((N_PAIRS, LANES, LANES), F32)],
        compiler_params=pltpu.CompilerParams(
            dimension_semantics=("arbitrary", "arbitrary"), vmem_limit_bytes=VMEM_LIMIT),
        name="layer_long",
    )(x, *consts)


def _head_ones():
    i, j = np.indices((MXU_DIM, MXU_DIM))
    return jnp.asarray((i // HEAD) == (j // HEAD), BF16)


def kernel(x_prompt, x_sample, state_shift, state_wkv, state_pool, norm_w, w_in, mu_shift, w_decay_b,
           w0, w_aaa_b, a0, k_k, k_a, r_k, gn_w, gn_b, pool_w, pool_scale, w_out, norm_f):
    depth = norm_w.shape[0]
    n_p, t_p, _ = x_prompt.shape
    n_s, t_s, _ = x_sample.shape
    assert depth == 1, "a stacked trunk needs a residual-only output kernel between layers"
    assert t_p % (WKV_CHUNKS_PER_STEP * CHUNK) == 0
    g_ones = _head_ones()
    row = lambda z: z.reshape(1, -1).astype(F32)

    hp = x_prompt.astype(F32).reshape(n_p * t_p, D_MODEL)
    hs = jnp.transpose(x_sample.astype(F32), (1, 0, 2)).reshape(t_s * n_s, D_MODEL)
    outs = [[] for _ in range(6)]
    for l in range(depth):
        zl = jnp.zeros((D_LORA, D_RWKV), F32)
        w_lora = jnp.concatenate([jnp.concatenate([w_decay_b[l], zl], axis=1),
                                  jnp.concatenate([zl, w_aaa_b[l]], axis=1)], axis=0).astype(BF16)
        prep_params = (row(norm_w[l]), w_in[l].astype(BF16), row(mu_shift[l]), w_lora, row(w0[l]),
                       row(a0[l]), row(k_k[l]), row(k_a[l]), row(r_k[l]), pool_w[l].astype(BF16),
                       row(pool_scale[l]), g_ones)
        out_params = (row(gn_w[l]), row(gn_b[l]), w_out[l].astype(BF16), row(norm_f), g_ones)

        hp, nsh, npl, s_fin = _fused_call(hp, prep_params, out_params,
                                          n_seq=n_p, n_chunks=WKV_CHUNKS_PER_STEP, t0=0)
        outs[0].append(nsh[:, SUBLANES - 1])
        outs[1].append(s_fin)
        outs[2].append(npl[:, -POOL_KEEP:])

        sh_hist = state_shift[l].astype(F32)[None]
        pool_hist = jnp.transpose(state_pool[l].astype(F32), (1, 0, 2)).reshape(1, POOL_KEEP * n_s, D_POOL)
        bon, sg, op, nsh, npl, *scan_ops = _short_prep_call(hs, sh_hist, pool_hist, prep_params,
                                                            t_s=t_s, n_s=n_s, t0=PAST_LEN)
        s_fin, y = _short_state_call(jnp.transpose(state_wkv[l].astype(F32), (1, 2, 3, 0)), *scan_ops)
        s_fin = jnp.transpose(s_fin, (3, 0, 1, 2))
        hs = _out_call(y, bon, sg, op, hs, *out_params, rows=t_s * n_s)
        outs[3].append(nsh[0])
        outs[4].append(s_fin)
        outs[5].append(jnp.transpose(npl[0].reshape(POOL_KEEP, n_s, D_POOL), (1, 0, 2)))

    y_prompt = hp.reshape(n_p, t_p, D_MODEL).astype(x_prompt.dtype)
    y_sample = jnp.transpose(hs.reshape(t_s, n_s, D_MODEL), (1, 0, 2)).astype(x_sample.dtype)
    return (y_prompt, y_sample) + tuple(jnp.stack(o, axis=0) for o in outs)
```

```python
import functools

import jax
import jax.numpy as jnp
import numpy as np
from jax import lax
from jax.experimental import pallas as pl
from jax.experimental.pallas import tpu as pltpu

F32 = jnp.float32
BF16 = jnp.bfloat16

D_MODEL = 1024
D_RWKV = 512
D_POOL = 512
HEAD = 64
N_HEADS = D_RWKV // HEAD
LANES = 128
SUBLANES = 8
MXU_DIM = 256
N_PAIRS = D_RWKV // LANES
D_LORA = 64
POOL_WINDOWS = (2, 4, 8, 16)
POOL_GROUP = D_POOL // len(POOL_WINDOWS)
POOL_KEEP = max(POOL_WINDOWS) - 1
D_SHIFT = 3 * D_RWKV + 2 * D_LORA
D_IN = D_SHIFT + D_RWKV + 2 * D_POOL
PAST_LEN = 16384
NORM_EPS = 1e-6
GN_EPS = 64e-5
L2_EPS = 1e-12
CHUNK = 64
VMEM_LIMIT = 56 * 1024 * 1024
P_A = (1, 1)
P_T = (1, 1)
P_U = (1, 1)
P_S = (1, 1)
P_Y = (1, 1)
P_CUMSUM = 2
WKV_CHUNKS_PER_STEP = 8

NN = (((1,), (0,)), ((), ()))
NT = (((1,), (1,)), ((), ()))
TN = (((0,), (0,)), ((), ()))


def _pieces(x, n):
    if x.dtype == BF16:
        return [x]
    out = []
    rem = x
    for i in range(n):
        p = rem.astype(BF16)
        out.append(p)
        if i + 1 < n:
            rem = rem - p.astype(F32)
    return out


def _mm(a, b, dims=NN, na=1, nb=1):
    ap = _pieces(a, na)
    bp = _pieces(b, nb)
    order = max(len(ap), len(bp))
    acc = None
    for i, x in enumerate(ap):
        for j, y in enumerate(bp):
            if i + j >= order:
                continue
            t = lax.dot_general(x, y, dims, preferred_element_type=F32)
            acc = t if acc is None else acc + t
    return acc


def _seg_sum(x, g_ones, pieces=1):
    cols = []
    for s in range(x.shape[1] // MXU_DIM):
        cols.append(_mm(x[:, s * MXU_DIM:(s + 1) * MXU_DIM], g_ones, NN, pieces, 1))
    return cols[0] if len(cols) == 1 else jnp.concatenate(cols, axis=1)


def _sigmoid(x):
    return 1.0 / (1.0 + jnp.exp(-x))


def _pool_hist_rows(stride):
    return -(-POOL_KEEP * stride // SUBLANES) * SUBLANES


def _norm_operands(x_ref, nw_ref):
    x = x_ref[...]
    rstd = lax.rsqrt(jnp.mean(x * x, axis=-1, keepdims=True) + NORM_EPS)
    return (x * nw_ref[...]).astype(BF16), rstd


def _prep_stages(h, rstd, win_ref, j, sh_ref, ph_ref, mu_ref, wl_ref, w0_ref, a0_ref, kk_ref,
                 ka_ref, rk_ref, pw_ref, ps_ref, g_ref,
                 r_o, k_o, v_o, lw_o, a_o, b_o, bon_o, sg_o, op_o, nsh_o, npl_o,
                 p_ext, u_ext, *, stride, t0, hs):
    rows = h.shape[0]
    ph = _pool_hist_rows(stride)
    width = MXU_DIM

    @pl.when(j == 0)
    def _():
        p_ext[0:hs, :] = jnp.zeros((hs, D_SHIFT), F32) if sh_ref is None else sh_ref[0]
        u_ext[0:ph, :] = jnp.zeros((ph, D_POOL), F32) if ph_ref is None else ph_ref[0]

    def proj_cols(c0, c1):
        return jnp.dot(h, win_ref[:, c0:c1], preferred_element_type=F32) * rstd

    def shifted(c0, c1):
        p = proj_cols(c0, c1)
        p_ext[hs:hs + rows, c0:c1] = p
        prev = p_ext[pl.ds(hs - stride, rows), c0:c1]
        return p + mu_ref[:, c0:c1] * (prev - p)

    xwa = shifted(3 * D_RWKV, D_SHIFT)
    for c in range(0, D_RWKV, width):
        r_o[:, c:c + width] = shifted(c, c + width)
    lane = lax.broadcasted_iota(jnp.int32, xwa.shape, 1)
    lora_in = jnp.where(lane < D_LORA, jnp.tanh(xwa), xwa)
    lora = jnp.dot(lora_in.astype(BF16), wl_ref[...], preferred_element_type=F32)
    lw_o[...] = (-0.6065306597126334) * _sigmoid(w0_ref[...] + lora[:, :D_RWKV])
    alpha = _sigmoid(a0_ref[...] + lora[:, D_RWKV:])
    for c in range(0, D_RWKV, width):
        sl = slice(c, c + width)
        k = shifted(D_RWKV + c, D_RWKV + c + width)
        k_o[:, sl] = k * (1.0 + (alpha[:, sl] - 1.0) * ka_ref[:, sl])
        kk = k * kk_ref[:, sl]
        kk = kk * lax.rsqrt(_seg_sum(kk * kk, g_ref[...]) + L2_EPS)
        a_o[:, sl] = -kk
        b_o[:, sl] = kk * alpha[:, sl]
        v_o[:, sl] = shifted(2 * D_RWKV + c, 2 * D_RWKV + c + width)
    last_p = p_ext[rows:rows + hs, :]
    p_ext[0:hs, :] = last_p
    nsh_o[0] = last_p
    for c in range(0, D_RWKV, width):
        g_rwkv = proj_cols(D_SHIFT + c, D_SHIFT + c + width)
        sg_o[:, c:c + width] = g_rwkv * _sigmoid(g_rwkv)
    for c in range(0, D_POOL, width):
        u_ext[ph:ph + rows, c:c + width] = proj_cols(D_SHIFT + D_RWKV + c, D_SHIFT + D_RWKV + c + width)
        op_o[:, c:c + width] = proj_cols(D_SHIFT + D_RWKV + D_POOL + c,
                                         D_SHIFT + D_RWKV + D_POOL + c + width)

    def bonus():
        bon_o[...] = _seg_sum(r_o[...] * k_o[...] * rk_ref[...], g_ref[...]) * v_o[...]

    def pool(gi, win):
        sl = slice(gi * POOL_GROUP, (gi + 1) * POOL_GROUP)
        row = lax.broadcasted_iota(jnp.int32, (rows, 1), 0)
        pos = t0 + (j * rows + row) // stride
        u = u_ext[ph:ph + rows, sl]
        tot = u
        for back in range(1, win):
            tot = tot + u_ext[pl.ds(ph - back * stride, rows), sl]
        inv_cnt = 1.0 / jnp.minimum(pos + 1, win).astype(F32)
        d = tot * inv_cnt - u
        o = jnp.dot(d.astype(BF16), pw_ref[gi], preferred_element_type=F32)
        gp = op_o[:, sl]
        op_o[:, sl] = o * ps_ref[:, sl] * (gp * _sigmoid(gp))

    def carry_pool():
        last_u = u_ext[rows:rows + ph, :]
        u_ext[0:ph, :] = last_u
        npl_o[0] = last_u

    pools = [functools.partial(pool, gi, win) for gi, win in enumerate(POOL_WINDOWS)]
    return [bonus] + pools + [carry_pool]


def _stack(x, m0):
    return jnp.concatenate([jnp.where(m0, x, 0.0), jnp.where(m0, 0.0, x)], axis=0)


def _wkv_zero_state(z_ref, is_first):
    @pl.when(is_first)
    def _():
        z_ref[...] = jnp.zeros(z_ref.shape, F32)


def _wkv_store_state(z_ref, so_ref, is_last):
    @pl.when(is_last)
    def _():
        for p in range(N_PAIRS):
            s2 = z_ref[p].T
            so_ref[0, 2 * p] = s2[:HEAD, :HEAD]
            so_ref[0, 2 * p + 1] = s2[HEAD:, HEAD:]


def _wkv_tile(r_ref, k_ref, v_ref, lw_ref, a_ref, b_ref, tri_ref, mss_ref, msq_ref, y_ref, z_ref, *,
              n_chunks, fillers=()):
    fillers = list(fillers)
    fill = lambda: fillers.pop(0)() if fillers else None
    lane = lax.broadcasted_iota(jnp.int32, (CHUNK, LANES), 1)
    m0 = lane < HEAD
    strict = mss_ref[0][:, :LANES] > 0.5
    incl2 = mss_ref[1] > 0.5
    n_levels = mss_ref.shape[0] - 3
    lane_head4 = lax.broadcasted_iota(jnp.int32, (CHUNK, 2 * LANES), 1) // HEAD
    eye = msq_ref[0]
    same_head = msq_ref[1] > 0.5
    zblk = jnp.zeros((CHUNK, LANES), BF16)
    zblk2 = jnp.zeros((2 * CHUNK, LANES), BF16)
    bf = lambda x: x.astype(BF16)
    lanes = lambda p: slice(p * LANES, (p + 1) * LANES)
    each = lambda fn, *lists: [fn(*args) for args in zip(*lists)]
    cat0 = lambda *xs: jnp.concatenate(xs, axis=0)
    cat1 = lambda *xs: jnp.concatenate(xs, axis=1)

    scaled = []
    for ci in range(n_chunks):
        rows = slice(ci * CHUNK, (ci + 1) * CHUNK)
        lw = lw_ref[rows, :]
        cs = _mm(tri_ref[...], lw, NN, 1, P_CUMSUM)
        cs_end = cs[CHUNK - 1:CHUNK, :]
        g_inv = jnp.exp(-cs)
        g_tail = jnp.exp(cs_end - cs)
        r_t = r_ref[rows, :] * jnp.exp(cs)
        scaled.append(dict(
            g_end=jnp.exp(cs_end), r_f32=r_t, r_t=bf(r_t),
            a_t=bf(a_ref[rows, :] * jnp.exp(cs - lw)),
            b_t=bf(b_ref[rows, :] * g_inv), k_t=bf(k_ref[rows, :] * g_inv),
            b_h=bf(b_ref[rows, :] * g_tail), k_h=bf(k_ref[rows, :] * g_tail), v=bf(v_ref[rows, :])))
    chains = [(ci, p) for ci in range(n_chunks) for p in range(N_PAIRS)]
    pair = lambda name: [scaled[ci][name][:, lanes(p)] for ci, p in chains]
    at, rt, bt, kt, bh, kh, v, r_f32 = (pair(n) for n in ("a_t", "r_t", "b_t", "k_t", "b_h", "k_h", "v",
                                                          "r_f32"))
    v2 = each(lambda x: _stack(x, m0), v)
    scores = each(lambda a, r, b, k: _mm(cat0(a, r), cat0(_stack(b, m0), _stack(k, m0)), NT, *P_A),
                  at, rt, bt, kt)
    a_ab = each(lambda s: jnp.where(strict, s[:CHUNK, :LANES], 0.0), scores)
    a_ak = each(lambda s: jnp.where(strict, s[:CHUNK, LANES:], 0.0), scores)
    a_r = each(lambda s: jnp.where(incl2, s[CHUNK:, :], 0.0), scores)
    fill()
    stack4 = lambda x: cat0(*[jnp.where(lane_head4 == h, x, 0.0) for h in range(4)])
    join = lambda xs: [cat1(xs[i], xs[i + 1]) for i in range(0, len(xs), 2)]
    split = lambda xs: [x[:, s] for x in xs for s in (slice(0, LANES), slice(LANES, 2 * LANES))]
    a_ab4 = join(a_ab)
    t_inv4 = each(lambda x: mss_ref[2] + x * mss_ref[3], a_ab4)
    for lvl in range(1, n_levels):
        t_b = each(bf, t_inv4)
        inner = each(lambda x, t: _mm(x * mss_ref[3 + lvl], stack4(t), NN, *P_T), a_ab4, t_b)
        t_inv4 = each(lambda t, tb, w: t + _mm(tb, stack4(bf(w)), NN, *P_T), t_inv4, t_b, inner)
        fill()
    t_inv = split(each(bf, t_inv4))
    akv = split(each(lambda x, y: _mm(x, stack4(y), NN, *P_U), join(a_ak), join(v)))
    pq = each(lambda t, x, y: _mm(t, cat1(_stack(x, m0), _stack(bf(y), m0)), NN, *P_U), t_inv, at, akv)
    pq_b = each(bf, pq)
    bpq = each(lambda b, k, x, y: _mm(cat0(b, k), cat0(x, cat1(zblk, y)), TN, *P_S), bh, kh, pq_b, v)
    g_end = [scaled[ci]["g_end"][:, lanes(p)] for ci, p in chains]
    m_t = each(lambda g, x: eye * g + jnp.where(same_head, x[:, :LANES], 0.0), g_end, bpq)
    n_t = each(lambda x: jnp.where(same_head, x[:, LANES:], 0.0), bpq)
    yy = each(lambda x, y, z: _mm(x, cat0(cat1(_stack(y[:, :LANES], m0), _stack(y[:, LANES:], m0)),
                                         cat1(zblk2, z)), NN, *P_Y), a_r, pq_b, v2)
    y1 = each(lambda r, x: r + x[:, :LANES], r_f32, yy)
    y2 = each(lambda x: x[:, LANES:], yy)

    while fillers:
        fill()
    state = [z_ref[p] for p in range(N_PAIRS)]
    for ci in range(n_chunks):
        for p in range(N_PAIRS):
            i = ci * N_PAIRS + p
            both = _mm(cat0(y1[i], m_t[i]), state[p], NN, *P_S)
            y_ref[ci * CHUNK:(ci + 1) * CHUNK, lanes(p)] = both[:CHUNK] + y2[i]
            state[p] = both[CHUNK:] + n_t[i]
    for p in range(N_PAIRS):
        z_ref[p] = state[p]


def _wkv_masks():
    ri, ci = np.indices((CHUNK, 4 * CHUNK))
    ci = ci % CHUNK
    side = [ci < ri, ci <= ri, ri == ci]
    m = 1
    while m < CHUNK:
        side.append(((ri // (2 * m)) == (ci // (2 * m))) & ((ri // m) % 2 == 1) & ((ci // m) % 2 == 0))
        m *= 2
    rq, cq = np.indices((LANES, LANES))
    square = [rq == cq, (rq // HEAD) == (cq // HEAD)]
    rt, ct = np.indices((CHUNK, CHUNK))
    return (jnp.asarray(ct <= rt, BF16), jnp.asarray(np.stack(side), F32),
            jnp.asarray(np.stack(square), F32))


def _short_pairs(t_s):
    idx = {}
    for kind, inclusive in (("ab", False), ("ak", False), ("rb", True), ("rk", True)):
        for t in range(t_s):
            for s in range(t + 1 if inclusive else t):
                idx[(kind, t, s)] = len(idx)
    return idx


def _short_scale_kernel(r_ref, k_ref, v_ref, lw_ref, a_ref, b_ref,
                        xa_o, xr_o, bh_o, kh_o, vt_o, g4_o, as_o, *, t_s, n_s):
    slab = lambda ref, t: ref[t * n_s:(t + 1) * n_s, :].T
    cs = []
    for t in range(t_s):
        cs.append(slab(lw_ref, t) if t == 0 else cs[-1] + slab(lw_ref, t))
    a_t = [slab(a_ref, t) * (jnp.exp(cs[t - 1]) if t else 1.0) for t in range(t_s)]
    r_t = [slab(r_ref, t) * jnp.exp(cs[t]) for t in range(t_s)]
    b_raw = [slab(b_ref, t) for t in range(t_s)]
    k_raw = [slab(k_ref, t) for t in range(t_s)]
    g_inv = [jnp.exp(-cs[t]) for t in range(t_s)]
    b_t = [b_raw[t] * g_inv[t] for t in range(t_s)]
    k_t = [k_raw[t] * g_inv[t] for t in range(t_s)]
    for t in range(t_s):
        g_tail = jnp.exp(cs[t_s - 1] - cs[t])
        xa_o[t] = a_t[t]
        xr_o[t] = r_t[t]
        bh_o[t] = b_raw[t] * g_tail
        kh_o[t] = k_raw[t] * g_tail
        vt_o[t] = slab(v_ref, t)
    g4_o[...] = jnp.exp(cs[t_s - 1])
    head_sum = lambda x: jnp.sum(x.reshape(N_HEADS, HEAD, n_s), axis=1)
    left = {"ab": a_t, "ak": a_t, "rb": r_t, "rk": r_t}
    right = {"ab": b_t, "ak": k_t, "rb": b_t, "rk": k_t}
    for (kind, t, s), row in _short_pairs(t_s).items():
        as_o[row] = head_sum(left[kind][t] * right[kind][s])


def _short_state_kernel(s_ref, xa_ref, xr_ref, bh_ref, kh_ref, vt_ref, g4_ref, as_ref, *refs, t_s):
    out_ins, (so_ref, o_ref, y_ref) = refs[:-3], refs[-3:]
    idx = _short_pairs(t_s)
    head = pl.program_id(0)
    scal = {key: as_ref[row, pl.ds(head, 1), :] for key, row in idx.items()}

    def value_row(v, carry):
        s_v = s_ref[0, v]
        vv = [vt_ref[t, pl.ds(v, 1), :] for t in range(t_s)]
        u = []
        for t in range(t_s):
            acc = jnp.sum(s_v * xa_ref[t], axis=0, keepdims=True)
            for s in range(t):
                acc = acc + scal[("ab", t, s)] * u[s] + scal[("ak", t, s)] * vv[s]
            u.append(acc)
        for t in range(t_s):
            acc = jnp.sum(s_v * xr_ref[t], axis=0, keepdims=True)
            for s in range(t + 1):
                acc = acc + scal[("rb", t, s)] * u[s] + scal[("rk", t, s)] * vv[s]
            y_ref[t, pl.ds(head * HEAD + v, 1), :] = acc
        new = s_v * g4_ref[...]
        for s in range(t_s):
            new = new + u[s] * bh_ref[s] + vv[s] * kh_ref[s]
        so_ref[0, v] = new
        return carry

    lax.fori_loop(0, HEAD, value_row, 0, unroll=8)

    @pl.when(head == pl.num_programs(0) - 1)
    def _():
        _out_kernel(y_ref, *out_ins, o_ref)


def _short_prep_kernel(x_ref, sh_ref, ph_ref, nw_ref, win_ref, *refs, t_s, n_s, t0):
    prep_rest, refs = refs[:N_PREP_PARAMS - 2], refs[N_PREP_PARAMS - 2:]
    bon_o, sg_o, op_o, nsh_o, npl_o = refs[:5]
    scale_outs, seq_s, (p_ext, u_ext) = refs[5:12], refs[12:18], refs[18:]
    h, rstd = _norm_operands(x_ref, nw_ref)
    for finish in _prep_stages(h, rstd, win_ref, pl.program_id(0), sh_ref, ph_ref, *prep_rest, *seq_s,
                               bon_o, sg_o, op_o, nsh_o, npl_o, p_ext, u_ext,
                               stride=n_s, t0=t0, hs=sh_ref.shape[1]):
        finish()
    _short_scale_kernel(*seq_s, *scale_outs, t_s=t_s, n_s=n_s)


def _short_prep_call(x, sh_hist, pool_hist, params, *, t_s, n_s, t0):
    assert n_s % LANES == 0 and len(params) == N_PREP_PARAMS
    rows = t_s * n_s
    hs, ph = sh_hist.shape[1], _pool_hist_rows(n_s)
    assert pool_hist.shape[1] == ph
    n_scal = len(_short_pairs(t_s))
    full = lambda shape: pl.BlockSpec(shape, lambda i: tuple(0 for _ in shape))
    vec = (t_s, D_RWKV, n_s)
    out_shapes = ([(rows, D_RWKV)] * 3 + [(1, hs, D_SHIFT), (1, ph, D_POOL)] + [vec] * 5
                  + [(D_RWKV, n_s), (n_scal, N_HEADS, n_s)])
    return pl.pallas_call(
        functools.partial(_short_prep_kernel, t_s=t_s, n_s=n_s, t0=t0),
        grid=(1,),
        in_specs=[full(a.shape) for a in (x, sh_hist, pool_hist) + tuple(params)],
        out_specs=[full(s) for s in out_shapes],
        out_shape=[jax.ShapeDtypeStruct(s, F32) for s in out_shapes],
        scratch_shapes=[pltpu.VMEM((rows, D_RWKV), F32)] * 6 + [pltpu.VMEM((hs + rows, D_SHIFT), F32),
                                                                pltpu.VMEM((ph + rows, D_POOL), F32)],
        compiler_params=pltpu.CompilerParams(
            dimension_semantics=("arbitrary",), vmem_limit_bytes=VMEM_LIMIT),
        name="prep_short",
    )(x, sh_hist, pool_hist, *params)


def _short_state_call(s0, xa, xr, bh, kh, vt, g4, scal, bon, sg, op, x, out_params):
    t_s, _, n_s = xa.shape
    assert len(out_params) == N_OUT_PARAMS and x.shape[0] == t_s * n_s
    st = pl.BlockSpec((1, HEAD, HEAD, n_s), lambda h: (h, 0, 0, 0))
    per_head = pl.BlockSpec((t_s, HEAD, n_s), lambda h: (0, h, 0))
    whole = lambda a: pl.BlockSpec(a.shape, lambda h: (0,) * a.ndim)
    out_ins = (bon, sg, op, x) + tuple(out_params)
    return pl.pallas_call(
        functools.partial(_short_state_kernel, t_s=t_s),
        grid=(N_HEADS,),
        in_specs=[st] + [per_head] * 5 + [pl.BlockSpec((HEAD, n_s), lambda h: (h, 0)), whole(scal)]
        + [whole(a) for a in out_ins],
        out_specs=[st, whole(x)],
        out_shape=[jax.ShapeDtypeStruct(s0.shape, F32), jax.ShapeDtypeStruct(x.shape, F32)],
        scratch_shapes=[pltpu.VMEM(xa.shape, F32)],
        compiler_params=pltpu.CompilerParams(
            dimension_semantics=("arbitrary",), vmem_limit_bytes=VMEM_LIMIT),
        name="layer_short_state",
    )(s0, xa, xr, bh, kh, vt, g4, scal, *out_ins)


def _gated_rwkv(y, bon, sg, gw_ref, gb_ref, g_ref):
    g_ones = g_ref[...]
    mu = _seg_sum(y, g_ones) * (1.0 / HEAD)
    d = y - mu
    var = _seg_sum(d * d, g_ones) * (1.0 / HEAD)
    yn = d * lax.rsqrt(var + GN_EPS) * gw_ref[...] + gb_ref[...]
    return (yn + bon) * sg


def _out_project(o_rwkv, o_pool, x, wo_ref, nf_ref):
    out = (jnp.dot(o_rwkv.astype(BF16), wo_ref[0:D_RWKV, :], preferred_element_type=F32)
           + jnp.dot(o_pool.astype(BF16), wo_ref[D_RWKV:, :], preferred_element_type=F32))
    res = x + out
    return res * lax.rsqrt(jnp.mean(res * res, axis=-1, keepdims=True) + NORM_EPS) * nf_ref[...]


def _out_kernel(y_ref, bon_ref, sg_ref, op_ref, x_ref, gw_ref, gb_ref, wo_ref, nf_ref, g_ref, o_ref):
    if len(y_ref.shape) == 3:
        y = jnp.concatenate([y_ref[t].T for t in range(y_ref.shape[0])], axis=0)
    else:
        y = y_ref[...]
    o_rwkv = _gated_rwkv(y, bon_ref[...], sg_ref[...], gw_ref, gb_ref, g_ref)
    o_ref[...] = _out_project(o_rwkv, op_ref[...], x_ref[...], wo_ref, nf_ref)


N_PREP_PARAMS = 12
N_WKV_CONSTS = 3
N_OUT_PARAMS = 5


def _fused_kernel(*refs, n_chunks, t0):
    x_ref, nw_ref, win_ref = refs[:3]
    pos = 1 + N_PREP_PARAMS
    prep_rest = refs[3:pos]
    wkv_consts = refs[pos:pos + N_WKV_CONSTS]
    pos += N_WKV_CONSTS
    gw_ref, gb_ref, wo_ref, nf_ref, g_ref = refs[pos:pos + N_OUT_PARAMS]
    pos += N_OUT_PARAMS
    o_ref, nsh_o, npl_o, so_ref = refs[pos:pos + 4]
    pos += 4
    seq_s = refs[pos:pos + 6]
    bon_s, sg_s, op_s, y_s, p_ext, u_ext, z_ref = refs[pos + 6:]
    j = pl.program_id(1)
    _wkv_zero_state(z_ref, j == 0)
    finish_prep = _prep_stages(*_norm_operands(x_ref, nw_ref), win_ref, j, None, None, *prep_rest,
                               *seq_s, bon_s, sg_s, op_s, nsh_o, npl_o, p_ext, u_ext,
                               stride=1, t0=t0, hs=nsh_o.shape[1])
    _wkv_tile(*seq_s, *wkv_consts, y_s, z_ref, n_chunks=n_chunks, fillers=finish_prep)
    _out_kernel(y_s, bon_s, sg_s, op_s, x_ref, gw_ref, gb_ref, wo_ref, nf_ref, g_ref, o_ref)
    _wkv_store_state(z_ref, so_ref, j == pl.num_programs(1) - 1)


def _fused_call(x, prep_params, out_params, *, n_seq, n_chunks, t0):
    assert len(prep_params) == N_PREP_PARAMS and len(out_params) == N_OUT_PARAMS
    total = x.shape[0]
    rows = n_chunks * CHUNK
    tiles = total // (n_seq * rows)
    hs, ph = SUBLANES, _pool_hist_rows(1)
    assert CHUNK == HEAD and 2 * CHUNK == LANES
    wkv_consts = _wkv_masks()
    act = pl.BlockSpec((rows, D_MODEL), lambda b, j: (b * tiles + j, 0))
    per_seq = lambda shape: pl.BlockSpec((1,) + shape[1:], lambda b, j: (b,) + (0,) * (len(shape) - 1))
    const = lambda a: pl.BlockSpec(a.shape, lambda b, j: (0,) * a.ndim, pipeline_mode=pl.Buffered(1))
    consts = tuple(prep_params) + tuple(wkv_consts) + tuple(out_params)
    seq_shapes = [(n_seq, hs, D_SHIFT), (n_seq, ph, D_POOL), (n_seq, N_HEADS, HEAD, HEAD)]
    seq_scratch = pltpu.VMEM((rows, D_RWKV), F32)
    return pl.pallas_call(
        functools.partial(_fused_kernel, n_chunks=n_chunks, t0=t0),
        grid=(n_seq, tiles),
        in_specs=[act] + [const(a) for a in consts],
        out_specs=[act] + [per_seq(s) for s in seq_shapes],
        out_shape=[jax.ShapeDtypeStruct((total, D_MODEL), F32)]
        + [jax.ShapeDtypeStruct(s, F32) for s in seq_shapes],
        scratch_shapes=[seq_scratch] * 10 + [pltpu.VMEM((hs + rows, D_SHIFT), F32),
                                             pltpu.VMEM((ph + rows, D_POOL), F32),
                                             pltpu.VMEM((N_PAIRS, LANES, LANES), F32)],
        compiler_params=pltpu.CompilerParams(
            dimension_semantics=("arbitrary", "arbitrary"), vmem_limit_bytes=VMEM_LIMIT),
        name="layer_long",
    )(x, *consts)


def _head_ones():
    i, j = np.indices((MXU_DIM, MXU_DIM))
    return jnp.asarray((i // HEAD) == (j // HEAD), BF16)


def kernel(x_prompt, x_sample, state_shift, state_wkv, state_pool, norm_w, w_in, mu_shift, w_decay_b,
           w0, w_aaa_b, a0, k_k, k_a, r_k, gn_w, gn_b, pool_w, pool_scale, w_out, norm_f):
    depth = norm_w.shape[0]
    n_p, t_p, _ = x_prompt.shape
    n_s, t_s, _ = x_sample.shape
    assert depth == 1, "a stacked trunk needs a residual-only output kernel between layers"
    assert t_p % (WKV_CHUNKS_PER_STEP * CHUNK) == 0
    g_ones = _head_ones()
    row = lambda z: z.reshape(1, -1).astype(F32)

    hp = x_prompt.astype(F32).reshape(n_p * t_p, D_MODEL)
    hs = jnp.transpose(x_sample.astype(F32), (1, 0, 2)).reshape(t_s * n_s, D_MODEL)
    outs = [[] for _ in range(6)]
    for l in range(depth):
        zl = jnp.zeros((D_LORA, D_RWKV), F32)
        w_lora = jnp.concatenate([jnp.concatenate([w_decay_b[l], zl], axis=1),
                                  jnp.concatenate([zl, w_aaa_b[l]], axis=1)], axis=0).astype(BF16)
        prep_params = (row(norm_w[l]), w_in[l].astype(BF16), row(mu_shift[l]), w_lora, row(w0[l]),
                       row(a0[l]), row(k_k[l]), row(k_a[l]), row(r_k[l]), pool_w[l].astype(BF16),
                       row(pool_scale[l]), g_ones)
        out_params = (row(gn_w[l]), row(gn_b[l]), w_out[l].astype(BF16), row(norm_f), g_ones)

        hp, nsh, npl, s_fin = _fused_call(hp, prep_params, out_params,
                                          n_seq=n_p, n_chunks=WKV_CHUNKS_PER_STEP, t0=0)
        outs[0].append(nsh[:, SUBLANES - 1])
        outs[1].append(s_fin)
        outs[2].append(npl[:, -POOL_KEEP:])

        sh_hist = state_shift[l].astype(F32)[None]
        pool_hist = jnp.transpose(state_pool[l].astype(F32), (1, 0, 2)).reshape(1, POOL_KEEP * n_s, D_POOL)
        bon, sg, op, nsh, npl, *scan_ops = _short_prep_call(hs, sh_hist, pool_hist, prep_params,
                                                            t_s=t_s, n_s=n_s, t0=PAST_LEN)
        s_fin, hs = _short_state_call(jnp.transpose(state_wkv[l].astype(F32), (1, 2, 3, 0)), *scan_ops,
                                      bon, sg, op, hs, out_params)
        s_fin = jnp.transpose(s_fin, (3, 0, 1, 2))
        outs[3].append(nsh[0])
        outs[4].append(s_fin)
        outs[5].append(jnp.transpose(npl[0].reshape(POOL_KEEP, n_s, D_POOL), (1, 0, 2)))

    y_prompt = hp.reshape(n_p, t_p, D_MODEL).astype(x_prompt.dtype)
    y_sample = jnp.transpose(hs.reshape(t_s, n_s, D_MODEL), (1, 0, 2)).astype(x_sample.dtype)
    return (y_prompt, y_sample) + tuple(jnp.stack(o, axis=0) for o in outs)
```

```python
import functools

import jax
import jax.numpy as jnp
import numpy as np
from jax import lax
from jax.experimental import pallas as pl
from jax.experimental.pallas import tpu as pltpu

F32 = jnp.float32
BF16 = jnp.bfloat16

D_MODEL = 1024
D_RWKV = 512
D_POOL = 512
HEAD = 64
N_HEADS = D_RWKV // HEAD
LANES = 128
SUBLANES = 8
MXU_DIM = 256
N_PAIRS = D_RWKV // LANES
D_LORA = 64
POOL_WINDOWS = (2, 4, 8, 16)
POOL_GROUP = D_POOL // len(POOL_WINDOWS)
POOL_KEEP = max(POOL_WINDOWS) - 1
D_SHIFT = 3 * D_RWKV + 2 * D_LORA
D_IN = D_SHIFT + D_RWKV + 2 * D_POOL
PAST_LEN = 16384
NORM_EPS = 1e-6
GN_EPS = 64e-5
L2_EPS = 1e-12
CHUNK = 64
VMEM_LIMIT = 56 * 1024 * 1024
P_A = (1, 1)
P_T = (1, 1)
P_U = (1, 1)
P_S = (1, 1)
P_Y = (1, 1)
P_CUMSUM = 2
WKV_CHUNKS_PER_STEP = 8

NN = (((1,), (0,)), ((), ()))
NT = (((1,), (1,)), ((), ()))
TN = (((0,), (0,)), ((), ()))


def _pieces(x, n):
    if x.dtype == BF16:
        return [x]
    out = []
    rem = x
    for i in range(n):
        p = rem.astype(BF16)
        out.append(p)
        if i + 1 < n:
            rem = rem - p.astype(F32)
    return out


def _mm(a, b, dims=NN, na=1, nb=1):
    ap = _pieces(a, na)
    bp = _pieces(b, nb)
    order = max(len(ap), len(bp))
    acc = None
    for i, x in enumerate(ap):
        for j, y in enumerate(bp):
            if i + j >= order:
                continue
            t = lax.dot_general(x, y, dims, preferred_element_type=F32)
            acc = t if acc is None else acc + t
    return acc


def _seg_sum(x, g_ones, pieces=1):
    cols = []
    for s in range(x.shape[1] // MXU_DIM):
        cols.append(_mm(x[:, s * MXU_DIM:(s + 1) * MXU_DIM], g_ones, NN, pieces, 1))
    return cols[0] if len(cols) == 1 else jnp.concatenate(cols, axis=1)


def _sigmoid(x):
    return 1.0 / (1.0 + jnp.exp(-x))


def _pool_hist_rows(stride):
    return -(-POOL_KEEP * stride // SUBLANES) * SUBLANES


def _norm_operands(x_ref, nw_ref):
    x = x_ref[...]
    rstd = lax.rsqrt(jnp.mean(x * x, axis=-1, keepdims=True) + NORM_EPS)
    return (x * nw_ref[...]).astype(BF16), rstd


def _prep_stages(h, rstd, win_ref, j, sh_ref, ph_ref, mu_ref, wl_ref, w0_ref, a0_ref, kk_ref,
                 ka_ref, rk_ref, pw_ref, ps_ref, g_ref,
                 r_o, k_o, v_o, lw_o, a_o, b_o, bon_o, sg_o, op_o, nsh_o, npl_o,
                 p_ext, u_ext, *, stride, t0, hs):
    rows = h.shape[0]
    ph = _pool_hist_rows(stride)
    width = MXU_DIM

    @pl.when(j == 0)
    def _():
        p_ext[0:hs, :] = jnp.zeros((hs, D_SHIFT), F32) if sh_ref is None else sh_ref[0]
        u_ext[0:ph, :] = jnp.zeros((ph, D_POOL), F32) if ph_ref is None else ph_ref[0]

    def proj_cols(c0, c1):
        return jnp.dot(h, win_ref[:, c0:c1], preferred_element_type=F32) * rstd

    def shifted(c0, c1):
        p = proj_cols(c0, c1)
        p_ext[hs:hs + rows, c0:c1] = p
        prev = p_ext[pl.ds(hs - stride, rows), c0:c1]
        return p + mu_ref[:, c0:c1] * (prev - p)

    xwa = shifted(3 * D_RWKV, D_SHIFT)
    for c in range(0, D_RWKV, width):
        r_o[:, c:c + width] = shifted(c, c + width)
    lane = lax.broadcasted_iota(jnp.int32, xwa.shape, 1)
    lora_in = jnp.where(lane < D_LORA, jnp.tanh(xwa), xwa)
    lora = jnp.dot(lora_in.astype(BF16), wl_ref[...], preferred_element_type=F32)
    lw_o[...] = (-0.6065306597126334) * _sigmoid(w0_ref[...] + lora[:, :D_RWKV])
    alpha = _sigmoid(a0_ref[...] + lora[:, D_RWKV:])
    for c in range(0, D_RWKV, width):
        sl = slice(c, c + width)
        k = shifted(D_RWKV + c, D_RWKV + c + width)
        k_o[:, sl] = k * (1.0 + (alpha[:, sl] - 1.0) * ka_ref[:, sl])
        kk = k * kk_ref[:, sl]
        kk = kk * lax.rsqrt(_seg_sum(kk * kk, g_ref[...]) + L2_EPS)
        a_o[:, sl] = -kk
        b_o[:, sl] = kk * alpha[:, sl]
        v_o[:, sl] = shifted(2 * D_RWKV + c, 2 * D_RWKV + c + width)
    last_p = p_ext[rows:rows + hs, :]
    p_ext[0:hs, :] = last_p
    nsh_o[0] = last_p
    for c in range(0, D_RWKV, width):
        g_rwkv = proj_cols(D_SHIFT + c, D_SHIFT + c + width)
        sg_o[:, c:c + width] = g_rwkv * _sigmoid(g_rwkv)
    for c in range(0, D_POOL, width):
        u_ext[ph:ph + rows, c:c + width] = proj_cols(D_SHIFT + D_RWKV + c, D_SHIFT + D_RWKV + c + width)
        op_o[:, c:c + width] = proj_cols(D_SHIFT + D_RWKV + D_POOL + c,
                                         D_SHIFT + D_RWKV + D_POOL + c + width)

    def bonus():
        bon_o[...] = _seg_sum(r_o[...] * k_o[...] * rk_ref[...], g_ref[...]) * v_o[...]

    def pool(gi, win):
        sl = slice(gi * POOL_GROUP, (gi + 1) * POOL_GROUP)
        row = lax.broadcasted_iota(jnp.int32, (rows, 1), 0)
        pos = t0 + (j * rows + row) // stride
        u = u_ext[ph:ph + rows, sl]
        tot = u
        for back in range(1, win):
            tot = tot + u_ext[pl.ds(ph - back * stride, rows), sl]
        inv_cnt = 1.0 / jnp.minimum(pos + 1, win).astype(F32)
        d = tot * inv_cnt - u
        o = jnp.dot(d.astype(BF16), pw_ref[gi], preferred_element_type=F32)
        gp = op_o[:, sl]
        op_o[:, sl] = o * ps_ref[:, sl] * (gp * _sigmoid(gp))

    def carry_pool():
        last_u = u_ext[rows:rows + ph, :]
        u_ext[0:ph, :] = last_u
        npl_o[0] = last_u

    pools = [functools.partial(pool, gi, win) for gi, win in enumerate(POOL_WINDOWS)]
    return [bonus] + pools + [carry_pool]


def _stack(x, m0):
    return jnp.concatenate([jnp.where(m0, x, 0.0), jnp.where(m0, 0.0, x)], axis=0)


def _wkv_zero_state(z_ref, is_first):
    @pl.when(is_first)
    def _():
        z_ref[...] = jnp.zeros(z_ref.shape, F32)


def _wkv_store_state(z_ref, so_ref, is_last):
    @pl.when(is_last)
    def _():
        for p in range(N_PAIRS):
            s2 = z_ref[p].T
            so_ref[0, 2 * p] = s2[:HEAD, :HEAD]
            so_ref[0, 2 * p + 1] = s2[HEAD:, HEAD:]


def _wkv_tile(r_ref, k_ref, v_ref, lw_ref, a_ref, b_ref, tri_ref, mss_ref, msq_ref, y_ref, z_ref, *,
              n_chunks, fillers=()):
    fillers = list(fillers)
    fill = lambda: fillers.pop(0)() if fillers else None
    lane = lax.broadcasted_iota(jnp.int32, (CHUNK, LANES), 1)
    m0 = lane < HEAD
    strict = mss_ref[0][:, :LANES] > 0.5
    incl2 = mss_ref[1] > 0.5
    n_levels = mss_ref.shape[0] - 3
    lane_head4 = lax.broadcasted_iota(jnp.int32, (CHUNK, 2 * LANES), 1) // HEAD
    eye = msq_ref[0]
    same_head = msq_ref[1] > 0.5
    zblk = jnp.zeros((CHUNK, LANES), BF16)
    zblk2 = jnp.zeros((2 * CHUNK, LANES), BF16)
    bf = lambda x: x.astype(BF16)
    lanes = lambda p: slice(p * LANES, (p + 1) * LANES)
    each = lambda fn, *lists: [fn(*args) for args in zip(*lists)]
    cat0 = lambda *xs: jnp.concatenate(xs, axis=0)
    cat1 = lambda *xs: jnp.concatenate(xs, axis=1)

    scaled = []
    for ci in range(n_chunks):
        rows = slice(ci * CHUNK, (ci + 1) * CHUNK)
        lw = lw_ref[rows, :]
        cs = _mm(tri_ref[...], lw, NN, 1, P_CUMSUM)
        cs_end = cs[CHUNK - 1:CHUNK, :]
        g_inv = jnp.exp(-cs)
        g_tail = jnp.exp(cs_end - cs)
        r_t = r_ref[rows, :] * jnp.exp(cs)
        scaled.append(dict(
            g_end=jnp.exp(cs_end), r_f32=r_t, r_t=bf(r_t),
            a_t=bf(a_ref[rows, :] * jnp.exp(cs - lw)),
            b_t=bf(b_ref[rows, :] * g_inv), k_t=bf(k_ref[rows, :] * g_inv),
            b_h=bf(b_ref[rows, :] * g_tail), k_h=bf(k_ref[rows, :] * g_tail), v=bf(v_ref[rows, :])))
    chains = [(ci, p) for ci in range(n_chunks) for p in range(N_PAIRS)]
    pair = lambda name: [scaled[ci][name][:, lanes(p)] for ci, p in chains]
    at, rt, bt, kt, bh, kh, v, r_f32 = (pair(n) for n in ("a_t", "r_t", "b_t", "k_t", "b_h", "k_h", "v",
                                                          "r_f32"))
    v2 = each(lambda x: _stack(x, m0), v)
    scores = each(lambda a, r, b, k: _mm(cat0(a, r), cat0(_stack(b, m0), _stack(k, m0)), NT, *P_A),
                  at, rt, bt, kt)
    a_ab = each(lambda s: jnp.where(strict, s[:CHUNK, :LANES], 0.0), scores)
    a_ak = each(lambda s: jnp.where(strict, s[:CHUNK, LANES:], 0.0), scores)
    a_r = each(lambda s: jnp.where(incl2, s[CHUNK:, :], 0.0), scores)
    fill()
    stack4 = lambda x: cat0(*[jnp.where(lane_head4 == h, x, 0.0) for h in range(4)])
    join = lambda xs: [cat1(xs[i], xs[i + 1]) for i in range(0, len(xs), 2)]
    split = lambda xs: [x[:, s] for x in xs for s in (slice(0, LANES), slice(LANES, 2 * LANES))]
    akv = split(each(lambda x, y: _mm(x, stack4(y), NN, *P_U), join(a_ak), join(v)))
    a_ab4 = join(a_ab)
    t_inv4 = each(lambda x: mss_ref[2] + x * mss_ref[3], a_ab4)
    for lvl in range(1, n_levels):
        t_b = each(bf, t_inv4)
        inner = each(lambda x, t: _mm(x * mss_ref[3 + lvl], stack4(t), NN, *P_T), a_ab4, t_b)
        t_inv4 = each(lambda t, tb, w: t + _mm(tb, stack4(bf(w)), NN, *P_T), t_inv4, t_b, inner)
        fill()
    t_inv = split(each(bf, t_inv4))
    pq = each(lambda t, x, y: _mm(t, cat1(_stack(x, m0), _stack(bf(y), m0)), NN, *P_U), t_inv, at, akv)
    pq_b = each(bf, pq)
    bpq = each(lambda b, k, x, y: _mm(cat0(b, k), cat0(x, cat1(zblk, y)), TN, *P_S), bh, kh, pq_b, v)
    g_end = [scaled[ci]["g_end"][:, lanes(p)] for ci, p in chains]
    m_t = each(lambda g, x: eye * g + jnp.where(same_head, x[:, :LANES], 0.0), g_end, bpq)
    n_t = each(lambda x: jnp.where(same_head, x[:, LANES:], 0.0), bpq)
    yy = each(lambda x, y, z: _mm(x, cat0(cat1(_stack(y[:, :LANES], m0), _stack(y[:, LANES:], m0)),
                                         cat1(zblk2, z)), NN, *P_Y), a_r, pq_b, v2)
    y1 = each(lambda r, x: r + x[:, :LANES], r_f32, yy)
    y2 = each(lambda x: x[:, LANES:], yy)

    while fillers:
        fill()
    state = [z_ref[p] for p in range(N_PAIRS)]
    for ci in range(n_chunks):
        for p in range(N_PAIRS):
            i = ci * N_PAIRS + p
            both = _mm(cat0(m_t[i], y1[i]), state[p], NN, *P_S)
            state[p] = both[:LANES] + n_t[i]
            y_ref[ci * CHUNK:(ci + 1) * CHUNK, lanes(p)] = both[LANES:] + y2[i]
    for p in range(N_PAIRS):
        z_ref[p] = state[p]


def _wkv_masks():
    ri, ci = np.indices((CHUNK, 4 * CHUNK))
    ci = ci % CHUNK
    side = [ci < ri, ci <= ri, ri == ci]
    m = 1
    while m < CHUNK:
        side.append(((ri // (2 * m)) == (ci // (2 * m))) & ((ri // m) % 2 == 1) & ((ci // m) % 2 == 0))
        m *= 2
    rq, cq = np.indices((LANES, LANES))
    square = [rq == cq, (rq // HEAD) == (cq // HEAD)]
    rt, ct = np.indices((CHUNK, CHUNK))
    return (jnp.asarray(ct <= rt, BF16), jnp.asarray(np.stack(side), F32),
            jnp.asarray(np.stack(square), F32))


def _short_pairs(t_s):
    idx = {}
    for kind, inclusive in (("ab", False), ("ak", False), ("rb", True), ("rk", True)):
        for t in range(t_s):
            for s in range(t + 1 if inclusive else t):
                idx[(kind, t, s)] = len(idx)
    return idx


def _short_scale_kernel(r_ref, k_ref, v_ref, lw_ref, a_ref, b_ref,
                        xa_o, xr_o, bh_o, kh_o, vt_o, g4_o, as_o, *, t_s, n_s):
    slab = lambda ref, t: ref[t * n_s:(t + 1) * n_s, :].T
    cs = []
    for t in range(t_s):
        cs.append(slab(lw_ref, t) if t == 0 else cs[-1] + slab(lw_ref, t))
    a_t = [slab(a_ref, t) * (jnp.exp(cs[t - 1]) if t else 1.0) for t in range(t_s)]
    r_t = [slab(r_ref, t) * jnp.exp(cs[t]) for t in range(t_s)]
    b_raw = [slab(b_ref, t) for t in range(t_s)]
    k_raw = [slab(k_ref, t) for t in range(t_s)]
    g_inv = [jnp.exp(-cs[t]) for t in range(t_s)]
    b_t = [b_raw[t] * g_inv[t] for t in range(t_s)]
    k_t = [k_raw[t] * g_inv[t] for t in range(t_s)]
    for t in range(t_s):
        g_tail = jnp.exp(cs[t_s - 1] - cs[t])
        xa_o[t] = a_t[t]
        xr_o[t] = r_t[t]
        bh_o[t] = b_raw[t] * g_tail
        kh_o[t] = k_raw[t] * g_tail
        vt_o[t] = slab(v_ref, t)
    g4_o[...] = jnp.exp(cs[t_s - 1])
    head_sum = lambda x: jnp.sum(x.reshape(N_HEADS, HEAD, n_s), axis=1)
    left = {"ab": a_t, "ak": a_t, "rb": r_t, "rk": r_t}
    right = {"ab": b_t, "ak": k_t, "rb": b_t, "rk": k_t}
    for (kind, t, s), row in _short_pairs(t_s).items():
        as_o[row] = head_sum(left[kind][t] * right[kind][s])


def _short_state_kernel(s_ref, xa_ref, xr_ref, bh_ref, kh_ref, vt_ref, g4_ref, as_ref, *refs, t_s):
    out_ins, (so_ref, o_ref, y_ref) = refs[:-3], refs[-3:]
    idx = _short_pairs(t_s)
    head = pl.program_id(0)
    scal = {key: as_ref[row, pl.ds(head, 1), :] for key, row in idx.items()}

    def value_row(v, carry):
        s_v = s_ref[0, v]
        vv = [vt_ref[t, pl.ds(v, 1), :] for t in range(t_s)]
        u = []
        for t in range(t_s):
            acc = jnp.sum(s_v * xa_ref[t], axis=0, keepdims=True)
            for s in range(t):
                acc = acc + scal[("ab", t, s)] * u[s] + scal[("ak", t, s)] * vv[s]
            u.append(acc)
        for t in range(t_s):
            acc = jnp.sum(s_v * xr_ref[t], axis=0, keepdims=True)
            for s in range(t + 1):
                acc = acc + scal[("rb", t, s)] * u[s] + scal[("rk", t, s)] * vv[s]
            y_ref[t, pl.ds(head * HEAD + v, 1), :] = acc
        new = s_v * g4_ref[...]
        for s in range(t_s):
            new = new + u[s] * bh_ref[s] + vv[s] * kh_ref[s]
        so_ref[0, v] = new
        return carry

    lax.fori_loop(0, HEAD, value_row, 0, unroll=8)

    @pl.when(head == pl.num_programs(0) - 1)
    def _():
        _out_kernel(y_ref, *out_ins, o_ref)


def _short_prep_kernel(x_ref, sh_ref, ph_ref, nw_ref, win_ref, *refs, t_s, n_s, t0):
    prep_rest, refs = refs[:N_PREP_PARAMS - 2], refs[N_PREP_PARAMS - 2:]
    bon_o, sg_o, op_o, nsh_o, npl_o = refs[:5]
    scale_outs, seq_s, (p_ext, u_ext) = refs[5:12], refs[12:18], refs[18:]
    h, rstd = _norm_operands(x_ref, nw_ref)
    for finish in _prep_stages(h, rstd, win_ref, pl.program_id(0), sh_ref, ph_ref, *prep_rest, *seq_s,
                               bon_o, sg_o, op_o, nsh_o, npl_o, p_ext, u_ext,
                               stride=n_s, t0=t0, hs=sh_ref.shape[1]):
        finish()
    _short_scale_kernel(*seq_s, *scale_outs, t_s=t_s, n_s=n_s)


def _short_prep_call(x, sh_hist, pool_hist, params, *, t_s, n_s, t0):
    assert n_s % LANES == 0 and len(params) == N_PREP_PARAMS
    rows = t_s * n_s
    hs, ph = sh_hist.shape[1], _pool_hist_rows(n_s)
    assert pool_hist.shape[1] == ph
    n_scal = len(_short_pairs(t_s))
    full = lambda shape: pl.BlockSpec(shape, lambda i: tuple(0 for _ in shape))
    vec = (t_s, D_RWKV, n_s)
    out_shapes = ([(rows, D_RWKV)] * 3 + [(1, hs, D_SHIFT), (1, ph, D_POOL)] + [vec] * 5
                  + [(D_RWKV, n_s), (n_scal, N_HEADS, n_s)])
    return pl.pallas_call(
        functools.partial(_short_prep_kernel, t_s=t_s, n_s=n_s, t0=t0),
        grid=(1,),
        in_specs=[full(a.shape) for a in (x, sh_hist, pool_hist) + tuple(params)],
        out_specs=[full(s) for s in out_shapes],
        out_shape=[jax.ShapeDtypeStruct(s, F32) for s in out_shapes],
        scratch_shapes=[pltpu.VMEM((rows, D_RWKV), F32)] * 6 + [pltpu.VMEM((hs + rows, D_SHIFT), F32),
                                                                pltpu.VMEM((ph + rows, D_POOL), F32)],
        compiler_params=pltpu.CompilerParams(
            dimension_semantics=("arbitrary",), vmem_limit_bytes=VMEM_LIMIT),
        name="prep_short",
    )(x, sh_hist, pool_hist, *params)


def _short_state_call(s0, xa, xr, bh, kh, vt, g4, scal, bon, sg, op, x, out_params):
    t_s, _, n_s = xa.shape
    assert len(out_params) == N_OUT_PARAMS and x.shape[0] == t_s * n_s
    st = pl.BlockSpec((1, HEAD, HEAD, n_s), lambda h: (h, 0, 0, 0))
    per_head = pl.BlockSpec((t_s, HEAD, n_s), lambda h: (0, h, 0))
    whole = lambda a: pl.BlockSpec(a.shape, lambda h: (0,) * a.ndim)
    out_ins = (bon, sg, op, x) + tuple(out_params)
    return pl.pallas_call(
        functools.partial(_short_state_kernel, t_s=t_s),
        grid=(N_HEADS,),
        in_specs=[st] + [per_head] * 5 + [pl.BlockSpec((HEAD, n_s), lambda h: (h, 0)), whole(scal)]
        + [whole(a) for a in out_ins],
        out_specs=[st, whole(x)],
        out_shape=[jax.ShapeDtypeStruct(s0.shape, F32), jax.ShapeDtypeStruct(x.shape, F32)],
        scratch_shapes=[pltpu.VMEM(xa.shape, F32)],
        compiler_params=pltpu.CompilerParams(
            dimension_semantics=("arbitrary",), vmem_limit_bytes=VMEM_LIMIT),
        name="layer_short_state",
    )(s0, xa, xr, bh, kh, vt, g4, scal, *out_ins)


def _gated_rwkv(y, bon, sg, gw_ref, gb_ref, g_ref):
    g_ones = g_ref[...]
    mu = _seg_sum(y, g_ones) * (1.0 / HEAD)
    d = y - mu
    var = _seg_sum(d * d, g_ones) * (1.0 / HEAD)
    yn = d * lax.rsqrt(var + GN_EPS) * gw_ref[...] + gb_ref[...]
    return (yn + bon) * sg


def _out_project(o_rwkv, o_pool, x, wo_ref, nf_ref):
    out = (jnp.dot(o_rwkv.astype(BF16), wo_ref[0:D_RWKV, :], preferred_element_type=F32)
           + jnp.dot(o_pool.astype(BF16), wo_ref[D_RWKV:, :], preferred_element_type=F32))
    res = x + out
    return res * lax.rsqrt(jnp.mean(res * res, axis=-1, keepdims=True) + NORM_EPS) * nf_ref[...]


def _out_kernel(y_ref, bon_ref, sg_ref, op_ref, x_ref, gw_ref, gb_ref, wo_ref, nf_ref, g_ref, o_ref):
    if len(y_ref.shape) == 3:
        y = jnp.concatenate([y_ref[t].T for t in range(y_ref.shape[0])], axis=0)
    else:
        y = y_ref[...]
    o_rwkv = _gated_rwkv(y, bon_ref[...], sg_ref[...], gw_ref, gb_ref, g_ref)
    o_ref[...] = _out_project(o_rwkv, op_ref[...], x_ref[...], wo_ref, nf_ref)


N_PREP_PARAMS = 12
N_WKV_CONSTS = 3
N_OUT_PARAMS = 5


def _fused_kernel(*refs, n_chunks, t0):
    x_ref, nw_ref, win_ref = refs[:3]
    pos = 1 + N_PREP_PARAMS
    prep_rest = refs[3:pos]
    wkv_consts = refs[pos:pos + N_WKV_CONSTS]
    pos += N_WKV_CONSTS
    gw_ref, gb_ref, wo_ref, nf_ref, g_ref = refs[pos:pos + N_OUT_PARAMS]
    pos += N_OUT_PARAMS
    o_ref, nsh_o, npl_o, so_ref = refs[pos:pos + 4]
    pos += 4
    seq_s = refs[pos:pos + 6]
    bon_s, sg_s, op_s, y_s, p_ext, u_ext, z_ref = refs[pos + 6:]
    j = pl.program_id(1)
    _wkv_zero_state(z_ref, j == 0)
    finish_prep = _prep_stages(*_norm_operands(x_ref, nw_ref), win_ref, j, None, None, *prep_rest,
                               *seq_s, bon_s, sg_s, op_s, nsh_o, npl_o, p_ext, u_ext,
                               stride=1, t0=t0, hs=nsh_o.shape[1])
    _wkv_tile(*seq_s, *wkv_consts, y_s, z_ref, n_chunks=n_chunks, fillers=finish_prep)
    _out_kernel(y_s, bon_s, sg_s, op_s, x_ref, gw_ref, gb_ref, wo_ref, nf_ref, g_ref, o_ref)
    _wkv_store_state(z_ref, so_ref, j == pl.num_programs(1) - 1)


def _fused_call(x, prep_params, out_params, *, n_seq, n_chunks, t0):
    assert len(prep_params) == N_PREP_PARAMS and len(out_params) == N_OUT_PARAMS
    total = x.shape[0]
    rows = n_chunks * CHUNK
    tiles = total // (n_seq * rows)
    hs, ph = SUBLANES, _pool_hist_rows(1)
    assert CHUNK == HEAD and 2 * CHUNK == LANES
    wkv_consts = _wkv_masks()
    act = pl.BlockSpec((rows, D_MODEL), lambda b, j: (b * tiles + j, 0))
    per_seq = lambda shape: pl.BlockSpec((1,) + shape[1:], lambda b, j: (b,) + (0,) * (len(shape) - 1))
    const = lambda a: pl.BlockSpec(a.shape, lambda b, j: (0,) * a.ndim, pipeline_mode=pl.Buffered(1))
    consts = tuple(prep_params) + tuple(wkv_consts) + tuple(out_params)
    seq_shapes = [(n_seq, hs, D_SHIFT), (n_seq, ph, D_POOL), (n_seq, N_HEADS, HEAD, HEAD)]
    seq_scratch = pltpu.VMEM((rows, D_RWKV), F32)
    return pl.pallas_call(
        functools.partial(_fused_kernel, n_chunks=n_chunks, t0=t0),
        grid=(n_seq, tiles),
        in_specs=[act] + [const(a) for a in consts],
        out_specs=[act] + [per_seq(s) for s in seq_shapes],
        out_shape=[jax.ShapeDtypeStruct((total, D_MODEL), F32)]
        + [jax.ShapeDtypeStruct(s, F32) for s in seq_shapes],
        scratch_shapes=[seq_scratch] * 10 + [pltpu.VMEM((hs + rows, D_SHIFT), F32),
                                             pltpu.VMEM((ph + rows, D_POOL), F32),
                                             pltpu.VMEM((N_PAIRS, LANES, LANES), F32)],
        compiler_params=pltpu.CompilerParams(
            dimension_semantics=("arbitrary", "arbitrary"), vmem_limit_bytes=VMEM_LIMIT),
        name="layer_long",
    )(x, *consts)


def _head_ones():
    i, j = np.indices((MXU_DIM, MXU_DIM))
    return jnp.asarray((i // HEAD) == (j // HEAD), BF16)


def kernel(x_prompt, x_sample, state_shift, state_wkv, state_pool, norm_w, w_in, mu_shift, w_decay_b,
           w0, w_aaa_b, a0, k_k, k_a, r_k, gn_w, gn_b, pool_w, pool_scale, w_out, norm_f):
    depth = norm_w.shape[0]
    n_p, t_p, _ = x_prompt.shape
    n_s, t_s, _ = x_sample.shape
    assert depth == 1, "a stacked trunk needs a residual-only output kernel between layers"
    assert t_p % (WKV_CHUNKS_PER_STEP * CHUNK) == 0
    g_ones = _head_ones()
    row = lambda z: z.reshape(1, -1).astype(F32)

    hp = x_prompt.astype(F32).reshape(n_p * t_p, D_MODEL)
    hs = jnp.transpose(x_sample.astype(F32), (1, 0, 2)).reshape(t_s * n_s, D_MODEL)
    outs = [[] for _ in range(6)]
    for l in range(depth):
        zl = jnp.zeros((D_LORA, D_RWKV), F32)
        w_lora = jnp.concatenate([jnp.concatenate([w_decay_b[l], zl], axis=1),
                                  jnp.concatenate([zl, w_aaa_b[l]], axis=1)], axis=0).astype(BF16)
        prep_params = (row(norm_w[l]), w_in[l].astype(BF16), row(mu_shift[l]), w_lora, row(w0[l]),
                       row(a0[l]), row(k_k[l]), row(k_a[l]), row(r_k[l]), pool_w[l].astype(BF16),
                       row(pool_scale[l]), g_ones)
        out_params = (row(gn_w[l]), row(gn_b[l]), w_out[l].astype(BF16), row(norm_f), g_ones)

        hp, nsh, npl, s_fin = _fused_call(hp, prep_params, out_params,
                                          n_seq=n_p, n_chunks=WKV_CHUNKS_PER_STEP, t0=0)
        outs[0].append(nsh[:, SUBLANES - 1])
        outs[1].append(s_fin)
        outs[2].append(npl[:, -POOL_KEEP:])

        sh_hist = state_shift[l].astype(F32)[None]
        pool_hist = jnp.transpose(state_pool[l].astype(F32), (1, 0, 2)).reshape(1, POOL_KEEP * n_s, D_POOL)
        bon, sg, op, nsh, npl, *scan_ops = _short_prep_call(hs, sh_hist, pool_hist, prep_params,
                                                            t_s=t_s, n_s=n_s, t0=PAST_LEN)
        s_fin, hs = _short_state_call(jnp.transpose(state_wkv[l].astype(F32), (1, 2, 3, 0)), *scan_ops,
                                      bon, sg, op, hs, out_params)
        s_fin = jnp.transpose(s_fin, (3, 0, 1, 2))
        outs[3].append(nsh[0])
        outs[4].append(s_fin)
        outs[5].append(jnp.transpose(npl[0].reshape(POOL_KEEP, n_s, D_POOL), (1, 0, 2)))

    y_prompt = hp.reshape(n_p, t_p, D_MODEL).astype(x_prompt.dtype)
    y_sample = jnp.transpose(hs.reshape(t_s, n_s, D_MODEL), (1, 0, 2)).astype(x_sample.dtype)
    return (y_prompt, y_sample) + tuple(jnp.stack(o, axis=0) for o in outs)
```

```python
import functools

import jax
import jax.numpy as jnp
import numpy as np
from jax import lax
from jax.experimental import pallas as pl
from jax.experimental.pallas import tpu as pltpu

F32 = jnp.float32
BF16 = jnp.bfloat16

D_MODEL = 1024
D_RWKV = 512
D_POOL = 512
HEAD = 64
N_HEADS = D_RWKV // HEAD
LANES = 128
SUBLANES = 8
MXU_DIM = 256
N_PAIRS = D_RWKV // LANES
D_LORA = 64
POOL_WINDOWS = (2, 4, 8, 16)
POOL_GROUP = D_POOL // len(POOL_WINDOWS)
POOL_KEEP = max(POOL_WINDOWS) - 1
D_SHIFT = 3 * D_RWKV + 2 * D_LORA
D_IN = D_SHIFT + D_RWKV + 2 * D_POOL
PAST_LEN = 16384
NORM_EPS = 1e-6
GN_EPS = 64e-5
L2_EPS = 1e-12
CHUNK = 64
VMEM_LIMIT = 56 * 1024 * 1024
P_A = (1, 1)
P_T = (1, 1)
P_U = (1, 1)
P_S = (1, 1)
P_Y = (1, 1)
P_CUMSUM = 2
WKV_CHUNKS_PER_STEP = 8

NN = (((1,), (0,)), ((), ()))
NT = (((1,), (1,)), ((), ()))
TN = (((0,), (0,)), ((), ()))


def _pieces(x, n):
    if x.dtype == BF16:
        return [x]
    out = []
    rem = x
    for i in range(n):
        p = rem.astype(BF16)
        out.append(p)
        if i + 1 < n:
            rem = rem - p.astype(F32)
    return out


def _mm(a, b, dims=NN, na=1, nb=1):
    ap = _pieces(a, na)
    bp = _pieces(b, nb)
    order = max(len(ap), len(bp))
    acc = None
    for i, x in enumerate(ap):
        for j, y in enumerate(bp):
            if i + j >= order:
                continue
            t = lax.dot_general(x, y, dims, preferred_element_type=F32)
            acc = t if acc is None else acc + t
    return acc


def _seg_sum(x, g_ones, pieces=1):
    cols = []
    for s in range(x.shape[1] // MXU_DIM):
        cols.append(_mm(x[:, s * MXU_DIM:(s + 1) * MXU_DIM], g_ones, NN, pieces, 1))
    return cols[0] if len(cols) == 1 else jnp.concatenate(cols, axis=1)


def _sigmoid(x):
    return 1.0 / (1.0 + jnp.exp(-x))


def _pool_hist_rows(stride):
    return -(-POOL_KEEP * stride // SUBLANES) * SUBLANES


def _norm_operands(x_ref, nw_ref):
    x = x_ref[...]
    rstd = lax.rsqrt(jnp.mean(x * x, axis=-1, keepdims=True) + NORM_EPS)
    return (x * nw_ref[...]).astype(BF16), rstd


def _prep_stages(h, rstd, win_ref, j, sh_ref, ph_ref, mu_ref, wl_ref, w0_ref, a0_ref, kk_ref,
                 ka_ref, rk_ref, pw_ref, ps_ref, g_ref,
                 r_o, k_o, v_o, lw_o, a_o, b_o, bon_o, sg_o, op_o, nsh_o, npl_o,
                 p_ext, u_ext, *, stride, t0, hs):
    rows = h.shape[0]
    ph = _pool_hist_rows(stride)
    width = MXU_DIM

    @pl.when(j == 0)
    def _():
        p_ext[0:hs, :] = jnp.zeros((hs, D_SHIFT), F32) if sh_ref is None else sh_ref[0]
        u_ext[0:ph, :] = jnp.zeros((ph, D_POOL), F32) if ph_ref is None else ph_ref[0]

    def proj_cols(c0, c1):
        return jnp.dot(h, win_ref[:, c0:c1], preferred_element_type=F32) * rstd

    def shifted(c0, c1):
        p = proj_cols(c0, c1)
        p_ext[hs:hs + rows, c0:c1] = p
        prev = p_ext[pl.ds(hs - stride, rows), c0:c1]
        return p + mu_ref[:, c0:c1] * (prev - p)

    xwa = shifted(3 * D_RWKV, D_SHIFT)
    for c in range(0, D_RWKV, width):
        r_o[:, c:c + width] = shifted(c, c + width)
    lane = lax.broadcasted_iota(jnp.int32, xwa.shape, 1)
    lora_in = jnp.where(lane < D_LORA, jnp.tanh(xwa), xwa)
    lora = jnp.dot(lora_in.astype(BF16), wl_ref[...], preferred_element_type=F32)
    lw_o[...] = (-0.6065306597126334) * _sigmoid(w0_ref[...] + lora[:, :D_RWKV])
    alpha = _sigmoid(a0_ref[...] + lora[:, D_RWKV:])
    for c in range(0, D_RWKV, width):
        sl = slice(c, c + width)
        k = shifted(D_RWKV + c, D_RWKV + c + width)
        k_o[:, sl] = k * (1.0 + (alpha[:, sl] - 1.0) * ka_ref[:, sl])
        kk = k * kk_ref[:, sl]
        kk = kk * lax.rsqrt(_seg_sum(kk * kk, g_ref[...]) + L2_EPS)
        a_o[:, sl] = -kk
        b_o[:, sl] = kk * alpha[:, sl]
        v_o[:, sl] = shifted(2 * D_RWKV + c, 2 * D_RWKV + c + width)
    last_p = p_ext[rows:rows + hs, :]
    p_ext[0:hs, :] = last_p
    nsh_o[0] = last_p
    for c in range(0, D_RWKV, width):
        g_rwkv = proj_cols(D_SHIFT + c, D_SHIFT + c + width)
        sg_o[:, c:c + width] = g_rwkv * _sigmoid(g_rwkv)
    for c in range(0, D_POOL, width):
        u_ext[ph:ph + rows, c:c + width] = proj_cols(D_SHIFT + D_RWKV + c, D_SHIFT + D_RWKV + c + width)
        op_o[:, c:c + width] = proj_cols(D_SHIFT + D_RWKV + D_POOL + c,
                                         D_SHIFT + D_RWKV + D_POOL + c + width)

    def bonus():
        bon_o[...] = _seg_sum(r_o[...] * k_o[...] * rk_ref[...], g_ref[...]) * v_o[...]

    def pool(gi, win):
        sl = slice(gi * POOL_GROUP, (gi + 1) * POOL_GROUP)
        row = lax.broadcasted_iota(jnp.int32, (rows, 1), 0)
        pos = t0 + (j * rows + row) // stride
        u = u_ext[ph:ph + rows, sl]
        tot = u
        for back in range(1, win):
            tot = tot + u_ext[pl.ds(ph - back * stride, rows), sl]
        inv_cnt = 1.0 / jnp.minimum(pos + 1, win).astype(F32)
        d = tot * inv_cnt - u
        o = jnp.dot(d.astype(BF16), pw_ref[gi], preferred_element_type=F32)
        gp = op_o[:, sl]
        op_o[:, sl] = o * ps_ref[:, sl] * (gp * _sigmoid(gp))

    def carry_pool():
        last_u = u_ext[rows:rows + ph, :]
        u_ext[0:ph, :] = last_u
        npl_o[0] = last_u

    pools = [functools.partial(pool, gi, win) for gi, win in enumerate(POOL_WINDOWS)]
    return [bonus] + pools + [carry_pool]


def _stack(x, m0):
    return jnp.concatenate([jnp.where(m0, x, 0.0), jnp.where(m0, 0.0, x)], axis=0)


def _wkv_zero_state(z_ref, is_first):
    @pl.when(is_first)
    def _():
        z_ref[...] = jnp.zeros(z_ref.shape, F32)


def _wkv_store_state(z_ref, so_ref, is_last):
    @pl.when(is_last)
    def _():
        for p in range(N_PAIRS):
            s2 = z_ref[p].T
            so_ref[0, 2 * p] = s2[:HEAD, :HEAD]
            so_ref[0, 2 * p + 1] = s2[HEAD:, HEAD:]


def _wkv_tile(r_ref, k_ref, v_ref, lw_ref, a_ref, b_ref, tri_ref, mss_ref, msq_ref, y_ref, z_ref, *,
              n_chunks, fillers=()):
    fillers = list(fillers)
    fill = lambda: fillers.pop(0)() if fillers else None
    lane = lax.broadcasted_iota(jnp.int32, (CHUNK, LANES), 1)
    m0 = lane < HEAD
    strict = mss_ref[0][:, :LANES] > 0.5
    incl2 = mss_ref[1] > 0.5
    n_levels = mss_ref.shape[0] - 3
    lane_head4 = lax.broadcasted_iota(jnp.int32, (CHUNK, 2 * LANES), 1) // HEAD
    eye = msq_ref[0]
    same_head = msq_ref[1] > 0.5
    zblk = jnp.zeros((CHUNK, LANES), BF16)
    zblk2 = jnp.zeros((2 * CHUNK, LANES), BF16)
    bf = lambda x: x.astype(BF16)
    lanes = lambda p: slice(p * LANES, (p + 1) * LANES)
    each = lambda fn, *lists: [fn(*args) for args in zip(*lists)]
    cat0 = lambda *xs: jnp.concatenate(xs, axis=0)
    cat1 = lambda *xs: jnp.concatenate(xs, axis=1)

    scaled = []
    for ci in range(n_chunks):
        rows = slice(ci * CHUNK, (ci + 1) * CHUNK)
        lw = lw_ref[rows, :]
        cs = _mm(tri_ref[...], lw, NN, 1, P_CUMSUM)
        cs_end = cs[CHUNK - 1:CHUNK, :]
        g_inv = jnp.exp(-cs)
        g_tail = jnp.exp(cs_end - cs)
        r_t = r_ref[rows, :] * jnp.exp(cs)
        scaled.append(dict(
            g_end=jnp.exp(cs_end), r_f32=r_t, r_t=bf(r_t),
            a_t=bf(a_ref[rows, :] * jnp.exp(cs - lw)),
            b_t=bf(b_ref[rows, :] * g_inv), k_t=bf(k_ref[rows, :] * g_inv),
            b_h=bf(b_ref[rows, :] * g_tail), k_h=bf(k_ref[rows, :] * g_tail), v=bf(v_ref[rows, :])))
    chains = [(ci, p) for ci in range(n_chunks) for p in range(N_PAIRS)]
    pair = lambda name: [scaled[ci][name][:, lanes(p)] for ci, p in chains]
    at, rt, bt, kt, bh, kh, v, r_f32 = (pair(n) for n in ("a_t", "r_t", "b_t", "k_t", "b_h", "k_h", "v",
                                                          "r_f32"))
    v2 = each(lambda x: _stack(x, m0), v)
    scores = each(lambda a, r, b, k: _mm(cat0(a, r), cat0(_stack(b, m0), _stack(k, m0)), NT, *P_A),
                  at, rt, bt, kt)
    a_ab = each(lambda s: jnp.where(strict, s[:CHUNK, :LANES], 0.0), scores)
    a_ak = each(lambda s: jnp.where(strict, s[:CHUNK, LANES:], 0.0), scores)
    a_r = each(lambda s: jnp.where(incl2, s[CHUNK:, :], 0.0), scores)
    stack4 = lambda x: cat0(*[jnp.where(lane_head4 == h, x, 0.0) for h in range(4)])
    join = lambda xs: [cat1(xs[i], xs[i + 1]) for i in range(0, len(xs), 2)]
    split = lambda xs: [x[:, s] for x in xs for s in (slice(0, LANES), slice(LANES, 2 * LANES))]
    akv = split(each(lambda x, y: _mm(x, stack4(y), NN, *P_U), join(a_ak), join(v)))
    a_ab4 = join(a_ab)
    t_inv4 = each(lambda x: mss_ref[2] + x * mss_ref[3], a_ab4)
    for lvl in range(1, n_levels):
        t_b = each(bf, t_inv4)
        inner = each(lambda x, t: _mm(x * mss_ref[3 + lvl], stack4(t), NN, *P_T), a_ab4, t_b)
        t_inv4 = each(lambda t, tb, w: t + _mm(tb, stack4(bf(w)), NN, *P_T), t_inv4, t_b, inner)
    t_inv = split(each(bf, t_inv4))
    pq = each(lambda t, x, y: _mm(t, cat1(_stack(x, m0), _stack(bf(y), m0)), NN, *P_U), t_inv, at, akv)
    pq_b = each(bf, pq)
    bpq = each(lambda b, k, x, y: _mm(cat0(b, k), cat0(x, cat1(zblk, y)), TN, *P_S), bh, kh, pq_b, v)
    g_end = [scaled[ci]["g_end"][:, lanes(p)] for ci, p in chains]
    m_t = each(lambda g, x: eye * g + jnp.where(same_head, x[:, :LANES], 0.0), g_end, bpq)
    n_t = each(lambda x: jnp.where(same_head, x[:, LANES:], 0.0), bpq)
    yy = each(lambda x, y, z: _mm(x, cat0(cat1(_stack(y[:, :LANES], m0), _stack(y[:, LANES:], m0)),
                                         cat1(zblk2, z)), NN, *P_Y), a_r, pq_b, v2)
    y1 = each(lambda r, x: r + x[:, :LANES], r_f32, yy)
    y2 = each(lambda x: x[:, LANES:], yy)

    state = [z_ref[p] for p in range(N_PAIRS)]
    for ci in range(n_chunks):
        for p in range(N_PAIRS):
            i = ci * N_PAIRS + p
            both = _mm(cat0(m_t[i], y1[i]), state[p], NN, *P_S)
            state[p] = both[:LANES] + n_t[i]
            y_ref[ci * CHUNK:(ci + 1) * CHUNK, lanes(p)] = both[LANES:] + y2[i]
        fill()
    for p in range(N_PAIRS):
        z_ref[p] = state[p]
    while fillers:
        fill()


def _wkv_masks():
    ri, ci = np.indices((CHUNK, 4 * CHUNK))
    ci = ci % CHUNK
    side = [ci < ri, ci <= ri, ri == ci]
    m = 1
    while m < CHUNK:
        side.append(((ri // (2 * m)) == (ci // (2 * m))) & ((ri // m) % 2 == 1) & ((ci // m) % 2 == 0))
        m *= 2
    rq, cq = np.indices((LANES, LANES))
    square = [rq == cq, (rq // HEAD) == (cq // HEAD)]
    rt, ct = np.indices((CHUNK, CHUNK))
    return (jnp.asarray(ct <= rt, BF16), jnp.asarray(np.stack(side), F32),
            jnp.asarray(np.stack(square), F32))


def _short_pairs(t_s):
    idx = {}
    for kind, inclusive in (("ab", False), ("ak", False), ("rb", True), ("rk", True)):
        for t in range(t_s):
            for s in range(t + 1 if inclusive else t):
                idx[(kind, t, s)] = len(idx)
    return idx


def _short_scale_kernel(r_ref, k_ref, v_ref, lw_ref, a_ref, b_ref,
                        xa_o, xr_o, bh_o, kh_o, vt_o, g4_o, as_o, *, t_s, n_s):
    slab = lambda ref, t: ref[t * n_s:(t + 1) * n_s, :].T
    cs = []
    for t in range(t_s):
        cs.append(slab(lw_ref, t) if t == 0 else cs[-1] + slab(lw_ref, t))
    a_t = [slab(a_ref, t) * (jnp.exp(cs[t - 1]) if t else 1.0) for t in range(t_s)]
    r_t = [slab(r_ref, t) * jnp.exp(cs[t]) for t in range(t_s)]
    b_raw = [slab(b_ref, t) for t in range(t_s)]
    k_raw = [slab(k_ref, t) for t in range(t_s)]
    g_inv = [jnp.exp(-cs[t]) for t in range(t_s)]
    b_t = [b_raw[t] * g_inv[t] for t in range(t_s)]
    k_t = [k_raw[t] * g_inv[t] for t in range(t_s)]
    for t in range(t_s):
        g_tail = jnp.exp(cs[t_s - 1] - cs[t])
        xa_o[t] = a_t[t]
        xr_o[t] = r_t[t]
        bh_o[t] = b_raw[t] * g_tail
        kh_o[t] = k_raw[t] * g_tail
        vt_o[t] = slab(v_ref, t)
    g4_o[...] = jnp.exp(cs[t_s - 1])
    head_sum = lambda x: jnp.sum(x.reshape(N_HEADS, HEAD, n_s), axis=1)
    left = {"ab": a_t, "ak": a_t, "rb": r_t, "rk": r_t}
    right = {"ab": b_t, "ak": k_t, "rb": b_t, "rk": k_t}
    for (kind, t, s), row in _short_pairs(t_s).items():
        as_o[row] = head_sum(left[kind][t] * right[kind][s])


def _short_state_kernel(s_ref, xa_ref, xr_ref, bh_ref, kh_ref, vt_ref, g4_ref, as_ref, *refs, t_s):
    out_ins, (so_ref, o_ref, y_ref) = refs[:-3], refs[-3:]
    idx = _short_pairs(t_s)
    head = pl.program_id(0)
    scal = {key: as_ref[row, pl.ds(head, 1), :] for key, row in idx.items()}

    def value_row(v, carry):
        s_v = s_ref[0, v]
        vv = [vt_ref[t, pl.ds(v, 1), :] for t in range(t_s)]
        u = []
        for t in range(t_s):
            acc = jnp.sum(s_v * xa_ref[t], axis=0, keepdims=True)
            for s in range(t):
                acc = acc + scal[("ab", t, s)] * u[s] + scal[("ak", t, s)] * vv[s]
            u.append(acc)
        for t in range(t_s):
            acc = jnp.sum(s_v * xr_ref[t], axis=0, keepdims=True)
            for s in range(t + 1):
                acc = acc + scal[("rb", t, s)] * u[s] + scal[("rk", t, s)] * vv[s]
            y_ref[t, pl.ds(head * HEAD + v, 1), :] = acc
        new = s_v * g4_ref[...]
        for s in range(t_s):
            new = new + u[s] * bh_ref[s] + vv[s] * kh_ref[s]
        so_ref[0, v] = new
        return carry

    lax.fori_loop(0, HEAD, value_row, 0, unroll=8)

    @pl.when(head == pl.num_programs(0) - 1)
    def _():
        _out_kernel(y_ref, *out_ins, o_ref)


def _short_prep_kernel(x_ref, sh_ref, ph_ref, nw_ref, win_ref, *refs, t_s, n_s, t0):
    prep_rest, refs = refs[:N_PREP_PARAMS - 2], refs[N_PREP_PARAMS - 2:]
    bon_o, sg_o, op_o, nsh_o, npl_o = refs[:5]
    scale_outs, seq_s, (p_ext, u_ext) = refs[5:12], refs[12:18], refs[18:]
    h, rstd = _norm_operands(x_ref, nw_ref)
    for finish in _prep_stages(h, rstd, win_ref, pl.program_id(0), sh_ref, ph_ref, *prep_rest, *seq_s,
                               bon_o, sg_o, op_o, nsh_o, npl_o, p_ext, u_ext,
                               stride=n_s, t0=t0, hs=sh_ref.shape[1]):
        finish()
    _short_scale_kernel(*seq_s, *scale_outs, t_s=t_s, n_s=n_s)


def _short_prep_call(x, sh_hist, pool_hist, params, *, t_s, n_s, t0):
    assert n_s % LANES == 0 and len(params) == N_PREP_PARAMS
    rows = t_s * n_s
    hs, ph = sh_hist.shape[1], _pool_hist_rows(n_s)
    assert pool_hist.shape[1] == ph
    n_scal = len(_short_pairs(t_s))
    full = lambda shape: pl.BlockSpec(shape, lambda i: tuple(0 for _ in shape))
    vec = (t_s, D_RWKV, n_s)
    out_shapes = ([(rows, D_RWKV)] * 3 + [(1, hs, D_SHIFT), (1, ph, D_POOL)] + [vec] * 5
                  + [(D_RWKV, n_s), (n_scal, N_HEADS, n_s)])
    return pl.pallas_call(
        functools.partial(_short_prep_kernel, t_s=t_s, n_s=n_s, t0=t0),
        grid=(1,),
        in_specs=[full(a.shape) for a in (x, sh_hist, pool_hist) + tuple(params)],
        out_specs=[full(s) for s in out_shapes],
        out_shape=[jax.ShapeDtypeStruct(s, F32) for s in out_shapes],
        scratch_shapes=[pltpu.VMEM((rows, D_RWKV), F32)] * 6 + [pltpu.VMEM((hs + rows, D_SHIFT), F32),
                                                                pltpu.VMEM((ph + rows, D_POOL), F32)],
        compiler_params=pltpu.CompilerParams(
            dimension_semantics=("arbitrary",), vmem_limit_bytes=VMEM_LIMIT),
        name="prep_short",
    )(x, sh_hist, pool_hist, *params)


def _short_state_call(s0, xa, xr, bh, kh, vt, g4, scal, bon, sg, op, x, out_params):
    t_s, _, n_s = xa.shape
    assert len(out_params) == N_OUT_PARAMS and x.shape[0] == t_s * n_s
    st = pl.BlockSpec((1, HEAD, HEAD, n_s), lambda h: (h, 0, 0, 0))
    per_head = pl.BlockSpec((t_s, HEAD, n_s), lambda h: (0, h, 0))
    whole = lambda a: pl.BlockSpec(a.shape, lambda h: (0,) * a.ndim)
    out_ins = (bon, sg, op, x) + tuple(out_params)
    return pl.pallas_call(
        functools.partial(_short_state_kernel, t_s=t_s),
        grid=(N_HEADS,),
        in_specs=[st] + [per_head] * 5 + [pl.BlockSpec((HEAD, n_s), lambda h: (h, 0)), whole(scal)]
        + [whole(a) for a in out_ins],
        out_specs=[st, whole(x)],
        out_shape=[jax.ShapeDtypeStruct(s0.shape, F32), jax.ShapeDtypeStruct(x.shape, F32)],
        scratch_shapes=[pltpu.VMEM(xa.shape, F32)],
        compiler_params=pltpu.CompilerParams(
            dimension_semantics=("arbitrary",), vmem_limit_bytes=VMEM_LIMIT),
        name="layer_short_state",
    )(s0, xa, xr, bh, kh, vt, g4, scal, *out_ins)


def _gated_rwkv(y, bon, sg, gw_ref, gb_ref, g_ref):
    g_ones = g_ref[...]
    mu = _seg_sum(y, g_ones) * (1.0 / HEAD)
    d = y - mu
    var = _seg_sum(d * d, g_ones) * (1.0 / HEAD)
    yn = d * lax.rsqrt(var + GN_EPS) * gw_ref[...] + gb_ref[...]
    return (yn + bon) * sg


def _out_project(o_rwkv, o_pool, x, wo_ref, nf_ref):
    out = (jnp.dot(o_rwkv.astype(BF16), wo_ref[0:D_RWKV, :], preferred_element_type=F32)
           + jnp.dot(o_pool.astype(BF16), wo_ref[D_RWKV:, :], preferred_element_type=F32))
    res = x + out
    return res * lax.rsqrt(jnp.mean(res * res, axis=-1, keepdims=True) + NORM_EPS) * nf_ref[...]


def _out_kernel(y_ref, bon_ref, sg_ref, op_ref, x_ref, gw_ref, gb_ref, wo_ref, nf_ref, g_ref, o_ref):
    if len(y_ref.shape) == 3:
        y = jnp.concatenate([y_ref[t].T for t in range(y_ref.shape[0])], axis=0)
    else:
        y = y_ref[...]
    o_rwkv = _gated_rwkv(y, bon_ref[...], sg_ref[...], gw_ref, gb_ref, g_ref)
    o_ref[...] = _out_project(o_rwkv, op_ref[...], x_ref[...], wo_ref, nf_ref)


N_PREP_PARAMS = 12
N_WKV_CONSTS = 3
N_OUT_PARAMS = 5


def _fused_kernel(*refs, n_chunks, t0):
    x_ref, nw_ref, win_ref = refs[:3]
    pos = 1 + N_PREP_PARAMS
    prep_rest = refs[3:pos]
    wkv_consts = refs[pos:pos + N_WKV_CONSTS]
    pos += N_WKV_CONSTS
    gw_ref, gb_ref, wo_ref, nf_ref, g_ref = refs[pos:pos + N_OUT_PARAMS]
    pos += N_OUT_PARAMS
    o_ref, nsh_o, npl_o, so_ref = refs[pos:pos + 4]
    pos += 4
    seq_s = refs[pos:pos + 6]
    bon_s, sg_s, op_s, y_s, p_ext, u_ext, z_ref = refs[pos + 6:]
    j = pl.program_id(1)
    _wkv_zero_state(z_ref, j == 0)
    finish_prep = _prep_stages(*_norm_operands(x_ref, nw_ref), win_ref, j, None, None, *prep_rest,
                               *seq_s, bon_s, sg_s, op_s, nsh_o, npl_o, p_ext, u_ext,
                               stride=1, t0=t0, hs=nsh_o.shape[1])
    _wkv_tile(*seq_s, *wkv_consts, y_s, z_ref, n_chunks=n_chunks, fillers=finish_prep)
    _out_kernel(y_s, bon_s, sg_s, op_s, x_ref, gw_ref, gb_ref, wo_ref, nf_ref, g_ref, o_ref)
    _wkv_store_state(z_ref, so_ref, j == pl.num_programs(1) - 1)


def _fused_call(x, prep_params, out_params, *, n_seq, n_chunks, t0):
    assert len(prep_params) == N_PREP_PARAMS and len(out_params) == N_OUT_PARAMS
    total = x.shape[0]
    rows = n_chunks * CHUNK
    tiles = total // (n_seq * rows)
    hs, ph = SUBLANES, _pool_hist_rows(1)
    assert CHUNK == HEAD and 2 * CHUNK == LANES
    wkv_consts = _wkv_masks()
    act = pl.BlockSpec((rows, D_MODEL), lambda b, j: (b * tiles + j, 0))
    per_seq = lambda shape: pl.BlockSpec((1,) + shape[1:], lambda b, j: (b,) + (0,) * (len(shape) - 1))
    const = lambda a: pl.BlockSpec(a.shape, lambda b, j: (0,) * a.ndim, pipeline_mode=pl.Buffered(1))
    consts = tuple(prep_params) + tuple(wkv_consts) + tuple(out_params)
    seq_shapes = [(n_seq, hs, D_SHIFT), (n_seq, ph, D_POOL), (n_seq, N_HEADS, HEAD, HEAD)]
    seq_scratch = pltpu.VMEM((rows, D_RWKV), F32)
    return pl.pallas_call(
        functools.partial(_fused_kernel, n_chunks=n_chunks, t0=t0),
        grid=(n_seq, tiles),
        in_specs=[act] + [const(a) for a in consts],
        out_specs=[act] + [per_seq(s) for s in seq_shapes],
        out_shape=[jax.ShapeDtypeStruct((total, D_MODEL), F32)]
        + [jax.ShapeDtypeStruct(s, F32) for s in seq_shapes],
        scratch_shapes=[seq_scratch] * 10 + [pltpu.VMEM((hs + rows, D_SHIFT), F32),
                                             pltpu.VMEM((ph + rows, D_POOL), F32),
                                             pltpu.VMEM((N_PAIRS, LANES, LANES), F32)],
        compiler_params=pltpu.CompilerParams(
            dimension_semantics=("arbitrary", "arbitrary"), vmem_limit_bytes=VMEM_LIMIT),
        name="layer_long",
    )(x, *consts)


def _head_ones():
    i, j = np.indices((MXU_DIM, MXU_DIM))
    return jnp.asarray((i // HEAD) == (j // HEAD), BF16)


def kernel(x_prompt, x_sample, state_shift, state_wkv, state_pool, norm_w, w_in, mu_shift, w_decay_b,
           w0, w_aaa_b, a0, k_k, k_a, r_k, gn_w, gn_b, pool_w, pool_scale, w_out, norm_f):
    depth = norm_w.shape[0]
    n_p, t_p, _ = x_prompt.shape
    n_s, t_s, _ = x_sample.shape
    assert depth == 1, "a stacked trunk needs a residual-only output kernel between layers"
    assert t_p % (WKV_CHUNKS_PER_STEP * CHUNK) == 0
    g_ones = _head_ones()
    row = lambda z: z.reshape(1, -1).astype(F32)

    hp = x_prompt.astype(F32).reshape(n_p * t_p, D_MODEL)
    hs = jnp.transpose(x_sample.astype(F32), (1, 0, 2)).reshape(t_s * n_s, D_MODEL)
    outs = [[] for _ in range(6)]
    for l in range(depth):
        zl = jnp.zeros((D_LORA, D_RWKV), F32)
        w_lora = jnp.concatenate([jnp.concatenate([w_decay_b[l], zl], axis=1),
                                  jnp.concatenate([zl, w_aaa_b[l]], axis=1)], axis=0).astype(BF16)
        prep_params = (row(norm_w[l]), w_in[l].astype(BF16), row(mu_shift[l]), w_lora, row(w0[l]),
                       row(a0[l]), row(k_k[l]), row(k_a[l]), row(r_k[l]), pool_w[l].astype(BF16),
                       row(pool_scale[l]), g_ones)
        out_params = (row(gn_w[l]), row(gn_b[l]), w_out[l].astype(BF16), row(norm_f), g_ones)

        hp, nsh, npl, s_fin = _fused_call(hp, prep_params, out_params,
                                          n_seq=n_p, n_chunks=WKV_CHUNKS_PER_STEP, t0=0)
        outs[0].append(nsh[:, SUBLANES - 1])
        outs[1].append(s_fin)
        outs[2].append(npl[:, -POOL_KEEP:])

        sh_hist = state_shift[l].astype(F32)[None]
        pool_hist = jnp.transpose(state_pool[l].astype(F32), (1, 0, 2)).reshape(1, POOL_KEEP * n_s, D_POOL)
        bon, sg, op, nsh, npl, *scan_ops = _short_prep_call(hs, sh_hist, pool_hist, prep_params,
                                                            t_s=t_s, n_s=n_s, t0=PAST_LEN)
        s_fin, hs = _short_state_call(jnp.transpose(state_wkv[l].astype(F32), (1, 2, 3, 0)), *scan_ops,
                                      bon, sg, op, hs, out_params)
        s_fin = jnp.transpose(s_fin, (3, 0, 1, 2))
        outs[3].append(nsh[0])
        outs[4].append(s_fin)
        outs[5].append(jnp.transpose(npl[0].reshape(POOL_KEEP, n_s, D_POOL), (1, 0, 2)))

    y_prompt = hp.reshape(n_p, t_p, D_MODEL).astype(x_prompt.dtype)
    y_sample = jnp.transpose(hs.reshape(t_s, n_s, D_MODEL), (1, 0, 2)).astype(x_sample.dtype)
    return (y_prompt, y_sample) + tuple(jnp.stack(o, axis=0) for o in outs)
```

```python
import functools

import jax
import jax.numpy as jnp
import numpy as np
from jax import lax
from jax.experimental import pallas as pl
from jax.experimental.pallas import tpu as pltpu

F32 = jnp.float32
BF16 = jnp.bfloat16

D_MODEL = 1024
D_RWKV = 512
D_POOL = 512
HEAD = 64
N_HEADS = D_RWKV // HEAD
LANES = 128
SUBLANES = 8
MXU_DIM = 256
N_PAIRS = D_RWKV // LANES
D_LORA = 64
POOL_WINDOWS = (2, 4, 8, 16)
POOL_GROUP = D_POOL // len(POOL_WINDOWS)
POOL_KEEP = max(POOL_WINDOWS) - 1
D_SHIFT = 3 * D_RWKV + 2 * D_LORA
D_IN = D_SHIFT + D_RWKV + 2 * D_POOL
PAST_LEN = 16384
NORM_EPS = 1e-6
GN_EPS = 64e-5
L2_EPS = 1e-12
CHUNK = 64
VMEM_LIMIT = 56 * 1024 * 1024
P_A = (1, 1)
P_T = (1, 1)
P_U = (1, 1)
P_S = (1, 1)
P_Y = (1, 1)
P_CUMSUM = 2
WKV_CHUNKS_PER_STEP = 8
WKV_GROUP_CHUNKS = 4

NN = (((1,), (0,)), ((), ()))
NT = (((1,), (1,)), ((), ()))
TN = (((0,), (0,)), ((), ()))


def _pieces(x, n):
    if x.dtype == BF16:
        return [x]
    out = []
    rem = x
    for i in range(n):
        p = rem.astype(BF16)
        out.append(p)
        if i + 1 < n:
            rem = rem - p.astype(F32)
    return out


def _mm(a, b, dims=NN, na=1, nb=1):
    ap = _pieces(a, na)
    bp = _pieces(b, nb)
    order = max(len(ap), len(bp))
    acc = None
    for i, x in enumerate(ap):
        for j, y in enumerate(bp):
            if i + j >= order:
                continue
            t = lax.dot_general(x, y, dims, preferred_element_type=F32)
            acc = t if acc is None else acc + t
    return acc


def _seg_sum(x, g_ones, pieces=1):
    cols = []
    for s in range(x.shape[1] // MXU_DIM):
        cols.append(_mm(x[:, s * MXU_DIM:(s + 1) * MXU_DIM], g_ones, NN, pieces, 1))
    return cols[0] if len(cols) == 1 else jnp.concatenate(cols, axis=1)


def _sigmoid(x):
    return 1.0 / (1.0 + jnp.exp(-x))


def _pool_hist_rows(stride):
    return -(-POOL_KEEP * stride // SUBLANES) * SUBLANES


def _norm_operands(x_ref, nw_ref):
    x = x_ref[...]
    rstd = lax.rsqrt(jnp.mean(x * x, axis=-1, keepdims=True) + NORM_EPS)
    return (x * nw_ref[...]).astype(BF16), rstd


def _prep_stages(h, rstd, win_ref, j, sh_ref, ph_ref, mu_ref, wl_ref, w0_ref, a0_ref, kk_ref,
                 ka_ref, rk_ref, pw_ref, ps_ref, g_ref,
                 r_o, k_o, v_o, lw_o, a_o, b_o, bon_o, sg_o, op_o, nsh_o, npl_o,
                 p_ext, u_ext, *, stride, t0, hs):
    rows = h.shape[0]
    ph = _pool_hist_rows(stride)
    width = MXU_DIM

    @pl.when(j == 0)
    def _():
        p_ext[0:hs, :] = jnp.zeros((hs, D_SHIFT), F32) if sh_ref is None else sh_ref[0]
        u_ext[0:ph, :] = jnp.zeros((ph, D_POOL), F32) if ph_ref is None else ph_ref[0]

    def proj_cols(c0, c1):
        return jnp.dot(h, win_ref[:, c0:c1], preferred_element_type=F32) * rstd

    def shifted(c0, c1):
        p = proj_cols(c0, c1)
        p_ext[hs:hs + rows, c0:c1] = p
        prev = p_ext[pl.ds(hs - stride, rows), c0:c1]
        return p + mu_ref[:, c0:c1] * (prev - p)

    xwa = shifted(3 * D_RWKV, D_SHIFT)
    for c in range(0, D_RWKV, width):
        r_o[:, c:c + width] = shifted(c, c + width)
    lane = lax.broadcasted_iota(jnp.int32, xwa.shape, 1)
    lora_in = jnp.where(lane < D_LORA, jnp.tanh(xwa), xwa)
    lora = jnp.dot(lora_in.astype(BF16), wl_ref[...], preferred_element_type=F32)
    lw_o[...] = (-0.6065306597126334) * _sigmoid(w0_ref[...] + lora[:, :D_RWKV])
    alpha = _sigmoid(a0_ref[...] + lora[:, D_RWKV:])
    for c in range(0, D_RWKV, width):
        sl = slice(c, c + width)
        k = shifted(D_RWKV + c, D_RWKV + c + width)
        k_o[:, sl] = k * (1.0 + (alpha[:, sl] - 1.0) * ka_ref[:, sl])
        kk = k * kk_ref[:, sl]
        kk = kk * lax.rsqrt(_seg_sum(kk * kk, g_ref[...]) + L2_EPS)
        a_o[:, sl] = -kk
        b_o[:, sl] = kk * alpha[:, sl]
        v_o[:, sl] = shifted(2 * D_RWKV + c, 2 * D_RWKV + c + width)
    last_p = p_ext[rows:rows + hs, :]
    p_ext[0:hs, :] = last_p
    nsh_o[0] = last_p
    for c in range(0, D_RWKV, width):
        g_rwkv = proj_cols(D_SHIFT + c, D_SHIFT + c + width)
        sg_o[:, c:c + width] = g_rwkv * _sigmoid(g_rwkv)
    for c in range(0, D_POOL, width):
        u_ext[ph:ph + rows, c:c + width] = proj_cols(D_SHIFT + D_RWKV + c, D_SHIFT + D_RWKV + c + width)
        op_o[:, c:c + width] = proj_cols(D_SHIFT + D_RWKV + D_POOL + c,
                                         D_SHIFT + D_RWKV + D_POOL + c + width)

    def bonus():
        bon_o[...] = _seg_sum(r_o[...] * k_o[...] * rk_ref[...], g_ref[...]) * v_o[...]

    def pool(gi, win):
        sl = slice(gi * POOL_GROUP, (gi + 1) * POOL_GROUP)
        row = lax.broadcasted_iota(jnp.int32, (rows, 1), 0)
        pos = t0 + (j * rows + row) // stride
        u = u_ext[ph:ph + rows, sl]
        tot = u
        for back in range(1, win):
            tot = tot + u_ext[pl.ds(ph - back * stride, rows), sl]
        inv_cnt = 1.0 / jnp.minimum(pos + 1, win).astype(F32)
        d = tot * inv_cnt - u
        o = jnp.dot(d.astype(BF16), pw_ref[gi], preferred_element_type=F32)
        gp = op_o[:, sl]
        op_o[:, sl] = o * ps_ref[:, sl] * (gp * _sigmoid(gp))

    def carry_pool():
        last_u = u_ext[rows:rows + ph, :]
        u_ext[0:ph, :] = last_u
        npl_o[0] = last_u

    pools = [functools.partial(pool, gi, win) for gi, win in enumerate(POOL_WINDOWS)]
    return [bonus] + pools + [carry_pool]


def _stack(x, m0):
    return jnp.concatenate([jnp.where(m0, x, 0.0), jnp.where(m0, 0.0, x)], axis=0)


def _wkv_zero_state(z_ref, is_first):
    @pl.when(is_first)
    def _():
        z_ref[...] = jnp.zeros(z_ref.shape, F32)


def _wkv_store_state(z_ref, so_ref, is_last):
    @pl.when(is_last)
    def _():
        for p in range(N_PAIRS):
            s2 = z_ref[p].T
            so_ref[0, 2 * p] = s2[:HEAD, :HEAD]
            so_ref[0, 2 * p + 1] = s2[HEAD:, HEAD:]


def _wkv_tile(r_ref, k_ref, v_ref, lw_ref, a_ref, b_ref, tri_ref, mss_ref, msq_ref, y_ref, z_ref, *,
              n_chunks, fillers=()):
    fillers = list(fillers)
    fill = lambda: fillers.pop(0)() if fillers else None
    lane = lax.broadcasted_iota(jnp.int32, (CHUNK, LANES), 1)
    m0 = lane < HEAD
    strict = mss_ref[0][:, :LANES] > 0.5
    incl2 = mss_ref[1] > 0.5
    n_levels = mss_ref.shape[0] - 3
    lane_head4 = lax.broadcasted_iota(jnp.int32, (CHUNK, 2 * LANES), 1) // HEAD
    eye = msq_ref[0]
    same_head = msq_ref[1] > 0.5
    zblk = jnp.zeros((CHUNK, LANES), BF16)
    zblk2 = jnp.zeros((2 * CHUNK, LANES), BF16)
    bf = lambda x: x.astype(BF16)
    lanes = lambda p: slice(p * LANES, (p + 1) * LANES)
    each = lambda fn, *lists: [fn(*args) for args in zip(*lists)]
    cat0 = lambda *xs: jnp.concatenate(xs, axis=0)
    cat1 = lambda *xs: jnp.concatenate(xs, axis=1)

    stack4 = lambda x: cat0(*[jnp.where(lane_head4 == h, x, 0.0) for h in range(4)])
    join = lambda xs: [cat1(xs[i], xs[i + 1]) for i in range(0, len(xs), 2)]
    split = lambda xs: [x[:, s] for x in xs for s in (slice(0, LANES), slice(LANES, 2 * LANES))]

    def transitions(chunk_ids):
        scaled = {}
        for ci in chunk_ids:
            rows = slice(ci * CHUNK, (ci + 1) * CHUNK)
            lw = lw_ref[rows, :]
            cs = _mm(tri_ref[...], lw, NN, 1, P_CUMSUM)
            cs_end = cs[CHUNK - 1:CHUNK, :]
            g_inv = jnp.exp(-cs)
            g_tail = jnp.exp(cs_end - cs)
            r_t = r_ref[rows, :] * jnp.exp(cs)
            scaled[ci] = dict(
                g_end=jnp.exp(cs_end), r_f32=r_t, r_t=bf(r_t),
                a_t=bf(a_ref[rows, :] * jnp.exp(cs - lw)),
                b_t=bf(b_ref[rows, :] * g_inv), k_t=bf(k_ref[rows, :] * g_inv),
                b_h=bf(b_ref[rows, :] * g_tail), k_h=bf(k_ref[rows, :] * g_tail), v=bf(v_ref[rows, :]))
        chains = [(ci, p) for ci in chunk_ids for p in range(N_PAIRS)]
        pair = lambda name: [scaled[ci][name][:, lanes(p)] for ci, p in chains]
        at, rt, bt, kt, bh, kh, v, r_f32 = (pair(n) for n in ("a_t", "r_t", "b_t", "k_t", "b_h", "k_h", "v",
                                                              "r_f32"))
        v2 = each(lambda x: _stack(x, m0), v)
        yield
        scores = each(lambda a, r, b, k: _mm(cat0(a, r), cat0(_stack(b, m0), _stack(k, m0)), NT, *P_A),
                      at, rt, bt, kt)
        a_ab = each(lambda s: jnp.where(strict, s[:CHUNK, :LANES], 0.0), scores)
        a_ak = each(lambda s: jnp.where(strict, s[:CHUNK, LANES:], 0.0), scores)
        a_r = each(lambda s: jnp.where(incl2, s[CHUNK:, :], 0.0), scores)
        akv = split(each(lambda x, y: _mm(x, stack4(y), NN, *P_U), join(a_ak), join(v)))
        yield
        a_ab4 = join(a_ab)
        t_inv4 = each(lambda x: mss_ref[2] + x * mss_ref[3], a_ab4)
        for lvl in range(1, n_levels):
            t_b = each(bf, t_inv4)
            inner = each(lambda x, t: _mm(x * mss_ref[3 + lvl], stack4(t), NN, *P_T), a_ab4, t_b)
            t_inv4 = each(lambda t, tb, w: t + _mm(tb, stack4(bf(w)), NN, *P_T), t_inv4, t_b, inner)
            yield
        t_inv = split(each(bf, t_inv4))
        pq = each(lambda t, x, y: _mm(t, cat1(_stack(x, m0), _stack(bf(y), m0)), NN, *P_U), t_inv, at, akv)
        pq_b = each(bf, pq)
        yield
        bpq = each(lambda b, k, x, y: _mm(cat0(b, k), cat0(x, cat1(zblk, y)), TN, *P_S), bh, kh, pq_b, v)
        g_end = [scaled[ci]["g_end"][:, lanes(p)] for ci, p in chains]
        m_t = each(lambda g, x: eye * g + jnp.where(same_head, x[:, :LANES], 0.0), g_end, bpq)
        n_t = each(lambda x: jnp.where(same_head, x[:, LANES:], 0.0), bpq)
        yield
        yy = each(lambda x, y, z: _mm(x, cat0(cat1(_stack(y[:, :LANES], m0), _stack(y[:, LANES:], m0)),
                                             cat1(zblk2, z)), NN, *P_Y), a_r, pq_b, v2)
        y1 = each(lambda r, x: r + x[:, :LANES], r_f32, yy)
        y2 = each(lambda x: x[:, LANES:], yy)
        return m_t, n_t, y1, y2

    def advance(gen, n_stages, done):
        for _ in range(n_stages):
            if not done:
                try:
                    next(gen)
                except StopIteration as stop:
                    done.append(stop.value)

    groups = [range(g, g + WKV_GROUP_CHUNKS) for g in range(0, n_chunks, WKV_GROUP_CHUNKS)]
    stages_per_step = -(-(n_levels + 4) // WKV_GROUP_CHUNKS)
    fills_per_step = -(-len(fillers) // WKV_GROUP_CHUNKS)
    ready = []
    advance(transitions(groups[0]), n_levels + 5, ready)
    state = [z_ref[p] for p in range(N_PAIRS)]
    for gi, group in enumerate(groups):
        m_t, n_t, y1, y2 = ready[0]
        ready = []
        upcoming = transitions(groups[gi + 1]) if gi + 1 < len(groups) else None
        for k, ci in enumerate(group):
            for p in range(N_PAIRS):
                i = k * N_PAIRS + p
                both = _mm(cat0(m_t[i], y1[i]), state[p], NN, *P_S)
                state[p] = both[:LANES] + n_t[i]
                y_ref[ci * CHUNK:(ci + 1) * CHUNK, lanes(p)] = both[LANES:] + y2[i]
            if upcoming is not None:
                advance(upcoming, stages_per_step, ready)
            else:
                for _ in range(fills_per_step):
                    fill()
        if upcoming is not None:
            advance(upcoming, n_levels + 5, ready)
    for p in range(N_PAIRS):
        z_ref[p] = state[p]
    while fillers:
        fill()


def _wkv_masks():
    ri, ci = np.indices((CHUNK, 4 * CHUNK))
    ci = ci % CHUNK
    side = [ci < ri, ci <= ri, ri == ci]
    m = 1
    while m < CHUNK:
        side.append(((ri // (2 * m)) == (ci // (2 * m))) & ((ri // m) % 2 == 1) & ((ci // m) % 2 == 0))
        m *= 2
    rq, cq = np.indices((LANES, LANES))
    square = [rq == cq, (rq // HEAD) == (cq // HEAD)]
    rt, ct = np.indices((CHUNK, CHUNK))
    return (jnp.asarray(ct <= rt, BF16), jnp.asarray(np.stack(side), F32),
            jnp.asarray(np.stack(square), F32))


def _short_pairs(t_s):
    idx = {}
    for kind, inclusive in (("ab", False), ("ak", False), ("rb", True), ("rk", True)):
        for t in range(t_s):
            for s in range(t + 1 if inclusive else t):
                idx[(kind, t, s)] = len(idx)
    return idx


def _short_scale_kernel(r_ref, k_ref, v_ref, lw_ref, a_ref, b_ref,
                        xa_o, xr_o, bh_o, kh_o, vt_o, g4_o, as_o, *, t_s, n_s):
    slab = lambda ref, t: ref[t * n_s:(t + 1) * n_s, :].T
    cs = []
    for t in range(t_s):
        cs.append(slab(lw_ref, t) if t == 0 else cs[-1] + slab(lw_ref, t))
    a_t = [slab(a_ref, t) * (jnp.exp(cs[t - 1]) if t else 1.0) for t in range(t_s)]
    r_t = [slab(r_ref, t) * jnp.exp(cs[t]) for t in range(t_s)]
    b_raw = [slab(b_ref, t) for t in range(t_s)]
    k_raw = [slab(k_ref, t) for t in range(t_s)]
    g_inv = [jnp.exp(-cs[t]) for t in range(t_s)]
    b_t = [b_raw[t] * g_inv[t] for t in range(t_s)]
    k_t = [k_raw[t] * g_inv[t] for t in range(t_s)]
    for t in range(t_s):
        g_tail = jnp.exp(cs[t_s - 1] - cs[t])
        xa_o[t] = a_t[t]
        xr_o[t] = r_t[t]
        bh_o[t] = b_raw[t] * g_tail
        kh_o[t] = k_raw[t] * g_tail
        vt_o[t] = slab(v_ref, t)
    g4_o[...] = jnp.exp(cs[t_s - 1])
    head_sum = lambda x: jnp.sum(x.reshape(N_HEADS, HEAD, n_s), axis=1)
    left = {"ab": a_t, "ak": a_t, "rb": r_t, "rk": r_t}
    right = {"ab": b_t, "ak": k_t, "rb": b_t, "rk": k_t}
    for (kind, t, s), row in _short_pairs(t_s).items():
        as_o[row] = head_sum(left[kind][t] * right[kind][s])


def _short_state_kernel(s_ref, xa_ref, xr_ref, bh_ref, kh_ref, vt_ref, g4_ref, as_ref, *refs, t_s):
    out_ins, (so_ref, o_ref, y_ref) = refs[:-3], refs[-3:]
    idx = _short_pairs(t_s)
    head = pl.program_id(0)
    scal = {key: as_ref[row, pl.ds(head, 1), :] for key, row in idx.items()}

    def value_row(v, carry):
        s_v = s_ref[0, v]
        vv = [vt_ref[t, pl.ds(v, 1), :] for t in range(t_s)]
        u = []
        for t in range(t_s):
            acc = jnp.sum(s_v * xa_ref[t], axis=0, keepdims=True)
            for s in range(t):
                acc = acc + scal[("ab", t, s)] * u[s] + scal[("ak", t, s)] * vv[s]
            u.append(acc)
        for t in range(t_s):
            acc = jnp.sum(s_v * xr_ref[t], axis=0, keepdims=True)
            for s in range(t + 1):
                acc = acc + scal[("rb", t, s)] * u[s] + scal[("rk", t, s)] * vv[s]
            y_ref[t, pl.ds(head * HEAD + v, 1), :] = acc
        new = s_v * g4_ref[...]
        for s in range(t_s):
            new = new + u[s] * bh_ref[s] + vv[s] * kh_ref[s]
        so_ref[0, v] = new
        return carry

    lax.fori_loop(0, HEAD, value_row, 0, unroll=8)

    @pl.when(head == pl.num_programs(0) - 1)
    def _():
        _out_kernel(y_ref, *out_ins, o_ref)


def _short_prep_kernel(x_ref, sh_ref, ph_ref, nw_ref, win_ref, *refs, t_s, n_s, t0):
    prep_rest, refs = refs[:N_PREP_PARAMS - 2], refs[N_PREP_PARAMS - 2:]
    bon_o, sg_o, op_o, nsh_o, npl_o = refs[:5]
    scale_outs, seq_s, (p_ext, u_ext) = refs[5:12], refs[12:18], refs[18:]
    h, rstd = _norm_operands(x_ref, nw_ref)
    for finish in _prep_stages(h, rstd, win_ref, pl.program_id(0), sh_ref, ph_ref, *prep_rest, *seq_s,
                               bon_o, sg_o, op_o, nsh_o, npl_o, p_ext, u_ext,
                               stride=n_s, t0=t0, hs=sh_ref.shape[1]):
        finish()
    _short_scale_kernel(*seq_s, *scale_outs, t_s=t_s, n_s=n_s)


def _short_prep_call(x, sh_hist, pool_hist, params, *, t_s, n_s, t0):
    assert n_s % LANES == 0 and len(params) == N_PREP_PARAMS
    rows = t_s * n_s
    hs, ph = sh_hist.shape[1], _pool_hist_rows(n_s)
    assert pool_hist.shape[1] == ph
    n_scal = len(_short_pairs(t_s))
    full = lambda shape: pl.BlockSpec(shape, lambda i: tuple(0 for _ in shape))
    vec = (t_s, D_RWKV, n_s)
    out_shapes = ([(rows, D_RWKV)] * 3 + [(1, hs, D_SHIFT), (1, ph, D_POOL)] + [vec] * 5
                  + [(D_RWKV, n_s), (n_scal, N_HEADS, n_s)])
    return pl.pallas_call(
        functools.partial(_short_prep_kernel, t_s=t_s, n_s=n_s, t0=t0),
        grid=(1,),
        in_specs=[full(a.shape) for a in (x, sh_hist, pool_hist) + tuple(params)],
        out_specs=[full(s) for s in out_shapes],
        out_shape=[jax.ShapeDtypeStruct(s, F32) for s in out_shapes],
        scratch_shapes=[pltpu.VMEM((rows, D_RWKV), F32)] * 6 + [pltpu.VMEM((hs + rows, D_SHIFT), F32),
                                                                pltpu.VMEM((ph + rows, D_POOL), F32)],
        compiler_params=pltpu.CompilerParams(
            dimension_semantics=("arbitrary",), vmem_limit_bytes=VMEM_LIMIT),
        name="prep_short",
    )(x, sh_hist, pool_hist, *params)


def _short_state_call(s0, xa, xr, bh, kh, vt, g4, scal, bon, sg, op, x, out_params):
    t_s, _, n_s = xa.shape
    assert len(out_params) == N_OUT_PARAMS and x.shape[0] == t_s * n_s
    st = pl.BlockSpec((1, HEAD, HEAD, n_s), lambda h: (h, 0, 0, 0))
    per_head = pl.BlockSpec((t_s, HEAD, n_s), lambda h: (0, h, 0))
    whole = lambda a: pl.BlockSpec(a.shape, lambda h: (0,) * a.ndim)
    out_ins = (bon, sg, op, x) + tuple(out_params)
    return pl.pallas_call(
        functools.partial(_short_state_kernel, t_s=t_s),
        grid=(N_HEADS,),
        in_specs=[st] + [per_head] * 5 + [pl.BlockSpec((HEAD, n_s), lambda h: (h, 0)), whole(scal)]
        + [whole(a) for a in out_ins],
        out_specs=[st, whole(x)],
        out_shape=[jax.ShapeDtypeStruct(s0.shape, F32), jax.ShapeDtypeStruct(x.shape, F32)],
        scratch_shapes=[pltpu.VMEM(xa.shape, F32)],
        compiler_params=pltpu.CompilerParams(
            dimension_semantics=("arbitrary",), vmem_limit_bytes=VMEM_LIMIT),
        name="layer_short_state",
    )(s0, xa, xr, bh, kh, vt, g4, scal, *out_ins)


def _gated_rwkv(y, bon, sg, gw_ref, gb_ref, g_ref):
    g_ones = g_ref[...]
    mu = _seg_sum(y, g_ones) * (1.0 / HEAD)
    d = y - mu
    var = _seg_sum(d * d, g_ones) * (1.0 / HEAD)
    yn = d * lax.rsqrt(var + GN_EPS) * gw_ref[...] + gb_ref[...]
    return (yn + bon) * sg


def _out_project(o_rwkv, o_pool, x, wo_ref, nf_ref):
    out = (jnp.dot(o_rwkv.astype(BF16), wo_ref[0:D_RWKV, :], preferred_element_type=F32)
           + jnp.dot(o_pool.astype(BF16), wo_ref[D_RWKV:, :], preferred_element_type=F32))
    res = x + out
    return res * lax.rsqrt(jnp.mean(res * res, axis=-1, keepdims=True) + NORM_EPS) * nf_ref[...]


def _out_kernel(y_ref, bon_ref, sg_ref, op_ref, x_ref, gw_ref, gb_ref, wo_ref, nf_ref, g_ref, o_ref):
    if len(y_ref.shape) == 3:
        y = jnp.concatenate([y_ref[t].T for t in range(y_ref.shape[0])], axis=0)
    else:
        y = y_ref[...]
    o_rwkv = _gated_rwkv(y, bon_ref[...], sg_ref[...], gw_ref, gb_ref, g_ref)
    o_ref[...] = _out_project(o_rwkv, op_ref[...], x_ref[...], wo_ref, nf_ref)


N_PREP_PARAMS = 12
N_WKV_CONSTS = 3
N_OUT_PARAMS = 5


def _fused_kernel(*refs, n_chunks, t0):
    x_ref, nw_ref, win_ref = refs[:3]
    pos = 1 + N_PREP_PARAMS
    prep_rest = refs[3:pos]
    wkv_consts = refs[pos:pos + N_WKV_CONSTS]
    pos += N_WKV_CONSTS
    gw_ref, gb_ref, wo_ref, nf_ref, g_ref = refs[pos:pos + N_OUT_PARAMS]
    pos += N_OUT_PARAMS
    o_ref, nsh_o, npl_o, so_ref = refs[pos:pos + 4]
    pos += 4
    seq_s = refs[pos:pos + 6]
    bon_s, sg_s, op_s, y_s, p_ext, u_ext, z_ref = refs[pos + 6:]
    j = pl.program_id(1)
    _wkv_zero_state(z_ref, j == 0)
    finish_prep = _prep_stages(*_norm_operands(x_ref, nw_ref), win_ref, j, None, None, *prep_rest,
                               *seq_s, bon_s, sg_s, op_s, nsh_o, npl_o, p_ext, u_ext,
                               stride=1, t0=t0, hs=nsh_o.shape[1])
    _wkv_tile(*seq_s, *wkv_consts, y_s, z_ref, n_chunks=n_chunks, fillers=finish_prep)
    _out_kernel(y_s, bon_s, sg_s, op_s, x_ref, gw_ref, gb_ref, wo_ref, nf_ref, g_ref, o_ref)
    _wkv_store_state(z_ref, so_ref, j == pl.num_programs(1) - 1)


def _fused_call(x, prep_params, out_params, *, n_seq, n_chunks, t0):
    assert len(prep_params) == N_PREP_PARAMS and len(out_params) == N_OUT_PARAMS
    total = x.shape[0]
    rows = n_chunks * CHUNK
    tiles = total // (n_seq * rows)
    hs, ph = SUBLANES, _pool_hist_rows(1)
    assert CHUNK == HEAD and 2 * CHUNK == LANES
    wkv_consts = _wkv_masks()
    act = pl.BlockSpec((rows, D_MODEL), lambda b, j: (b * tiles + j, 0))
    per_seq = lambda shape: pl.BlockSpec((1,) + shape[1:], lambda b, j: (b,) + (0,) * (len(shape) - 1))
    const = lambda a: pl.BlockSpec(a.shape, lambda b, j: (0,) * a.ndim, pipeline_mode=pl.Buffered(1))
    consts = tuple(prep_params) + tuple(wkv_consts) + tuple(out_params)
    seq_shapes = [(n_seq, hs, D_SHIFT), (n_seq, ph, D_POOL), (n_seq, N_HEADS, HEAD, HEAD)]
    seq_scratch = pltpu.VMEM((rows, D_RWKV), F32)
    return pl.pallas_call(
        functools.partial(_fused_kernel, n_chunks=n_chunks, t0=t0),
        grid=(n_seq, tiles),
        in_specs=[act] + [const(a) for a in consts],
        out_specs=[act] + [per_seq(s) for s in seq_shapes],
        out_shape=[jax.ShapeDtypeStruct((total, D_MODEL), F32)]
        + [jax.ShapeDtypeStruct(s, F32) for s in seq_shapes],
        scratch_shapes=[seq_scratch] * 10 + [pltpu.VMEM((hs + rows, D_SHIFT), F32),
                                             pltpu.VMEM((ph + rows, D_POOL), F32),
                                             pltpu.VMEM((N_PAIRS, LANES, LANES), F32)],
        compiler_params=pltpu.CompilerParams(
            dimension_semantics=("arbitrary", "arbitrary"), vmem_limit_bytes=VMEM_LIMIT),
        name="layer_long",
    )(x, *consts)


def _head_ones():
    i, j = np.indices((MXU_DIM, MXU_DIM))
    return jnp.asarray((i // HEAD) == (j // HEAD), BF16)


def kernel(x_prompt, x_sample, state_shift, state_wkv, state_pool, norm_w, w_in, mu_shift, w_decay_b,
           w0, w_aaa_b, a0, k_k, k_a, r_k, gn_w, gn_b, pool_w, pool_scale, w_out, norm_f):
    depth = norm_w.shape[0]
    n_p, t_p, _ = x_prompt.shape
    n_s, t_s, _ = x_sample.shape
    assert depth == 1, "a stacked trunk needs a residual-only output kernel between layers"
    assert t_p % (WKV_CHUNKS_PER_STEP * CHUNK) == 0
    g_ones = _head_ones()
    row = lambda z: z.reshape(1, -1).astype(F32)

    hp = x_prompt.astype(F32).reshape(n_p * t_p, D_MODEL)
    hs = jnp.transpose(x_sample.astype(F32), (1, 0, 2)).reshape(t_s * n_s, D_MODEL)
    outs = [[] for _ in range(6)]
    for l in range(depth):
        zl = jnp.zeros((D_LORA, D_RWKV), F32)
        w_lora = jnp.concatenate([jnp.concatenate([w_decay_b[l], zl], axis=1),
                                  jnp.concatenate([zl, w_aaa_b[l]], axis=1)], axis=0).astype(BF16)
        prep_params = (row(norm_w[l]), w_in[l].astype(BF16), row(mu_shift[l]), w_lora, row(w0[l]),
                       row(a0[l]), row(k_k[l]), row(k_a[l]), row(r_k[l]), pool_w[l].astype(BF16),
                       row(pool_scale[l]), g_ones)
        out_params = (row(gn_w[l]), row(gn_b[l]), w_out[l].astype(BF16), row(norm_f), g_ones)

        hp, nsh, npl, s_fin = _fused_call(hp, prep_params, out_params,
                                          n_seq=n_p, n_chunks=WKV_CHUNKS_PER_STEP, t0=0)
        outs[0].append(nsh[:, SUBLANES - 1])
        outs[1].append(s_fin)
        outs[2].append(npl[:, -POOL_KEEP:])

        sh_hist = state_shift[l].astype(F32)[None]
        pool_hist = jnp.transpose(state_pool[l].astype(F32), (1, 0, 2)).reshape(1, POOL_KEEP * n_s, D_POOL)
        bon, sg, op, nsh, npl, *scan_ops = _short_prep_call(hs, sh_hist, pool_hist, prep_params,
                                                            t_s=t_s, n_s=n_s, t0=PAST_LEN)
        s_fin, hs = _short_state_call(jnp.transpose(state_wkv[l].astype(F32), (1, 2, 3, 0)), *scan_ops,
                                      bon, sg, op, hs, out_params)
        s_fin = jnp.transpose(s_fin, (3, 0, 1, 2))
        outs[3].append(nsh[0])
        outs[4].append(s_fin)
        outs[5].append(jnp.transpose(npl[0].reshape(POOL_KEEP, n_s, D_POOL), (1, 0, 2)))

    y_prompt = hp.reshape(n_p, t_p, D_MODEL).astype(x_prompt.dtype)
    y_sample = jnp.transpose(hs.reshape(t_s, n_s, D_MODEL), (1, 0, 2)).astype(x_sample.dtype)
    return (y_prompt, y_sample) + tuple(jnp.stack(o, axis=0) for o in outs)
```

```python
import functools

import jax
import jax.numpy as jnp
import numpy as np
from jax import lax
from jax.experimental import pallas as pl
from jax.experimental.pallas import tpu as pltpu

F32 = jnp.float32
BF16 = jnp.bfloat16

D_MODEL = 1024
D_RWKV = 512
D_POOL = 512
HEAD = 64
N_HEADS = D_RWKV // HEAD
LANES = 128
SUBLANES = 8
MXU_DIM = 256
N_PAIRS = D_RWKV // LANES
D_LORA = 64
POOL_WINDOWS = (2, 4, 8, 16)
POOL_GROUP = D_POOL // len(POOL_WINDOWS)
POOL_KEEP = max(POOL_WINDOWS) - 1
D_SHIFT = 3 * D_RWKV + 2 * D_LORA
D_IN = D_SHIFT + D_RWKV + 2 * D_POOL
PAST_LEN = 16384
NORM_EPS = 1e-6
GN_EPS = 64e-5
L2_EPS = 1e-12
CHUNK = 64
VMEM_LIMIT = 56 * 1024 * 1024
P_A = (1, 1)
P_T = (1, 1)
P_U = (1, 1)
P_S = (1, 1)
P_Y = (1, 1)
P_CUMSUM = 2
WKV_CHUNKS_PER_STEP = 8

NN = (((1,), (0,)), ((), ()))
NT = (((1,), (1,)), ((), ()))
TN = (((0,), (0,)), ((), ()))


def _pieces(x, n):
    if x.dtype == BF16:
        return [x]
    out = []
    rem = x
    for i in range(n):
        p = rem.astype(BF16)
        out.append(p)
        if i + 1 < n:
            rem = rem - p.astype(F32)
    return out


def _mm(a, b, dims=NN, na=1, nb=1):
    ap = _pieces(a, na)
    bp = _pieces(b, nb)
    order = max(len(ap), len(bp))
    acc = None
    for i, x in enumerate(ap):
        for j, y in enumerate(bp):
            if i + j >= order:
                continue
            t = lax.dot_general(x, y, dims, preferred_element_type=F32)
            acc = t if acc is None else acc + t
    return acc


def _seg_sum(x, g_ones, pieces=1):
    cols = []
    for s in range(x.shape[1] // MXU_DIM):
        cols.append(_mm(x[:, s * MXU_DIM:(s + 1) * MXU_DIM], g_ones, NN, pieces, 1))
    return cols[0] if len(cols) == 1 else jnp.concatenate(cols, axis=1)


def _sigmoid(x):
    return 1.0 / (1.0 + jnp.exp(-x))


def _pool_hist_rows(stride):
    return -(-POOL_KEEP * stride // SUBLANES) * SUBLANES


def _norm_operands(x_ref, nw_ref):
    x = x_ref[...]
    rstd = lax.rsqrt(jnp.mean(x * x, axis=-1, keepdims=True) + NORM_EPS)
    return (x * nw_ref[...]).astype(BF16), rstd


def _prep_stages(h, rstd, win_ref, j, sh_ref, ph_ref, mu_ref, wl_ref, w0_ref, a0_ref, kk_ref,
                 ka_ref, rk_ref, pw_ref, ps_ref, g_ref,
                 r_o, k_o, v_o, lw_o, a_o, b_o, bon_o, sg_o, op_o, nsh_o, npl_o,
                 p_ext, u_ext, *, stride, t0, hs):
    rows = h.shape[0]
    ph = _pool_hist_rows(stride)
    width = MXU_DIM

    @pl.when(j == 0)
    def _():
        p_ext[0:hs, :] = jnp.zeros((hs, D_SHIFT), F32) if sh_ref is None else sh_ref[0]
        u_ext[0:ph, :] = jnp.zeros((ph, D_POOL), F32) if ph_ref is None else ph_ref[0]

    def proj_cols(c0, c1):
        return jnp.dot(h, win_ref[:, c0:c1], preferred_element_type=F32) * rstd

    def shifted(c0, c1):
        p = proj_cols(c0, c1)
        p_ext[hs:hs + rows, c0:c1] = p
        prev = p_ext[pl.ds(hs - stride, rows), c0:c1]
        return p + mu_ref[:, c0:c1] * (prev - p)

    xwa = shifted(3 * D_RWKV, D_SHIFT)
    for c in range(0, D_RWKV, width):
        r_o[:, c:c + width] = shifted(c, c + width)
    lane = lax.broadcasted_iota(jnp.int32, xwa.shape, 1)
    lora_in = jnp.where(lane < D_LORA, jnp.tanh(xwa), xwa)
    lora = jnp.dot(lora_in.astype(BF16), wl_ref[...], preferred_element_type=F32)
    lw_o[...] = (-0.6065306597126334) * _sigmoid(w0_ref[...] + lora[:, :D_RWKV])
    alpha = _sigmoid(a0_ref[...] + lora[:, D_RWKV:])
    for c in range(0, D_RWKV, width):
        sl = slice(c, c + width)
        k = shifted(D_RWKV + c, D_RWKV + c + width)
        k_o[:, sl] = k * (1.0 + (alpha[:, sl] - 1.0) * ka_ref[:, sl])
        kk = k * kk_ref[:, sl]
        kk = kk * lax.rsqrt(_seg_sum(kk * kk, g_ref[...]) + L2_EPS)
        a_o[:, sl] = -kk
        b_o[:, sl] = kk * alpha[:, sl]
        v_o[:, sl] = shifted(2 * D_RWKV + c, 2 * D_RWKV + c + width)
    last_p = p_ext[rows:rows + hs, :]
    p_ext[0:hs, :] = last_p
    nsh_o[0] = last_p
    for c in range(0, D_RWKV, width):
        g_rwkv = proj_cols(D_SHIFT + c, D_SHIFT + c + width)
        sg_o[:, c:c + width] = g_rwkv * _sigmoid(g_rwkv)
    for c in range(0, D_POOL, width):
        u_ext[ph:ph + rows, c:c + width] = proj_cols(D_SHIFT + D_RWKV + c, D_SHIFT + D_RWKV + c + width)
        op_o[:, c:c + width] = proj_cols(D_SHIFT + D_RWKV + D_POOL + c,
                                         D_SHIFT + D_RWKV + D_POOL + c + width)

    def bonus():
        bon_o[...] = _seg_sum(r_o[...] * k_o[...] * rk_ref[...], g_ref[...]) * v_o[...]

    def pool(gi, win):
        sl = slice(gi * POOL_GROUP, (gi + 1) * POOL_GROUP)
        row = lax.broadcasted_iota(jnp.int32, (rows, 1), 0)
        pos = t0 + (j * rows + row) // stride
        u = u_ext[ph:ph + rows, sl]
        tot = u
        for back in range(1, win):
            tot = tot + u_ext[pl.ds(ph - back * stride, rows), sl]
        inv_cnt = 1.0 / jnp.minimum(pos + 1, win).astype(F32)
        d = tot * inv_cnt - u
        o = jnp.dot(d.astype(BF16), pw_ref[gi], preferred_element_type=F32)
        gp = op_o[:, sl]
        op_o[:, sl] = o * ps_ref[:, sl] * (gp * _sigmoid(gp))

    def carry_pool():
        last_u = u_ext[rows:rows + ph, :]
        u_ext[0:ph, :] = last_u
        npl_o[0] = last_u

    pools = [functools.partial(pool, gi, win) for gi, win in enumerate(POOL_WINDOWS)]
    return [bonus] + pools + [carry_pool]


def _stack(x, m0):
    return jnp.concatenate([jnp.where(m0, x, 0.0), jnp.where(m0, 0.0, x)], axis=0)


def _wkv_zero_state(z_ref, is_first):
    @pl.when(is_first)
    def _():
        z_ref[...] = jnp.zeros(z_ref.shape, F32)


def _wkv_store_state(z_ref, so_ref, is_last):
    @pl.when(is_last)
    def _():
        for p in range(N_PAIRS):
            s2 = z_ref[p].T
            so_ref[0, 2 * p] = s2[:HEAD, :HEAD]
            so_ref[0, 2 * p + 1] = s2[HEAD:, HEAD:]


def _wkv_tile(r_ref, k_ref, v_ref, lw_ref, a_ref, b_ref, tri_ref, mss_ref, msq_ref, y_ref, z_ref, *,
              n_chunks, fillers=()):
    fillers = list(fillers)
    fill = lambda: fillers.pop(0)() if fillers else None
    lane = lax.broadcasted_iota(jnp.int32, (CHUNK, LANES), 1)
    m0 = lane < HEAD
    strict = mss_ref[0][:, :LANES] > 0.5
    incl2 = mss_ref[1] > 0.5
    n_levels = mss_ref.shape[0] - 3
    lane_head4 = lax.broadcasted_iota(jnp.int32, (CHUNK, 2 * LANES), 1) // HEAD
    eye = msq_ref[0]
    same_head = msq_ref[1] > 0.5
    zblk = jnp.zeros((CHUNK, LANES), BF16)
    zblk2 = jnp.zeros((2 * CHUNK, LANES), BF16)
    bf = lambda x: x.astype(BF16)
    lanes = lambda p: slice(p * LANES, (p + 1) * LANES)
    each = lambda fn, *lists: [fn(*args) for args in zip(*lists)]
    cat0 = lambda *xs: jnp.concatenate(xs, axis=0)
    cat1 = lambda *xs: jnp.concatenate(xs, axis=1)

    scaled = []
    for ci in range(n_chunks):
        rows = slice(ci * CHUNK, (ci + 1) * CHUNK)
        lw = lw_ref[rows, :]
        cs = _mm(tri_ref[...], lw, NN, 1, P_CUMSUM)
        cs_end = cs[CHUNK - 1:CHUNK, :]
        g_inv = jnp.exp(-cs)
        g_tail = jnp.exp(cs_end - cs)
        r_t = r_ref[rows, :] * jnp.exp(cs)
        scaled.append(dict(
            g_end=jnp.exp(cs_end), r_f32=r_t, r_t=bf(r_t),
            a_t=bf(a_ref[rows, :] * jnp.exp(cs - lw)),
            b_t=bf(b_ref[rows, :] * g_inv), k_t=bf(k_ref[rows, :] * g_inv),
            b_h=bf(b_ref[rows, :] * g_tail), k_h=bf(k_ref[rows, :] * g_tail), v=bf(v_ref[rows, :])))
    chains = [(ci, p) for ci in range(n_chunks) for p in range(N_PAIRS)]
    pair = lambda name: [scaled[ci][name][:, lanes(p)] for ci, p in chains]
    at, rt, bt, kt, bh, kh, v, r_f32 = (pair(n) for n in ("a_t", "r_t", "b_t", "k_t", "b_h", "k_h", "v",
                                                          "r_f32"))
    v2 = each(lambda x: _stack(x, m0), v)
    scores = each(lambda a, r, b, k: _mm(cat0(a, r), cat0(_stack(b, m0), _stack(k, m0)), NT, *P_A),
                  at, rt, bt, kt)
    a_ab = each(lambda s: jnp.where(strict, s[:CHUNK, :LANES], 0.0), scores)
    a_ak = each(lambda s: jnp.where(strict, s[:CHUNK, LANES:], 0.0), scores)
    a_r = each(lambda s: jnp.where(incl2, s[CHUNK:, :], 0.0), scores)
    stack4 = lambda x: cat0(*[jnp.where(lane_head4 == h, x, 0.0) for h in range(4)])
    join = lambda xs: [cat1(xs[i], xs[i + 1]) for i in range(0, len(xs), 2)]
    split = lambda xs: [x[:, s] for x in xs for s in (slice(0, LANES), slice(LANES, 2 * LANES))]
    akv = split(each(lambda x, y: _mm(x, stack4(y), NN, *P_U), join(a_ak), join(v)))
    a_ab4 = join(a_ab)
    t_inv4 = each(lambda x: mss_ref[2] + x * mss_ref[3], a_ab4)
    for lvl in range(1, n_levels):
        t_b = each(bf, t_inv4)
        inner = each(lambda x, t: _mm(x * mss_ref[3 + lvl], stack4(t), NN, *P_T), a_ab4, t_b)
        t_inv4 = each(lambda t, tb, w: t + _mm(tb, stack4(bf(w)), NN, *P_T), t_inv4, t_b, inner)
    t_inv = split(each(bf, t_inv4))
    pq = each(lambda t, x, y: _mm(t, cat1(_stack(x, m0), _stack(bf(y), m0)), NN, *P_U), t_inv, at, akv)
    pq_b = each(bf, pq)
    bpq = each(lambda b, k, x, y: _mm(cat0(b, k), cat0(x, cat1(zblk, y)), TN, *P_S), bh, kh, pq_b, v)
    g_end = [scaled[ci]["g_end"][:, lanes(p)] for ci, p in chains]
    m_t = each(lambda g, x: eye * g + jnp.where(same_head, x[:, :LANES], 0.0), g_end, bpq)
    n_t = each(lambda x: jnp.where(same_head, x[:, LANES:], 0.0), bpq)
    yy = each(lambda x, y, z: _mm(x, cat0(cat1(_stack(y[:, :LANES], m0), _stack(y[:, LANES:], m0)),
                                         cat1(zblk2, z)), NN, *P_Y), a_r, pq_b, v2)
    y1 = each(lambda r, x: r + x[:, :LANES], r_f32, yy)
    y2 = each(lambda x: x[:, LANES:], yy)

    state = [z_ref[p] for p in range(N_PAIRS)]
    for ci in range(n_chunks):
        for p in range(N_PAIRS):
            i = ci * N_PAIRS + p
            both = _mm(cat0(m_t[i], y1[i]), state[p], NN, *P_S)
            state[p] = both[:LANES] + n_t[i]
            y_ref[ci * CHUNK:(ci + 1) * CHUNK, lanes(p)] = both[LANES:] + y2[i]
        fill()
    for p in range(N_PAIRS):
        z_ref[p] = state[p]
    while fillers:
        fill()


def _wkv_masks():
    ri, ci = np.indices((CHUNK, 4 * CHUNK))
    ci = ci % CHUNK
    side = [ci < ri, ci <= ri, ri == ci]
    m = 1
    while m < CHUNK:
        side.append(((ri // (2 * m)) == (ci // (2 * m))) & ((ri // m) % 2 == 1) & ((ci // m) % 2 == 0))
        m *= 2
    rq, cq = np.indices((LANES, LANES))
    square = [rq == cq, (rq // HEAD) == (cq // HEAD)]
    rt, ct = np.indices((CHUNK, CHUNK))
    return (jnp.asarray(ct <= rt, BF16), jnp.asarray(np.stack(side), F32),
            jnp.asarray(np.stack(square), F32))


def _short_pairs(t_s):
    idx = {}
    for kind, inclusive in (("ab", False), ("ak", False), ("rb", True), ("rk", True)):
        for t in range(t_s):
            for s in range(t + 1 if inclusive else t):
                idx[(kind, t, s)] = len(idx)
    return idx


def _short_scale_kernel(r_ref, k_ref, v_ref, lw_ref, a_ref, b_ref,
                        xa_o, xr_o, bh_o, kh_o, vt_o, g4_o, as_o, *, t_s, n_s):
    slab = lambda ref, t: ref[t * n_s:(t + 1) * n_s, :].T
    cs = []
    for t in range(t_s):
        cs.append(slab(lw_ref, t) if t == 0 else cs[-1] + slab(lw_ref, t))
    a_t = [slab(a_ref, t) * (jnp.exp(cs[t - 1]) if t else 1.0) for t in range(t_s)]
    r_t = [slab(r_ref, t) * jnp.exp(cs[t]) for t in range(t_s)]
    b_raw = [slab(b_ref, t) for t in range(t_s)]
    k_raw = [slab(k_ref, t) for t in range(t_s)]
    g_inv = [jnp.exp(-cs[t]) for t in range(t_s)]
    b_t = [b_raw[t] * g_inv[t] for t in range(t_s)]
    k_t = [k_raw[t] * g_inv[t] for t in range(t_s)]
    for t in range(t_s):
        g_tail = jnp.exp(cs[t_s - 1] - cs[t])
        xa_o[t] = a_t[t]
        xr_o[t] = r_t[t]
        bh_o[t] = b_raw[t] * g_tail
        kh_o[t] = k_raw[t] * g_tail
        vt_o[t] = slab(v_ref, t)
    g4_o[...] = jnp.exp(cs[t_s - 1])
    head_sum = lambda x: jnp.sum(x.reshape(N_HEADS, HEAD, n_s), axis=1)
    left = {"ab": a_t, "ak": a_t, "rb": r_t, "rk": r_t}
    right = {"ab": b_t, "ak": k_t, "rb": b_t, "rk": k_t}
    for (kind, t, s), row in _short_pairs(t_s).items():
        as_o[row] = head_sum(left[kind][t] * right[kind][s])


def _short_state_kernel(s_ref, xa_ref, xr_ref, bh_ref, kh_ref, vt_ref, g4_ref, as_ref, *refs, t_s):
    out_ins, (so_ref, o_ref, y_ref) = refs[:-3], refs[-3:]
    idx = _short_pairs(t_s)
    head = pl.program_id(0)
    scal = {key: as_ref[row, pl.ds(head, 1), :] for key, row in idx.items()}

    def value_row(v, carry):
        s_v = s_ref[0, v]
        vv = [vt_ref[t, pl.ds(v, 1), :] for t in range(t_s)]
        u = []
        for t in range(t_s):
            acc = jnp.sum(s_v * xa_ref[t], axis=0, keepdims=True)
            for s in range(t):
                acc = acc + scal[("ab", t, s)] * u[s] + scal[("ak", t, s)] * vv[s]
            u.append(acc)
        for t in range(t_s):
            acc = jnp.sum(s_v * xr_ref[t], axis=0, keepdims=True)
            for s in range(t + 1):
                acc = acc + scal[("rb", t, s)] * u[s] + scal[("rk", t, s)] * vv[s]
            y_ref[t, pl.ds(head * HEAD + v, 1), :] = acc
        new = s_v * g4_ref[...]
        for s in range(t_s):
            new = new + u[s] * bh_ref[s] + vv[s] * kh_ref[s]
        so_ref[0, v] = new
        return carry

    lax.fori_loop(0, HEAD, value_row, 0, unroll=8)

    @pl.when(head == pl.num_programs(0) - 1)
    def _():
        _out_kernel(y_ref, *out_ins, o_ref)


def _short_prep_kernel(x_ref, sh_ref, ph_ref, nw_ref, win_ref, *refs, t_s, n_s, t0):
    prep_rest, refs = refs[:N_PREP_PARAMS - 2], refs[N_PREP_PARAMS - 2:]
    bon_o, sg_o, op_o, nsh_o, npl_o = refs[:5]
    scale_outs, seq_s, (p_ext, u_ext) = refs[5:12], refs[12:18], refs[18:]
    h, rstd = _norm_operands(x_ref, nw_ref)
    for finish in _prep_stages(h, rstd, win_ref, pl.program_id(0), sh_ref, ph_ref, *prep_rest, *seq_s,
                               bon_o, sg_o, op_o, nsh_o, npl_o, p_ext, u_ext,
                               stride=n_s, t0=t0, hs=sh_ref.shape[1]):
        finish()
    _short_scale_kernel(*seq_s, *scale_outs, t_s=t_s, n_s=n_s)


def _short_prep_call(x, sh_hist, pool_hist, params, *, t_s, n_s, t0):
    assert n_s % LANES == 0 and len(params) == N_PREP_PARAMS
    rows = t_s * n_s
    hs, ph = sh_hist.shape[1], _pool_hist_rows(n_s)
    assert pool_hist.shape[1] == ph
    n_scal = len(_short_pairs(t_s))
    full = lambda shape: pl.BlockSpec(shape, lambda i: tuple(0 for _ in shape))
    vec = (t_s, D_RWKV, n_s)
    out_shapes = ([(rows, D_RWKV)] * 3 + [(1, hs, D_SHIFT), (1, ph, D_POOL)] + [vec] * 5
                  + [(D_RWKV, n_s), (n_scal, N_HEADS, n_s)])
    return pl.pallas_call(
        functools.partial(_short_prep_kernel, t_s=t_s, n_s=n_s, t0=t0),
        grid=(1,),
        in_specs=[full(a.shape) for a in (x, sh_hist, pool_hist) + tuple(params)],
        out_specs=[full(s) for s in out_shapes],
        out_shape=[jax.ShapeDtypeStruct(s, F32) for s in out_shapes],
        scratch_shapes=[pltpu.VMEM((rows, D_RWKV), F32)] * 6 + [pltpu.VMEM((hs + rows, D_SHIFT), F32),
                                                                pltpu.VMEM((ph + rows, D_POOL), F32)],
        compiler_params=pltpu.CompilerParams(
            dimension_semantics=("arbitrary",), vmem_limit_bytes=VMEM_LIMIT),
        name="prep_short",
    )(x, sh_hist, pool_hist, *params)


def _short_state_call(s0, xa, xr, bh, kh, vt, g4, scal, bon, sg, op, x, out_params):
    t_s, _, n_s = xa.shape
    assert len(out_params) == N_OUT_PARAMS and x.shape[0] == t_s * n_s
    st = pl.BlockSpec((1, HEAD, HEAD, n_s), lambda h: (h, 0, 0, 0))
    per_head = pl.BlockSpec((t_s, HEAD, n_s), lambda h: (0, h, 0))
    whole = lambda a: pl.BlockSpec(a.shape, lambda h: (0,) * a.ndim)
    out_ins = (bon, sg, op, x) + tuple(out_params)
    return pl.pallas_call(
        functools.partial(_short_state_kernel, t_s=t_s),
        grid=(N_HEADS,),
        in_specs=[st] + [per_head] * 5 + [pl.BlockSpec((HEAD, n_s), lambda h: (h, 0)), whole(scal)]
        + [whole(a) for a in out_ins],
        out_specs=[st, whole(x)],
        out_shape=[jax.ShapeDtypeStruct(s0.shape, F32), jax.ShapeDtypeStruct(x.shape, F32)],
        scratch_shapes=[pltpu.VMEM(xa.shape, F32)],
        compiler_params=pltpu.CompilerParams(
            dimension_semantics=("arbitrary",), vmem_limit_bytes=VMEM_LIMIT),
        name="layer_short_state",
    )(s0, xa, xr, bh, kh, vt, g4, scal, *out_ins)


def _gated_rwkv(y, bon, sg, gw_ref, gb_ref, g_ref):
    g_ones = g_ref[...]
    mu = _seg_sum(y, g_ones) * (1.0 / HEAD)
    d = y - mu
    var = _seg_sum(d * d, g_ones) * (1.0 / HEAD)
    yn = d * lax.rsqrt(var + GN_EPS) * gw_ref[...] + gb_ref[...]
    return (yn + bon) * sg


def _out_project(o_rwkv, o_pool, x, wo_ref, nf_ref):
    out = (jnp.dot(o_rwkv.astype(BF16), wo_ref[0:D_RWKV, :], preferred_element_type=F32)
           + jnp.dot(o_pool.astype(BF16), wo_ref[D_RWKV:, :], preferred_element_type=F32))
    res = x + out
    return res * lax.rsqrt(jnp.mean(res * res, axis=-1, keepdims=True) + NORM_EPS) * nf_ref[...]


def _out_kernel(y_ref, bon_ref, sg_ref, op_ref, x_ref, gw_ref, gb_ref, wo_ref, nf_ref, g_ref, o_ref):
    if len(y_ref.shape) == 3:
        y = jnp.concatenate([y_ref[t].T for t in range(y_ref.shape[0])], axis=0)
    else:
        y = y_ref[...]
    o_rwkv = _gated_rwkv(y, bon_ref[...], sg_ref[...], gw_ref, gb_ref, g_ref)
    o_ref[...] = _out_project(o_rwkv, op_ref[...], x_ref[...], wo_ref, nf_ref)


N_PREP_PARAMS = 12
N_WKV_CONSTS = 3
N_OUT_PARAMS = 5


def _fused_kernel(*refs, n_chunks, t0):
    x_ref, x_next_ref, nw_ref, win_ref = refs[:4]
    pos = 2 + N_PREP_PARAMS
    prep_rest = refs[4:pos]
    wkv_consts = refs[pos:pos + N_WKV_CONSTS]
    pos += N_WKV_CONSTS
    gw_ref, gb_ref, wo_ref, nf_ref, g_ref = refs[pos:pos + N_OUT_PARAMS]
    pos += N_OUT_PARAMS
    o_ref, nsh_o, npl_o, so_ref = refs[pos:pos + 4]
    pos += 4
    seq_s = refs[pos:pos + 6]
    bon_s, sg_s, op_s, y_s, p_ext, u_ext, z_ref, h_s, rstd_s = refs[pos + 6:]
    j = pl.program_id(1)

    def prepare(src_ref):
        h_s[...], rstd_s[...] = _norm_operands(src_ref, nw_ref)

    @pl.when((pl.program_id(0) == 0) & (j == 0))
    def _():
        prepare(x_ref)

    _wkv_zero_state(z_ref, j == 0)
    finish_prep = _prep_stages(h_s[...], rstd_s[...], win_ref, j, None, None, *prep_rest,
                               *seq_s, bon_s, sg_s, op_s, nsh_o, npl_o, p_ext, u_ext,
                               stride=1, t0=t0, hs=nsh_o.shape[1])
    _wkv_tile(*seq_s, *wkv_consts, y_s, z_ref, n_chunks=n_chunks,
              fillers=finish_prep + [functools.partial(prepare, x_next_ref)])
    _out_kernel(y_s, bon_s, sg_s, op_s, x_ref, gw_ref, gb_ref, wo_ref, nf_ref, g_ref, o_ref)
    _wkv_store_state(z_ref, so_ref, j == pl.num_programs(1) - 1)


def _fused_call(x, prep_params, out_params, *, n_seq, n_chunks, t0):
    assert len(prep_params) == N_PREP_PARAMS and len(out_params) == N_OUT_PARAMS
    total = x.shape[0]
    rows = n_chunks * CHUNK
    tiles = total // (n_seq * rows)
    hs, ph = SUBLANES, _pool_hist_rows(1)
    assert CHUNK == HEAD and 2 * CHUNK == LANES
    wkv_consts = _wkv_masks()
    act = pl.BlockSpec((rows, D_MODEL), lambda b, j: (b * tiles + j, 0))
    ahead = pl.BlockSpec((rows, D_MODEL), lambda b, j: (jnp.minimum(b * tiles + j + 1, n_seq * tiles - 1), 0))
    per_seq = lambda shape: pl.BlockSpec((1,) + shape[1:], lambda b, j: (b,) + (0,) * (len(shape) - 1))
    const = lambda a: pl.BlockSpec(a.shape, lambda b, j: (0,) * a.ndim, pipeline_mode=pl.Buffered(1))
    consts = tuple(prep_params) + tuple(wkv_consts) + tuple(out_params)
    seq_shapes = [(n_seq, hs, D_SHIFT), (n_seq, ph, D_POOL), (n_seq, N_HEADS, HEAD, HEAD)]
    seq_scratch = pltpu.VMEM((rows, D_RWKV), F32)
    return pl.pallas_call(
        functools.partial(_fused_kernel, n_chunks=n_chunks, t0=t0),
        grid=(n_seq, tiles),
        in_specs=[act, ahead] + [const(a) for a in consts],
        out_specs=[act] + [per_seq(s) for s in seq_shapes],
        out_shape=[jax.ShapeDtypeStruct((total, D_MODEL), F32)]
        + [jax.ShapeDtypeStruct(s, F32) for s in seq_shapes],
        scratch_shapes=[seq_scratch] * 10 + [pltpu.VMEM((hs + rows, D_SHIFT), F32),
                                             pltpu.VMEM((ph + rows, D_POOL), F32),
                                             pltpu.VMEM((N_PAIRS, LANES, LANES), F32),
                                             pltpu.VMEM((rows, D_MODEL), BF16),
                                             pltpu.VMEM((rows, 1), F32)],
        compiler_params=pltpu.CompilerParams(
            dimension_semantics=("arbitrary", "arbitrary"), vmem_limit_bytes=VMEM_LIMIT),
        name="layer_long",
    )(x, x, *consts)


def _head_ones():
    i, j = np.indices((MXU_DIM, MXU_DIM))
    return jnp.asarray((i // HEAD) == (j // HEAD), BF16)


def kernel(x_prompt, x_sample, state_shift, state_wkv, state_pool, norm_w, w_in, mu_shift, w_decay_b,
           w0, w_aaa_b, a0, k_k, k_a, r_k, gn_w, gn_b, pool_w, pool_scale, w_out, norm_f):
    depth = norm_w.shape[0]
    n_p, t_p, _ = x_prompt.shape
    n_s, t_s, _ = x_sample.shape
    assert depth == 1, "a stacked trunk needs a residual-only output kernel between layers"
    assert t_p % (WKV_CHUNKS_PER_STEP * CHUNK) == 0
    g_ones = _head_ones()
    row = lambda z: z.reshape(1, -1).astype(F32)

    hp = x_prompt.astype(F32).reshape(n_p * t_p, D_MODEL)
    hs = jnp.transpose(x_sample.astype(F32), (1, 0, 2)).reshape(t_s * n_s, D_MODEL)
    outs = [[] for _ in range(6)]
    for l in range(depth):
        zl = jnp.zeros((D_LORA, D_RWKV), F32)
        w_lora = jnp.concatenate([jnp.concatenate([w_decay_b[l], zl], axis=1),
                                  jnp.concatenate([zl, w_aaa_b[l]], axis=1)], axis=0).astype(BF16)
        prep_params = (row(norm_w[l]), w_in[l].astype(BF16), row(mu_shift[l]), w_lora, row(w0[l]),
                       row(a0[l]), row(k_k[l]), row(k_a[l]), row(r_k[l]), pool_w[l].astype(BF16),
                       row(pool_scale[l]), g_ones)
        out_params = (row(gn_w[l]), row(gn_b[l]), w_out[l].astype(BF16), row(norm_f), g_ones)

        hp, nsh, npl, s_fin = _fused_call(hp, prep_params, out_params,
                                          n_seq=n_p, n_chunks=WKV_CHUNKS_PER_STEP, t0=0)
        outs[0].append(nsh[:, SUBLANES - 1])
        outs[1].append(s_fin)
        outs[2].append(npl[:, -POOL_KEEP:])

        sh_hist = state_shift[l].astype(F32)[None]
        pool_hist = jnp.transpose(state_pool[l].astype(F32), (1, 0, 2)).reshape(1, POOL_KEEP * n_s, D_POOL)
        bon, sg, op, nsh, npl, *scan_ops = _short_prep_call(hs, sh_hist, pool_hist, prep_params,
                                                            t_s=t_s, n_s=n_s, t0=PAST_LEN)
        s_fin, hs = _short_state_call(jnp.transpose(state_wkv[l].astype(F32), (1, 2, 3, 0)), *scan_ops,
                                      bon, sg, op, hs, out_params)
        s_fin = jnp.transpose(s_fin, (3, 0, 1, 2))
        outs[3].append(nsh[0])
        outs[4].append(s_fin)
        outs[5].append(jnp.transpose(npl[0].reshape(POOL_KEEP, n_s, D_POOL), (1, 0, 2)))

    y_prompt = hp.reshape(n_p, t_p, D_MODEL).astype(x_prompt.dtype)
    y_sample = jnp.transpose(hs.reshape(t_s, n_s, D_MODEL), (1, 0, 2)).astype(x_sample.dtype)
    return (y_prompt, y_sample) + tuple(jnp.stack(o, axis=0) for o in outs)
```

```python
import functools

import jax
import jax.numpy as jnp
import numpy as np
from jax import lax
from jax.experimental import pallas as pl
from jax.experimental.pallas import tpu as pltpu

F32 = jnp.float32
BF16 = jnp.bfloat16

D_MODEL = 1024
D_RWKV = 512
D_POOL = 512
HEAD = 64
N_HEADS = D_RWKV // HEAD
LANES = 128
SUBLANES = 8
MXU_DIM = 256
N_PAIRS = D_RWKV // LANES
D_LORA = 64
POOL_WINDOWS = (2, 4, 8, 16)
POOL_GROUP = D_POOL // len(POOL_WINDOWS)
POOL_KEEP = max(POOL_WINDOWS) - 1
D_SHIFT = 3 * D_RWKV + 2 * D_LORA
D_IN = D_SHIFT + D_RWKV + 2 * D_POOL
PAST_LEN = 16384
NORM_EPS = 1e-6
GN_EPS = 64e-5
L2_EPS = 1e-12
CHUNK = 64
VMEM_LIMIT = 56 * 1024 * 1024
P_A = (1, 1)
P_T = (1, 1)
P_U = (1, 1)
P_S = (1, 1)
P_Y = (1, 1)
P_CUMSUM = 2
WKV_CHUNKS_PER_STEP = 8

NN = (((1,), (0,)), ((), ()))
NT = (((1,), (1,)), ((), ()))
TN = (((0,), (0,)), ((), ()))


def _pieces(x, n):
    if x.dtype == BF16:
        return [x]
    out = []
    rem = x
    for i in range(n):
        p = rem.astype(BF16)
        out.append(p)
        if i + 1 < n:
            rem = rem - p.astype(F32)
    return out


def _mm(a, b, dims=NN, na=1, nb=1):
    ap = _pieces(a, na)
    bp = _pieces(b, nb)
    order = max(len(ap), len(bp))
    acc = None
    for i, x in enumerate(ap):
        for j, y in enumerate(bp):
            if i + j >= order:
                continue
            t = lax.dot_general(x, y, dims, preferred_element_type=F32)
            acc = t if acc is None else acc + t
    return acc


def _seg_sum(x, g_ones, pieces=1):
    cols = []
    for s in range(x.shape[1] // MXU_DIM):
        cols.append(_mm(x[:, s * MXU_DIM:(s + 1) * MXU_DIM], g_ones, NN, pieces, 1))
    return cols[0] if len(cols) == 1 else jnp.concatenate(cols, axis=1)


def _sigmoid(x):
    return 1.0 / (1.0 + jnp.exp(-x))


def _pool_hist_rows(stride):
    return -(-POOL_KEEP * stride // SUBLANES) * SUBLANES


def _norm_operands(x_ref, nw_ref):
    x = x_ref[...]
    rstd = lax.rsqrt(jnp.mean(x * x, axis=-1, keepdims=True) + NORM_EPS)
    return (x * nw_ref[...]).astype(BF16), rstd


def _prep_stages(h, rstd, win_ref, j, sh_ref, ph_ref, mu_ref, wl_ref, w0_ref, a0_ref, kk_ref,
                 ka_ref, rk_ref, pw_ref, ps_ref, g_ref,
                 r_o, k_o, v_o, lw_o, a_o, b_o, bon_o, sg_o, op_o, nsh_o, npl_o,
                 p_ext, u_ext, *, stride, t0, hs):
    rows = h.shape[0]
    ph = _pool_hist_rows(stride)
    width = MXU_DIM

    @pl.when(j == 0)
    def _():
        p_ext[0:hs, :] = jnp.zeros((hs, D_SHIFT), F32) if sh_ref is None else sh_ref[0]
        u_ext[0:ph, :] = jnp.zeros((ph, D_POOL), F32) if ph_ref is None else ph_ref[0]

    def proj_cols(c0, c1):
        return jnp.dot(h, win_ref[:, c0:c1], preferred_element_type=F32) * rstd

    def shifted(c0, c1):
        p = proj_cols(c0, c1)
        p_ext[hs:hs + rows, c0:c1] = p
        prev = p_ext[pl.ds(hs - stride, rows), c0:c1]
        return p + mu_ref[:, c0:c1] * (prev - p)

    xwa = shifted(3 * D_RWKV, D_SHIFT)
    for c in range(0, D_RWKV, width):
        r_o[:, c:c + width] = shifted(c, c + width)
    lane = lax.broadcasted_iota(jnp.int32, xwa.shape, 1)
    lora_in = jnp.where(lane < D_LORA, jnp.tanh(xwa), xwa)
    lora = jnp.dot(lora_in.astype(BF16), wl_ref[...], preferred_element_type=F32)
    lw_o[...] = (-0.6065306597126334) * _sigmoid(w0_ref[...] + lora[:, :D_RWKV])
    alpha = _sigmoid(a0_ref[...] + lora[:, D_RWKV:])
    for c in range(0, D_RWKV, width):
        sl = slice(c, c + width)
        k = shifted(D_RWKV + c, D_RWKV + c + width)
        k_o[:, sl] = k * (1.0 + (alpha[:, sl] - 1.0) * ka_ref[:, sl])
        kk = k * kk_ref[:, sl]
        kk = kk * lax.rsqrt(_seg_sum(kk * kk, g_ref[...]) + L2_EPS)
        a_o[:, sl] = -kk
        b_o[:, sl] = kk * alpha[:, sl]
        v_o[:, sl] = shifted(2 * D_RWKV + c, 2 * D_RWKV + c + width)
    last_p = p_ext[rows:rows + hs, :]
    p_ext[0:hs, :] = last_p
    nsh_o[0] = last_p
    for c in range(0, D_RWKV, width):
        g_rwkv = proj_cols(D_SHIFT + c, D_SHIFT + c + width)
        sg_o[:, c:c + width] = g_rwkv * _sigmoid(g_rwkv)
    for c in range(0, D_POOL, width):
        u_ext[ph:ph + rows, c:c + width] = proj_cols(D_SHIFT + D_RWKV + c, D_SHIFT + D_RWKV + c + width)
        op_o[:, c:c + width] = proj_cols(D_SHIFT + D_RWKV + D_POOL + c,
                                         D_SHIFT + D_RWKV + D_POOL + c + width)

    def bonus():
        bon_o[...] = _seg_sum(r_o[...] * k_o[...] * rk_ref[...], g_ref[...]) * v_o[...]

    def pool(gi, win):
        sl = slice(gi * POOL_GROUP, (gi + 1) * POOL_GROUP)
        row = lax.broadcasted_iota(jnp.int32, (rows, 1), 0)
        pos = t0 + (j * rows + row) // stride
        u = u_ext[ph:ph + rows, sl]
        tot = u
        for back in range(1, win):
            tot = tot + u_ext[pl.ds(ph - back * stride, rows), sl]
        inv_cnt = 1.0 / jnp.minimum(pos + 1, win).astype(F32)
        d = tot * inv_cnt - u
        o = jnp.dot(d.astype(BF16), pw_ref[gi], preferred_element_type=F32)
        gp = op_o[:, sl]
        op_o[:, sl] = o * ps_ref[:, sl] * (gp * _sigmoid(gp))

    def carry_pool():
        last_u = u_ext[rows:rows + ph, :]
        u_ext[0:ph, :] = last_u
        npl_o[0] = last_u

    pools = [functools.partial(pool, gi, win) for gi, win in enumerate(POOL_WINDOWS)]
    return [bonus] + pools + [carry_pool]


def _stack(x, m0):
    return jnp.concatenate([jnp.where(m0, x, 0.0), jnp.where(m0, 0.0, x)], axis=0)


def _wkv_zero_state(z_ref, is_first):
    @pl.when(is_first)
    def _():
        z_ref[...] = jnp.zeros(z_ref.shape, F32)


def _wkv_store_state(z_ref, so_ref, is_last):
    @pl.when(is_last)
    def _():
        for p in range(N_PAIRS):
            s2 = z_ref[p].T
            so_ref[0, 2 * p] = s2[:HEAD, :HEAD]
            so_ref[0, 2 * p + 1] = s2[HEAD:, HEAD:]


def _wkv_tile(r_ref, k_ref, v_ref, lw_ref, a_ref, b_ref, tri_ref, mss_ref, msq_ref, y_ref, z_ref, *,
              n_chunks, fillers=()):
    fillers = list(fillers)
    fill = lambda: fillers.pop(0)() if fillers else None
    lane = lax.broadcasted_iota(jnp.int32, (CHUNK, LANES), 1)
    m0 = lane < HEAD
    strict = mss_ref[0][:, :LANES] > 0.5
    incl2 = mss_ref[1] > 0.5
    n_levels = mss_ref.shape[0] - 3
    lane_head4 = lax.broadcasted_iota(jnp.int32, (CHUNK, 2 * LANES), 1) // HEAD
    eye = msq_ref[0]
    same_head = msq_ref[1] > 0.5
    zblk = jnp.zeros((CHUNK, LANES), BF16)
    zblk2 = jnp.zeros((2 * CHUNK, LANES), BF16)
    bf = lambda x: x.astype(BF16)
    lanes = lambda p: slice(p * LANES, (p + 1) * LANES)
    each = lambda fn, *lists: [fn(*args) for args in zip(*lists)]
    cat0 = lambda *xs: jnp.concatenate(xs, axis=0)
    cat1 = lambda *xs: jnp.concatenate(xs, axis=1)

    scaled = []
    for ci in range(n_chunks):
        rows = slice(ci * CHUNK, (ci + 1) * CHUNK)
        lw = lw_ref[rows, :]
        cs = _mm(tri_ref[...], lw, NN, 1, P_CUMSUM)
        cs_end = cs[CHUNK - 1:CHUNK, :]
        g_inv = jnp.exp(-cs)
        g_tail = jnp.exp(cs_end - cs)
        r_t = r_ref[rows, :] * jnp.exp(cs)
        scaled.append(dict(
            g_end=jnp.exp(cs_end), r_f32=r_t, r_t=bf(r_t),
            a_t=bf(a_ref[rows, :] * jnp.exp(cs - lw)),
            b_t=bf(b_ref[rows, :] * g_inv), k_t=bf(k_ref[rows, :] * g_inv),
            b_h=bf(b_ref[rows, :] * g_tail), k_h=bf(k_ref[rows, :] * g_tail), v=bf(v_ref[rows, :])))
    chains = [(ci, p) for ci in range(n_chunks) for p in range(N_PAIRS)]
    pair = lambda name: [scaled[ci][name][:, lanes(p)] for ci, p in chains]
    at, rt, bt, kt, bh, kh, v, r_f32 = (pair(n) for n in ("a_t", "r_t", "b_t", "k_t", "b_h", "k_h", "v",
                                                          "r_f32"))
    v2 = each(lambda x: _stack(x, m0), v)
    scores = each(lambda a, r, b, k: _mm(cat0(a, r), cat0(_stack(b, m0), _stack(k, m0)), NT, *P_A),
                  at, rt, bt, kt)
    a_ab = each(lambda s: jnp.where(strict, s[:CHUNK, :LANES], 0.0), scores)
    a_ak = each(lambda s: jnp.where(strict, s[:CHUNK, LANES:], 0.0), scores)
    a_r = each(lambda s: jnp.where(incl2, s[CHUNK:, :], 0.0), scores)
    stack4 = lambda x: cat0(*[jnp.where(lane_head4 == h, x, 0.0) for h in range(4)])
    join = lambda xs: [cat1(xs[i], xs[i + 1]) for i in range(0, len(xs), 2)]
    split = lambda xs: [x[:, s] for x in xs for s in (slice(0, LANES), slice(LANES, 2 * LANES))]
    akv = split(each(lambda x, y: _mm(x, stack4(y), NN, *P_U), join(a_ak), join(v)))
    a_ab4 = join(a_ab)
    t_inv4 = each(lambda x: mss_ref[2] + x * mss_ref[3], a_ab4)
    for lvl in range(1, n_levels):
        t_b = each(bf, t_inv4)
        inner = each(lambda x, t: _mm(x * mss_ref[3 + lvl], stack4(t), NN, *P_T), a_ab4, t_b)
        t_inv4 = each(lambda t, tb, w: t + _mm(tb, stack4(bf(w)), NN, *P_T), t_inv4, t_b, inner)
    t_inv = split(each(bf, t_inv4))
    pq = each(lambda t, x, y: _mm(t, cat1(_stack(x, m0), _stack(bf(y), m0)), NN, *P_U), t_inv, at, akv)
    pq_b = each(bf, pq)
    def finish_chunk(ci):
        out = []
        for p in range(N_PAIRS):
            i = ci * N_PAIRS + p
            bpq = _mm(cat0(bh[i], kh[i]), cat0(pq_b[i], cat1(zblk, v[i])), TN, *P_S)
            m_t = eye * scaled[ci]["g_end"][:, lanes(p)] + jnp.where(same_head, bpq[:, :LANES], 0.0)
            n_t = jnp.where(same_head, bpq[:, LANES:], 0.0)
            yy = _mm(a_r[i], cat0(cat1(_stack(pq_b[i][:, :LANES], m0), _stack(pq_b[i][:, LANES:], m0)),
                                  cat1(zblk2, v2[i])), NN, *P_Y)
            out.append((m_t, n_t, r_f32[i] + yy[:, :LANES], yy[:, LANES:]))
        return out

    state = [z_ref[p] for p in range(N_PAIRS)]
    upcoming = finish_chunk(0)
    for ci in range(n_chunks):
        current = upcoming
        for p in range(N_PAIRS):
            m_t, n_t, y1, y2 = current[p]
            both = _mm(cat0(m_t, y1), state[p], NN, *P_S)
            state[p] = both[:LANES] + n_t
            y_ref[ci * CHUNK:(ci + 1) * CHUNK, lanes(p)] = both[LANES:] + y2
        if ci + 1 < n_chunks:
            upcoming = finish_chunk(ci + 1)
        fill()
    for p in range(N_PAIRS):
        z_ref[p] = state[p]
    while fillers:
        fill()


def _wkv_masks():
    ri, ci = np.indices((CHUNK, 4 * CHUNK))
    ci = ci % CHUNK
    side = [ci < ri, ci <= ri, ri == ci]
    m = 1
    while m < CHUNK:
        side.append(((ri // (2 * m)) == (ci // (2 * m))) & ((ri // m) % 2 == 1) & ((ci // m) % 2 == 0))
        m *= 2
    rq, cq = np.indices((LANES, LANES))
    square = [rq == cq, (rq // HEAD) == (cq // HEAD)]
    rt, ct = np.indices((CHUNK, CHUNK))
    return (jnp.asarray(ct <= rt, BF16), jnp.asarray(np.stack(side), F32),
            jnp.asarray(np.stack(square), F32))


def _short_pairs(t_s):
    idx = {}
    for kind, inclusive in (("ab", False), ("ak", False), ("rb", True), ("rk", True)):
        for t in range(t_s):
            for s in range(t + 1 if inclusive else t):
                idx[(kind, t, s)] = len(idx)
    return idx


def _short_scale_kernel(r_ref, k_ref, v_ref, lw_ref, a_ref, b_ref,
                        xa_o, xr_o, bh_o, kh_o, vt_o, g4_o, as_o, *, t_s, n_s):
    slab = lambda ref, t: ref[t * n_s:(t + 1) * n_s, :].T
    cs = []
    for t in range(t_s):
        cs.append(slab(lw_ref, t) if t == 0 else cs[-1] + slab(lw_ref, t))
    a_t = [slab(a_ref, t) * (jnp.exp(cs[t - 1]) if t else 1.0) for t in range(t_s)]
    r_t = [slab(r_ref, t) * jnp.exp(cs[t]) for t in range(t_s)]
    b_raw = [slab(b_ref, t) for t in range(t_s)]
    k_raw = [slab(k_ref, t) for t in range(t_s)]
    g_inv = [jnp.exp(-cs[t]) for t in range(t_s)]
    b_t = [b_raw[t] * g_inv[t] for t in range(t_s)]
    k_t = [k_raw[t] * g_inv[t] for t in range(t_s)]
    for t in range(t_s):
        g_tail = jnp.exp(cs[t_s - 1] - cs[t])
        xa_o[t] = a_t[t]
        xr_o[t] = r_t[t]
        bh_o[t] = b_raw[t] * g_tail
        kh_o[t] = k_raw[t] * g_tail
        vt_o[t] = slab(v_ref, t)
    g4_o[...] = jnp.exp(cs[t_s - 1])
    head_sum = lambda x: jnp.sum(x.reshape(N_HEADS, HEAD, n_s), axis=1)
    left = {"ab": a_t, "ak": a_t, "rb": r_t, "rk": r_t}
    right = {"ab": b_t, "ak": k_t, "rb": b_t, "rk": k_t}
    for (kind, t, s), row in _short_pairs(t_s).items():
        as_o[row] = head_sum(left[kind][t] * right[kind][s])


def _short_state_kernel(s_ref, xa_ref, xr_ref, bh_ref, kh_ref, vt_ref, g4_ref, as_ref, *refs, t_s):
    out_ins, (so_ref, o_ref, y_ref) = refs[:-3], refs[-3:]
    idx = _short_pairs(t_s)
    head = pl.program_id(0)
    scal = {key: as_ref[row, pl.ds(head, 1), :] for key, row in idx.items()}

    def value_row(v, carry):
        s_v = s_ref[0, v]
        vv = [vt_ref[t, pl.ds(v, 1), :] for t in range(t_s)]
        u = []
        for t in range(t_s):
            acc = jnp.sum(s_v * xa_ref[t], axis=0, keepdims=True)
            for s in range(t):
                acc = acc + scal[("ab", t, s)] * u[s] + scal[("ak", t, s)] * vv[s]
            u.append(acc)
        for t in range(t_s):
            acc = jnp.sum(s_v * xr_ref[t], axis=0, keepdims=True)
            for s in range(t + 1):
                acc = acc + scal[("rb", t, s)] * u[s] + scal[("rk", t, s)] * vv[s]
            y_ref[t, pl.ds(head * HEAD + v, 1), :] = acc
        new = s_v * g4_ref[...]
        for s in range(t_s):
            new = new + u[s] * bh_ref[s] + vv[s] * kh_ref[s]
        so_ref[0, v] = new
        return carry

    lax.fori_loop(0, HEAD, value_row, 0, unroll=8)

    @pl.when(head == pl.num_programs(0) - 1)
    def _():
        _out_kernel(y_ref, *out_ins, o_ref)


def _short_prep_kernel(x_ref, sh_ref, ph_ref, nw_ref, win_ref, *refs, t_s, n_s, t0):
    prep_rest, refs = refs[:N_PREP_PARAMS - 2], refs[N_PREP_PARAMS - 2:]
    bon_o, sg_o, op_o, nsh_o, npl_o = refs[:5]
    scale_outs, seq_s, (p_ext, u_ext) = refs[5:12], refs[12:18], refs[18:]
    h, rstd = _norm_operands(x_ref, nw_ref)
    for finish in _prep_stages(h, rstd, win_ref, pl.program_id(0), sh_ref, ph_ref, *prep_rest, *seq_s,
                               bon_o, sg_o, op_o, nsh_o, npl_o, p_ext, u_ext,
                               stride=n_s, t0=t0, hs=sh_ref.shape[1]):
        finish()
    _short_scale_kernel(*seq_s, *scale_outs, t_s=t_s, n_s=n_s)


def _short_prep_call(x, sh_hist, pool_hist, params, *, t_s, n_s, t0):
    assert n_s % LANES == 0 and len(params) == N_PREP_PARAMS
    rows = t_s * n_s
    hs, ph = sh_hist.shape[1], _pool_hist_rows(n_s)
    assert pool_hist.shape[1] == ph
    n_scal = len(_short_pairs(t_s))
    full = lambda shape: pl.BlockSpec(shape, lambda i: tuple(0 for _ in shape))
    vec = (t_s, D_RWKV, n_s)
    out_shapes = ([(rows, D_RWKV)] * 3 + [(1, hs, D_SHIFT), (1, ph, D_POOL)] + [vec] * 5
                  + [(D_RWKV, n_s), (n_scal, N_HEADS, n_s)])
    return pl.pallas_call(
        functools.partial(_short_prep_kernel, t_s=t_s, n_s=n_s, t0=t0),
        grid=(1,),
        in_specs=[full(a.shape) for a in (x, sh_hist, pool_hist) + tuple(params)],
        out_specs=[full(s) for s in out_shapes],
        out_shape=[jax.ShapeDtypeStruct(s, F32) for s in out_shapes],
        scratch_shapes=[pltpu.VMEM((rows, D_RWKV), F32)] * 6 + [pltpu.VMEM((hs + rows, D_SHIFT), F32),
                                                                pltpu.VMEM((ph + rows, D_POOL), F32)],
        compiler_params=pltpu.CompilerParams(
            dimension_semantics=("arbitrary",), vmem_limit_bytes=VMEM_LIMIT),
        name="prep_short",
    )(x, sh_hist, pool_hist, *params)


def _short_state_call(s0, xa, xr, bh, kh, vt, g4, scal, bon, sg, op, x, out_params):
    t_s, _, n_s = xa.shape
    assert len(out_params) == N_OUT_PARAMS and x.shape[0] == t_s * n_s
    st = pl.BlockSpec((1, HEAD, HEAD, n_s), lambda h: (h, 0, 0, 0))
    per_head = pl.BlockSpec((t_s, HEAD, n_s), lambda h: (0, h, 0))
    whole = lambda a: pl.BlockSpec(a.shape, lambda h: (0,) * a.ndim)
    out_ins = (bon, sg, op, x) + tuple(out_params)
    return pl.pallas_call(
        functools.partial(_short_state_kernel, t_s=t_s),
        grid=(N_HEADS,),
        in_specs=[st] + [per_head] * 5 + [pl.BlockSpec((HEAD, n_s), lambda h: (h, 0)), whole(scal)]
        + [whole(a) for a in out_ins],
        out_specs=[st, whole(x)],
        out_shape=[jax.ShapeDtypeStruct(s0.shape, F32), jax.ShapeDtypeStruct(x.shape, F32)],
        scratch_shapes=[pltpu.VMEM(xa.shape, F32)],
        compiler_params=pltpu.CompilerParams(
            dimension_semantics=("arbitrary",), vmem_limit_bytes=VMEM_LIMIT),
        name="layer_short_state",
    )(s0, xa, xr, bh, kh, vt, g4, scal, *out_ins)


def _gated_rwkv(y, bon, sg, gw_ref, gb_ref, g_ref):
    g_ones = g_ref[...]
    mu = _seg_sum(y, g_ones) * (1.0 / HEAD)
    d = y - mu
    var = _seg_sum(d * d, g_ones) * (1.0 / HEAD)
    yn = d * lax.rsqrt(var + GN_EPS) * gw_ref[...] + gb_ref[...]
    return (yn + bon) * sg


def _out_project(o_rwkv, o_pool, x, wo_ref, nf_ref):
    out = (jnp.dot(o_rwkv.astype(BF16), wo_ref[0:D_RWKV, :], preferred_element_type=F32)
           + jnp.dot(o_pool.astype(BF16), wo_ref[D_RWKV:, :], preferred_element_type=F32))
    res = x + out
    return res * lax.rsqrt(jnp.mean(res * res, axis=-1, keepdims=True) + NORM_EPS) * nf_ref[...]


def _out_kernel(y_ref, bon_ref, sg_ref, op_ref, x_ref, gw_ref, gb_ref, wo_ref, nf_ref, g_ref, o_ref):
    if len(y_ref.shape) == 3:
        y = jnp.concatenate([y_ref[t].T for t in range(y_ref.shape[0])], axis=0)
    else:
        y = y_ref[...]
    o_rwkv = _gated_rwkv(y, bon_ref[...], sg_ref[...], gw_ref, gb_ref, g_ref)
    o_ref[...] = _out_project(o_rwkv, op_ref[...], x_ref[...], wo_ref, nf_ref)


N_PREP_PARAMS = 12
N_WKV_CONSTS = 3
N_OUT_PARAMS = 5


def _fused_kernel(*refs, n_chunks, t0):
    x_ref, x_next_ref, nw_ref, win_ref = refs[:4]
    pos = 2 + N_PREP_PARAMS
    prep_rest = refs[4:pos]
    wkv_consts = refs[pos:pos + N_WKV_CONSTS]
    pos += N_WKV_CONSTS
    gw_ref, gb_ref, wo_ref, nf_ref, g_ref = refs[pos:pos + N_OUT_PARAMS]
    pos += N_OUT_PARAMS
    o_ref, nsh_o, npl_o, so_ref = refs[pos:pos + 4]
    pos += 4
    seq_s = refs[pos:pos + 6]
    bon_s, sg_s, op_s, y_s, p_ext, u_ext, z_ref, h_s, rstd_s = refs[pos + 6:]
    j = pl.program_id(1)

    def prepare(src_ref):
        h_s[...], rstd_s[...] = _norm_operands(src_ref, nw_ref)

    @pl.when((pl.program_id(0) == 0) & (j == 0))
    def _():
        prepare(x_ref)

    _wkv_zero_state(z_ref, j == 0)
    finish_prep = _prep_stages(h_s[...], rstd_s[...], win_ref, j, None, None, *prep_rest,
                               *seq_s, bon_s, sg_s, op_s, nsh_o, npl_o, p_ext, u_ext,
                               stride=1, t0=t0, hs=nsh_o.shape[1])
    _wkv_tile(*seq_s, *wkv_consts, y_s, z_ref, n_chunks=n_chunks,
              fillers=finish_prep + [functools.partial(prepare, x_next_ref)])
    _out_kernel(y_s, bon_s, sg_s, op_s, x_ref, gw_ref, gb_ref, wo_ref, nf_ref, g_ref, o_ref)
    _wkv_store_state(z_ref, so_ref, j == pl.num_programs(1) - 1)


def _fused_call(x, prep_params, out_params, *, n_seq, n_chunks, t0):
    assert len(prep_params) == N_PREP_PARAMS and len(out_params) == N_OUT_PARAMS
    total = x.shape[0]
    rows = n_chunks * CHUNK
    tiles = total // (n_seq * rows)
    hs, ph = SUBLANES, _pool_hist_rows(1)
    assert CHUNK == HEAD and 2 * CHUNK == LANES
    wkv_consts = _wkv_masks()
    act = pl.BlockSpec((rows, D_MODEL), lambda b, j: (b * tiles + j, 0))
    ahead = pl.BlockSpec((rows, D_MODEL), lambda b, j: (jnp.minimum(b * tiles + j + 1, n_seq * tiles - 1), 0))
    per_seq = lambda shape: pl.BlockSpec((1,) + shape[1:], lambda b, j: (b,) + (0,) * (len(shape) - 1))
    const = lambda a: pl.BlockSpec(a.shape, lambda b, j: (0,) * a.ndim, pipeline_mode=pl.Buffered(1))
    consts = tuple(prep_params) + tuple(wkv_consts) + tuple(out_params)
    seq_shapes = [(n_seq, hs, D_SHIFT), (n_seq, ph, D_POOL), (n_seq, N_HEADS, HEAD, HEAD)]
    seq_scratch = pltpu.VMEM((rows, D_RWKV), F32)
    return pl.pallas_call(
        functools.partial(_fused_kernel, n_chunks=n_chunks, t0=t0),
        grid=(n_seq, tiles),
        in_specs=[act, ahead] + [const(a) for a in consts],
        out_specs=[act] + [per_seq(s) for s in seq_shapes],
        out_shape=[jax.ShapeDtypeStruct((total, D_MODEL), F32)]
        + [jax.ShapeDtypeStruct(s, F32) for s in seq_shapes],
        scratch_shapes=[seq_scratch] * 10 + [pltpu.VMEM((hs + rows, D_SHIFT), F32),
                                             pltpu.VMEM((ph + rows, D_POOL), F32),
                                             pltpu.VMEM((N_PAIRS, LANES, LANES), F32),
                                             pltpu.VMEM((rows, D_MODEL), BF16),
                                             pltpu.VMEM((rows, 1), F32)],
        compiler_params=pltpu.CompilerParams(
            dimension_semantics=("arbitrary", "arbitrary"), vmem_limit_bytes=VMEM_LIMIT),
        name="layer_long",
    )(x, x, *consts)


def _head_ones():
    i, j = np.indices((MXU_DIM, MXU_DIM))
    return jnp.asarray((i // HEAD) == (j // HEAD), BF16)


def kernel(x_prompt, x_sample, state_shift, state_wkv, state_pool, norm_w, w_in, mu_shift, w_decay_b,
           w0, w_aaa_b, a0, k_k, k_a, r_k, gn_w, gn_b, pool_w, pool_scale, w_out, norm_f):
    depth = norm_w.shape[0]
    n_p, t_p, _ = x_prompt.shape
    n_s, t_s, _ = x_sample.shape
    assert depth == 1, "a stacked trunk needs a residual-only output kernel between layers"
    assert t_p % (WKV_CHUNKS_PER_STEP * CHUNK) == 0
    g_ones = _head_ones()
    row = lambda z: z.reshape(1, -1).astype(F32)

    hp = x_prompt.astype(F32).reshape(n_p * t_p, D_MODEL)
    hs = jnp.transpose(x_sample.astype(F32), (1, 0, 2)).reshape(t_s * n_s, D_MODEL)
    outs = [[] for _ in range(6)]
    for l in range(depth):
        zl = jnp.zeros((D_LORA, D_RWKV), F32)
        w_lora = jnp.concatenate([jnp.concatenate([w_decay_b[l], zl], axis=1),
                                  jnp.concatenate([zl, w_aaa_b[l]], axis=1)], axis=0).astype(BF16)
        prep_params = (row(norm_w[l]), w_in[l].astype(BF16), row(mu_shift[l]), w_lora, row(w0[l]),
                       row(a0[l]), row(k_k[l]), row(k_a[l]), row(r_k[l]), pool_w[l].astype(BF16),
                       row(pool_scale[l]), g_ones)
        out_params = (row(gn_w[l]), row(gn_b[l]), w_out[l].astype(BF16), row(norm_f), g_ones)

        hp, nsh, npl, s_fin = _fused_call(hp, prep_params, out_params,
                                          n_seq=n_p, n_chunks=WKV_CHUNKS_PER_STEP, t0=0)
        outs[0].append(nsh[:, SUBLANES - 1])
        outs[1].append(s_fin)
        outs[2].append(npl[:, -POOL_KEEP:])

        sh_hist = state_shift[l].astype(F32)[None]
        pool_hist = jnp.transpose(state_pool[l].astype(F32), (1, 0, 2)).reshape(1, POOL_KEEP * n_s, D_POOL)
        bon, sg, op, nsh, npl, *scan_ops = _short_prep_call(hs, sh_hist, pool_hist, prep_params,
                                                            t_s=t_s, n_s=n_s, t0=PAST_LEN)
        s_fin, hs = _short_state_call(jnp.transpose(state_wkv[l].astype(F32), (1, 2, 3, 0)), *scan_ops,
                                      bon, sg, op, hs, out_params)
        s_fin = jnp.transpose(s_fin, (3, 0, 1, 2))
        outs[3].append(nsh[0])
        outs[4].append(s_fin)
        outs[5].append(jnp.transpose(npl[0].reshape(POOL_KEEP, n_s, D_POOL), (1, 0, 2)))

    y_prompt = hp.reshape(n_p, t_p, D_MODEL).astype(x_prompt.dtype)
    y_sample = jnp.transpose(hs.reshape(t_s, n_s, D_MODEL), (1, 0, 2)).astype(x_sample.dtype)
    return (y_prompt, y_sample) + tuple(jnp.stack(o, axis=0) for o in outs)
```

```python
import functools

import jax
import jax.numpy as jnp
import numpy as np
from jax import lax
from jax.experimental import pallas as pl
from jax.experimental.pallas import tpu as pltpu

F32 = jnp.float32
BF16 = jnp.bfloat16

D_MODEL = 1024
D_RWKV = 512
D_POOL = 512
HEAD = 64
N_HEADS = D_RWKV // HEAD
LANES = 128
SUBLANES = 8
MXU_DIM = 256
N_PAIRS = D_RWKV // LANES
D_LORA = 64
POOL_WINDOWS = (2, 4, 8, 16)
POOL_GROUP = D_POOL // len(POOL_WINDOWS)
POOL_KEEP = max(POOL_WINDOWS) - 1
D_SHIFT = 3 * D_RWKV + 2 * D_LORA
D_IN = D_SHIFT + D_RWKV + 2 * D_POOL
PAST_LEN = 16384
NORM_EPS = 1e-6
GN_EPS = 64e-5
L2_EPS = 1e-12
CHUNK = 64
VMEM_LIMIT = 56 * 1024 * 1024
P_A = (1, 1)
P_T = (1, 1)
P_U = (1, 1)
P_S = (1, 1)
P_Y = (1, 1)
P_CUMSUM = 2
WKV_CHUNKS_PER_STEP = 8

NN = (((1,), (0,)), ((), ()))
NT = (((1,), (1,)), ((), ()))
TN = (((0,), (0,)), ((), ()))


def _pieces(x, n):
    if x.dtype == BF16:
        return [x]
    out = []
    rem = x
    for i in range(n):
        p = rem.astype(BF16)
        out.append(p)
        if i + 1 < n:
            rem = rem - p.astype(F32)
    return out


def _mm(a, b, dims=NN, na=1, nb=1):
    ap = _pieces(a, na)
    bp = _pieces(b, nb)
    order = max(len(ap), len(bp))
    acc = None
    for i, x in enumerate(ap):
        for j, y in enumerate(bp):
            if i + j >= order:
                continue
            t = lax.dot_general(x, y, dims, preferred_element_type=F32)
            acc = t if acc is None else acc + t
    return acc


def _seg_sum(x, g_ones, pieces=1):
    cols = []
    for s in range(x.shape[1] // MXU_DIM):
        cols.append(_mm(x[:, s * MXU_DIM:(s + 1) * MXU_DIM], g_ones, NN, pieces, 1))
    return cols[0] if len(cols) == 1 else jnp.concatenate(cols, axis=1)


def _sigmoid(x):
    return 1.0 / (1.0 + jnp.exp(-x))


def _pool_hist_rows(stride):
    return -(-POOL_KEEP * stride // SUBLANES) * SUBLANES


def _norm_operands(x_ref, nw_ref):
    x = x_ref[...]
    rstd = lax.rsqrt(jnp.mean(x * x, axis=-1, keepdims=True) + NORM_EPS)
    return (x * nw_ref[...]).astype(BF16), rstd


def _prep_stages(h, rstd, win_ref, j, sh_ref, ph_ref, mu_ref, wl_ref, w0_ref, a0_ref, kk_ref,
                 ka_ref, rk_ref, pw_ref, ps_ref, g_ref,
                 r_o, k_o, v_o, lw_o, a_o, b_o, bon_o, sg_o, op_o, nsh_o, npl_o,
                 p_ext, u_ext, *, stride, t0, hs):
    rows = h.shape[0]
    ph = _pool_hist_rows(stride)
    width = MXU_DIM

    @pl.when(j == 0)
    def _():
        p_ext[0:hs, :] = jnp.zeros((hs, D_SHIFT), F32) if sh_ref is None else sh_ref[0]
        u_ext[0:ph, :] = jnp.zeros((ph, D_POOL), F32) if ph_ref is None else ph_ref[0]

    def proj_cols(c0, c1):
        return jnp.dot(h, win_ref[:, c0:c1], preferred_element_type=F32) * rstd

    def shifted(c0, c1):
        p = proj_cols(c0, c1)
        p_ext[hs:hs + rows, c0:c1] = p
        prev = p_ext[pl.ds(hs - stride, rows), c0:c1]
        return p + mu_ref[:, c0:c1] * (prev - p)

    xwa = shifted(3 * D_RWKV, D_SHIFT)
    for c in range(0, D_RWKV, width):
        r_o[:, c:c + width] = shifted(c, c + width)
    lane = lax.broadcasted_iota(jnp.int32, xwa.shape, 1)
    lora_in = jnp.where(lane < D_LORA, jnp.tanh(xwa), xwa)
    lora = jnp.dot(lora_in.astype(BF16), wl_ref[...], preferred_element_type=F32)
    lw_o[...] = (-0.6065306597126334) * _sigmoid(w0_ref[...] + lora[:, :D_RWKV])
    alpha = _sigmoid(a0_ref[...] + lora[:, D_RWKV:])
    for c in range(0, D_RWKV, width):
        sl = slice(c, c + width)
        k = shifted(D_RWKV + c, D_RWKV + c + width)
        k_o[:, sl] = k * (1.0 + (alpha[:, sl] - 1.0) * ka_ref[:, sl])
        kk = k * kk_ref[:, sl]
        kk = kk * lax.rsqrt(_seg_sum(kk * kk, g_ref[...]) + L2_EPS)
        a_o[:, sl] = -kk
        b_o[:, sl] = kk * alpha[:, sl]
        v_o[:, sl] = shifted(2 * D_RWKV + c, 2 * D_RWKV + c + width)
    last_p = p_ext[rows:rows + hs, :]
    p_ext[0:hs, :] = last_p
    nsh_o[0] = last_p
    for c in range(0, D_RWKV, width):
        g_rwkv = proj_cols(D_SHIFT + c, D_SHIFT + c + width)
        sg_o[:, c:c + width] = g_rwkv * _sigmoid(g_rwkv)
    for c in range(0, D_POOL, width):
        u_ext[ph:ph + rows, c:c + width] = proj_cols(D_SHIFT + D_RWKV + c, D_SHIFT + D_RWKV + c + width)
        op_o[:, c:c + width] = proj_cols(D_SHIFT + D_RWKV + D_POOL + c,
                                         D_SHIFT + D_RWKV + D_POOL + c + width)

    def bonus():
        bon_o[...] = _seg_sum(r_o[...] * k_o[...] * rk_ref[...], g_ref[...]) * v_o[...]

    def pool(gi, win):
        sl = slice(gi * POOL_GROUP, (gi + 1) * POOL_GROUP)
        row = lax.broadcasted_iota(jnp.int32, (rows, 1), 0)
        pos = t0 + (j * rows + row) // stride
        u = u_ext[ph:ph + rows, sl]
        tot = u
        for back in range(1, win):
            tot = tot + u_ext[pl.ds(ph - back * stride, rows), sl]
        inv_cnt = 1.0 / jnp.minimum(pos + 1, win).astype(F32)
        d = tot * inv_cnt - u
        o = jnp.dot(d.astype(BF16), pw_ref[gi], preferred_element_type=F32)
        gp = op_o[:, sl]
        op_o[:, sl] = o * ps_ref[:, sl] * (gp * _sigmoid(gp))

    def carry_pool():
        last_u = u_ext[rows:rows + ph, :]
        u_ext[0:ph, :] = last_u
        npl_o[0] = last_u

    pools = [functools.partial(pool, gi, win) for gi, win in enumerate(POOL_WINDOWS)]
    return [bonus] + pools + [carry_pool]


def _stack(x, m0):
    return jnp.concatenate([jnp.where(m0, x, 0.0), jnp.where(m0, 0.0, x)], axis=0)


def _wkv_zero_state(z_ref, is_first):
    @pl.when(is_first)
    def _():
        z_ref[...] = jnp.zeros(z_ref.shape, F32)


def _wkv_store_state(z_ref, so_ref, is_last):
    @pl.when(is_last)
    def _():
        for p in range(N_PAIRS):
            s2 = z_ref[p].T
            so_ref[0, 2 * p] = s2[:HEAD, :HEAD]
            so_ref[0, 2 * p + 1] = s2[HEAD:, HEAD:]


def _wkv_tile(r_ref, k_ref, v_ref, lw_ref, a_ref, b_ref, tri_ref, mss_ref, msq_ref, y_ref, z_ref, *,
              n_chunks, fillers=()):
    fillers = list(fillers)
    fill = lambda: fillers.pop(0)() if fillers else None
    lane = lax.broadcasted_iota(jnp.int32, (CHUNK, LANES), 1)
    m0 = lane < HEAD
    strict = mss_ref[0][:, :LANES] > 0.5
    incl2 = mss_ref[1] > 0.5
    n_levels = mss_ref.shape[0] - 3
    lane_head4 = lax.broadcasted_iota(jnp.int32, (CHUNK, 2 * LANES), 1) // HEAD
    eye = msq_ref[0]
    same_head = msq_ref[1] > 0.5
    zblk = jnp.zeros((CHUNK, LANES), BF16)
    zblk2 = jnp.zeros((2 * CHUNK, LANES), BF16)
    bf = lambda x: x.astype(BF16)
    lanes = lambda p: slice(p * LANES, (p + 1) * LANES)
    each = lambda fn, *lists: [fn(*args) for args in zip(*lists)]
    cat0 = lambda *xs: jnp.concatenate(xs, axis=0)
    cat1 = lambda *xs: jnp.concatenate(xs, axis=1)

    stack4 = lambda x: cat0(*[jnp.where(lane_head4 == h, x, 0.0) for h in range(4)])
    join = lambda xs: [cat1(xs[i], xs[i + 1]) for i in range(0, len(xs), 2)]
    split = lambda xs: [x[:, s] for x in xs for s in (slice(0, LANES), slice(LANES, 2 * LANES))]

    def solve(chunk_ids):
        scaled = []
        for ci in chunk_ids:
            rows = slice(ci * CHUNK, (ci + 1) * CHUNK)
            lw = lw_ref[rows, :]
            cs = _mm(tri_ref[...], lw, NN, 1, P_CUMSUM)
            cs_end = cs[CHUNK - 1:CHUNK, :]
            g_inv = jnp.exp(-cs)
            g_tail = jnp.exp(cs_end - cs)
            r_t = r_ref[rows, :] * jnp.exp(cs)
            scaled.append(dict(
                g_end=jnp.exp(cs_end), r_f32=r_t, r_t=bf(r_t),
                a_t=bf(a_ref[rows, :] * jnp.exp(cs - lw)),
                b_t=bf(b_ref[rows, :] * g_inv), k_t=bf(k_ref[rows, :] * g_inv),
                b_h=bf(b_ref[rows, :] * g_tail), k_h=bf(k_ref[rows, :] * g_tail), v=bf(v_ref[rows, :])))
        chains = [(k, p) for k in range(len(scaled)) for p in range(N_PAIRS)]
        pair = lambda name: [scaled[k][name][:, lanes(p)] for k, p in chains]
        at, rt, bt, kt, bh, kh, v, r_f32, g_end = (pair(n) for n in ("a_t", "r_t", "b_t", "k_t", "b_h", "k_h",
                                                                     "v", "r_f32", "g_end"))
        v2 = each(lambda x: _stack(x, m0), v)
        yield
        scores = each(lambda a, r, b, k: _mm(cat0(a, r), cat0(_stack(b, m0), _stack(k, m0)), NT, *P_A),
                      at, rt, bt, kt)
        a_ab = each(lambda s: jnp.where(strict, s[:CHUNK, :LANES], 0.0), scores)
        a_ak = each(lambda s: jnp.where(strict, s[:CHUNK, LANES:], 0.0), scores)
        a_r = each(lambda s: jnp.where(incl2, s[CHUNK:, :], 0.0), scores)
        akv = split(each(lambda x, y: _mm(x, stack4(y), NN, *P_U), join(a_ak), join(v)))
        yield
        a_ab4 = join(a_ab)
        t_inv4 = each(lambda x: mss_ref[2] + x * mss_ref[3], a_ab4)
        for lvl in range(1, n_levels):
            t_b = each(bf, t_inv4)
            inner = each(lambda x, t: _mm(x * mss_ref[3 + lvl], stack4(t), NN, *P_T), a_ab4, t_b)
            t_inv4 = each(lambda t, tb, w: t + _mm(tb, stack4(bf(w)), NN, *P_T), t_inv4, t_b, inner)
            yield
        t_inv = split(each(bf, t_inv4))
        pq = each(lambda t, x, y: _mm(t, cat1(_stack(x, m0), _stack(bf(y), m0)), NN, *P_U), t_inv, at, akv)
        return dict(pq_b=each(bf, pq), a_r=a_r, bh=bh, kh=kh, v=v, v2=v2, r_f32=r_f32, g_end=g_end)

    def advance(gen, done):
        if not done:
            try:
                next(gen)
            except StopIteration as stop:
                done.append(stop.value)

    half = n_chunks // 2
    first, second = solve(range(half)), solve(range(half, n_chunks))
    first_done, second_done = [], []
    advance(first, first_done)
    advance(first, first_done)
    while not (first_done and second_done):
        advance(first, first_done)
        advance(second, second_done)
    solved = {name: first_done[0][name] + second_done[0][name] for name in first_done[0]}
    pq_b, a_r, bh, kh, v, v2, r_f32, g_end = (solved[n] for n in ("pq_b", "a_r", "bh", "kh", "v", "v2",
                                                                  "r_f32", "g_end"))

    def finish_chunk(ci):
        out = []
        for p in range(N_PAIRS):
            i = ci * N_PAIRS + p
            bpq = _mm(cat0(bh[i], kh[i]), cat0(pq_b[i], cat1(zblk, v[i])), TN, *P_S)
            m_t = eye * g_end[i] + jnp.where(same_head, bpq[:, :LANES], 0.0)
            n_t = jnp.where(same_head, bpq[:, LANES:], 0.0)
            yy = _mm(a_r[i], cat0(cat1(_stack(pq_b[i][:, :LANES], m0), _stack(pq_b[i][:, LANES:], m0)),
                                  cat1(zblk2, v2[i])), NN, *P_Y)
            out.append((m_t, n_t, r_f32[i] + yy[:, :LANES], yy[:, LANES:]))
        return out

    state = [z_ref[p] for p in range(N_PAIRS)]
    upcoming = finish_chunk(0)
    for ci in range(n_chunks):
        current = upcoming
        for p in range(N_PAIRS):
            m_t, n_t, y1, y2 = current[p]
            both = _mm(cat0(m_t, y1), state[p], NN, *P_S)
            state[p] = both[:LANES] + n_t
            y_ref[ci * CHUNK:(ci + 1) * CHUNK, lanes(p)] = both[LANES:] + y2
        if ci + 1 < n_chunks:
            upcoming = finish_chunk(ci + 1)
        fill()
    for p in range(N_PAIRS):
        z_ref[p] = state[p]
    while fillers:
        fill()


def _wkv_masks():
    ri, ci = np.indices((CHUNK, 4 * CHUNK))
    ci = ci % CHUNK
    side = [ci < ri, ci <= ri, ri == ci]
    m = 1
    while m < CHUNK:
        side.append(((ri // (2 * m)) == (ci // (2 * m))) & ((ri // m) % 2 == 1) & ((ci // m) % 2 == 0))
        m *= 2
    rq, cq = np.indices((LANES, LANES))
    square = [rq == cq, (rq // HEAD) == (cq // HEAD)]
    rt, ct = np.indices((CHUNK, CHUNK))
    return (jnp.asarray(ct <= rt, BF16), jnp.asarray(np.stack(side), F32),
            jnp.asarray(np.stack(square), F32))


def _short_pairs(t_s):
    idx = {}
    for kind, inclusive in (("ab", False), ("ak", False), ("rb", True), ("rk", True)):
        for t in range(t_s):
            for s in range(t + 1 if inclusive else t):
                idx[(kind, t, s)] = len(idx)
    return idx


def _short_scale_kernel(r_ref, k_ref, v_ref, lw_ref, a_ref, b_ref,
                        xa_o, xr_o, bh_o, kh_o, vt_o, g4_o, as_o, *, t_s, n_s):
    slab = lambda ref, t: ref[t * n_s:(t + 1) * n_s, :].T
    cs = []
    for t in range(t_s):
        cs.append(slab(lw_ref, t) if t == 0 else cs[-1] + slab(lw_ref, t))
    a_t = [slab(a_ref, t) * (jnp.exp(cs[t - 1]) if t else 1.0) for t in range(t_s)]
    r_t = [slab(r_ref, t) * jnp.exp(cs[t]) for t in range(t_s)]
    b_raw = [slab(b_ref, t) for t in range(t_s)]
    k_raw = [slab(k_ref, t) for t in range(t_s)]
    g_inv = [jnp.exp(-cs[t]) for t in range(t_s)]
    b_t = [b_raw[t] * g_inv[t] for t in range(t_s)]
    k_t = [k_raw[t] * g_inv[t] for t in range(t_s)]
    for t in range(t_s):
        g_tail = jnp.exp(cs[t_s - 1] - cs[t])
        xa_o[t] = a_t[t]
        xr_o[t] = r_t[t]
        bh_o[t] = b_raw[t] * g_tail
        kh_o[t] = k_raw[t] * g_tail
        vt_o[t] = slab(v_ref, t)
    g4_o[...] = jnp.exp(cs[t_s - 1])
    head_sum = lambda x: jnp.sum(x.reshape(N_HEADS, HEAD, n_s), axis=1)
    left = {"ab": a_t, "ak": a_t, "rb": r_t, "rk": r_t}
    right = {"ab": b_t, "ak": k_t, "rb": b_t, "rk": k_t}
    for (kind, t, s), row in _short_pairs(t_s).items():
        as_o[row] = head_sum(left[kind][t] * right[kind][s])


def _short_state_kernel(s_ref, xa_ref, xr_ref, bh_ref, kh_ref, vt_ref, g4_ref, as_ref, *refs, t_s):
    out_ins, (so_ref, o_ref, y_ref) = refs[:-3], refs[-3:]
    idx = _short_pairs(t_s)
    head = pl.program_id(0)
    scal = {key: as_ref[row, pl.ds(head, 1), :] for key, row in idx.items()}

    def value_row(v, carry):
        s_v = s_ref[0, v]
        vv = [vt_ref[t, pl.ds(v, 1), :] for t in range(t_s)]
        u = []
        for t in range(t_s):
            acc = jnp.sum(s_v * xa_ref[t], axis=0, keepdims=True)
            for s in range(t):
                acc = acc + scal[("ab", t, s)] * u[s] + scal[("ak", t, s)] * vv[s]
            u.append(acc)
        for t in range(t_s):
            acc = jnp.sum(s_v * xr_ref[t], axis=0, keepdims=True)
            for s in range(t + 1):
                acc = acc + scal[("rb", t, s)] * u[s] + scal[("rk", t, s)] * vv[s]
            y_ref[t, pl.ds(head * HEAD + v, 1), :] = acc
        new = s_v * g4_ref[...]
        for s in range(t_s):
            new = new + u[s] * bh_ref[s] + vv[s] * kh_ref[s]
        so_ref[0, v] = new
        return carry

    lax.fori_loop(0, HEAD, value_row, 0, unroll=8)

    @pl.when(head == pl.num_programs(0) - 1)
    def _():
        _out_kernel(y_ref, *out_ins, o_ref)


def _short_prep_kernel(x_ref, sh_ref, ph_ref, nw_ref, win_ref, *refs, t_s, n_s, t0):
    prep_rest, refs = refs[:N_PREP_PARAMS - 2], refs[N_PREP_PARAMS - 2:]
    bon_o, sg_o, op_o, nsh_o, npl_o = refs[:5]
    scale_outs, seq_s, (p_ext, u_ext) = refs[5:12], refs[12:18], refs[18:]
    h, rstd = _norm_operands(x_ref, nw_ref)
    for finish in _prep_stages(h, rstd, win_ref, pl.program_id(0), sh_ref, ph_ref, *prep_rest, *seq_s,
                               bon_o, sg_o, op_o, nsh_o, npl_o, p_ext, u_ext,
                               stride=n_s, t0=t0, hs=sh_ref.shape[1]):
        finish()
    _short_scale_kernel(*seq_s, *scale_outs, t_s=t_s, n_s=n_s)


def _short_prep_call(x, sh_hist, pool_hist, params, *, t_s, n_s, t0):
    assert n_s % LANES == 0 and len(params) == N_PREP_PARAMS
    rows = t_s * n_s
    hs, ph = sh_hist.shape[1], _pool_hist_rows(n_s)
    assert pool_hist.shape[1] == ph
    n_scal = len(_short_pairs(t_s))
    full = lambda shape: pl.BlockSpec(shape, lambda i: tuple(0 for _ in shape))
    vec = (t_s, D_RWKV, n_s)
    out_shapes = ([(rows, D_RWKV)] * 3 + [(1, hs, D_SHIFT), (1, ph, D_POOL)] + [vec] * 5
                  + [(D_RWKV, n_s), (n_scal, N_HEADS, n_s)])
    return pl.pallas_call(
        functools.partial(_short_prep_kernel, t_s=t_s, n_s=n_s, t0=t0),
        grid=(1,),
        in_specs=[full(a.shape) for a in (x, sh_hist, pool_hist) + tuple(params)],
        out_specs=[full(s) for s in out_shapes],
        out_shape=[jax.ShapeDtypeStruct(s, F32) for s in out_shapes],
        scratch_shapes=[pltpu.VMEM((rows, D_RWKV), F32)] * 6 + [pltpu.VMEM((hs + rows, D_SHIFT), F32),
                                                                pltpu.VMEM((ph + rows, D_POOL), F32)],
        compiler_params=pltpu.CompilerParams(
            dimension_semantics=("arbitrary",), vmem_limit_bytes=VMEM_LIMIT),
        name="prep_short",
    )(x, sh_hist, pool_hist, *params)


def _short_state_call(s0, xa, xr, bh, kh, vt, g4, scal, bon, sg, op, x, out_params):
    t_s, _, n_s = xa.shape
    assert len(out_params) == N_OUT_PARAMS and x.shape[0] == t_s * n_s
    st = pl.BlockSpec((1, HEAD, HEAD, n_s), lambda h: (h, 0, 0, 0))
    per_head = pl.BlockSpec((t_s, HEAD, n_s), lambda h: (0, h, 0))
    whole = lambda a: pl.BlockSpec(a.shape, lambda h: (0,) * a.ndim)
    out_ins = (bon, sg, op, x) + tuple(out_params)
    return pl.pallas_call(
        functools.partial(_short_state_kernel, t_s=t_s),
        grid=(N_HEADS,),
        in_specs=[st] + [per_head] * 5 + [pl.BlockSpec((HEAD, n_s), lambda h: (h, 0)), whole(scal)]
        + [whole(a) for a in out_ins],
        out_specs=[st, whole(x)],
        out_shape=[jax.ShapeDtypeStruct(s0.shape, F32), jax.ShapeDtypeStruct(x.shape, F32)],
        scratch_shapes=[pltpu.VMEM(xa.shape, F32)],
        compiler_params=pltpu.CompilerParams(
            dimension_semantics=("arbitrary",), vmem_limit_bytes=VMEM_LIMIT),
        name="layer_short_state",
    )(s0, xa, xr, bh, kh, vt, g4, scal, *out_ins)


def _gated_rwkv(y, bon, sg, gw_ref, gb_ref, g_ref):
    g_ones = g_ref[...]
    mu = _seg_sum(y, g_ones) * (1.0 / HEAD)
    d = y - mu
    var = _seg_sum(d * d, g_ones) * (1.0 / HEAD)
    yn = d * lax.rsqrt(var + GN_EPS) * gw_ref[...] + gb_ref[...]
    return (yn + bon) * sg


def _out_project(o_rwkv, o_pool, x, wo_ref, nf_ref):
    out = (jnp.dot(o_rwkv.astype(BF16), wo_ref[0:D_RWKV, :], preferred_element_type=F32)
           + jnp.dot(o_pool.astype(BF16), wo_ref[D_RWKV:, :], preferred_element_type=F32))
    res = x + out
    return res * lax.rsqrt(jnp.mean(res * res, axis=-1, keepdims=True) + NORM_EPS) * nf_ref[...]


def _out_kernel(y_ref, bon_ref, sg_ref, op_ref, x_ref, gw_ref, gb_ref, wo_ref, nf_ref, g_ref, o_ref):
    if len(y_ref.shape) == 3:
        y = jnp.concatenate([y_ref[t].T for t in range(y_ref.shape[0])], axis=0)
    else:
        y = y_ref[...]
    o_rwkv = _gated_rwkv(y, bon_ref[...], sg_ref[...], gw_ref, gb_ref, g_ref)
    o_ref[...] = _out_project(o_rwkv, op_ref[...], x_ref[...], wo_ref, nf_ref)


N_PREP_PARAMS = 12
N_WKV_CONSTS = 3
N_OUT_PARAMS = 5


def _fused_kernel(*refs, n_chunks, t0):
    x_ref, x_next_ref, nw_ref, win_ref = refs[:4]
    pos = 2 + N_PREP_PARAMS
    prep_rest = refs[4:pos]
    wkv_consts = refs[pos:pos + N_WKV_CONSTS]
    pos += N_WKV_CONSTS
    gw_ref, gb_ref, wo_ref, nf_ref, g_ref = refs[pos:pos + N_OUT_PARAMS]
    pos += N_OUT_PARAMS
    o_ref, nsh_o, npl_o, so_ref = refs[pos:pos + 4]
    pos += 4
    seq_s = refs[pos:pos + 6]
    bon_s, sg_s, op_s, y_s, p_ext, u_ext, z_ref, h_s, rstd_s = refs[pos + 6:]
    j = pl.program_id(1)

    def prepare(src_ref):
        h_s[...], rstd_s[...] = _norm_operands(src_ref, nw_ref)

    @pl.when((pl.program_id(0) == 0) & (j == 0))
    def _():
        prepare(x_ref)

    _wkv_zero_state(z_ref, j == 0)
    finish_prep = _prep_stages(h_s[...], rstd_s[...], win_ref, j, None, None, *prep_rest,
                               *seq_s, bon_s, sg_s, op_s, nsh_o, npl_o, p_ext, u_ext,
                               stride=1, t0=t0, hs=nsh_o.shape[1])
    _wkv_tile(*seq_s, *wkv_consts, y_s, z_ref, n_chunks=n_chunks,
              fillers=finish_prep + [functools.partial(prepare, x_next_ref)])
    _out_kernel(y_s, bon_s, sg_s, op_s, x_ref, gw_ref, gb_ref, wo_ref, nf_ref, g_ref, o_ref)
    _wkv_store_state(z_ref, so_ref, j == pl.num_programs(1) - 1)


def _fused_call(x, prep_params, out_params, *, n_seq, n_chunks, t0):
    assert len(prep_params) == N_PREP_PARAMS and len(out_params) == N_OUT_PARAMS
    total = x.shape[0]
    rows = n_chunks * CHUNK
    tiles = total // (n_seq * rows)
    hs, ph = SUBLANES, _pool_hist_rows(1)
    assert CHUNK == HEAD and 2 * CHUNK == LANES
    wkv_consts = _wkv_masks()
    act = pl.BlockSpec((rows, D_MODEL), lambda b, j: (b * tiles + j, 0))
    ahead = pl.BlockSpec((rows, D_MODEL), lambda b, j: (jnp.minimum(b * tiles + j + 1, n_seq * tiles - 1), 0))
    per_seq = lambda shape: pl.BlockSpec((1,) + shape[1:], lambda b, j: (b,) + (0,) * (len(shape) - 1))
    const = lambda a: pl.BlockSpec(a.shape, lambda b, j: (0,) * a.ndim, pipeline_mode=pl.Buffered(1))
    consts = tuple(prep_params) + tuple(wkv_consts) + tuple(out_params)
    seq_shapes = [(n_seq, hs, D_SHIFT), (n_seq, ph, D_POOL), (n_seq, N_HEADS, HEAD, HEAD)]
    seq_scratch = pltpu.VMEM((rows, D_RWKV), F32)
    return pl.pallas_call(
        functools.partial(_fused_kernel, n_chunks=n_chunks, t0=t0),
        grid=(n_seq, tiles),
        in_specs=[act, ahead] + [const(a) for a in consts],
        out_specs=[act] + [per_seq(s) for s in seq_shapes],
        out_shape=[jax.ShapeDtypeStruct((total, D_MODEL), F32)]
        + [jax.ShapeDtypeStruct(s, F32) for s in seq_shapes],
        scratch_shapes=[seq_scratch] * 10 + [pltpu.VMEM((hs + rows, D_SHIFT), F32),
                                             pltpu.VMEM((ph + rows, D_POOL), F32),
                                             pltpu.VMEM((N_PAIRS, LANES, LANES), F32),
                                             pltpu.VMEM((rows, D_MODEL), BF16),
                                             pltpu.VMEM((rows, 1), F32)],
        compiler_params=pltpu.CompilerParams(
            dimension_semantics=("arbitrary", "arbitrary"), vmem_limit_bytes=VMEM_LIMIT),
        name="layer_long",
    )(x, x, *consts)


def _head_ones():
    i, j = np.indices((MXU_DIM, MXU_DIM))
    return jnp.asarray((i // HEAD) == (j // HEAD), BF16)


def kernel(x_prompt, x_sample, state_shift, state_wkv, state_pool, norm_w, w_in, mu_shift, w_decay_b,
           w0, w_aaa_b, a0, k_k, k_a, r_k, gn_w, gn_b, pool_w, pool_scale, w_out, norm_f):
    depth = norm_w.shape[0]
    n_p, t_p, _ = x_prompt.shape
    n_s, t_s, _ = x_sample.shape
    assert depth == 1, "a stacked trunk needs a residual-only output kernel between layers"
    assert t_p % (WKV_CHUNKS_PER_STEP * CHUNK) == 0
    g_ones = _head_ones()
    row = lambda z: z.reshape(1, -1).astype(F32)

    hp = x_prompt.astype(F32).reshape(n_p * t_p, D_MODEL)
    hs = jnp.transpose(x_sample.astype(F32), (1, 0, 2)).reshape(t_s * n_s, D_MODEL)
    outs = [[] for _ in range(6)]
    for l in range(depth):
        zl = jnp.zeros((D_LORA, D_RWKV), F32)
        w_lora = jnp.concatenate([jnp.concatenate([w_decay_b[l], zl], axis=1),
                                  jnp.concatenate([zl, w_aaa_b[l]], axis=1)], axis=0).astype(BF16)
        prep_params = (row(norm_w[l]), w_in[l].astype(BF16), row(mu_shift[l]), w_lora, row(w0[l]),
                       row(a0[l]), row(k_k[l]), row(k_a[l]), row(r_k[l]), pool_w[l].astype(BF16),
                       row(pool_scale[l]), g_ones)
        out_params = (row(gn_w[l]), row(gn_b[l]), w_out[l].astype(BF16), row(norm_f), g_ones)

        hp, nsh, npl, s_fin = _fused_call(hp, prep_params, out_params,
                                          n_seq=n_p, n_chunks=WKV_CHUNKS_PER_STEP, t0=0)
        outs[0].append(nsh[:, SUBLANES - 1])
        outs[1].append(s_fin)
        outs[2].append(npl[:, -POOL_KEEP:])

        sh_hist = state_shift[l].astype(F32)[None]
        pool_hist = jnp.transpose(state_pool[l].astype(F32), (1, 0, 2)).reshape(1, POOL_KEEP * n_s, D_POOL)
        bon, sg, op, nsh, npl, *scan_ops = _short_prep_call(hs, sh_hist, pool_hist, prep_params,
                                                            t_s=t_s, n_s=n_s, t0=PAST_LEN)
        s_fin, hs = _short_state_call(jnp.transpose(state_wkv[l].astype(F32), (1, 2, 3, 0)), *scan_ops,
                                      bon, sg, op, hs, out_params)
        s_fin = jnp.transpose(s_fin, (3, 0, 1, 2))
        outs[3].append(nsh[0])
        outs[4].append(s_fin)
        outs[5].append(jnp.transpose(npl[0].reshape(POOL_KEEP, n_s, D_POOL), (1, 0, 2)))

    y_prompt = hp.reshape(n_p, t_p, D_MODEL).astype(x_prompt.dtype)
    y_sample = jnp.transpose(hs.reshape(t_s, n_s, D_MODEL), (1, 0, 2)).astype(x_sample.dtype)
    return (y_prompt, y_sample) + tuple(jnp.stack(o, axis=0) for o in outs)
```

```python
import functools

import jax
import jax.numpy as jnp
import numpy as np
from jax import lax
from jax.experimental import pallas as pl
from jax.experimental.pallas import tpu as pltpu

F32 = jnp.float32
BF16 = jnp.bfloat16

D_MODEL = 1024
D_RWKV = 512
D_POOL = 512
HEAD = 64
N_HEADS = D_RWKV // HEAD
LANES = 128
SUBLANES = 8
MXU_DIM = 256
N_PAIRS = D_RWKV // LANES
D_LORA = 64
POOL_WINDOWS = (2, 4, 8, 16)
POOL_GROUP = D_POOL // len(POOL_WINDOWS)
POOL_KEEP = max(POOL_WINDOWS) - 1
D_SHIFT = 3 * D_RWKV + 2 * D_LORA
D_IN = D_SHIFT + D_RWKV + 2 * D_POOL
PAST_LEN = 16384
NORM_EPS = 1e-6
GN_EPS = 64e-5
L2_EPS = 1e-12
CHUNK = 64
VMEM_LIMIT = 56 * 1024 * 1024
P_A = (1, 1)
P_T = (1, 1)
P_U = (1, 1)
P_S = (1, 1)
P_Y = (1, 1)
P_CUMSUM = 2
WKV_CHUNKS_PER_STEP = 8

NN = (((1,), (0,)), ((), ()))
NT = (((1,), (1,)), ((), ()))
TN = (((0,), (0,)), ((), ()))


def _pieces(x, n):
    if x.dtype == BF16:
        return [x]
    out = []
    rem = x
    for i in range(n):
        p = rem.astype(BF16)
        out.append(p)
        if i + 1 < n:
            rem = rem - p.astype(F32)
    return out


def _mm(a, b, dims=NN, na=1, nb=1):
    ap = _pieces(a, na)
    bp = _pieces(b, nb)
    order = max(len(ap), len(bp))
    acc = None
    for i, x in enumerate(ap):
        for j, y in enumerate(bp):
            if i + j >= order:
                continue
            t = lax.dot_general(x, y, dims, preferred_element_type=F32)
            acc = t if acc is None else acc + t
    return acc


def _seg_sum(x, g_ones, pieces=1):
    cols = []
    for s in range(x.shape[1] // MXU_DIM):
        cols.append(_mm(x[:, s * MXU_DIM:(s + 1) * MXU_DIM], g_ones, NN, pieces, 1))
    return cols[0] if len(cols) == 1 else jnp.concatenate(cols, axis=1)


def _sigmoid(x):
    return 1.0 / (1.0 + jnp.exp(-x))


def _pool_hist_rows(stride):
    return -(-POOL_KEEP * stride // SUBLANES) * SUBLANES


def _norm_operands(x_ref, nw_ref):
    x = x_ref[...]
    rstd = lax.rsqrt(jnp.mean(x * x, axis=-1, keepdims=True) + NORM_EPS)
    return (x * nw_ref[...]).astype(BF16), rstd


def _prep_stages(h, rstd, win_ref, j, sh_ref, ph_ref, mu_ref, wl_ref, w0_ref, a0_ref, kk_ref,
                 ka_ref, rk_ref, pw_ref, ps_ref, g_ref,
                 r_o, k_o, v_o, lw_o, a_o, b_o, bon_o, sg_o, op_o, nsh_o, npl_o,
                 p_ext, u_ext, *, stride, t0, hs):
    rows = h.shape[0]
    ph = _pool_hist_rows(stride)
    width = MXU_DIM

    @pl.when(j == 0)
    def _():
        p_ext[0:hs, :] = jnp.zeros((hs, D_SHIFT), F32) if sh_ref is None else sh_ref[0]
        u_ext[0:ph, :] = jnp.zeros((ph, D_POOL), F32) if ph_ref is None else ph_ref[0]

    def proj_cols(c0, c1):
        return jnp.dot(h, win_ref[:, c0:c1], preferred_element_type=F32) * rstd

    def shifted(c0, c1):
        p = proj_cols(c0, c1)
        p_ext[hs:hs + rows, c0:c1] = p
        prev = p_ext[pl.ds(hs - stride, rows), c0:c1]
        return p + mu_ref[:, c0:c1] * (prev - p)

    xwa = shifted(3 * D_RWKV, D_SHIFT)
    for c in range(0, D_RWKV, width):
        r_o[:, c:c + width] = shifted(c, c + width)
    lane = lax.broadcasted_iota(jnp.int32, xwa.shape, 1)
    lora_in = jnp.where(lane < D_LORA, jnp.tanh(xwa), xwa)
    lora = jnp.dot(lora_in.astype(BF16), wl_ref[...], preferred_element_type=F32)
    lw_o[...] = (-0.6065306597126334) * _sigmoid(w0_ref[...] + lora[:, :D_RWKV])
    alpha = _sigmoid(a0_ref[...] + lora[:, D_RWKV:])
    for c in range(0, D_RWKV, width):
        sl = slice(c, c + width)
        k = shifted(D_RWKV + c, D_RWKV + c + width)
        k_o[:, sl] = k * (1.0 + (alpha[:, sl] - 1.0) * ka_ref[:, sl])
        kk = k * kk_ref[:, sl]
        kk = kk * lax.rsqrt(_seg_sum(kk * kk, g_ref[...]) + L2_EPS)
        a_o[:, sl] = -kk
        b_o[:, sl] = kk * alpha[:, sl]
        v_o[:, sl] = shifted(2 * D_RWKV + c, 2 * D_RWKV + c + width)
    last_p = p_ext[rows:rows + hs, :]
    p_ext[0:hs, :] = last_p
    nsh_o[0] = last_p
    for c in range(0, D_RWKV, width):
        g_rwkv = proj_cols(D_SHIFT + c, D_SHIFT + c + width)
        sg_o[:, c:c + width] = g_rwkv * _sigmoid(g_rwkv)
    for c in range(0, D_POOL, width):
        u_ext[ph:ph + rows, c:c + width] = proj_cols(D_SHIFT + D_RWKV + c, D_SHIFT + D_RWKV + c + width)
        op_o[:, c:c + width] = proj_cols(D_SHIFT + D_RWKV + D_POOL + c,
                                         D_SHIFT + D_RWKV + D_POOL + c + width)

    def bonus():
        bon_o[...] = _seg_sum(r_o[...] * k_o[...] * rk_ref[...], g_ref[...]) * v_o[...]

    def pool(gi, win):
        sl = slice(gi * POOL_GROUP, (gi + 1) * POOL_GROUP)
        row = lax.broadcasted_iota(jnp.int32, (rows, 1), 0)
        pos = t0 + (j * rows + row) // stride
        u = u_ext[ph:ph + rows, sl]
        tot = u
        for back in range(1, win):
            tot = tot + u_ext[pl.ds(ph - back * stride, rows), sl]
        inv_cnt = 1.0 / jnp.minimum(pos + 1, win).astype(F32)
        d = tot * inv_cnt - u
        o = jnp.dot(d.astype(BF16), pw_ref[gi], preferred_element_type=F32)
        gp = op_o[:, sl]
        op_o[:, sl] = o * ps_ref[:, sl] * (gp * _sigmoid(gp))

    def carry_pool():
        last_u = u_ext[rows:rows + ph, :]
        u_ext[0:ph, :] = last_u
        npl_o[0] = last_u

    pools = [functools.partial(pool, gi, win) for gi, win in enumerate(POOL_WINDOWS)]
    return [bonus] + pools + [carry_pool]


def _stack(x, m0):
    return jnp.concatenate([jnp.where(m0, x, 0.0), jnp.where(m0, 0.0, x)], axis=0)


def _wkv_zero_state(z_ref, is_first):
    @pl.when(is_first)
    def _():
        z_ref[...] = jnp.zeros(z_ref.shape, F32)


def _wkv_store_state(z_ref, so_ref, is_last):
    @pl.when(is_last)
    def _():
        for p in range(N_PAIRS):
            s2 = z_ref[p].T
            so_ref[0, 2 * p] = s2[:HEAD, :HEAD]
            so_ref[0, 2 * p + 1] = s2[HEAD:, HEAD:]


def _wkv_tile(r_ref, k_ref, v_ref, lw_ref, a_ref, b_ref, tri_ref, mss_ref, msq_ref, y_ref, z_ref, *,
              n_chunks, fillers=(), early_fillers=0):
    fillers = list(fillers)
    fill = lambda: fillers.pop(0)() if fillers else None
    lane = lax.broadcasted_iota(jnp.int32, (CHUNK, LANES), 1)
    m0 = lane < HEAD
    strict = mss_ref[0][:, :LANES] > 0.5
    incl2 = mss_ref[1] > 0.5
    n_levels = mss_ref.shape[0] - 3
    lane_head4 = lax.broadcasted_iota(jnp.int32, (CHUNK, 2 * LANES), 1) // HEAD
    eye = msq_ref[0]
    same_head = msq_ref[1] > 0.5
    zblk = jnp.zeros((CHUNK, LANES), BF16)
    zblk2 = jnp.zeros((2 * CHUNK, LANES), BF16)
    bf = lambda x: x.astype(BF16)
    lanes = lambda p: slice(p * LANES, (p + 1) * LANES)
    each = lambda fn, *lists: [fn(*args) for args in zip(*lists)]
    cat0 = lambda *xs: jnp.concatenate(xs, axis=0)
    cat1 = lambda *xs: jnp.concatenate(xs, axis=1)

    scaled = []
    for ci in range(n_chunks):
        rows = slice(ci * CHUNK, (ci + 1) * CHUNK)
        lw = lw_ref[rows, :]
        cs = _mm(tri_ref[...], lw, NN, 1, P_CUMSUM)
        cs_end = cs[CHUNK - 1:CHUNK, :]
        g_inv = jnp.exp(-cs)
        g_tail = jnp.exp(cs_end - cs)
        r_t = r_ref[rows, :] * jnp.exp(cs)
        scaled.append(dict(
            g_end=jnp.exp(cs_end), r_f32=r_t, r_t=bf(r_t),
            a_t=bf(a_ref[rows, :] * jnp.exp(cs - lw)),
            b_t=bf(b_ref[rows, :] * g_inv), k_t=bf(k_ref[rows, :] * g_inv),
            b_h=bf(b_ref[rows, :] * g_tail), k_h=bf(k_ref[rows, :] * g_tail), v=bf(v_ref[rows, :])))
    chains = [(ci, p) for ci in range(n_chunks) for p in range(N_PAIRS)]
    pair = lambda name: [scaled[ci][name][:, lanes(p)] for ci, p in chains]
    at, rt, bt, kt, bh, kh, v, r_f32 = (pair(n) for n in ("a_t", "r_t", "b_t", "k_t", "b_h", "k_h", "v",
                                                          "r_f32"))
    v2 = each(lambda x: _stack(x, m0), v)
    scores = each(lambda a, r, b, k: _mm(cat0(a, r), cat0(_stack(b, m0), _stack(k, m0)), NT, *P_A),
                  at, rt, bt, kt)
    a_ab = each(lambda s: jnp.where(strict, s[:CHUNK, :LANES], 0.0), scores)
    a_ak = each(lambda s: jnp.where(strict, s[:CHUNK, LANES:], 0.0), scores)
    a_r = each(lambda s: jnp.where(incl2, s[CHUNK:, :], 0.0), scores)
    stack4 = lambda x: cat0(*[jnp.where(lane_head4 == h, x, 0.0) for h in range(4)])
    join = lambda xs: [cat1(xs[i], xs[i + 1]) for i in range(0, len(xs), 2)]
    split = lambda xs: [x[:, s] for x in xs for s in (slice(0, LANES), slice(LANES, 2 * LANES))]
    akv = split(each(lambda x, y: _mm(x, stack4(y), NN, *P_U), join(a_ak), join(v)))
    a_ab4 = join(a_ab)
    t_inv4 = each(lambda x: mss_ref[2] + x * mss_ref[3], a_ab4)
    for lvl in range(1, n_levels):
        t_b = each(bf, t_inv4)
        inner = each(lambda x, t: _mm(x * mss_ref[3 + lvl], stack4(t), NN, *P_T), a_ab4, t_b)
        t_inv4 = each(lambda t, tb, w: t + _mm(tb, stack4(bf(w)), NN, *P_T), t_inv4, t_b, inner)
        if lvl <= early_fillers:
            fill()
    t_inv = split(each(bf, t_inv4))
    pq = each(lambda t, x, y: _mm(t, cat1(_stack(x, m0), _stack(bf(y), m0)), NN, *P_U), t_inv, at, akv)
    pq_b = each(bf, pq)
    def finish_chunk(ci):
        out = []
        for p in range(N_PAIRS):
            i = ci * N_PAIRS + p
            bpq = _mm(cat0(bh[i], kh[i]), cat0(pq_b[i], cat1(zblk, v[i])), TN, *P_S)
            m_t = eye * scaled[ci]["g_end"][:, lanes(p)] + jnp.where(same_head, bpq[:, :LANES], 0.0)
            n_t = jnp.where(same_head, bpq[:, LANES:], 0.0)
            yy = _mm(a_r[i], cat0(cat1(_stack(pq_b[i][:, :LANES], m0), _stack(pq_b[i][:, LANES:], m0)),
                                  cat1(zblk2, v2[i])), NN, *P_Y)
            out.append((m_t, n_t, r_f32[i] + yy[:, :LANES], yy[:, LANES:]))
        return out

    state = [z_ref[p] for p in range(N_PAIRS)]
    upcoming = finish_chunk(0)
    for ci in range(n_chunks):
        current = upcoming
        for p in range(N_PAIRS):
            m_t, n_t, y1, y2 = current[p]
            both = _mm(cat0(m_t, y1), state[p], NN, *P_S)
            state[p] = both[:LANES] + n_t
            y_ref[ci * CHUNK:(ci + 1) * CHUNK, lanes(p)] = both[LANES:] + y2
        if ci + 1 < n_chunks:
            upcoming = finish_chunk(ci + 1)
        fill()
    for p in range(N_PAIRS):
        z_ref[p] = state[p]
    while fillers:
        fill()


def _wkv_masks():
    ri, ci = np.indices((CHUNK, 4 * CHUNK))
    ci = ci % CHUNK
    side = [ci < ri, ci <= ri, ri == ci]
    m = 1
    while m < CHUNK:
        side.append(((ri // (2 * m)) == (ci // (2 * m))) & ((ri // m) % 2 == 1) & ((ci // m) % 2 == 0))
        m *= 2
    rq, cq = np.indices((LANES, LANES))
    square = [rq == cq, (rq // HEAD) == (cq // HEAD)]
    rt, ct = np.indices((CHUNK, CHUNK))
    return (jnp.asarray(ct <= rt, BF16), jnp.asarray(np.stack(side), F32),
            jnp.asarray(np.stack(square), F32))


def _short_pairs(t_s):
    idx = {}
    for kind, inclusive in (("ab", False), ("ak", False), ("rb", True), ("rk", True)):
        for t in range(t_s):
            for s in range(t + 1 if inclusive else t):
                idx[(kind, t, s)] = len(idx)
    return idx


def _short_scale_kernel(r_ref, k_ref, v_ref, lw_ref, a_ref, b_ref,
                        xa_o, xr_o, bh_o, kh_o, vt_o, g4_o, as_o, *, t_s, n_s):
    slab = lambda ref, t: ref[t * n_s:(t + 1) * n_s, :].T
    cs = []
    for t in range(t_s):
        cs.append(slab(lw_ref, t) if t == 0 else cs[-1] + slab(lw_ref, t))
    a_t = [slab(a_ref, t) * (jnp.exp(cs[t - 1]) if t else 1.0) for t in range(t_s)]
    r_t = [slab(r_ref, t) * jnp.exp(cs[t]) for t in range(t_s)]
    b_raw = [slab(b_ref, t) for t in range(t_s)]
    k_raw = [slab(k_ref, t) for t in range(t_s)]
    g_inv = [jnp.exp(-cs[t]) for t in range(t_s)]
    b_t = [b_raw[t] * g_inv[t] for t in range(t_s)]
    k_t = [k_raw[t] * g_inv[t] for t in range(t_s)]
    for t in range(t_s):
        g_tail = jnp.exp(cs[t_s - 1] - cs[t])
        xa_o[t] = a_t[t]
        xr_o[t] = r_t[t]
        bh_o[t] = b_raw[t] * g_tail
        kh_o[t] = k_raw[t] * g_tail
        vt_o[t] = slab(v_ref, t)
    g4_o[...] = jnp.exp(cs[t_s - 1])
    head_sum = lambda x: jnp.sum(x.reshape(N_HEADS, HEAD, n_s), axis=1)
    left = {"ab": a_t, "ak": a_t, "rb": r_t, "rk": r_t}
    right = {"ab": b_t, "ak": k_t, "rb": b_t, "rk": k_t}
    for (kind, t, s), row in _short_pairs(t_s).items():
        as_o[row] = head_sum(left[kind][t] * right[kind][s])


def _short_state_kernel(s_ref, xa_ref, xr_ref, bh_ref, kh_ref, vt_ref, g4_ref, as_ref, *refs, t_s):
    out_ins, (so_ref, o_ref, y_ref) = refs[:-3], refs[-3:]
    idx = _short_pairs(t_s)
    head = pl.program_id(0)
    scal = {key: as_ref[row, pl.ds(head, 1), :] for key, row in idx.items()}

    def value_row(v, carry):
        s_v = s_ref[0, v]
        vv = [vt_ref[t, pl.ds(v, 1), :] for t in range(t_s)]
        u = []
        for t in range(t_s):
            acc = jnp.sum(s_v * xa_ref[t], axis=0, keepdims=True)
            for s in range(t):
                acc = acc + scal[("ab", t, s)] * u[s] + scal[("ak", t, s)] * vv[s]
            u.append(acc)
        for t in range(t_s):
            acc = jnp.sum(s_v * xr_ref[t], axis=0, keepdims=True)
            for s in range(t + 1):
                acc = acc + scal[("rb", t, s)] * u[s] + scal[("rk", t, s)] * vv[s]
            y_ref[t, pl.ds(head * HEAD + v, 1), :] = acc
        new = s_v * g4_ref[...]
        for s in range(t_s):
            new = new + u[s] * bh_ref[s] + vv[s] * kh_ref[s]
        so_ref[0, v] = new
        return carry

    lax.fori_loop(0, HEAD, value_row, 0, unroll=8)

    @pl.when(head == pl.num_programs(0) - 1)
    def _():
        _out_kernel(y_ref, *out_ins, o_ref)


def _short_prep_kernel(x_ref, sh_ref, ph_ref, nw_ref, win_ref, *refs, t_s, n_s, t0):
    prep_rest, refs = refs[:N_PREP_PARAMS - 2], refs[N_PREP_PARAMS - 2:]
    bon_o, sg_o, op_o, nsh_o, npl_o = refs[:5]
    scale_outs, seq_s, (p_ext, u_ext) = refs[5:12], refs[12:18], refs[18:]
    h, rstd = _norm_operands(x_ref, nw_ref)
    for finish in _prep_stages(h, rstd, win_ref, pl.program_id(0), sh_ref, ph_ref, *prep_rest, *seq_s,
                               bon_o, sg_o, op_o, nsh_o, npl_o, p_ext, u_ext,
                               stride=n_s, t0=t0, hs=sh_ref.shape[1]):
        finish()
    _short_scale_kernel(*seq_s, *scale_outs, t_s=t_s, n_s=n_s)


def _short_prep_call(x, sh_hist, pool_hist, params, *, t_s, n_s, t0):
    assert n_s % LANES == 0 and len(params) == N_PREP_PARAMS
    rows = t_s * n_s
    hs, ph = sh_hist.shape[1], _pool_hist_rows(n_s)
    assert pool_hist.shape[1] == ph
    n_scal = len(_short_pairs(t_s))
    full = lambda shape: pl.BlockSpec(shape, lambda i: tuple(0 for _ in shape))
    vec = (t_s, D_RWKV, n_s)
    out_shapes = ([(rows, D_RWKV)] * 3 + [(1, hs, D_SHIFT), (1, ph, D_POOL)] + [vec] * 5
                  + [(D_RWKV, n_s), (n_scal, N_HEADS, n_s)])
    return pl.pallas_call(
        functools.partial(_short_prep_kernel, t_s=t_s, n_s=n_s, t0=t0),
        grid=(1,),
        in_specs=[full(a.shape) for a in (x, sh_hist, pool_hist) + tuple(params)],
        out_specs=[full(s) for s in out_shapes],
        out_shape=[jax.ShapeDtypeStruct(s, F32) for s in out_shapes],
        scratch_shapes=[pltpu.VMEM((rows, D_RWKV), F32)] * 6 + [pltpu.VMEM((hs + rows, D_SHIFT), F32),
                                                                pltpu.VMEM((ph + rows, D_POOL), F32)],
        compiler_params=pltpu.CompilerParams(
            dimension_semantics=("arbitrary",), vmem_limit_bytes=VMEM_LIMIT),
        name="prep_short",
    )(x, sh_hist, pool_hist, *params)


def _short_state_call(s0, xa, xr, bh, kh, vt, g4, scal, bon, sg, op, x, out_params):
    t_s, _, n_s = xa.shape
    assert len(out_params) == N_OUT_PARAMS and x.shape[0] == t_s * n_s
    st = pl.BlockSpec((1, HEAD, HEAD, n_s), lambda h: (h, 0, 0, 0))
    per_head = pl.BlockSpec((t_s, HEAD, n_s), lambda h: (0, h, 0))
    whole = lambda a: pl.BlockSpec(a.shape, lambda h: (0,) * a.ndim)
    out_ins = (bon, sg, op, x) + tuple(out_params)
    return pl.pallas_call(
        functools.partial(_short_state_kernel, t_s=t_s),
        grid=(N_HEADS,),
        in_specs=[st] + [per_head] * 5 + [pl.BlockSpec((HEAD, n_s), lambda h: (h, 0)), whole(scal)]
        + [whole(a) for a in out_ins],
        out_specs=[st, whole(x)],
        out_shape=[jax.ShapeDtypeStruct(s0.shape, F32), jax.ShapeDtypeStruct(x.shape, F32)],
        scratch_shapes=[pltpu.VMEM(xa.shape, F32)],
        compiler_params=pltpu.CompilerParams(
            dimension_semantics=("arbitrary",), vmem_limit_bytes=VMEM_LIMIT),
        name="layer_short_state",
    )(s0, xa, xr, bh, kh, vt, g4, scal, *out_ins)


def _gated_rwkv(y, bon, sg, gw_ref, gb_ref, g_ref):
    g_ones = g_ref[...]
    mu = _seg_sum(y, g_ones) * (1.0 / HEAD)
    d = y - mu
    var = _seg_sum(d * d, g_ones) * (1.0 / HEAD)
    yn = d * lax.rsqrt(var + GN_EPS) * gw_ref[...] + gb_ref[...]
    return (yn + bon) * sg


def _out_project(o_rwkv, o_pool, x, wo_ref, nf_ref):
    out = (jnp.dot(o_rwkv.astype(BF16), wo_ref[0:D_RWKV, :], preferred_element_type=F32)
           + jnp.dot(o_pool.astype(BF16), wo_ref[D_RWKV:, :], preferred_element_type=F32))
    res = x + out
    return res * lax.rsqrt(jnp.mean(res * res, axis=-1, keepdims=True) + NORM_EPS) * nf_ref[...]


def _out_kernel(y_ref, bon_ref, sg_ref, op_ref, x_ref, gw_ref, gb_ref, wo_ref, nf_ref, g_ref, o_ref):
    if len(y_ref.shape) == 3:
        y = jnp.concatenate([y_ref[t].T for t in range(y_ref.shape[0])], axis=0)
    else:
        y = y_ref[...]
    o_rwkv = _gated_rwkv(y, bon_ref[...], sg_ref[...], gw_ref, gb_ref, g_ref)
    o_ref[...] = _out_project(o_rwkv, op_ref[...], x_ref[...], wo_ref, nf_ref)


N_PREP_PARAMS = 12
N_WKV_CONSTS = 3
N_OUT_PARAMS = 5


def _fused_kernel(*refs, n_chunks, t0):
    x_ref, x_next_ref, nw_ref, win_ref = refs[:4]
    pos = 2 + N_PREP_PARAMS
    prep_rest = refs[4:pos]
    wkv_consts = refs[pos:pos + N_WKV_CONSTS]
    pos += N_WKV_CONSTS
    gw_ref, gb_ref, wo_ref, nf_ref, g_ref = refs[pos:pos + N_OUT_PARAMS]
    pos += N_OUT_PARAMS
    o_ref, nsh_o, npl_o, so_ref = refs[pos:pos + 4]
    pos += 4
    seq_s = refs[pos:pos + 6]
    bon_s, sg_s, op_s, y_s, p_ext, u_ext, z_ref, h_s, rstd_s = refs[pos + 6:]
    j = pl.program_id(1)

    def prepare(src_ref):
        h_s[...], rstd_s[...] = _norm_operands(src_ref, nw_ref)

    @pl.when((pl.program_id(0) == 0) & (j == 0))
    def _():
        prepare(x_ref)

    _wkv_zero_state(z_ref, j == 0)
    finish_prep = _prep_stages(h_s[...], rstd_s[...], win_ref, j, None, None, *prep_rest,
                               *seq_s, bon_s, sg_s, op_s, nsh_o, npl_o, p_ext, u_ext,
                               stride=1, t0=t0, hs=nsh_o.shape[1])
    _wkv_tile(*seq_s, *wkv_consts, y_s, z_ref, n_chunks=n_chunks,
              fillers=[functools.partial(prepare, x_next_ref)] + finish_prep, early_fillers=2)
    _out_kernel(y_s, bon_s, sg_s, op_s, x_ref, gw_ref, gb_ref, wo_ref, nf_ref, g_ref, o_ref)
    _wkv_store_state(z_ref, so_ref, j == pl.num_programs(1) - 1)


def _fused_call(x, prep_params, out_params, *, n_seq, n_chunks, t0):
    assert len(prep_params) == N_PREP_PARAMS and len(out_params) == N_OUT_PARAMS
    total = x.shape[0]
    rows = n_chunks * CHUNK
    tiles = total // (n_seq * rows)
    hs, ph = SUBLANES, _pool_hist_rows(1)
    assert CHUNK == HEAD and 2 * CHUNK == LANES
    wkv_consts = _wkv_masks()
    act = pl.BlockSpec((rows, D_MODEL), lambda b, j: (b * tiles + j, 0))
    ahead = pl.BlockSpec((rows, D_MODEL), lambda b, j: (jnp.minimum(b * tiles + j + 1, n_seq * tiles - 1), 0))
    per_seq = lambda shape: pl.BlockSpec((1,) + shape[1:], lambda b, j: (b,) + (0,) * (len(shape) - 1))
    const = lambda a: pl.BlockSpec(a.shape, lambda b, j: (0,) * a.ndim, pipeline_mode=pl.Buffered(1))
    consts = tuple(prep_params) + tuple(wkv_consts) + tuple(out_params)
    seq_shapes = [(n_seq, hs, D_SHIFT), (n_seq, ph, D_POOL), (n_seq, N_HEADS, HEAD, HEAD)]
    seq_scratch = pltpu.VMEM((rows, D_RWKV), F32)
    return pl.pallas_call(
        functools.partial(_fused_kernel, n_chunks=n_chunks, t0=t0),
        grid=(n_seq, tiles),
        in_specs=[act, ahead] + [const(a) for a in consts],
        out_specs=[act] + [per_seq(s) for s in seq_shapes],
        out_shape=[jax.ShapeDtypeStruct((total, D_MODEL), F32)]
        + [jax.ShapeDtypeStruct(s, F32) for s in seq_shapes],
        scratch_shapes=[seq_scratch] * 10 + [pltpu.VMEM((hs + rows, D_SHIFT), F32),
                                             pltpu.VMEM((ph + rows, D_POOL), F32),
                                             pltpu.VMEM((N_PAIRS, LANES, LANES), F32),
                                             pltpu.VMEM((rows, D_MODEL), BF16),
                                             pltpu.VMEM((rows, 1), F32)],
        compiler_params=pltpu.CompilerParams(
            dimension_semantics=("arbitrary", "arbitrary"), vmem_limit_bytes=VMEM_LIMIT),
        name="layer_long",
    )(x, x, *consts)


def _head_ones():
    i, j = np.indices((MXU_DIM, MXU_DIM))
    return jnp.asarray((i // HEAD) == (j // HEAD), BF16)


def kernel(x_prompt, x_sample, state_shift, state_wkv, state_pool, norm_w, w_in, mu_shift, w_decay_b,
           w0, w_aaa_b, a0, k_k, k_a, r_k, gn_w, gn_b, pool_w, pool_scale, w_out, norm_f):
    depth = norm_w.shape[0]
    n_p, t_p, _ = x_prompt.shape
    n_s, t_s, _ = x_sample.shape
    assert depth == 1, "a stacked trunk needs a residual-only output kernel between layers"
    assert t_p % (WKV_CHUNKS_PER_STEP * CHUNK) == 0
    g_ones = _head_ones()
    row = lambda z: z.reshape(1, -1).astype(F32)

    hp = x_prompt.astype(F32).reshape(n_p * t_p, D_MODEL)
    hs = jnp.transpose(x_sample.astype(F32), (1, 0, 2)).reshape(t_s * n_s, D_MODEL)
    outs = [[] for _ in range(6)]
    for l in range(depth):
        zl = jnp.zeros((D_LORA, D_RWKV), F32)
        w_lora = jnp.concatenate([jnp.concatenate([w_decay_b[l], zl], axis=1),
                                  jnp.concatenate([zl, w_aaa_b[l]], axis=1)], axis=0).astype(BF16)
        prep_params = (row(norm_w[l]), w_in[l].astype(BF16), row(mu_shift[l]), w_lora, row(w0[l]),
                       row(a0[l]), row(k_k[l]), row(k_a[l]), row(r_k[l]), pool_w[l].astype(BF16),
                       row(pool_scale[l]), g_ones)
        out_params = (row(gn_w[l]), row(gn_b[l]), w_out[l].astype(BF16), row(norm_f), g_ones)

        hp, nsh, npl, s_fin = _fused_call(hp, prep_params, out_params,
                                          n_seq=n_p, n_chunks=WKV_CHUNKS_PER_STEP, t0=0)
        outs[0].append(nsh[:, SUBLANES - 1])
        outs[1].append(s_fin)
        outs[2].append(npl[:, -POOL_KEEP:])

        sh_hist = state_shift[l].astype(F32)[None]
        pool_hist = jnp.transpose(state_pool[l].astype(F32), (1, 0, 2)).reshape(1, POOL_KEEP * n_s, D_POOL)
        bon, sg, op, nsh, npl, *scan_ops = _short_prep_call(hs, sh_hist, pool_hist, prep_params,
                                                            t_s=t_s, n_s=n_s, t0=PAST_LEN)
        s_fin, hs = _short_state_call(jnp.transpose(state_wkv[l].astype(F32), (1, 2, 3, 0)), *scan_ops,
                                      bon, sg, op, hs, out_params)
        s_fin = jnp.transpose(s_fin, (3, 0, 1, 2))
        outs[3].append(nsh[0])
        outs[4].append(s_fin)
        outs[5].append(jnp.transpose(npl[0].reshape(POOL_KEEP, n_s, D_POOL), (1, 0, 2)))

    y_prompt = hp.reshape(n_p, t_p, D_MODEL).astype(x_prompt.dtype)
    y_sample = jnp.transpose(hs.reshape(t_s, n_s, D_MODEL), (1, 0, 2)).astype(x_sample.dtype)
    return (y_prompt, y_sample) + tuple(jnp.stack(o, axis=0) for o in outs)
```

```python
import functools

import jax
import jax.numpy as jnp
import numpy as np
from jax import lax
from jax.experimental import pallas as pl
from jax.experimental.pallas import tpu as pltpu

F32 = jnp.float32
BF16 = jnp.bfloat16

D_MODEL = 1024
D_RWKV = 512
D_POOL = 512
HEAD = 64
N_HEADS = D_RWKV // HEAD
LANES = 128
SUBLANES = 8
MXU_DIM = 256
N_PAIRS = D_RWKV // LANES
D_LORA = 64
POOL_WINDOWS = (2, 4, 8, 16)
POOL_GROUP = D_POOL // len(POOL_WINDOWS)
POOL_KEEP = max(POOL_WINDOWS) - 1
D_SHIFT = 3 * D_RWKV + 2 * D_LORA
D_IN = D_SHIFT + D_RWKV + 2 * D_POOL
PAST_LEN = 16384
NORM_EPS = 1e-6
GN_EPS = 64e-5
L2_EPS = 1e-12
CHUNK = 64
VMEM_LIMIT = 56 * 1024 * 1024
P_A = (1, 1)
P_T = (1, 1)
P_U = (1, 1)
P_S = (1, 1)
P_Y = (1, 1)
P_CUMSUM = 2
WKV_CHUNKS_PER_STEP = 8
OUT_ROW_BLOCKS = 2

NN = (((1,), (0,)), ((), ()))
NT = (((1,), (1,)), ((), ()))
TN = (((0,), (0,)), ((), ()))


def _pieces(x, n):
    if x.dtype == BF16:
        return [x]
    out = []
    rem = x
    for i in range(n):
        p = rem.astype(BF16)
        out.append(p)
        if i + 1 < n:
            rem = rem - p.astype(F32)
    return out


def _mm(a, b, dims=NN, na=1, nb=1):
    ap = _pieces(a, na)
    bp = _pieces(b, nb)
    order = max(len(ap), len(bp))
    acc = None
    for i, x in enumerate(ap):
        for j, y in enumerate(bp):
            if i + j >= order:
                continue
            t = lax.dot_general(x, y, dims, preferred_element_type=F32)
            acc = t if acc is None else acc + t
    return acc


def _seg_sum(x, g_ones, pieces=1):
    cols = []
    for s in range(x.shape[1] // MXU_DIM):
        cols.append(_mm(x[:, s * MXU_DIM:(s + 1) * MXU_DIM], g_ones, NN, pieces, 1))
    return cols[0] if len(cols) == 1 else jnp.concatenate(cols, axis=1)


def _sigmoid(x):
    return 1.0 / (1.0 + jnp.exp(-x))


def _pool_hist_rows(stride):
    return -(-POOL_KEEP * stride // SUBLANES) * SUBLANES


def _norm_operands(x_ref, nw_ref):
    x = x_ref[...]
    rstd = lax.rsqrt(jnp.mean(x * x, axis=-1, keepdims=True) + NORM_EPS)
    return (x * nw_ref[...]).astype(BF16), rstd


def _prep_stages(h, rstd, win_ref, j, sh_ref, ph_ref, mu_ref, wl_ref, w0_ref, a0_ref, kk_ref,
                 ka_ref, rk_ref, pw_ref, ps_ref, g_ref,
                 r_o, k_o, v_o, lw_o, a_o, b_o, bon_o, sg_o, op_o, nsh_o, npl_o,
                 p_ext, u_ext, *, stride, t0, hs):
    rows = h.shape[0]
    ph = _pool_hist_rows(stride)
    width = MXU_DIM

    @pl.when(j == 0)
    def _():
        p_ext[0:hs, :] = jnp.zeros((hs, D_SHIFT), F32) if sh_ref is None else sh_ref[0]
        u_ext[0:ph, :] = jnp.zeros((ph, D_POOL), F32) if ph_ref is None else ph_ref[0]

    def proj_cols(c0, c1):
        return jnp.dot(h, win_ref[:, c0:c1], preferred_element_type=F32) * rstd

    def shifted(c0, c1):
        p = proj_cols(c0, c1)
        p_ext[hs:hs + rows, c0:c1] = p
        prev = p_ext[pl.ds(hs - stride, rows), c0:c1]
        return p + mu_ref[:, c0:c1] * (prev - p)

    xwa = shifted(3 * D_RWKV, D_SHIFT)
    for c in range(0, D_RWKV, width):
        r_o[:, c:c + width] = shifted(c, c + width)
    lane = lax.broadcasted_iota(jnp.int32, xwa.shape, 1)
    lora_in = jnp.where(lane < D_LORA, jnp.tanh(xwa), xwa)
    lora = jnp.dot(lora_in.astype(BF16), wl_ref[...], preferred_element_type=F32)
    lw_o[...] = (-0.6065306597126334) * _sigmoid(w0_ref[...] + lora[:, :D_RWKV])
    alpha = _sigmoid(a0_ref[...] + lora[:, D_RWKV:])
    for c in range(0, D_RWKV, width):
        sl = slice(c, c + width)
        k = shifted(D_RWKV + c, D_RWKV + c + width)
        k_o[:, sl] = k * (1.0 + (alpha[:, sl] - 1.0) * ka_ref[:, sl])
        kk = k * kk_ref[:, sl]
        kk = kk * lax.rsqrt(_seg_sum(kk * kk, g_ref[...]) + L2_EPS)
        a_o[:, sl] = -kk
        b_o[:, sl] = kk * alpha[:, sl]
        v_o[:, sl] = shifted(2 * D_RWKV + c, 2 * D_RWKV + c + width)
    last_p = p_ext[rows:rows + hs, :]
    p_ext[0:hs, :] = last_p
    nsh_o[0] = last_p
    for c in range(0, D_RWKV, width):
        g_rwkv = proj_cols(D_SHIFT + c, D_SHIFT + c + width)
        sg_o[:, c:c + width] = g_rwkv * _sigmoid(g_rwkv)
    for c in range(0, D_POOL, width):
        u_ext[ph:ph + rows, c:c + width] = proj_cols(D_SHIFT + D_RWKV + c, D_SHIFT + D_RWKV + c + width)
        op_o[:, c:c + width] = proj_cols(D_SHIFT + D_RWKV + D_POOL + c,
                                         D_SHIFT + D_RWKV + D_POOL + c + width)

    def bonus():
        bon_o[...] = _seg_sum(r_o[...] * k_o[...] * rk_ref[...], g_ref[...]) * v_o[...]

    def pool(gi, win):
        sl = slice(gi * POOL_GROUP, (gi + 1) * POOL_GROUP)
        row = lax.broadcasted_iota(jnp.int32, (rows, 1), 0)
        pos = t0 + (j * rows + row) // stride
        u = u_ext[ph:ph + rows, sl]
        tot = u
        for back in range(1, win):
            tot = tot + u_ext[pl.ds(ph - back * stride, rows), sl]
        inv_cnt = 1.0 / jnp.minimum(pos + 1, win).astype(F32)
        d = tot * inv_cnt - u
        o = jnp.dot(d.astype(BF16), pw_ref[gi], preferred_element_type=F32)
        gp = op_o[:, sl]
        op_o[:, sl] = o * ps_ref[:, sl] * (gp * _sigmoid(gp))

    def carry_pool():
        last_u = u_ext[rows:rows + ph, :]
        u_ext[0:ph, :] = last_u
        npl_o[0] = last_u

    pools = [functools.partial(pool, gi, win) for gi, win in enumerate(POOL_WINDOWS)]
    return [bonus] + pools + [carry_pool]


def _stack(x, m0):
    return jnp.concatenate([jnp.where(m0, x, 0.0), jnp.where(m0, 0.0, x)], axis=0)


def _wkv_zero_state(z_ref, is_first):
    @pl.when(is_first)
    def _():
        z_ref[...] = jnp.zeros(z_ref.shape, F32)


def _wkv_store_state(z_ref, so_ref, is_last):
    @pl.when(is_last)
    def _():
        for p in range(N_PAIRS):
            s2 = z_ref[p].T
            so_ref[0, 2 * p] = s2[:HEAD, :HEAD]
            so_ref[0, 2 * p + 1] = s2[HEAD:, HEAD:]


def _wkv_tile(r_ref, k_ref, v_ref, lw_ref, a_ref, b_ref, tri_ref, mss_ref, msq_ref, y_ref, z_ref, *,
              n_chunks, fillers=()):
    fillers = list(fillers)
    fill = lambda: fillers.pop(0)() if fillers else None
    lane = lax.broadcasted_iota(jnp.int32, (CHUNK, LANES), 1)
    m0 = lane < HEAD
    strict = mss_ref[0][:, :LANES] > 0.5
    incl2 = mss_ref[1] > 0.5
    n_levels = mss_ref.shape[0] - 3
    lane_head4 = lax.broadcasted_iota(jnp.int32, (CHUNK, 2 * LANES), 1) // HEAD
    eye = msq_ref[0]
    same_head = msq_ref[1] > 0.5
    zblk = jnp.zeros((CHUNK, LANES), BF16)
    zblk2 = jnp.zeros((2 * CHUNK, LANES), BF16)
    bf = lambda x: x.astype(BF16)
    lanes = lambda p: slice(p * LANES, (p + 1) * LANES)
    each = lambda fn, *lists: [fn(*args) for args in zip(*lists)]
    cat0 = lambda *xs: jnp.concatenate(xs, axis=0)
    cat1 = lambda *xs: jnp.concatenate(xs, axis=1)

    scaled = []
    for ci in range(n_chunks):
        rows = slice(ci * CHUNK, (ci + 1) * CHUNK)
        lw = lw_ref[rows, :]
        cs = _mm(tri_ref[...], lw, NN, 1, P_CUMSUM)
        cs_end = cs[CHUNK - 1:CHUNK, :]
        g_inv = jnp.exp(-cs)
        g_tail = jnp.exp(cs_end - cs)
        r_t = r_ref[rows, :] * jnp.exp(cs)
        scaled.append(dict(
            g_end=jnp.exp(cs_end), r_f32=r_t, r_t=bf(r_t),
            a_t=bf(a_ref[rows, :] * jnp.exp(cs - lw)),
            b_t=bf(b_ref[rows, :] * g_inv), k_t=bf(k_ref[rows, :] * g_inv),
            b_h=bf(b_ref[rows, :] * g_tail), k_h=bf(k_ref[rows, :] * g_tail), v=bf(v_ref[rows, :])))
    chains = [(ci, p) for ci in range(n_chunks) for p in range(N_PAIRS)]
    pair = lambda name: [scaled[ci][name][:, lanes(p)] for ci, p in chains]
    at, rt, bt, kt, bh, kh, v, r_f32 = (pair(n) for n in ("a_t", "r_t", "b_t", "k_t", "b_h", "k_h", "v",
                                                          "r_f32"))
    v2 = each(lambda x: _stack(x, m0), v)
    scores = each(lambda a, r, b, k: _mm(cat0(a, r), cat0(_stack(b, m0), _stack(k, m0)), NT, *P_A),
                  at, rt, bt, kt)
    a_ab = each(lambda s: jnp.where(strict, s[:CHUNK, :LANES], 0.0), scores)
    a_ak = each(lambda s: jnp.where(strict, s[:CHUNK, LANES:], 0.0), scores)
    a_r = each(lambda s: jnp.where(incl2, s[CHUNK:, :], 0.0), scores)
    stack4 = lambda x: cat0(*[jnp.where(lane_head4 == h, x, 0.0) for h in range(4)])
    join = lambda xs: [cat1(xs[i], xs[i + 1]) for i in range(0, len(xs), 2)]
    split = lambda xs: [x[:, s] for x in xs for s in (slice(0, LANES), slice(LANES, 2 * LANES))]
    akv = split(each(lambda x, y: _mm(x, stack4(y), NN, *P_U), join(a_ak), join(v)))
    a_ab4 = join(a_ab)
    t_inv4 = each(lambda x: mss_ref[2] + x * mss_ref[3], a_ab4)
    for lvl in range(1, n_levels):
        t_b = each(bf, t_inv4)
        inner = each(lambda x, t: _mm(x * mss_ref[3 + lvl], stack4(t), NN, *P_T), a_ab4, t_b)
        t_inv4 = each(lambda t, tb, w: t + _mm(tb, stack4(bf(w)), NN, *P_T), t_inv4, t_b, inner)
    t_inv = split(each(bf, t_inv4))
    pq = each(lambda t, x, y: _mm(t, cat1(_stack(x, m0), _stack(bf(y), m0)), NN, *P_U), t_inv, at, akv)
    pq_b = each(bf, pq)
    def finish_chunk(ci):
        out = []
        for p in range(N_PAIRS):
            i = ci * N_PAIRS + p
            bpq = _mm(cat0(bh[i], kh[i]), cat0(pq_b[i], cat1(zblk, v[i])), TN, *P_S)
            m_t = eye * scaled[ci]["g_end"][:, lanes(p)] + jnp.where(same_head, bpq[:, :LANES], 0.0)
            n_t = jnp.where(same_head, bpq[:, LANES:], 0.0)
            yy = _mm(a_r[i], cat0(cat1(_stack(pq_b[i][:, :LANES], m0), _stack(pq_b[i][:, LANES:], m0)),
                                  cat1(zblk2, v2[i])), NN, *P_Y)
            out.append((m_t, n_t, r_f32[i] + yy[:, :LANES], yy[:, LANES:]))
        return out

    state = [z_ref[p] for p in range(N_PAIRS)]
    upcoming = finish_chunk(0)
    for ci in range(n_chunks):
        current = upcoming
        for p in range(N_PAIRS):
            m_t, n_t, y1, y2 = current[p]
            both = _mm(cat0(m_t, y1), state[p], NN, *P_S)
            state[p] = both[:LANES] + n_t
            y_ref[ci * CHUNK:(ci + 1) * CHUNK, lanes(p)] = both[LANES:] + y2
        if ci + 1 < n_chunks:
            upcoming = finish_chunk(ci + 1)
        fill()
    for p in range(N_PAIRS):
        z_ref[p] = state[p]
    while fillers:
        fill()


def _wkv_masks():
    ri, ci = np.indices((CHUNK, 4 * CHUNK))
    ci = ci % CHUNK
    side = [ci < ri, ci <= ri, ri == ci]
    m = 1
    while m < CHUNK:
        side.append(((ri // (2 * m)) == (ci // (2 * m))) & ((ri // m) % 2 == 1) & ((ci // m) % 2 == 0))
        m *= 2
    rq, cq = np.indices((LANES, LANES))
    square = [rq == cq, (rq // HEAD) == (cq // HEAD)]
    rt, ct = np.indices((CHUNK, CHUNK))
    return (jnp.asarray(ct <= rt, BF16), jnp.asarray(np.stack(side), F32),
            jnp.asarray(np.stack(square), F32))


def _short_pairs(t_s):
    idx = {}
    for kind, inclusive in (("ab", False), ("ak", False), ("rb", True), ("rk", True)):
        for t in range(t_s):
            for s in range(t + 1 if inclusive else t):
                idx[(kind, t, s)] = len(idx)
    return idx


def _short_scale_kernel(r_ref, k_ref, v_ref, lw_ref, a_ref, b_ref,
                        xa_o, xr_o, bh_o, kh_o, vt_o, g4_o, as_o, *, t_s, n_s):
    slab = lambda ref, t: ref[t * n_s:(t + 1) * n_s, :].T
    cs = []
    for t in range(t_s):
        cs.append(slab(lw_ref, t) if t == 0 else cs[-1] + slab(lw_ref, t))
    a_t = [slab(a_ref, t) * (jnp.exp(cs[t - 1]) if t else 1.0) for t in range(t_s)]
    r_t = [slab(r_ref, t) * jnp.exp(cs[t]) for t in range(t_s)]
    b_raw = [slab(b_ref, t) for t in range(t_s)]
    k_raw = [slab(k_ref, t) for t in range(t_s)]
    g_inv = [jnp.exp(-cs[t]) for t in range(t_s)]
    b_t = [b_raw[t] * g_inv[t] for t in range(t_s)]
    k_t = [k_raw[t] * g_inv[t] for t in range(t_s)]
    for t in range(t_s):
        g_tail = jnp.exp(cs[t_s - 1] - cs[t])
        xa_o[t] = a_t[t]
        xr_o[t] = r_t[t]
        bh_o[t] = b_raw[t] * g_tail
        kh_o[t] = k_raw[t] * g_tail
        vt_o[t] = slab(v_ref, t)
    g4_o[...] = jnp.exp(cs[t_s - 1])
    head_sum = lambda x: jnp.sum(x.reshape(N_HEADS, HEAD, n_s), axis=1)
    left = {"ab": a_t, "ak": a_t, "rb": r_t, "rk": r_t}
    right = {"ab": b_t, "ak": k_t, "rb": b_t, "rk": k_t}
    for (kind, t, s), row in _short_pairs(t_s).items():
        as_o[row] = head_sum(left[kind][t] * right[kind][s])


def _short_state_kernel(s_ref, xa_ref, xr_ref, bh_ref, kh_ref, vt_ref, g4_ref, as_ref, *refs, t_s):
    out_ins, (so_ref, o_ref, y_ref) = refs[:-3], refs[-3:]
    idx = _short_pairs(t_s)
    head = pl.program_id(0)
    scal = {key: as_ref[row, pl.ds(head, 1), :] for key, row in idx.items()}

    def value_row(v, carry):
        s_v = s_ref[0, v]
        vv = [vt_ref[t, pl.ds(v, 1), :] for t in range(t_s)]
        u = []
        for t in range(t_s):
            acc = jnp.sum(s_v * xa_ref[t], axis=0, keepdims=True)
            for s in range(t):
                acc = acc + scal[("ab", t, s)] * u[s] + scal[("ak", t, s)] * vv[s]
            u.append(acc)
        for t in range(t_s):
            acc = jnp.sum(s_v * xr_ref[t], axis=0, keepdims=True)
            for s in range(t + 1):
                acc = acc + scal[("rb", t, s)] * u[s] + scal[("rk", t, s)] * vv[s]
            y_ref[t, pl.ds(head * HEAD + v, 1), :] = acc
        new = s_v * g4_ref[...]
        for s in range(t_s):
            new = new + u[s] * bh_ref[s] + vv[s] * kh_ref[s]
        so_ref[0, v] = new
        return carry

    lax.fori_loop(0, HEAD, value_row, 0, unroll=8)

    @pl.when(head == pl.num_programs(0) - 1)
    def _():
        _out_kernel(y_ref, *out_ins, o_ref)


def _short_prep_kernel(x_ref, sh_ref, ph_ref, nw_ref, win_ref, *refs, t_s, n_s, t0):
    prep_rest, refs = refs[:N_PREP_PARAMS - 2], refs[N_PREP_PARAMS - 2:]
    bon_o, sg_o, op_o, nsh_o, npl_o = refs[:5]
    scale_outs, seq_s, (p_ext, u_ext) = refs[5:12], refs[12:18], refs[18:]
    h, rstd = _norm_operands(x_ref, nw_ref)
    for finish in _prep_stages(h, rstd, win_ref, pl.program_id(0), sh_ref, ph_ref, *prep_rest, *seq_s,
                               bon_o, sg_o, op_o, nsh_o, npl_o, p_ext, u_ext,
                               stride=n_s, t0=t0, hs=sh_ref.shape[1]):
        finish()
    _short_scale_kernel(*seq_s, *scale_outs, t_s=t_s, n_s=n_s)


def _short_prep_call(x, sh_hist, pool_hist, params, *, t_s, n_s, t0):
    assert n_s % LANES == 0 and len(params) == N_PREP_PARAMS
    rows = t_s * n_s
    hs, ph = sh_hist.shape[1], _pool_hist_rows(n_s)
    assert pool_hist.shape[1] == ph
    n_scal = len(_short_pairs(t_s))
    full = lambda shape: pl.BlockSpec(shape, lambda i: tuple(0 for _ in shape))
    vec = (t_s, D_RWKV, n_s)
    out_shapes = ([(rows, D_RWKV)] * 3 + [(1, hs, D_SHIFT), (1, ph, D_POOL)] + [vec] * 5
                  + [(D_RWKV, n_s), (n_scal, N_HEADS, n_s)])
    return pl.pallas_call(
        functools.partial(_short_prep_kernel, t_s=t_s, n_s=n_s, t0=t0),
        grid=(1,),
        in_specs=[full(a.shape) for a in (x, sh_hist, pool_hist) + tuple(params)],
        out_specs=[full(s) for s in out_shapes],
        out_shape=[jax.ShapeDtypeStruct(s, F32) for s in out_shapes],
        scratch_shapes=[pltpu.VMEM((rows, D_RWKV), F32)] * 6 + [pltpu.VMEM((hs + rows, D_SHIFT), F32),
                                                                pltpu.VMEM((ph + rows, D_POOL), F32)],
        compiler_params=pltpu.CompilerParams(
            dimension_semantics=("arbitrary",), vmem_limit_bytes=VMEM_LIMIT),
        name="prep_short",
    )(x, sh_hist, pool_hist, *params)


def _short_state_call(s0, xa, xr, bh, kh, vt, g4, scal, bon, sg, op, x, out_params):
    t_s, _, n_s = xa.shape
    assert len(out_params) == N_OUT_PARAMS and x.shape[0] == t_s * n_s
    st = pl.BlockSpec((1, HEAD, HEAD, n_s), lambda h: (h, 0, 0, 0))
    per_head = pl.BlockSpec((t_s, HEAD, n_s), lambda h: (0, h, 0))
    whole = lambda a: pl.BlockSpec(a.shape, lambda h: (0,) * a.ndim)
    out_ins = (bon, sg, op, x) + tuple(out_params)
    return pl.pallas_call(
        functools.partial(_short_state_kernel, t_s=t_s),
        grid=(N_HEADS,),
        in_specs=[st] + [per_head] * 5 + [pl.BlockSpec((HEAD, n_s), lambda h: (h, 0)), whole(scal)]
        + [whole(a) for a in out_ins],
        out_specs=[st, whole(x)],
        out_shape=[jax.ShapeDtypeStruct(s0.shape, F32), jax.ShapeDtypeStruct(x.shape, F32)],
        scratch_shapes=[pltpu.VMEM(xa.shape, F32)],
        compiler_params=pltpu.CompilerParams(
            dimension_semantics=("arbitrary",), vmem_limit_bytes=VMEM_LIMIT),
        name="layer_short_state",
    )(s0, xa, xr, bh, kh, vt, g4, scal, *out_ins)


def _gated_rwkv(y, bon, sg, gw_ref, gb_ref, g_ref):
    g_ones = g_ref[...]
    mu = _seg_sum(y, g_ones) * (1.0 / HEAD)
    d = y - mu
    var = _seg_sum(d * d, g_ones) * (1.0 / HEAD)
    yn = d * lax.rsqrt(var + GN_EPS) * gw_ref[...] + gb_ref[...]
    return (yn + bon) * sg


def _out_project(o_rwkv, o_pool, x, wo_ref, nf_ref):
    out = (jnp.dot(o_rwkv.astype(BF16), wo_ref[0:D_RWKV, :], preferred_element_type=F32)
           + jnp.dot(o_pool.astype(BF16), wo_ref[D_RWKV:, :], preferred_element_type=F32))
    res = x + out
    return res * lax.rsqrt(jnp.mean(res * res, axis=-1, keepdims=True) + NORM_EPS) * nf_ref[...]


def _out_kernel(y_ref, bon_ref, sg_ref, op_ref, x_ref, gw_ref, gb_ref, wo_ref, nf_ref, g_ref, o_ref,
                row_blocks=1):
    if len(y_ref.shape) == 3:
        y = jnp.concatenate([y_ref[t].T for t in range(y_ref.shape[0])], axis=0)
        o_rwkv = _gated_rwkv(y, bon_ref[...], sg_ref[...], gw_ref, gb_ref, g_ref)
        o_ref[...] = _out_project(o_rwkv, op_ref[...], x_ref[...], wo_ref, nf_ref)
        return
    step = y_ref.shape[0] // row_blocks
    for i in range(row_blocks):
        rows = slice(i * step, (i + 1) * step)
        o_rwkv = _gated_rwkv(y_ref[rows, :], bon_ref[rows, :], sg_ref[rows, :], gw_ref, gb_ref, g_ref)
        o_ref[rows, :] = _out_project(o_rwkv, op_ref[rows, :], x_ref[rows, :], wo_ref, nf_ref)


N_PREP_PARAMS = 12
N_WKV_CONSTS = 3
N_OUT_PARAMS = 5


def _fused_kernel(*refs, n_chunks, t0):
    x_ref, x_next_ref, nw_ref, win_ref = refs[:4]
    pos = 2 + N_PREP_PARAMS
    prep_rest = refs[4:pos]
    wkv_consts = refs[pos:pos + N_WKV_CONSTS]
    pos += N_WKV_CONSTS
    gw_ref, gb_ref, wo_ref, nf_ref, g_ref = refs[pos:pos + N_OUT_PARAMS]
    pos += N_OUT_PARAMS
    o_ref, nsh_o, npl_o, so_ref = refs[pos:pos + 4]
    pos += 4
    seq_s = refs[pos:pos + 6]
    bon_s, sg_s, op_s, y_s, p_ext, u_ext, z_ref, h_s, rstd_s = refs[pos + 6:]
    j = pl.program_id(1)

    def prepare(src_ref):
        h_s[...], rstd_s[...] = _norm_operands(src_ref, nw_ref)

    @pl.when((pl.program_id(0) == 0) & (j == 0))
    def _():
        prepare(x_ref)

    _wkv_zero_state(z_ref, j == 0)
    finish_prep = _prep_stages(h_s[...], rstd_s[...], win_ref, j, None, None, *prep_rest,
                               *seq_s, bon_s, sg_s, op_s, nsh_o, npl_o, p_ext, u_ext,
                               stride=1, t0=t0, hs=nsh_o.shape[1])
    _wkv_tile(*seq_s, *wkv_consts, y_s, z_ref, n_chunks=n_chunks,
              fillers=finish_prep + [functools.partial(prepare, x_next_ref)])
    _out_kernel(y_s, bon_s, sg_s, op_s, x_ref, gw_ref, gb_ref, wo_ref, nf_ref, g_ref, o_ref,
                row_blocks=OUT_ROW_BLOCKS)
    _wkv_store_state(z_ref, so_ref, j == pl.num_programs(1) - 1)


def _fused_call(x, prep_params, out_params, *, n_seq, n_chunks, t0):
    assert len(prep_params) == N_PREP_PARAMS and len(out_params) == N_OUT_PARAMS
    total = x.shape[0]
    rows = n_chunks * CHUNK
    tiles = total // (n_seq * rows)
    hs, ph = SUBLANES, _pool_hist_rows(1)
    assert CHUNK == HEAD and 2 * CHUNK == LANES
    wkv_consts = _wkv_masks()
    act = pl.BlockSpec((rows, D_MODEL), lambda b, j: (b * tiles + j, 0))
    ahead = pl.BlockSpec((rows, D_MODEL), lambda b, j: (jnp.minimum(b * tiles + j + 1, n_seq * tiles - 1), 0))
    per_seq = lambda shape: pl.BlockSpec((1,) + shape[1:], lambda b, j: (b,) + (0,) * (len(shape) - 1))
    const = lambda a: pl.BlockSpec(a.shape, lambda b, j: (0,) * a.ndim, pipeline_mode=pl.Buffered(1))
    consts = tuple(prep_params) + tuple(wkv_consts) + tuple(out_params)
    seq_shapes = [(n_seq, hs, D_SHIFT), (n_seq, ph, D_POOL), (n_seq, N_HEADS, HEAD, HEAD)]
    seq_scratch = pltpu.VMEM((rows, D_RWKV), F32)
    return pl.pallas_call(
        functools.partial(_fused_kernel, n_chunks=n_chunks, t0=t0),
        grid=(n_seq, tiles),
        in_specs=[act, ahead] + [const(a) for a in consts],
        out_specs=[act] + [per_seq(s) for s in seq_shapes],
        out_shape=[jax.ShapeDtypeStruct((total, D_MODEL), F32)]
        + [jax.ShapeDtypeStruct(s, F32) for s in seq_shapes],
        scratch_shapes=[seq_scratch] * 10 + [pltpu.VMEM((hs + rows, D_SHIFT), F32),
                                             pltpu.VMEM((ph + rows, D_POOL), F32),
                                             pltpu.VMEM((N_PAIRS, LANES, LANES), F32),
                                             pltpu.VMEM((rows, D_MODEL), BF16),
                                             pltpu.VMEM((rows, 1), F32)],
        compiler_params=pltpu.CompilerParams(
            dimension_semantics=("arbitrary", "arbitrary"), vmem_limit_bytes=VMEM_LIMIT),
        name="layer_long",
    )(x, x, *consts)


def _head_ones():
    i, j = np.indices((MXU_DIM, MXU_DIM))
    return jnp.asarray((i // HEAD) == (j // HEAD), BF16)


def kernel(x_prompt, x_sample, state_shift, state_wkv, state_pool, norm_w, w_in, mu_shift, w_decay_b,
           w0, w_aaa_b, a0, k_k, k_a, r_k, gn_w, gn_b, pool_w, pool_scale, w_out, norm_f):
    depth = norm_w.shape[0]
    n_p, t_p, _ = x_prompt.shape
    n_s, t_s, _ = x_sample.shape
    assert depth == 1, "a stacked trunk needs a residual-only output kernel between layers"
    assert t_p % (WKV_CHUNKS_PER_STEP * CHUNK) == 0
    g_ones = _head_ones()
    row = lambda z: z.reshape(1, -1).astype(F32)

    hp = x_prompt.astype(F32).reshape(n_p * t_p, D_MODEL)
    hs = jnp.transpose(x_sample.astype(F32), (1, 0, 2)).reshape(t_s * n_s, D_MODEL)
    outs = [[] for _ in range(6)]
    for l in range(depth):
        zl = jnp.zeros((D_LORA, D_RWKV), F32)
        w_lora = jnp.concatenate([jnp.concatenate([w_decay_b[l], zl], axis=1),
                                  jnp.concatenate([zl, w_aaa_b[l]], axis=1)], axis=0).astype(BF16)
        prep_params = (row(norm_w[l]), w_in[l].astype(BF16), row(mu_shift[l]), w_lora, row(w0[l]),
                       row(a0[l]), row(k_k[l]), row(k_a[l]), row(r_k[l]), pool_w[l].astype(BF16),
                       row(pool_scale[l]), g_ones)
        out_params = (row(gn_w[l]), row(gn_b[l]), w_out[l].astype(BF16), row(norm_f), g_ones)

        hp, nsh, npl, s_fin = _fused_call(hp, prep_params, out_params,
                                          n_seq=n_p, n_chunks=WKV_CHUNKS_PER_STEP, t0=0)
        outs[0].append(nsh[:, SUBLANES - 1])
        outs[1].append(s_fin)
        outs[2].append(npl[:, -POOL_KEEP:])

        sh_hist = state_shift[l].astype(F32)[None]
        pool_hist = jnp.transpose(state_pool[l].astype(F32), (1, 0, 2)).reshape(1, POOL_KEEP * n_s, D_POOL)
        bon, sg, op, nsh, npl, *scan_ops = _short_prep_call(hs, sh_hist, pool_hist, prep_params,
                                                            t_s=t_s, n_s=n_s, t0=PAST_LEN)
        s_fin, hs = _short_state_call(jnp.transpose(state_wkv[l].astype(F32), (1, 2, 3, 0)), *scan_ops,
                                      bon, sg, op, hs, out_params)
        s_fin = jnp.transpose(s_fin, (3, 0, 1, 2))
        outs[3].append(nsh[0])
        outs[4].append(s_fin)
        outs[5].append(jnp.transpose(npl[0].reshape(POOL_KEEP, n_s, D_POOL), (1, 0, 2)))

    y_prompt = hp.reshape(n_p, t_p, D_MODEL).astype(x_prompt.dtype)
    y_sample = jnp.transpose(hs.reshape(t_s, n_s, D_MODEL), (1, 0, 2)).astype(x_sample.dtype)
    return (y_prompt, y_sample) + tuple(jnp.stack(o, axis=0) for o in outs)
```
